```python
import math
import jax
import jax.numpy as jnp
from jax import lax
import numpy as np

D_MODEL = 2048
BATCH = 4
SEQ = 2048
DEPTH = 2

MEM_LEN = 256
MIX_W = D_MODEL
LRU_W = MIX_W // 4
LRU_BLOCKS = 4
LRU_CONV = 4
LRU_C = 8.0
ML_W = 3 * MIX_W // 8
ML_HEADS = 4
ML_DH = ML_W // ML_HEADS
ML_QKV_BLOCK = 4
ML_CONV = 4
ML_CHUNK = 128
MO_W = MIX_W - LRU_W - ML_W
MO_HEADS = 6
MO_DH = MO_W // MO_HEADS
MOBA_BLOCK = 256
MOBA_TOPK = 3
MOBA_Q_CHUNK = 16
XA_HEADS = 4
XA_DH = 128
XA_W = XA_HEADS * XA_DH

RMS_EPS = 1e-6
LN_EPS = 1e-5
NEG_INF = -1e30

SPLIT_SIZES = (LRU_W, LRU_W, ML_W, ML_W, ML_W, ML_HEADS, ML_HEADS, MO_W, MO_W, MO_W, MO_W)
IN_COLS = sum(SPLIT_SIZES)
SPLIT_POINTS = tuple(int(v) for v in np.cumsum(SPLIT_SIZES)[:-1])

kernel_name = "hybrid_lru_mlstm_moba_block"


def rms_norm(x, g):
    x32 = x.astype(jnp.float32)
    y = x32 * lax.rsqrt(jnp.mean(x32 * x32, axis=-1, keepdims=True) + RMS_EPS)
    return (y * g.astype(jnp.float32)).astype(x.dtype)


def causal_conv(x, w, b):
    K = w.shape[0]
    T = x.shape[1]
    xp = jnp.pad(x, ((0, 0), (K - 1, 0), (0, 0)))
    return sum(xp[:, j:j + T] * w[j] for j in range(K)) + b


def block_diag(x, w):
    G, bi, bo = w.shape
    xs = x.reshape(x.shape[:-1] + (G, bi))
    return jnp.einsum('...gi,gio->...go', xs, w).reshape(x.shape[:-1] + (G * bo,))


def alibi_slopes(n):
    def pow2(m):
        start = 2.0 ** (-8.0 / m)
        return [start ** (i + 1) for i in range(m)]
    if math.log2(n).is_integer():
        s = pow2(n)
    else:
        c = 2 ** int(math.floor(math.log2(n)))
        s = pow2(c) + pow2(2 * c)[0::2][:n - c]
    return jnp.array(s, dtype=jnp.float32)


def rg_lru_branch(xb, zb, conv_w, conv_b, wa, ba, wx, bx, lam):
    xc = causal_conv(xb, conv_w, conv_b).astype(jnp.float32)
    r = jax.nn.sigmoid(block_diag(xc, wa) + ba)
    i = jax.nn.sigmoid(block_diag(xc, wx) + bx)
    log_a = -LRU_C * r * jax.nn.softplus(-lam.astype(jnp.float32))
    a = jnp.exp(log_a)
    u = jnp.sqrt(-jnp.expm1(2.0 * log_a)) * (i * xc)

    def combine(left, right):
        a1, b1 = left
        a2, b2 = right
        return a1 * a2, a2 * b1 + b2

    _, h = lax.associative_scan(combine, (a, u), axis=1)
    return (h * jax.nn.silu(zb.astype(jnp.float32))).astype(xb.dtype)


def mlstm_chunkwise(q, k, v, ig, lf):
    B, H, T, dh = q.shape
    L = ML_CHUNK
    nc = T // L
    q, k, v = (t.reshape(B, H, nc, L, dh) for t in (q, k, v))
    ig = ig.reshape(B, H, nc, L)
    lf = lf.reshape(B, H, nc, L)
    b = jnp.cumsum(lf, axis=-1)
    g = b[..., -1]
    a = g[..., None] - b + ig
    m_loc = jnp.max(a, axis=-1)
    w = jnp.exp(a - m_loc[..., None])
    c_loc = jnp.einsum('bhcl,bhcld,bhcle->bhcde', w, k, v)
    n_loc = jnp.einsum('bhcl,bhcld->bhcd', w, k)

    def step(carry, inp):
        c_st, n_st, m_st = carry
        c_l, n_l, m_l, g_c = inp
        m_new = jnp.maximum(g_c + m_st, m_l)
        s_old = jnp.exp(g_c + m_st - m_new)
        s_new = jnp.exp(m_l - m_new)
        c_new = s_old[..., None, None] * c_st + s_new[..., None, None] * c_l
        n_new = s_old[..., None] * n_st + s_new[..., None] * n_l
        return (c_new, n_new, m_new), (c_st, n_st, m_st)

    init = (jnp.zeros((B, H, dh, dh), jnp.float32), jnp.zeros((B, H, dh), jnp.float32),
            jnp.zeros((B, H), jnp.float32))
    xs = tuple(jnp.moveaxis(t, 2, 0) for t in (c_loc, n_loc, m_loc, g))
    _, (c_prev, n_prev, m_prev) = lax.scan(step, init, xs)
    c_prev = jnp.moveaxis(c_prev, 0, 2)
    n_prev = jnp.moveaxis(n_prev, 0, 2)
    m_prev = jnp.moveaxis(m_prev, 0, 2)
    causal = jnp.tril(jnp.ones((L, L), dtype=bool))
    d = jnp.where(causal, b[..., :, None] - b[..., None, :] + ig[..., None, :], -jnp.inf)
    inter = b + m_prev[..., None]
    m = jnp.maximum(inter, jnp.max(d, axis=-1))
    s = jnp.einsum('bhcjd,bhcsd->bhcjs', q, k) * jnp.exp(d - m[..., None])
    s_int = jnp.exp(inter - m)
    num = (jnp.einsum('bhcjs,bhcsd->bhcjd', s, v)
           + s_int[..., None] * jnp.einsum('bhcjd,bhcde->bhcje', q, c_prev))
    den = jnp.sum(s, axis=-1) + s_int * jnp.einsum('bhcjd,bhcd->bhcj', q, n_prev)
    h = num / jnp.maximum(jnp.abs(den), jnp.exp(-m))[..., None]
    return h.reshape(B, H, T, dh)


def mlstm_branch(u, o, z, ig_pre, fg_pre, conv_w, conv_b, wq, wk, wv, bi, bf, norm_g):
    B, T, _ = u.shape
    uc = jax.nn.silu(causal_conv(u, conv_w, conv_b))
    q = block_diag(uc, wq)
    k = block_diag(uc, wk) * (ML_DH ** -0.5)
    v = block_diag(u, wv)

    def heads(t):
        return t.astype(jnp.float32).reshape(B, T, ML_HEADS, ML_DH).transpose(0, 2, 1, 3)

    ig = (ig_pre.astype(jnp.float32) + bi).transpose(0, 2, 1)
    lf = jax.nn.log_sigmoid(fg_pre.astype(jnp.float32) + bf).transpose(0, 2, 1)
    h = mlstm_chunkwise(heads(q), heads(k), heads(v), ig, lf).transpose(0, 2, 1, 3)
    h = jax.nn.sigmoid(o.astype(jnp.float32)).reshape(B, T, ML_HEADS, ML_DH) * h
    mu = jnp.mean(h, axis=-1, keepdims=True)
    var = jnp.mean(jnp.square(h - mu), axis=-1, keepdims=True)
    hn = (h - mu) * lax.rsqrt(var + LN_EPS) * norm_g.astype(jnp.float32).reshape(ML_HEADS, ML_DH)
    return (hn.reshape(B, T, ML_W) * jax.nn.silu(z.astype(jnp.float32))).astype(u.dtype)


def moba_attention(q, k, v):
    B, H, T, dh = q.shape
    nb = -(-T // MOBA_BLOCK)
    tp = nb * MOBA_BLOCK
    pad = ((0, 0), (0, 0), (0, tp - T), (0, 0))
    q, k, v = (jnp.pad(t, pad) for t in (q, k, v))
    kb = k.reshape(B, H, nb, MOBA_BLOCK, dh)
    vb = v.reshape(B, H, nb, MOBA_BLOCK, dh)
    kmean = jnp.mean(kb.astype(jnp.float32), axis=3)
    n_sel = min(MOBA_TOPK, nb - 1)
    slopes = alibi_slopes(H)
    scale = dh ** -0.5
    nq = tp // MOBA_Q_CHUNK
    qc = jnp.moveaxis(q.reshape(B, H, nq, MOBA_Q_CHUNK, dh), 2, 0)
    bidx = jnp.arange(B)[:, None, None, None]
    hidx = jnp.arange(H)[None, :, None, None]
    s_local = jnp.arange(MOBA_BLOCK)

    def one_chunk(args):
        qch, ci = args
        qch = qch.astype(jnp.float32)
        t = ci * MOBA_Q_CHUNK + jnp.arange(MOBA_Q_CHUNK)
        bq = (ci * MOBA_Q_CHUNK) // MOBA_BLOCK
        k_own = lax.dynamic_index_in_dim(kb, bq, axis=2, keepdims=False).astype(jnp.float32)
        v_own = lax.dynamic_index_in_dim(vb, bq, axis=2, keepdims=False).astype(jnp.float32)
        dist_own = (t[:, None] - (bq * MOBA_BLOCK + s_local)[None, :]).astype(jnp.float32)
        l_own = jnp.einsum('bhqd,bhsd->bhqs', qch, k_own) * scale - slopes[:, None, None] * dist_own
        l_own = jnp.where(dist_own >= 0, l_own, NEG_INF)
        if n_sel == 0:
            p_own = jax.nn.softmax(l_own, axis=-1)
            return jnp.einsum('bhqs,bhsd->bhqd', p_own, v_own)
        gate = jnp.einsum('bhqd,bhnd->bhqn', qch, kmean)
        gate = jnp.where(jnp.arange(nb) < bq, gate, -jnp.inf)
        _, idx = lax.top_k(gate, n_sel)
        k_sel = kb[bidx, hidx, idx].astype(jnp.float32)
        v_sel = vb[bidx, hidx, idx].astype(jnp.float32)
        dist_sel = (t[:, None, None] - (idx[..., None] * MOBA_BLOCK + s_local)).astype(jnp.float32)
        l_sel = (jnp.einsum('bhqd,bhqnsd->bhqns', qch, k_sel) * scale
                 - slopes[:, None, None, None] * dist_sel)
        l_sel = jnp.where((jnp.arange(n_sel) < bq)[:, None], l_sel, NEG_INF)
        n_past = n_sel * MOBA_BLOCK
        logits = jnp.concatenate(
            [l_sel.reshape(B, H, MOBA_Q_CHUNK, n_past), l_own], axis=-1)
        p = jax.nn.softmax(logits, axis=-1)
        p_sel = p[..., :n_past].reshape(B, H, MOBA_Q_CHUNK, n_sel, MOBA_BLOCK)
        p_own = p[..., n_past:]
        return (jnp.einsum('bhqns,bhqnsd->bhqd', p_sel, v_sel)
                + jnp.einsum('bhqs,bhsd->bhqd', p_own, v_own))

    out = lax.map(one_chunk, (qc, jnp.arange(nq)))
    out = jnp.moveaxis(out, 0, 2).reshape(B, H, tp, dh)
    return out[:, :, :T]


def moba_branch(qb, kb, vb, zb):
    B, T, _ = qb.shape

    def heads(t):
        return t.reshape(B, T, MO_HEADS, MO_DH).transpose(0, 2, 1, 3)

    o = moba_attention(heads(qb), heads(kb), heads(vb)).transpose(0, 2, 1, 3).reshape(B, T, MO_W)
    return (o * jax.nn.silu(zb.astype(jnp.float32))).astype(qb.dtype)


def memory_cross_attention(h, mem_n, wq, wkv, wo):
    B, T, _ = h.shape
    M = mem_n.shape[1]
    q = (h @ wq).reshape(B, T, XA_HEADS, XA_DH)
    k, v = jnp.split(mem_n @ wkv, 2, axis=-1)
    k = k.reshape(B, M, XA_HEADS, XA_DH)
    v = v.reshape(B, M, XA_HEADS, XA_DH)
    s = jnp.einsum('bthd,bmhd->bhtm', q, k).astype(jnp.float32) * (XA_DH ** -0.5)
    p = jax.nn.softmax(s, axis=-1).astype(v.dtype)
    o = jnp.einsum('bhtm,bmhd->bthd', p, v).reshape(B, T, XA_W)
    return o @ wo


def setup_inputs(seed: int = 0) -> dict:
    key = jax.random.key(seed)
    ks = jax.random.split(key, 26)

    def nrm(k, shape, scale):
        return scale * jax.random.normal(k, shape, jnp.float32)

    def gain(k, shape):
        return 1.0 + 0.02 * jax.random.normal(k, shape, jnp.float32)

    u = jax.random.uniform(ks[10], (DEPTH, LRU_W), jnp.float32, 0.9, 0.999)
    a = u ** (1.0 / LRU_C)
    lam = jnp.log(a) - jnp.log1p(-a)
    bl = LRU_W // LRU_BLOCKS
    nqkv = ML_W // ML_QKV_BLOCK
    return {
        "x": nrm(ks[0], (BATCH, SEQ, D_MODEL), 1.0),
        "mem": nrm(ks[1], (BATCH, MEM_LEN, D_MODEL), 1.0),
        "mix_norm_g": gain(ks[2], (DEPTH, D_MODEL)),
        "w_in": nrm(ks[3], (DEPTH, D_MODEL, IN_COLS), D_MODEL ** -0.5),
        "lru_conv_w": nrm(ks[4], (DEPTH, LRU_CONV, LRU_W), LRU_CONV ** -0.5),
        "lru_conv_b": nrm(ks[5], (DEPTH, LRU_W), 0.02),
        "lru_wa": nrm(ks[6], (DEPTH, LRU_BLOCKS, bl, bl), bl ** -0.5),
        "lru_ba": nrm(ks[7], (DEPTH, LRU_W), 0.02),
        "lru_wx": nrm(ks[8], (DEPTH, LRU_BLOCKS, bl, bl), bl ** -0.5),
        "lru_bx": nrm(ks[9], (DEPTH, LRU_W), 0.02),
        "lru_lambda": lam,
        "ml_conv_w": nrm(ks[11], (DEPTH, ML_CONV, ML_W), ML_CONV ** -0.5),
        "ml_conv_b": nrm(ks[12], (DEPTH, ML_W), 0.02),
        "ml_wq": nrm(ks[13], (DEPTH, nqkv, ML_QKV_BLOCK, ML_QKV_BLOCK), ML_QKV_BLOCK ** -0.5),
        "ml_wk": nrm(ks[14], (DEPTH, nqkv, ML_QKV_BLOCK, ML_QKV_BLOCK), ML_QKV_BLOCK ** -0.5),
        "ml_wv": nrm(ks[15], (DEPTH, nqkv, ML_QKV_BLOCK, ML_QKV_BLOCK), ML_QKV_BLOCK ** -0.5),
        "ml_bi": nrm(ks[16], (DEPTH, ML_HEADS), 0.1),
        "ml_bf": jnp.linspace(3.0, 6.0, ML_HEADS, dtype=jnp.float32)[None, :] + nrm(ks[17], (DEPTH, ML_HEADS), 0.02),
        "ml_norm_g": gain(ks[18], (DEPTH, ML_W)),
        "w_out": nrm(ks[19], (DEPTH, MIX_W, D_MODEL), MIX_W ** -0.5),
        "xa_norm_g": gain(ks[20], (DEPTH, D_MODEL)),
        "mem_norm_g": gain(ks[21], (DEPTH, D_MODEL)),
        "xa_wq": nrm(ks[22], (DEPTH, D_MODEL, XA_W), D_MODEL ** -0.5),
        "xa_wkv": nrm(ks[23], (DEPTH, D_MODEL, 2 * XA_W), D_MODEL ** -0.5),
        "xa_wo": nrm(ks[24], (DEPTH, XA_W, D_MODEL), XA_W ** -0.5),
        "final_norm_g": gain(ks[25], (D_MODEL,)),
    }


def reference(x, mem, mix_norm_g, w_in, lru_conv_w, lru_conv_b, lru_wa, lru_ba, lru_wx, lru_bx,
              lru_lambda, ml_conv_w, ml_conv_b, ml_wq, ml_wk, ml_wv, ml_bi, ml_bf, ml_norm_g,
              w_out, xa_norm_g, mem_norm_g, xa_wq, xa_wkv, xa_wo, final_norm_g):
    for l in range(DEPTH):
        h = rms_norm(x, mix_norm_g[l])
        p = h @ w_in[l]
        (lru_x, lru_z, ml_u, ml_o, ml_z, ml_i, ml_f,
         mo_q, mo_k, mo_v, mo_z) = jnp.split(p, SPLIT_POINTS, axis=-1)
        y_lru = rg_lru_branch(lru_x, lru_z, lru_conv_w[l], lru_conv_b[l], lru_wa[l], lru_ba[l],
                              lru_wx[l], lru_bx[l], lru_lambda[l])
        y_ml = mlstm_branch(ml_u, ml_o, ml_z, ml_i, ml_f, ml_conv_w[l], ml_conv_b[l], ml_wq[l],
                            ml_wk[l], ml_wv[l], ml_bi[l], ml_bf[l], ml_norm_g[l])
        y_mo = moba_branch(mo_q, mo_k, mo_v, mo_z)
        y = jnp.concatenate([y_lru, y_ml, y_mo], axis=-1)
        x = x + y @ w_out[l]
        hx = rms_norm(x, xa_norm_g[l])
        hm = rms_norm(mem, mem_norm_g[l])
        x = x + memory_cross_attention(hx, hm, xa_wq[l], xa_wkv[l], xa_wo[l])
    return rms_norm(x, final_norm_g)
```

```python
import functools
import math

import jax
import jax.numpy as jnp
import numpy as np
from jax import lax
from jax.experimental import pallas as pl
from jax.experimental.pallas import tpu as pltpu

LANE = 128
SUBLANE = 8
V7X_VMEM_BYTES = 64 * 1024 * 1024

D_MODEL = 2048
MEM_LEN = 256
LRU_W = 512
LRU_BLOCKS = 4
LRU_BW = LRU_W // LRU_BLOCKS
LRU_C = 8.0
ML_W = 768
ML_HEADS = 4
ML_DH = 192
ML_DHP = 256
ML_WP = ML_HEADS * ML_DHP
ML_QKV_BLOCK = 4
ML_CHUNK = 128
MO_W = 768
MO_HEADS = 6
MO_DH = 128
MOBA_BLOCK = 256
MOBA_TOPK = 3
XA_HEADS = 4
XA_DH = 128
XA_W = XA_HEADS * XA_DH
RMS_EPS = 1e-6
LN_EPS = 1e-5
NEG_INF = -1e30

SPLIT_SIZES = (LRU_W, LRU_W, ML_W, ML_W, ML_W, ML_HEADS, ML_HEADS, MO_W, MO_W, MO_W, MO_W)
SPLIT_POINTS = tuple(int(v) for v in np.cumsum(SPLIT_SIZES)[:-1])

C_LRU_X = 0
C_LRU_Z = C_LRU_X + LRU_W
C_ML_U = C_LRU_Z + LRU_W
C_ML_O = C_ML_U + ML_WP
C_ML_Z = C_ML_O + ML_WP
C_MO_Q = C_ML_Z + ML_WP
C_MO_K = C_MO_Q + MO_W
C_MO_V = C_MO_K + MO_W
C_MO_Z = C_MO_V + MO_W
C_GATE = C_MO_Z + MO_W
IN_COLS_P = 7680
Y_W = LRU_W + ML_WP + MO_W

IN_TM = 1024
IN_TN = 768
OUT_TM = 512
XA_TM = 512

BF16 = jnp.bfloat16
F32 = jnp.float32


def _vmem_limit(nbytes):
    return int(min(V7X_VMEM_BYTES - (4 << 20), max(32 << 20, nbytes)))


def _rmsnorm_rows(x, g):
    ms = jnp.mean(x * x, axis=-1, keepdims=True)
    return x * lax.rsqrt(ms + RMS_EPS) * g


def _sigmoid(x):
    return jax.nn.sigmoid(x)


def _silu(x):
    return x * jax.nn.sigmoid(x)


def _softplus(x):
    return jnp.maximum(x, 0.0) + jnp.log1p(jnp.exp(-jnp.abs(x)))


def _shift_rows(x, s):
    rolled = pltpu.roll(x, s, axis=0)
    row = lax.broadcasted_iota(jnp.int32, x.shape, 0)
    return jnp.where(row >= s, rolled, 0.0)


def _causal_conv(x, w_ref, b_ref):
    k = w_ref.shape[0]
    acc = x * w_ref[k - 1:k, :]
    for j in range(k - 1):
        acc = acc + _shift_rows(x, k - 1 - j) * w_ref[j:j + 1, :]
    return acc + b_ref[...]


def _in_proj_kernel(x_ref, g_ref, w_ref, o_ref, xn_ref):
    @pl.when(pl.program_id(1) == 0)
    def _():
        xn_ref[...] = _rmsnorm_rows(x_ref[...], g_ref[...]).astype(BF16)

    o_ref[...] = jnp.dot(xn_ref[...], w_ref[...], preferred_element_type=F32).astype(o_ref.dtype)


def _in_proj(x2d, g, w):
    m = x2d.shape[0]
    n = w.shape[1]
    vmem = 2 * IN_TM * D_MODEL * 4 + IN_TM * D_MODEL * 2 + 2 * D_MODEL * IN_TN * 2 + 4 * IN_TM * IN_TN * 4
    return pl.pallas_call(
        _in_proj_kernel,
        grid=(m // IN_TM, n // IN_TN),
        in_specs=[
            pl.BlockSpec((IN_TM, D_MODEL), lambda i, j: (i, 0)),
            pl.BlockSpec((1, D_MODEL), lambda i, j: (0, 0)),
            pl.BlockSpec((D_MODEL, IN_TN), lambda i, j: (0, j)),
        ],
        out_specs=pl.BlockSpec((IN_TM, IN_TN), lambda i, j: (i, j)),
        out_shape=jax.ShapeDtypeStruct((m, n), BF16),
        scratch_shapes=[pltpu.VMEM((IN_TM, D_MODEL), BF16)],
        compiler_params=pltpu.CompilerParams(
            dimension_semantics=("arbitrary", "arbitrary"), vmem_limit_bytes=_vmem_limit(vmem)),
        name="in_proj",
    )(x2d, g, w)


def _lru_kernel(x_ref, z_ref, cw_ref, cb_ref, wax_ref, ba_ref, bx_ref, lam_ref, o_ref, a_s, u_s):
    t = x_ref.shape[1]
    x = x_ref[0].astype(F32)
    xc = _causal_conv(x, cw_ref, cb_ref)
    pre = jnp.dot(xc.astype(BF16), wax_ref[0], preferred_element_type=F32)
    r = _sigmoid(pre[:, :LRU_BW] + ba_ref[...])
    i = _sigmoid(pre[:, LRU_BW:] + bx_ref[...])
    log_a = (-LRU_C) * r * _softplus(-lam_ref[...])
    a = jnp.exp(log_a)
    a_s[...] = a
    u_s[...] = jnp.sqrt(-jnp.tanh(log_a) * (1.0 + a * a)) * (i * xc)

    row = lax.broadcasted_iota(jnp.int32, (SUBLANE, LRU_BW), 0)

    def block(blk, h_prev):
        r0 = pl.multiple_of(blk * SUBLANE, SUBLANE)
        a_b = a_s[pl.ds(r0, SUBLANE), :]
        u_b = u_s[pl.ds(r0, SUBLANE), :]
        for s in (1, 2, 4):
            a_sh = jnp.where(row >= s, pltpu.roll(a_b, s, axis=0), 1.0)
            u_sh = jnp.where(row >= s, pltpu.roll(u_b, s, axis=0), 0.0)
            u_b = a_b * u_sh + u_b
            a_b = a_b * a_sh
        h = a_b * h_prev + u_b
        u_s[pl.ds(r0, SUBLANE), :] = h
        return jnp.broadcast_to(h[SUBLANE - 1:SUBLANE, :], (SUBLANE, LRU_BW))

    lax.fori_loop(0, t // SUBLANE, block, jnp.zeros((SUBLANE, LRU_BW), F32), unroll=4)
    z = z_ref[0].astype(F32)
    o_ref[0] = (u_s[...] * _silu(z)).astype(o_ref.dtype)


def _lru_branch(p, cw, cb, wax, ba, bx, lam):
    b, t, _ = p.shape
    xb, zb = C_LRU_X // LRU_BW, C_LRU_Z // LRU_BW
    vec = pl.BlockSpec((1, LRU_BW), lambda bi, g: (0, g))
    return pl.pallas_call(
        _lru_kernel,
        grid=(b, LRU_BLOCKS),
        in_specs=[
            pl.BlockSpec((1, t, LRU_BW), lambda bi, g: (bi, 0, xb + g)),
            pl.BlockSpec((1, t, LRU_BW), lambda bi, g: (bi, 0, zb + g)),
            pl.BlockSpec((cw.shape[0], LRU_BW), lambda bi, g: (0, g)),
            vec,
            pl.BlockSpec((1, LRU_BW, 2 * LRU_BW), lambda bi, g: (g, 0, 0)),
            vec, vec, vec,
        ],
        out_specs=pl.BlockSpec((1, t, LRU_BW), lambda bi, g: (bi, 0, g)),
        out_shape=jax.ShapeDtypeStruct((b, t, LRU_W), BF16),
        scratch_shapes=[pltpu.VMEM((t, LRU_BW), F32), pltpu.VMEM((t, LRU_BW), F32)],
        compiler_params=pltpu.CompilerParams(
            dimension_semantics=("arbitrary", "arbitrary"), vmem_limit_bytes=_vmem_limit(24 * t * LRU_BW * 4)),
        name="rg_lru",
    )(p, p, cw, cb, wax, ba, bx, lam)


def _mlstm_kernel(u_ref, og_ref, z_ref, ig_ref, fg_ref, bi_ref, bf_ref, cw_ref, cb_ref,
                  dq_ref, dkt_ref, dv_ref, ng_ref, y_ref,
                  q_s, kt_s, v_s, c_s, ig_s, b_s, w_s, g_s, ml_s):
    t = u_ref.shape[1]
    nc = t // ML_CHUNK
    L = ML_CHUNK

    u = u_ref[0]
    uc = _silu(_causal_conv(u.astype(F32), cw_ref, cb_ref)).astype(BF16)
    q_s[...] = jnp.dot(uc, dq_ref[0], preferred_element_type=F32).astype(BF16)
    kt = lax.dot_general(dkt_ref[0], uc, (((1,), (1,)), ((), ())), preferred_element_type=F32)
    kt = kt * (ML_DH ** -0.5)
    for c in range(nc):
        kt_s[c] = kt[:, c * L:(c + 1) * L].astype(BF16)
    v = jnp.dot(u, dv_ref[0], preferred_element_type=F32)
    vlane = lax.broadcasted_iota(jnp.int32, v.shape, 1)
    v_s[...] = jnp.where(vlane == ML_DH, 1.0, v).astype(BF16)

    ig = ig_ref[0, 0] + bi_ref[0]
    lf = -_softplus(-(fg_ref[0, 0] + bf_ref[0]))
    glane = lax.broadcasted_iota(jnp.int32, (nc, L), 1)
    b = lf
    for k in range(int(math.log2(L))):
        sh = 1 << k
        b = b + jnp.where(glane >= sh, pltpu.roll(b, sh, axis=1), 0.0)
    g = b[:, L - 1:L]
    a = g - b + ig
    mloc = jnp.max(a, axis=1, keepdims=True)
    ig_s[...] = ig
    b_s[...] = b
    w_s[...] = jnp.exp(a - mloc)
    g_s[...] = jnp.broadcast_to(g, (nc, L))
    ml_s[...] = jnp.broadcast_to(mloc, (nc, L))

    c_s[...] = jnp.zeros(c_s.shape, F32)
    tri = (lax.broadcasted_iota(jnp.int32, (L, L), 0) >= lax.broadcasted_iota(jnp.int32, (L, L), 1))
    hmask = lax.broadcasted_iota(jnp.int32, (L, ML_DHP), 1) < ML_DH

    def chunk(c, m_st):
        r0 = pl.multiple_of(c * L, L)
        q_c = q_s[pl.ds(r0, L), :]
        kt_c = kt_s[c]
        v_c = v_s[pl.ds(r0, L), :]
        rb = jnp.broadcast_to(b_s[pl.ds(c, 1), :], (L, L))
        cb = rb.T
        d = cb - rb + jnp.broadcast_to(ig_s[pl.ds(c, 1), :], (L, L))
        d = jnp.where(tri, d, -jnp.inf)
        inter = cb[:, :1] + m_st[:, :1]
        m_j = jnp.maximum(inter, jnp.max(d, axis=1, keepdims=True))
        s_mat = jnp.dot(q_c, kt_c, preferred_element_type=F32) * jnp.exp(d - m_j)
        s_int = jnp.exp(inter - m_j)
        c_prev = c_s[...]
        nd = (jnp.dot(s_mat.astype(BF16), v_c, preferred_element_type=F32)
              + s_int * jnp.dot(q_c, c_prev.astype(BF16), preferred_element_type=F32))
        den = nd[:, ML_DH:ML_DH + 1]
        hh = nd / jnp.maximum(jnp.abs(den), jnp.exp(-m_j))

        hg = _sigmoid(og_ref[0, pl.ds(r0, L), :].astype(F32)) * hh
        mu = jnp.sum(jnp.where(hmask, hg, 0.0), axis=1, keepdims=True) * (1.0 / ML_DH)
        dev = jnp.where(hmask, hg - mu, 0.0)
        var = jnp.sum(dev * dev, axis=1, keepdims=True) * (1.0 / ML_DH)
        hn = dev * lax.rsqrt(var + LN_EPS) * ng_ref[...]
        y_ref[0, pl.ds(r0, L), :] = (hn * _silu(z_ref[0, pl.ds(r0, L), :].astype(F32))).astype(y_ref.dtype)

        ktw = (kt_c.astype(F32) * w_s[pl.ds(c, 1), :]).astype(BF16)
        c_loc = jnp.dot(ktw, v_c, preferred_element_type=F32)
        g_c = g_s[pl.ds(c, 1), :]
        ml_c = ml_s[pl.ds(c, 1), :]
        m_new = jnp.maximum(g_c + m_st, ml_c)
        s_old = jnp.exp(g_c + m_st - m_new)[:, :1]
        s_new = jnp.exp(ml_c - m_new)[:, :1]
        c_s[...] = s_old * c_prev + s_new * c_loc
        return m_new

    lax.fori_loop(0, nc, chunk, jnp.zeros((1, L), F32))


def _mlstm_branch(p, gates, gbias, cw, cb, dq, dkt, dv, ng):
    b, t, _ = p.shape
    nc = t // ML_CHUNK
    ub, ob, zb = C_ML_U // ML_DHP, C_ML_O // ML_DHP, C_ML_Z // ML_DHP
    seq = lambda base: pl.BlockSpec((1, t, ML_DHP), lambda bi, h: (bi, 0, base + h))
    dense = pl.BlockSpec((1, ML_DHP, ML_DHP), lambda bi, h: (h, 0, 0))
    vec = pl.BlockSpec((1, ML_DHP), lambda bi, h: (0, h))
    row = lambda: pltpu.VMEM((nc, ML_CHUNK), F32)
    return pl.pallas_call(
        _mlstm_kernel,
        grid=(b, ML_HEADS),
        in_specs=[
            seq(ub), seq(ob), seq(zb),
            pl.BlockSpec((1, 1, nc, ML_CHUNK), lambda bi, h: (bi, h, 0, 0)),
            pl.BlockSpec((1, 1, nc, ML_CHUNK), lambda bi, h: (bi, ML_HEADS + h, 0, 0)),
            pl.BlockSpec((1, 1, ML_CHUNK), lambda bi, h: (h, 0, 0)),
            pl.BlockSpec((1, 1, ML_CHUNK), lambda bi, h: (ML_HEADS + h, 0, 0)),
            pl.BlockSpec((cw.shape[0], ML_DHP), lambda bi, h: (0, h)),
            vec, dense, dense, dense, vec,
        ],
        out_specs=pl.BlockSpec((1, t, ML_DHP), lambda bi, h: (bi, 0, h)),
        out_shape=jax.ShapeDtypeStruct((b, t, ML_WP), BF16),
        scratch_shapes=[
            pltpu.VMEM((t, ML_DHP), BF16),
            pltpu.VMEM((nc, ML_DHP, ML_CHUNK), BF16),
            pltpu.VMEM((t, ML_DHP), BF16),
            pltpu.VMEM((ML_DHP, ML_DHP), F32),
            row(), row(), row(), row(), row(),
        ],
        compiler_params=pltpu.CompilerParams(
            dimension_semantics=("arbitrary", "arbitrary"), vmem_limit_bytes=_vmem_limit(20 * t * ML_DHP * 4)),
        name="mlstm",
    )(p, p, p, gates, gates, gbias, gbias, cw, cb, dq, dkt, dv, ng)


def _moba_kernel(q_ref, k_ref, v_ref, z_ref, slope_ref, o_ref, l_s):
    t = q_ref.shape[1]
    nb = t // MOBA_BLOCK
    bs = MOBA_BLOCK
    scale = MO_DH ** -0.5
    slope = slope_ref[0]

    krow = lax.broadcasted_iota(jnp.int32, (LANE, MO_DH), 0)
    kmean = jnp.zeros((LANE, MO_DH), F32)
    for n in range(nb):
        mean_n = jnp.sum(k_ref[0, n * bs:(n + 1) * bs, :].astype(F32), axis=0, keepdims=True) * (1.0 / bs)
        kmean = jnp.where(krow == n, mean_n, kmean)
    kmean = kmean.astype(BF16)

    lane = lax.broadcasted_iota(jnp.int32, (bs, LANE), 1)
    causal = (lax.broadcasted_iota(jnp.int32, (bs, bs), 0) >= lax.broadcasted_iota(jnp.int32, (bs, bs), 1))
    kpos = lax.broadcasted_iota(jnp.int32, (1, bs), 1).astype(F32)
    nt = (((1,), (1,)), ((), ()))

    for qb in range(nb):
        q_b = q_ref[0, qb * bs:(qb + 1) * bs, :]
        if qb > MOBA_TOPK:
            gate = lax.dot_general(q_b, kmean, nt, preferred_element_type=F32)
            beaten = jnp.zeros((bs, LANE), F32)
            for m in range(qb):
                gm = gate[:, m:m + 1]
                wins = (gm > gate) | ((gm == gate) & (lane > m))
                beaten = beaten + jnp.where(wins, 1.0, 0.0)
            keep = jnp.where(beaten < float(MOBA_TOPK), 1.0, 0.0)
        m_run = None
        for n in range(qb + 1):
            s = lax.dot_general(q_b, k_ref[0, n * bs:(n + 1) * bs, :], nt, preferred_element_type=F32)
            logit = s * scale + slope[:, :1] * (kpos + float((n - qb) * bs))
            if n == qb:
                logit = jnp.where(causal, logit, NEG_INF)
            elif qb > MOBA_TOPK:
                logit = jnp.where(keep[:, n:n + 1] > 0.5, logit, NEG_INF)
            l_s[:, n * bs:(n + 1) * bs] = logit
            mx = jnp.max(logit, axis=1, keepdims=True)
            m_run = mx if m_run is None else jnp.maximum(m_run, mx)
        nk = (qb + 1) * bs
        pr = jnp.exp(l_s[:, :nk] - m_run)
        den = jnp.sum(pr, axis=1, keepdims=True)
        acc = jnp.dot(pr.astype(BF16), v_ref[0, :nk, :], preferred_element_type=F32)
        z = z_ref[0, qb * bs:(qb + 1) * bs, :].astype(F32)
        o_ref[0, qb * bs:(qb + 1) * bs, :] = ((acc / den) * _silu(z)).astype(o_ref.dtype)


def _moba_branch(p, slopes):
    b, t, _ = p.shape
    qb, kb, vb, zb = (c // MO_DH for c in (C_MO_Q, C_MO_K, C_MO_V, C_MO_Z))
    seq = lambda base: pl.BlockSpec((1, t, MO_DH), lambda bi, h: (bi, 0, base + h))
    return pl.pallas_call(
        _moba_kernel,
        grid=(b, MO_HEADS),
        in_specs=[seq(qb), seq(kb), seq(vb), seq(zb),
                  pl.BlockSpec((1, 1, LANE), lambda bi, h: (h, 0, 0))],
        out_specs=pl.BlockSpec((1, t, MO_DH), lambda bi, h: (bi, 0, h)),
        out_shape=jax.ShapeDtypeStruct((b, t, MO_W), BF16),
        scratch_shapes=[pltpu.VMEM((MOBA_BLOCK, t), F32)],
        compiler_params=pltpu.CompilerParams(
            dimension_semantics=("arbitrary", "arbitrary"), vmem_limit_bytes=_vmem_limit(32 << 20)),
        name="moba",
    )(p, p, p, p, slopes)


def _out_proj_kernel(x_ref, yl_ref, ym_ref, yo_ref, w_ref, o_ref):
    acc = jnp.dot(yl_ref[...], w_ref[0:LRU_W, :], preferred_element_type=F32)
    acc = acc + jnp.dot(ym_ref[...], w_ref[LRU_W:LRU_W + ML_WP, :], preferred_element_type=F32)
    acc = acc + jnp.dot(yo_ref[...], w_ref[LRU_W + ML_WP:Y_W, :], preferred_element_type=F32)
    o_ref[...] = x_ref[...] + acc


def _out_proj(x2d, yl, ym, yo, w):
    m = x2d.shape[0]
    rows = lambda width: pl.BlockSpec((OUT_TM, width), lambda i: (i, 0))
    vmem = 4 * OUT_TM * D_MODEL * 4 + 2 * Y_W * D_MODEL * 2 + 2 * OUT_TM * Y_W * 2 + 2 * OUT_TM * D_MODEL * 4
    return pl.pallas_call(
        _out_proj_kernel,
        grid=(m // OUT_TM,),
        in_specs=[rows(D_MODEL), rows(LRU_W), rows(ML_WP), rows(MO_W),
                  pl.BlockSpec((Y_W, D_MODEL), lambda i: (0, 0))],
        out_specs=rows(D_MODEL),
        out_shape=jax.ShapeDtypeStruct((m, D_MODEL), F32),
        compiler_params=pltpu.CompilerParams(
            dimension_semantics=("arbitrary",), vmem_limit_bytes=_vmem_limit(vmem)),
        name="out_proj",
    )(x2d, yl, ym, yo, w)


def _mem_kv_kernel(mem_ref, g_ref, w_ref, o_ref):
    hm = _rmsnorm_rows(mem_ref[0], g_ref[...]).astype(BF16)
    o_ref[0] = jnp.dot(hm, w_ref[...], preferred_element_type=F32).astype(o_ref.dtype)


def _mem_kv(mem, g, wkv):
    b, m, _ = mem.shape
    return pl.pallas_call(
        _mem_kv_kernel,
        grid=(b,),
        in_specs=[pl.BlockSpec((1, m, D_MODEL), lambda bi: (bi, 0, 0)),
                  pl.BlockSpec((1, D_MODEL), lambda bi: (0, 0)),
                  pl.BlockSpec((D_MODEL, 2 * XA_W), lambda bi: (0, 0))],
        out_specs=pl.BlockSpec((1, m, 2 * XA_W), lambda bi: (bi, 0, 0)),
        out_shape=jax.ShapeDtypeStruct((b, m, 2 * XA_W), BF16),
        compiler_params=pltpu.CompilerParams(
            dimension_semantics=("arbitrary",), vmem_limit_bytes=_vmem_limit(32 << 20)),
        name="mem_kv",
    )(mem, g, wkv)


def _xattn_kernel(x_ref, g_ref, wq_ref, kv_ref, wo_ref, fg_ref, o_ref, *, final_norm):
    x = x_ref[0]
    hx = _rmsnorm_rows(x, g_ref[...]).astype(BF16)
    q = jnp.dot(hx, wq_ref[...], preferred_element_type=F32).astype(BF16)
    heads = []
    for h in range(XA_HEADS):
        q_h = q[:, h * XA_DH:(h + 1) * XA_DH]
        k_h = kv_ref[0, :, h * XA_DH:(h + 1) * XA_DH]
        v_h = kv_ref[0, :, XA_W + h * XA_DH:XA_W + (h + 1) * XA_DH]
        s = lax.dot_general(q_h, k_h, (((1,), (1,)), ((), ())), preferred_element_type=F32) * (XA_DH ** -0.5)
        e = jnp.exp(s - jnp.max(s, axis=1, keepdims=True))
        pr = e / jnp.sum(e, axis=1, keepdims=True)
        heads.append(jnp.dot(pr.astype(BF16), v_h, preferred_element_type=F32).astype(BF16))
    o = jnp.concatenate(heads, axis=1)
    y = x + jnp.dot(o, wo_ref[...], preferred_element_type=F32)
    if final_norm:
        y = _rmsnorm_rows(y, fg_ref[...])
    o_ref[0] = y


def _xattn(x, g, wq, kv, wo, fg, final_norm):
    b, t, _ = x.shape
    m = kv.shape[1]
    const = lambda shape: pl.BlockSpec(shape, lambda bi, i: (0,) * len(shape))
    return pl.pallas_call(
        functools.partial(_xattn_kernel, final_norm=final_norm),
        grid=(b, t // XA_TM),
        in_specs=[pl.BlockSpec((1, XA_TM, D_MODEL), lambda bi, i: (bi, i, 0)),
                  const((1, D_MODEL)),
                  const((D_MODEL, XA_W)),
                  pl.BlockSpec((1, m, 2 * XA_W), lambda bi, i: (bi, 0, 0)),
                  const((XA_W, D_MODEL)),
                  const((1, D_MODEL))],
        out_specs=pl.BlockSpec((1, XA_TM, D_MODEL), lambda bi, i: (bi, i, 0)),
        out_shape=jax.ShapeDtypeStruct((b, t, D_MODEL), F32),
        compiler_params=pltpu.CompilerParams(
            dimension_semantics=("arbitrary", "arbitrary"), vmem_limit_bytes=_vmem_limit(40 << 20)),
        name="mem_xattn",
    )(x, g, wq, kv, wo, fg)


def _pad_heads(w, heads, dh, dhp):
    w = w.reshape(w.shape[:-1] + (heads, dh))
    w = jnp.pad(w, [(0, 0)] * (w.ndim - 1) + [(0, dhp - dh)])
    return w.reshape(w.shape[:-2] + (heads * dhp,))


def _pad_ml(w):
    return _pad_heads(w, ML_HEADS, ML_DH, ML_DHP)


def _dense_heads(w):
    per_head = ML_DH // ML_QKV_BLOCK
    w = w.reshape(ML_HEADS, per_head, ML_QKV_BLOCK, ML_QKV_BLOCK)
    eye = jnp.eye(per_head, dtype=w.dtype)
    d = w[:, :, :, None, :] * eye[None, :, None, :, None]
    d = d.reshape(ML_HEADS, ML_DH, ML_DH)
    return jnp.pad(d, ((0, 0), (0, ML_DHP - ML_DH), (0, ML_DHP - ML_DH)))


def _alibi_slopes(n):
    def pow2(m):
        start = 2.0 ** (-8.0 / m)
        return [start ** (i + 1) for i in range(m)]
    if math.log2(n).is_integer():
        s = pow2(n)
    else:
        c = 2 ** int(math.floor(math.log2(n)))
        s = pow2(c) + pow2(2 * c)[0::2][:n - c]
    return np.asarray(s, dtype=np.float32)


def _layer_params(l, w_in, lru_conv_w, lru_conv_b, lru_wa, lru_ba, lru_wx, lru_bx, lru_lambda,
                  ml_conv_w, ml_conv_b, ml_wq, ml_wk, ml_wv, ml_bi, ml_bf, ml_norm_g, w_out):
    (w_lx, w_lz, w_mu, w_mo, w_mz, w_mi, w_mf, w_q, w_k, w_v, w_z) = jnp.split(w_in[l], SPLIT_POINTS, axis=-1)
    tail = IN_COLS_P - (C_GATE + 2 * ML_HEADS)
    w_in_p = jnp.concatenate(
        [w_lx, w_lz, _pad_ml(w_mu), _pad_ml(w_mo), _pad_ml(w_mz), w_q, w_k, w_v, w_z, w_mi, w_mf,
         jnp.zeros((D_MODEL, tail), w_in.dtype)], axis=-1).astype(BF16)
    w_o = w_out[l]
    w_out_p = jnp.concatenate(
        [w_o[:LRU_W], _pad_ml(w_o[LRU_W:LRU_W + ML_W].T).T, w_o[LRU_W + ML_W:]], axis=0).astype(BF16)
    dk = _dense_heads(ml_wk[l])
    return dict(
        w_in=w_in_p,
        w_out=w_out_p,
        lru_cw=lru_conv_w[l], lru_cb=lru_conv_b[l][None, :],
        lru_wax=jnp.concatenate([lru_wa[l], lru_wx[l]], axis=-1).astype(BF16),
        lru_ba=lru_ba[l][None, :], lru_bx=lru_bx[l][None, :], lru_lam=lru_lambda[l][None, :],
        ml_cw=_pad_ml(ml_conv_w[l]), ml_cb=_pad_ml(ml_conv_b[l])[None, :],
        ml_dq=_dense_heads(ml_wq[l]).astype(BF16),
        ml_dkt=jnp.swapaxes(dk, 1, 2).astype(BF16),
        ml_dv=_dense_heads(ml_wv[l]).astype(BF16),
        ml_gbias=jnp.broadcast_to(jnp.concatenate([ml_bi[l], ml_bf[l]])[:, None, None],
                                  (2 * ML_HEADS, 1, ML_CHUNK)),
        ml_ng=_pad_ml(ml_norm_g[l])[None, :],
    )


def kernel(x, mem, mix_norm_g, w_in, lru_conv_w, lru_conv_b, lru_wa, lru_ba, lru_wx, lru_bx, lru_lambda,
           ml_conv_w, ml_conv_b, ml_wq, ml_wk, ml_wv, ml_bi, ml_bf, ml_norm_g, w_out, xa_norm_g, mem_norm_g,
           xa_wq, xa_wkv, xa_wo, final_norm_g):
    b, t, d = x.shape
    depth = w_in.shape[0]
    nc = t // ML_CHUNK
    slopes = jnp.broadcast_to(jnp.asarray(_alibi_slopes(MO_HEADS))[:, None, None], (MO_HEADS, 1, LANE))
    for l in range(depth):
        lp = _layer_params(l, w_in, lru_conv_w, lru_conv_b, lru_wa, lru_ba, lru_wx, lru_bx, lru_lambda,
                           ml_conv_w, ml_conv_b, ml_wq, ml_wk, ml_wv, ml_bi, ml_bf, ml_norm_g, w_out)
        p = _in_proj(x.reshape(b * t, d), mix_norm_g[l][None, :], lp["w_in"]).reshape(b, t, IN_COLS_P)
        gates = p[:, :, C_GATE:C_GATE + 2 * ML_HEADS].astype(F32)
        gates = jnp.swapaxes(gates, 1, 2).reshape(b, 2 * ML_HEADS, nc, ML_CHUNK)
        y_lru = _lru_branch(p, lp["lru_cw"], lp["lru_cb"], lp["lru_wax"], lp["lru_ba"], lp["lru_bx"], lp["lru_lam"])
        y_ml = _mlstm_branch(p, gates, lp["ml_gbias"], lp["ml_cw"], lp["ml_cb"], lp["ml_dq"], lp["ml_dkt"],
                             lp["ml_dv"], lp["ml_ng"])
        y_mo = _moba_branch(p, slopes)
        x1 = _out_proj(x.reshape(b * t, d), y_lru.reshape(b * t, LRU_W), y_ml.reshape(b * t, ML_WP),
                       y_mo.reshape(b * t, MO_W), lp["w_out"]).reshape(b, t, d)
        kv = _mem_kv(mem, mem_norm_g[l][None, :], xa_wkv[l].astype(BF16))
        x = _xattn(x1, xa_norm_g[l][None, :], xa_wq[l].astype(BF16), kv, xa_wo[l].astype(BF16),
                   final_norm_g[None, :], final_norm=(l == depth - 1))
    return x
```

```python
import functools
import math

import jax
import jax.numpy as jnp
import numpy as np
from jax import lax
from jax.experimental import pallas as pl
from jax.experimental.pallas import tpu as pltpu

LANE = 128
SUBLANE = 8
V7X_VMEM_BYTES = 64 * 1024 * 1024

D_MODEL = 2048
LRU_W = 512
LRU_BLOCKS = 4
LRU_BW = LRU_W // LRU_BLOCKS
LRU_C = 8.0
ML_W = 768
ML_HEADS = 4
ML_DH = 192
ML_DHP = 256
ML_PAIRS = 2
ML_PW = 2 * ML_DH
ML_QKV_BLOCK = 4
ML_CHUNK = 128
MO_W = 768
MO_HEADS = 6
MO_DH = 128
MOBA_BLOCK = 256
MOBA_TOPK = 3
XA_HEADS = 4
XA_DH = 128
XA_W = XA_HEADS * XA_DH
RMS_EPS = 1e-6
LN_EPS = 1e-5
NEG_INF = -1e30
LOG2E = 1.4426950408889634

SPLIT_SIZES = (LRU_W, LRU_W, ML_W, ML_W, ML_W, ML_HEADS, ML_HEADS, MO_W, MO_W, MO_W, MO_W)
SPLIT_POINTS = tuple(int(v) for v in np.cumsum(SPLIT_SIZES)[:-1])

C_ML_U = 0
C_ML_O = C_ML_U + ML_W
C_ML_Z = C_ML_O + ML_W
C_LRU_X = C_ML_Z + ML_W
C_LRU_Z = C_LRU_X + LRU_W
C_MO_Q = C_LRU_Z + LRU_W
C_MO_K = C_MO_Q + MO_W
C_MO_V = C_MO_K + MO_W
C_MO_Z = C_MO_V + MO_W
C_GATE = C_MO_Z + MO_W
IN_TM = 1024
IN_TN = 512
IN_COLS_P = 6656
OUT_TM = 512
XA_TM = 512

BF16 = jnp.bfloat16
F32 = jnp.float32
NT_DIMS = (((1,), (1,)), ((), ()))


def _vmem_limit(nbytes):
    return int(min(V7X_VMEM_BYTES - (4 << 20), max(32 << 20, nbytes)))


def _rmsnorm_rows(x, g):
    ms = jnp.mean(x * x, axis=-1, keepdims=True)
    return x * lax.rsqrt(ms + RMS_EPS) * g


def _sigmoid(x):
    return jax.nn.sigmoid(x)


def _silu(x):
    return x * jax.nn.sigmoid(x)


def _softplus(x):
    return jnp.maximum(x, 0.0) + jnp.log1p(jnp.exp(-jnp.abs(x)))


def _shift_rows(x, s):
    rolled = pltpu.roll(x, s, axis=0)
    row = lax.broadcasted_iota(jnp.int32, x.shape, 0)
    return jnp.where(row >= s, rolled, 0.0)


def _causal_conv(x, w_ref, b_ref):
    k = w_ref.shape[0]
    acc = x * w_ref[k - 1:k, :]
    for j in range(k - 1):
        acc = acc + _shift_rows(x, k - 1 - j) * w_ref[j:j + 1, :]
    return acc + b_ref[...]


def _in_proj_kernel(x_ref, g_ref, w_ref, o_ref, xn_ref):
    @pl.when(pl.program_id(1) == 0)
    def _():
        xn_ref[...] = _rmsnorm_rows(x_ref[...], g_ref[...]).astype(BF16)

    o_ref[...] = jnp.dot(xn_ref[...], w_ref[...], preferred_element_type=F32).astype(o_ref.dtype)


def _in_proj(x2d, g, w):
    m = x2d.shape[0]
    n = w.shape[1]
    vmem = 2 * IN_TM * D_MODEL * 4 + IN_TM * D_MODEL * 2 + 2 * D_MODEL * IN_TN * 2 + 4 * IN_TM * IN_TN * 4
    return pl.pallas_call(
        _in_proj_kernel,
        grid=(m // IN_TM, n // IN_TN),
        in_specs=[
            pl.BlockSpec((IN_TM, D_MODEL), lambda i, j: (i, 0)),
            pl.BlockSpec((1, D_MODEL), lambda i, j: (0, 0)),
            pl.BlockSpec((D_MODEL, IN_TN), lambda i, j: (0, j)),
        ],
        out_specs=pl.BlockSpec((IN_TM, IN_TN), lambda i, j: (i, j)),
        out_shape=jax.ShapeDtypeStruct((m, n), BF16),
        scratch_shapes=[pltpu.VMEM((IN_TM, D_MODEL), BF16)],
        compiler_params=pltpu.CompilerParams(
            dimension_semantics=("arbitrary", "arbitrary"), vmem_limit_bytes=_vmem_limit(vmem)),
        name="in_proj",
    )(x2d, g, w)


def _lru_kernel(x_ref, z_ref, cw_ref, cb_ref, wax_ref, ba_ref, bx_ref, lam_ref, o_ref, a_s, u_s):
    t = x_ref.shape[1]
    x = x_ref[0].astype(F32)
    xc = _causal_conv(x, cw_ref, cb_ref)
    pre = jnp.dot(xc.astype(BF16), wax_ref[0], preferred_element_type=F32)
    r = _sigmoid(pre[:, :LRU_BW] + ba_ref[...])
    i = _sigmoid(pre[:, LRU_BW:] + bx_ref[...])
    log_a = (-LRU_C) * r * _softplus(-lam_ref[...])
    a = jnp.exp(log_a)
    a_s[...] = a
    u_s[...] = jnp.sqrt(-jnp.tanh(log_a) * (1.0 + a * a)) * (i * xc)

    row = lax.broadcasted_iota(jnp.int32, (SUBLANE, LRU_BW), 0)

    def block(blk, h_prev):
        r0 = pl.multiple_of(blk * SUBLANE, SUBLANE)
        a_b = a_s[pl.ds(r0, SUBLANE), :]
        u_b = u_s[pl.ds(r0, SUBLANE), :]
        for s in (1, 2, 4):
            a_sh = jnp.where(row >= s, pltpu.roll(a_b, s, axis=0), 1.0)
            u_sh = jnp.where(row >= s, pltpu.roll(u_b, s, axis=0), 0.0)
            u_b = a_b * u_sh + u_b
            a_b = a_b * a_sh
        h = a_b * h_prev + u_b
        u_s[pl.ds(r0, SUBLANE), :] = h
        return jnp.broadcast_to(h[SUBLANE - 1:SUBLANE, :], (SUBLANE, LRU_BW))

    lax.fori_loop(0, t // SUBLANE, block, jnp.zeros((SUBLANE, LRU_BW), F32), unroll=4)
    z = z_ref[0].astype(F32)
    o_ref[0] = (u_s[...] * _silu(z)).astype(o_ref.dtype)


def _lru_branch(p, cw, cb, wax, ba, bx, lam):
    b, t, _ = p.shape
    xb, zb = C_LRU_X // LRU_BW, C_LRU_Z // LRU_BW
    vec = pl.BlockSpec((1, LRU_BW), lambda bi, g: (0, g))
    return pl.pallas_call(
        _lru_kernel,
        grid=(b, LRU_BLOCKS),
        in_specs=[
            pl.BlockSpec((1, t, LRU_BW), lambda bi, g: (bi, 0, xb + g)),
            pl.BlockSpec((1, t, LRU_BW), lambda bi, g: (bi, 0, zb + g)),
            pl.BlockSpec((cw.shape[0], LRU_BW), lambda bi, g: (0, g)),
            vec,
            pl.BlockSpec((1, LRU_BW, 2 * LRU_BW), lambda bi, g: (g, 0, 0)),
            vec, vec, vec,
        ],
        out_specs=pl.BlockSpec((1, t, LRU_BW), lambda bi, g: (bi, 0, g)),
        out_shape=jax.ShapeDtypeStruct((b, t, LRU_W), BF16),
        scratch_shapes=[pltpu.VMEM((t, LRU_BW), F32), pltpu.VMEM((t, LRU_BW), F32)],
        compiler_params=pltpu.CompilerParams(
            dimension_semantics=("arbitrary", "arbitrary"), vmem_limit_bytes=_vmem_limit(24 * t * LRU_BW * 4)),
        name="rg_lru",
    )(p, p, cw, cb, wax, ba, bx, lam)


def _mlstm_kernel(u_ref, og_ref, z_ref, ig_ref, fg_ref, bi_ref, bf_ref, cw_ref, cb_ref,
                  dq_ref, dkt_ref, dv_ref, ng_ref, y_ref,
                  q_s, kt_s, v_s, cp_s, c_s, ig_s, b_s, w_s, mp_s, so_s, sn_s):
    t = u_ref.shape[1]
    nc = t // ML_CHUNK
    L = ML_CHUNK
    heads = range(2)

    u = u_ref[0]
    uc = _silu(_causal_conv(u.astype(F32), cw_ref, cb_ref)).astype(BF16)
    for hh in heads:
        q_s[hh] = jnp.dot(uc, dq_ref[0, hh], preferred_element_type=F32).astype(BF16)
        kt = lax.dot_general(dkt_ref[0, hh], uc, NT_DIMS, preferred_element_type=F32) * (ML_DH ** -0.5)
        for c in range(nc):
            kt_s[hh, c] = kt[:, c * L:(c + 1) * L].astype(BF16)
        v = jnp.dot(u, dv_ref[0, hh], preferred_element_type=F32)
        vlane = lax.broadcasted_iota(jnp.int32, v.shape, 1)
        v_s[hh] = jnp.where(vlane == ML_DH, 1.0, v).astype(BF16)

    glane = lax.broadcasted_iota(jnp.int32, (nc, L), 1)
    grow = lax.broadcasted_iota(jnp.int32, (nc, L), 0)
    for hh in heads:
        ig = ig_ref[0, hh] + bi_ref[hh]
        lf = -_softplus(-(fg_ref[0, hh] + bf_ref[hh]))
        b = lf
        for k in range(int(math.log2(L))):
            sh = 1 << k
            b = b + jnp.where(glane >= sh, pltpu.roll(b, sh, axis=1), 0.0)
        g = jnp.broadcast_to(b[:, L - 1:L], (nc, L))
        a = g - b + ig
        mloc = jnp.broadcast_to(jnp.max(a, axis=1, keepdims=True), (nc, L))
        m = jnp.zeros((1, L), F32)
        m_prev = jnp.zeros((nc, L), F32)
        m_next = jnp.zeros((nc, L), F32)
        for c in range(nc):
            m_prev = jnp.where(grow == c, m, m_prev)
            m = jnp.maximum(g[c:c + 1, :] + m, mloc[c:c + 1, :])
            m_next = jnp.where(grow == c, m, m_next)
        ig_s[hh] = ig
        b_s[hh] = b
        w_s[hh] = jnp.exp(a - mloc)
        mp_s[hh] = m_prev
        so_s[hh] = jnp.exp(g + m_prev - m_next)
        sn_s[hh] = jnp.exp(mloc - m_next)

    for hh in heads:
        c_s[...] = jnp.zeros(c_s.shape, F32)

        def state(c, carry, hh=hh):
            r0 = pl.multiple_of(c * L, L)
            c_prev = c_s[...]
            cp_s[hh, c] = c_prev.astype(BF16)
            ktw = (kt_s[hh, c].astype(F32) * w_s[hh, pl.ds(c, 1), :]).astype(BF16)
            c_loc = jnp.dot(ktw, v_s[hh, pl.ds(r0, L), :], preferred_element_type=F32)
            c_s[...] = so_s[hh, pl.ds(c, 1), :][:, :1] * c_prev + sn_s[hh, pl.ds(c, 1), :][:, :1] * c_loc
            return carry

        lax.fori_loop(0, nc, state, 0, unroll=2)

    tri = (lax.broadcasted_iota(jnp.int32, (L, L), 0) >= lax.broadcasted_iota(jnp.int32, (L, L), 1))

    def chunk(c, carry):
        r0 = pl.multiple_of(c * L, L)
        og = og_ref[0, pl.ds(r0, L), :].astype(F32)
        zz = z_ref[0, pl.ds(r0, L), :].astype(F32)
        gate = _sigmoid(og)
        zs = _silu(zz) * ng_ref[...]
        outs = []
        for hh in heads:
            q_c = q_s[hh, pl.ds(r0, L), :]
            v_c = v_s[hh, pl.ds(r0, L), :]
            rb = jnp.broadcast_to(b_s[hh, pl.ds(c, 1), :], (L, L))
            cb = rb.T
            d = cb - rb + jnp.broadcast_to(ig_s[hh, pl.ds(c, 1), :], (L, L))
            d = jnp.where(tri, d, -jnp.inf)
            inter = cb[:, :1] + mp_s[hh, pl.ds(c, 1), :][:, :1]
            m_j = jnp.maximum(inter, jnp.max(d, axis=1, keepdims=True))
            s_mat = jnp.dot(q_c, kt_s[hh, c], preferred_element_type=F32) * jnp.exp(d - m_j)
            s_int = jnp.exp(inter - m_j)
            nd = (jnp.dot(s_mat.astype(BF16), v_c, preferred_element_type=F32)
                  + s_int * jnp.dot(q_c, cp_s[hh, c], preferred_element_type=F32))
            den = nd[:, ML_DH:ML_DH + 1]
            inv = 1.0 / jnp.maximum(jnp.abs(den), jnp.exp(-m_j))
            hg = gate[:, hh * ML_DH:(hh + 1) * ML_DH] * (nd[:, :ML_DH] * inv)
            mu = jnp.mean(hg, axis=1, keepdims=True)
            dev = hg - mu
            var = jnp.mean(dev * dev, axis=1, keepdims=True)
            outs.append(dev * lax.rsqrt(var + LN_EPS) * zs[:, hh * ML_DH:(hh + 1) * ML_DH])
        y_ref[0, pl.ds(r0, L), :] = jnp.concatenate(outs, axis=1).astype(y_ref.dtype)
        return carry

    lax.fori_loop(0, nc, chunk, 0, unroll=2)


def _mlstm_branch(p, gates, gbias, cw, cb, dq, dkt, dv, ng):
    b, t, _ = p.shape
    nc = t // ML_CHUNK
    ub, ob, zb = C_ML_U // ML_PW, C_ML_O // ML_PW, C_ML_Z // ML_PW
    seq = lambda base: pl.BlockSpec((1, t, ML_PW), lambda bi, pr: (bi, 0, base + pr))
    vec = pl.BlockSpec((1, ML_PW), lambda bi, pr: (0, pr))
    rows = lambda: pltpu.VMEM((2, nc, ML_CHUNK), F32)
    return pl.pallas_call(
        _mlstm_kernel,
        grid=(b, ML_PAIRS),
        in_specs=[
            seq(ub), seq(ob), seq(zb),
            pl.BlockSpec((1, 2, nc, ML_CHUNK), lambda bi, pr: (bi, pr, 0, 0)),
            pl.BlockSpec((1, 2, nc, ML_CHUNK), lambda bi, pr: (bi, ML_PAIRS + pr, 0, 0)),
            pl.BlockSpec((2, 1, ML_CHUNK), lambda bi, pr: (pr, 0, 0)),
            pl.BlockSpec((2, 1, ML_CHUNK), lambda bi, pr: (ML_PAIRS + pr, 0, 0)),
            pl.BlockSpec((cw.shape[0], ML_PW), lambda bi, pr: (0, pr)),
            vec,
            pl.BlockSpec((1, 2, ML_PW, ML_DHP), lambda bi, pr: (pr, 0, 0, 0)),
            pl.BlockSpec((1, 2, ML_DHP, ML_PW), lambda bi, pr: (pr, 0, 0, 0)),
            pl.BlockSpec((1, 2, ML_PW, ML_DHP), lambda bi, pr: (pr, 0, 0, 0)),
            vec,
        ],
        out_specs=pl.BlockSpec((1, t, ML_PW), lambda bi, pr: (bi, 0, pr)),
        out_shape=jax.ShapeDtypeStruct((b, t, ML_W), BF16),
        scratch_shapes=[
            pltpu.VMEM((2, t, ML_DHP), BF16),
            pltpu.VMEM((2, nc, ML_DHP, ML_CHUNK), BF16),
            pltpu.VMEM((2, t, ML_DHP), BF16),
            pltpu.VMEM((2, nc, ML_DHP, ML_DHP), BF16),
            pltpu.VMEM((ML_DHP, ML_DHP), F32),
            rows(), rows(), rows(), rows(), rows(), rows(),
        ],
        compiler_params=pltpu.CompilerParams(
            dimension_semantics=("arbitrary", "arbitrary"), vmem_limit_bytes=_vmem_limit(56 << 20)),
        name="mlstm",
    )(p, p, p, gates, gates, gbias, gbias, cw, cb, dq, dkt, dv, ng)


def _moba_kernel(q_ref, k_ref, v_ref, z_ref, slope_ref, o_ref, l_s):
    t = q_ref.shape[1]
    nb = t // MOBA_BLOCK
    bs = MOBA_BLOCK
    qscale = (MO_DH ** -0.5) * LOG2E
    slope2 = slope_ref[0][:, :1] * LOG2E

    krow = lax.broadcasted_iota(jnp.int32, (LANE, MO_DH), 0)
    kmean = jnp.zeros((LANE, MO_DH), F32)
    for n in range(nb):
        mean_n = jnp.sum(k_ref[0, n * bs:(n + 1) * bs, :].astype(F32), axis=0, keepdims=True) * (1.0 / bs)
        kmean = jnp.where(krow == n, mean_n, kmean)
    kmean = kmean.astype(BF16)

    lane = lax.broadcasted_iota(jnp.int32, (bs, LANE), 1)
    causal = (lax.broadcasted_iota(jnp.int32, (bs, bs), 0) >= lax.broadcasted_iota(jnp.int32, (bs, bs), 1))
    kpos = lax.broadcasted_iota(jnp.int32, (1, bs), 1).astype(F32)

    for qb in range(nb):
        q_b = q_ref[0, qb * bs:(qb + 1) * bs, :]
        q_sc = (q_b.astype(F32) * qscale).astype(BF16)
        if qb > MOBA_TOPK:
            gate = lax.dot_general(q_b, kmean, NT_DIMS, preferred_element_type=F32)
            beaten = jnp.zeros((bs, LANE), F32)
            for m in range(qb):
                gm = gate[:, m:m + 1]
                wins = (gm > gate) | ((gm == gate) & (lane > m))
                beaten = beaten + jnp.where(wins, 1.0, 0.0)
            keep = jnp.where(beaten < float(MOBA_TOPK), 1.0, 0.0)
        m_run = None
        for n in range(qb + 1):
            s = lax.dot_general(q_sc, k_ref[0, n * bs:(n + 1) * bs, :], NT_DIMS, preferred_element_type=F32)
            logit = s + slope2 * (kpos + float((n - qb) * bs))
            if n == qb:
                logit = jnp.where(causal, logit, NEG_INF)
            elif qb > MOBA_TOPK:
                logit = jnp.where(keep[:, n:n + 1] > 0.5, logit, NEG_INF)
            l_s[:, n * bs:(n + 1) * bs] = logit
            mx = jnp.max(logit, axis=1, keepdims=True)
            m_run = mx if m_run is None else jnp.maximum(m_run, mx)
        nk = (qb + 1) * bs
        pr = jnp.exp2(l_s[:, :nk] - m_run)
        den = jnp.sum(pr, axis=1, keepdims=True)
        acc = jnp.dot(pr.astype(BF16), v_ref[0, :nk, :], preferred_element_type=F32)
        z = z_ref[0, qb * bs:(qb + 1) * bs, :].astype(F32)
        o_ref[0, qb * bs:(qb + 1) * bs, :] = (acc * ((1.0 / den) * _silu(z))).astype(o_ref.dtype)


def _moba_branch(p, slopes):
    b, t, _ = p.shape
    qb, kb, vb, zb = (c // MO_DH for c in (C_MO_Q, C_MO_K, C_MO_V, C_MO_Z))
    seq = lambda base: pl.BlockSpec((1, t, MO_DH), lambda bi, h: (bi, 0, base + h))
    return pl.pallas_call(
        _moba_kernel,
        grid=(b, MO_HEADS),
        in_specs=[seq(qb), seq(kb), seq(vb), seq(zb),
                  pl.BlockSpec((1, 1, LANE), lambda bi, h: (h, 0, 0))],
        out_specs=pl.BlockSpec((1, t, MO_DH), lambda bi, h: (bi, 0, h)),
        out_shape=jax.ShapeDtypeStruct((b, t, MO_W), BF16),
        scratch_shapes=[pltpu.VMEM((MOBA_BLOCK, t), F32)],
        compiler_params=pltpu.CompilerParams(
            dimension_semantics=("arbitrary", "arbitrary"), vmem_limit_bytes=_vmem_limit(32 << 20)),
        name="moba",
    )(p, p, p, p, slopes)


def _out_proj_kernel(x_ref, yl_ref, ym_ref, yo_ref, w_ref, o_ref):
    acc = jnp.dot(yl_ref[...], w_ref[0:LRU_W, :], preferred_element_type=F32)
    acc = acc + jnp.dot(ym_ref[...], w_ref[LRU_W:LRU_W + ML_W, :], preferred_element_type=F32)
    acc = acc + jnp.dot(yo_ref[...], w_ref[LRU_W + ML_W:, :], preferred_element_type=F32)
    o_ref[...] = x_ref[...] + acc


def _out_proj(x2d, yl, ym, yo, w):
    m = x2d.shape[0]
    kdim = w.shape[0]
    rows = lambda width: pl.BlockSpec((OUT_TM, width), lambda i: (i, 0))
    vmem = 4 * OUT_TM * D_MODEL * 4 + 2 * kdim * D_MODEL * 2 + 2 * OUT_TM * kdim * 2 + 2 * OUT_TM * D_MODEL * 4
    return pl.pallas_call(
        _out_proj_kernel,
        grid=(m // OUT_TM,),
        in_specs=[rows(D_MODEL), rows(LRU_W), rows(ML_W), rows(MO_W),
                  pl.BlockSpec((kdim, D_MODEL), lambda i: (0, 0))],
        out_specs=rows(D_MODEL),
        out_shape=jax.ShapeDtypeStruct((m, D_MODEL), F32),
        compiler_params=pltpu.CompilerParams(
            dimension_semantics=("arbitrary",), vmem_limit_bytes=_vmem_limit(vmem)),
        name="out_proj",
    )(x2d, yl, ym, yo, w)


def _mem_kv_kernel(mem_ref, g_ref, w_ref, o_ref):
    hm = _rmsnorm_rows(mem_ref[0], g_ref[...]).astype(BF16)
    o_ref[0] = jnp.dot(hm, w_ref[...], preferred_element_type=F32).astype(o_ref.dtype)


def _mem_kv(mem, g, wkv):
    b, m, _ = mem.shape
    return pl.pallas_call(
        _mem_kv_kernel,
        grid=(b,),
        in_specs=[pl.BlockSpec((1, m, D_MODEL), lambda bi: (bi, 0, 0)),
                  pl.BlockSpec((1, D_MODEL), lambda bi: (0, 0)),
                  pl.BlockSpec((D_MODEL, 2 * XA_W), lambda bi: (0, 0))],
        out_specs=pl.BlockSpec((1, m, 2 * XA_W), lambda bi: (bi, 0, 0)),
        out_shape=jax.ShapeDtypeStruct((b, m, 2 * XA_W), BF16),
        compiler_params=pltpu.CompilerParams(
            dimension_semantics=("arbitrary",), vmem_limit_bytes=_vmem_limit(32 << 20)),
        name="mem_kv",
    )(mem, g, wkv)


def _xattn_kernel(x_ref, g_ref, wq_ref, kv_ref, wo_ref, fg_ref, o_ref, *, final_norm):
    x = x_ref[0]
    hx = _rmsnorm_rows(x, g_ref[...]).astype(BF16)
    q = jnp.dot(hx, wq_ref[...], preferred_element_type=F32).astype(BF16)
    heads = []
    for h in range(XA_HEADS):
        q_h = q[:, h * XA_DH:(h + 1) * XA_DH]
        k_h = kv_ref[0, :, h * XA_DH:(h + 1) * XA_DH]
        v_h = kv_ref[0, :, XA_W + h * XA_DH:XA_W + (h + 1) * XA_DH]
        s = lax.dot_general(q_h, k_h, NT_DIMS, preferred_element_type=F32) * (XA_DH ** -0.5)
        e = jnp.exp(s - jnp.max(s, axis=1, keepdims=True))
        pr = e / jnp.sum(e, axis=1, keepdims=True)
        heads.append(jnp.dot(pr.astype(BF16), v_h, preferred_element_type=F32).astype(BF16))
    o = jnp.concatenate(heads, axis=1)
    y = x + jnp.dot(o, wo_ref[...], preferred_element_type=F32)
    if final_norm:
        y = _rmsnorm_rows(y, fg_ref[...])
    o_ref[0] = y


def _xattn(x, g, wq, kv, wo, fg, final_norm):
    b, t, _ = x.shape
    m = kv.shape[1]
    const = lambda shape: pl.BlockSpec(shape, lambda bi, i: (0,) * len(shape))
    return pl.pallas_call(
        functools.partial(_xattn_kernel, final_norm=final_norm),
        grid=(b, t // XA_TM),
        in_specs=[pl.BlockSpec((1, XA_TM, D_MODEL), lambda bi, i: (bi, i, 0)),
                  const((1, D_MODEL)),
                  const((D_MODEL, XA_W)),
                  pl.BlockSpec((1, m, 2 * XA_W), lambda bi, i: (bi, 0, 0)),
                  const((XA_W, D_MODEL)),
                  const((1, D_MODEL))],
        out_specs=pl.BlockSpec((1, XA_TM, D_MODEL), lambda bi, i: (bi, i, 0)),
        out_shape=jax.ShapeDtypeStruct((b, t, D_MODEL), F32),
        compiler_params=pltpu.CompilerParams(
            dimension_semantics=("arbitrary", "arbitrary"), vmem_limit_bytes=_vmem_limit(40 << 20)),
        name="mem_xattn",
    )(x, g, wq, kv, wo, fg)


def _dense_pairs(w):
    per_head = ML_DH // ML_QKV_BLOCK
    w = w.reshape(ML_HEADS, per_head, ML_QKV_BLOCK, ML_QKV_BLOCK)
    eye = jnp.eye(per_head, dtype=w.dtype)
    d = (w[:, :, :, None, :] * eye[None, :, None, :, None]).reshape(ML_HEADS, ML_DH, ML_DH)
    d = jnp.pad(d, ((0, 0), (0, 0), (0, ML_DHP - ML_DH))).reshape(ML_PAIRS, 2, ML_DH, ML_DHP)
    first = jnp.pad(d[:, 0], ((0, 0), (0, ML_DH), (0, 0)))
    second = jnp.pad(d[:, 1], ((0, 0), (ML_DH, 0), (0, 0)))
    return jnp.stack([first, second], axis=1)


def _alibi_slopes(n):
    def pow2(m):
        start = 2.0 ** (-8.0 / m)
        return [start ** (i + 1) for i in range(m)]
    if math.log2(n).is_integer():
        s = pow2(n)
    else:
        c = 2 ** int(math.floor(math.log2(n)))
        s = pow2(c) + pow2(2 * c)[0::2][:n - c]
    return np.asarray(s, dtype=np.float32)


def _layer_params(l, w_in, lru_conv_w, lru_conv_b, lru_wa, lru_ba, lru_wx, lru_bx, lru_lambda,
                  ml_conv_w, ml_conv_b, ml_wq, ml_wk, ml_wv, ml_bi, ml_bf, ml_norm_g, w_out):
    w = w_in[l]
    lru_end, ml_end, gate_end = 2 * LRU_W, 2 * LRU_W + 3 * ML_W, 2 * LRU_W + 3 * ML_W + 2 * ML_HEADS
    w_in_p = jnp.concatenate(
        [w[:, lru_end:ml_end], w[:, :lru_end], w[:, gate_end:], w[:, ml_end:gate_end],
         jnp.zeros((D_MODEL, IN_COLS_P - (C_GATE + 2 * ML_HEADS)), w.dtype)], axis=-1).astype(BF16)
    return dict(
        w_in=w_in_p,
        w_out=w_out[l].astype(BF16),
        lru_cw=lru_conv_w[l], lru_cb=lru_conv_b[l][None, :],
        lru_wax=jnp.concatenate([lru_wa[l], lru_wx[l]], axis=-1).astype(BF16),
        lru_ba=lru_ba[l][None, :], lru_bx=lru_bx[l][None, :], lru_lam=lru_lambda[l][None, :],
        ml_cw=ml_conv_w[l], ml_cb=ml_conv_b[l][None, :],
        ml_dq=_dense_pairs(ml_wq[l]).astype(BF16),
        ml_dkt=jnp.swapaxes(_dense_pairs(ml_wk[l]), 2, 3).astype(BF16),
        ml_dv=_dense_pairs(ml_wv[l]).astype(BF16),
        ml_gbias=jnp.broadcast_to(jnp.concatenate([ml_bi[l], ml_bf[l]])[:, None, None],
                                  (2 * ML_HEADS, 1, ML_CHUNK)),
        ml_ng=ml_norm_g[l][None, :],
    )


def kernel(x, mem, mix_norm_g, w_in, lru_conv_w, lru_conv_b, lru_wa, lru_ba, lru_wx, lru_bx, lru_lambda,
           ml_conv_w, ml_conv_b, ml_wq, ml_wk, ml_wv, ml_bi, ml_bf, ml_norm_g, w_out, xa_norm_g, mem_norm_g,
           xa_wq, xa_wkv, xa_wo, final_norm_g):
    b, t, d = x.shape
    depth = w_in.shape[0]
    nc = t // ML_CHUNK
    slopes = jnp.broadcast_to(jnp.asarray(_alibi_slopes(MO_HEADS))[:, None, None], (MO_HEADS, 1, LANE))
    for l in range(depth):
        lp = _layer_params(l, w_in, lru_conv_w, lru_conv_b, lru_wa, lru_ba, lru_wx, lru_bx, lru_lambda,
                           ml_conv_w, ml_conv_b, ml_wq, ml_wk, ml_wv, ml_bi, ml_bf, ml_norm_g, w_out)
        p = _in_proj(x.reshape(b * t, d), mix_norm_g[l][None, :], lp["w_in"]).reshape(b, t, IN_COLS_P)
        gates = p[:, :, C_GATE:C_GATE + 2 * ML_HEADS].astype(F32)
        gates = jnp.swapaxes(gates, 1, 2).reshape(b, 2 * ML_HEADS, nc, ML_CHUNK)
        y_lru = _lru_branch(p, lp["lru_cw"], lp["lru_cb"], lp["lru_wax"], lp["lru_ba"], lp["lru_bx"], lp["lru_lam"])
        y_ml = _mlstm_branch(p, gates, lp["ml_gbias"], lp["ml_cw"], lp["ml_cb"], lp["ml_dq"], lp["ml_dkt"],
                             lp["ml_dv"], lp["ml_ng"])
        y_mo = _moba_branch(p, slopes)
        x1 = _out_proj(x.reshape(b * t, d), y_lru.reshape(b * t, LRU_W), y_ml.reshape(b * t, ML_W),
                       y_mo.reshape(b * t, MO_W), lp["w_out"]).reshape(b, t, d)
        kv = _mem_kv(mem, mem_norm_g[l][None, :], xa_wkv[l].astype(BF16))
        x = _xattn(x1, xa_norm_g[l][None, :], xa_wq[l].astype(BF16), kv, xa_wo[l].astype(BF16),
                   final_norm_g[None, :], final_norm=(l == depth - 1))
    return x
```

```python
import functools
import math

import jax
import jax.numpy as jnp
import numpy as np
from jax import lax
from jax.experimental import pallas as pl
from jax.experimental.pallas import tpu as pltpu

LANE = 128
SUBLANE = 8
V7X_VMEM_BYTES = 64 * 1024 * 1024

D_MODEL = 2048
LRU_W = 512
LRU_BLOCKS = 4
LRU_BW = LRU_W // LRU_BLOCKS
LRU_C = 8.0
ML_W = 768
ML_HEADS = 4
ML_DH = 192
ML_DHP = 256
ML_PAIRS = 2
ML_PW = 2 * ML_DH
ML_QKV_BLOCK = 4
ML_QKV_SHIFT = 2
ML_CHUNK = 128
MO_W = 768
MO_HEADS = 6
MO_DH = 128
MOBA_BLOCK = 256
MOBA_TOPK = 3
XA_HEADS = 4
XA_DH = 128
XA_W = XA_HEADS * XA_DH
RMS_EPS = 1e-6
LN_EPS = 1e-5
NEG_INF = -1e30
LOG2E = 1.4426950408889634

REF_LRU_END = 2 * LRU_W
REF_ML_END = REF_LRU_END + 3 * ML_W
REF_GATE_END = REF_ML_END + 2 * ML_HEADS

C_ML_U = 0
C_ML_O = C_ML_U + ML_W
C_ML_Z = C_ML_O + ML_W
C_LRU_X = C_ML_Z + ML_W
C_LRU_Z = C_LRU_X + LRU_W
C_MO_Q = C_LRU_Z + LRU_W
C_MO_K = C_MO_Q + MO_W
C_MO_V = C_MO_K + MO_W
C_MO_Z = C_MO_V + MO_W
C_GATE = C_MO_Z + MO_W
IN_TM = 1024
IN_TN = 1664
IN_COLS_P = 6656
PREP_TR = 256
OUT_TM = 512
XA_TM = 512

BF16 = jnp.bfloat16
F32 = jnp.float32
NT_DIMS = (((1,), (1,)), ((), ()))


def _vmem_limit(nbytes):
    return int(min(V7X_VMEM_BYTES - (4 << 20), max(32 << 20, nbytes)))


def _rmsnorm_rows(x, g):
    ms = jnp.mean(x * x, axis=-1, keepdims=True)
    return x * lax.rsqrt(ms + RMS_EPS) * g


def _sigmoid(x):
    return jax.nn.sigmoid(x)


def _silu(x):
    return x * jax.nn.sigmoid(x)


def _softplus(x):
    return jnp.maximum(x, 0.0) + jnp.log1p(jnp.exp(-jnp.abs(x)))


def _shift_rows(x, s):
    rolled = pltpu.roll(x, s, axis=0)
    row = lax.broadcasted_iota(jnp.int32, x.shape, 0)
    return jnp.where(row >= s, rolled, 0.0)


def _causal_conv(x, w_ref, b_ref):
    k = w_ref.shape[0]
    acc = x * w_ref[k - 1:k, :]
    for j in range(k - 1):
        acc = acc + _shift_rows(x, k - 1 - j) * w_ref[j:j + 1, :]
    return acc + b_ref[...]


def _w_in_prep_kernel(w_ref, o_ref):
    n_in = w_ref.shape[2]
    o_ref[0, :, C_ML_U:C_LRU_X] = w_ref[0, :, REF_LRU_END:REF_ML_END].astype(BF16)
    o_ref[0, :, C_LRU_X:C_MO_Q] = w_ref[0, :, 0:REF_LRU_END].astype(BF16)
    tail = w_ref[0, :, REF_ML_END:n_in]
    o_ref[0, :, C_MO_Q:C_GATE] = tail[:, 2 * ML_HEADS:].astype(BF16)
    pad = IN_COLS_P - C_GATE
    lane = lax.broadcasted_iota(jnp.int32, (w_ref.shape[1], pad), 1)
    o_ref[0, :, C_GATE:IN_COLS_P] = jnp.where(lane < 2 * ML_HEADS, tail[:, :pad], 0.0).astype(BF16)


def _w_in_prep(w_in):
    depth, k, n = w_in.shape
    return pl.pallas_call(
        _w_in_prep_kernel,
        grid=(depth, k // PREP_TR),
        in_specs=[pl.BlockSpec((1, PREP_TR, n), lambda l, i: (l, i, 0))],
        out_specs=pl.BlockSpec((1, PREP_TR, IN_COLS_P), lambda l, i: (l, i, 0)),
        out_shape=jax.ShapeDtypeStruct((depth, k, IN_COLS_P), BF16),
        compiler_params=pltpu.CompilerParams(
            dimension_semantics=("arbitrary", "arbitrary"), vmem_limit_bytes=_vmem_limit(40 << 20)),
        name="w_in_prep",
    )(w_in)


def _in_proj_kernel(x_ref, g_ref, w_ref, o_ref, xn_ref):
    @pl.when(pl.program_id(1) == 0)
    def _():
        xn_ref[...] = _rmsnorm_rows(x_ref[...], g_ref[...]).astype(BF16)

    o_ref[...] = jnp.dot(xn_ref[...], w_ref[...], preferred_element_type=F32).astype(o_ref.dtype)


def _in_proj(x2d, g, w, l):
    m = x2d.shape[0]
    n = w.shape[2]
    vmem = (2 * IN_TM * D_MODEL * 4 + IN_TM * D_MODEL * 2 + 2 * D_MODEL * IN_TN * 2 + 2 * IN_TM * IN_TN * 2
            + 2 * IN_TM * IN_TN * 4)
    return pl.pallas_call(
        _in_proj_kernel,
        grid=(m // IN_TM, n // IN_TN),
        in_specs=[
            pl.BlockSpec((IN_TM, D_MODEL), lambda i, j: (i, 0)),
            pl.BlockSpec((None, 1, D_MODEL), lambda i, j: (l, 0, 0)),
            pl.BlockSpec((None, D_MODEL, IN_TN), lambda i, j: (l, 0, j)),
        ],
        out_specs=pl.BlockSpec((IN_TM, IN_TN), lambda i, j: (i, j)),
        out_shape=jax.ShapeDtypeStruct((m, n), BF16),
        scratch_shapes=[pltpu.VMEM((IN_TM, D_MODEL), BF16)],
        compiler_params=pltpu.CompilerParams(
            dimension_semantics=("arbitrary", "arbitrary"), vmem_limit_bytes=_vmem_limit(vmem)),
        name="in_proj",
    )(x2d, g, w)


def _lru_kernel(x_ref, z_ref, cw_ref, cb_ref, wax_ref, ba_ref, bx_ref, lam_ref, o_ref, a_s, u_s):
    t = x_ref.shape[1]
    x = x_ref[0].astype(F32)
    xc = _causal_conv(x, cw_ref, cb_ref)
    pre = jnp.dot(xc.astype(BF16), wax_ref[0], preferred_element_type=F32)
    r = _sigmoid(pre[:, :LRU_BW] + ba_ref[...])
    i = _sigmoid(pre[:, LRU_BW:] + bx_ref[...])
    log_a = (-LRU_C) * r * _softplus(-lam_ref[...])
    a = jnp.exp(log_a)
    a_s[...] = a
    u_s[...] = jnp.sqrt(-jnp.tanh(log_a) * (1.0 + a * a)) * (i * xc)

    row = lax.broadcasted_iota(jnp.int32, (SUBLANE, LRU_BW), 0)

    def block(blk, h_prev):
        r0 = pl.multiple_of(blk * SUBLANE, SUBLANE)
        a_b = a_s[pl.ds(r0, SUBLANE), :]
        u_b = u_s[pl.ds(r0, SUBLANE), :]
        for s in (1, 2, 4):
            a_sh = jnp.where(row >= s, pltpu.roll(a_b, s, axis=0), 1.0)
            u_sh = jnp.where(row >= s, pltpu.roll(u_b, s, axis=0), 0.0)
            u_b = a_b * u_sh + u_b
            a_b = a_b * a_sh
        h = a_b * h_prev + u_b
        u_s[pl.ds(r0, SUBLANE), :] = h
        return jnp.broadcast_to(h[SUBLANE - 1:SUBLANE, :], (SUBLANE, LRU_BW))

    lax.fori_loop(0, t // SUBLANE, block, jnp.zeros((SUBLANE, LRU_BW), F32), unroll=4)
    z = z_ref[0].astype(F32)
    o_ref[0] = (u_s[...] * _silu(z)).astype(o_ref.dtype)


def _lru_branch(p, cw, cb, wax, ba, bx, lam, l):
    b, t, _ = p.shape
    xb, zb = C_LRU_X // LRU_BW, C_LRU_Z // LRU_BW
    vec = pl.BlockSpec((None, 1, LRU_BW), lambda bi, g: (l, 0, g))
    return pl.pallas_call(
        _lru_kernel,
        grid=(b, LRU_BLOCKS),
        in_specs=[
            pl.BlockSpec((1, t, LRU_BW), lambda bi, g: (bi, 0, xb + g)),
            pl.BlockSpec((1, t, LRU_BW), lambda bi, g: (bi, 0, zb + g)),
            pl.BlockSpec((None, cw.shape[1], LRU_BW), lambda bi, g: (l, 0, g)),
            vec,
            pl.BlockSpec((None, 1, LRU_BW, 2 * LRU_BW), lambda bi, g: (l, g, 0, 0)),
            vec, vec, vec,
        ],
        out_specs=pl.BlockSpec((1, t, LRU_BW), lambda bi, g: (bi, 0, g)),
        out_shape=jax.ShapeDtypeStruct((b, t, LRU_W), BF16),
        scratch_shapes=[pltpu.VMEM((t, LRU_BW), F32), pltpu.VMEM((t, LRU_BW), F32)],
        compiler_params=pltpu.CompilerParams(
            dimension_semantics=("arbitrary", "arbitrary"), vmem_limit_bytes=_vmem_limit(24 * t * LRU_BW * 4)),
        name="rg_lru",
    )(p, p, cw, cb, wax, ba, bx, lam)


def _blockdiag_in_out(w_ref, hh):
    wh = w_ref[...][:, hh * ML_DH:(hh + 1) * ML_DH]
    wh = jnp.concatenate([wh, jnp.zeros((ML_QKV_BLOCK, ML_DHP - ML_DH), F32)], axis=1)
    r = lax.broadcasted_iota(jnp.int32, (ML_PW, ML_DHP), 0)
    c = lax.broadcasted_iota(jnp.int32, (ML_PW, ML_DHP), 1)
    d = jnp.zeros((ML_PW, ML_DHP), F32)
    for i in range(ML_QKV_BLOCK):
        d = jnp.where((r & (ML_QKV_BLOCK - 1)) == i, wh[i:i + 1, :], d)
    keep = (((r >> ML_QKV_SHIFT) - hh * (ML_DH // ML_QKV_BLOCK)) == (c >> ML_QKV_SHIFT)) & (c < ML_DH)
    return jnp.where(keep, d, 0.0)


def _blockdiag_out_in(w_ref, hh):
    wk = w_ref[...]
    o = lax.broadcasted_iota(jnp.int32, (ML_DHP, ML_PW), 0)
    r = lax.broadcasted_iota(jnp.int32, (ML_DHP, ML_PW), 1)
    d = jnp.zeros((ML_DHP, ML_PW), F32)
    for j in range(ML_QKV_BLOCK):
        d = jnp.where((o & (ML_QKV_BLOCK - 1)) == j, wk[j:j + 1, :], d)
    keep = ((o >> ML_QKV_SHIFT) == ((r >> ML_QKV_SHIFT) - hh * (ML_DH // ML_QKV_BLOCK))) & (o < ML_DH)
    return jnp.where(keep, d, 0.0)


def _mlstm_kernel(u_ref, og_ref, z_ref, ig_ref, fg_ref, bi_ref, bf_ref, cw_ref, cb_ref,
                  wq_ref, wkt_ref, wv_ref, ng_ref, y_ref,
                  dq_s, dkt_s, dv_s, q_s, kt_s, v_s, cp_s, c_s, ig_s, b_s, w_s, mp_s, so_s, sn_s):
    t = u_ref.shape[1]
    nc = t // ML_CHUNK
    L = ML_CHUNK
    heads = range(2)

    @pl.when(pl.program_id(1) == 0)
    def _():
        for hh in heads:
            dq_s[hh] = _blockdiag_in_out(wq_ref, hh).astype(BF16)
            dkt_s[hh] = _blockdiag_out_in(wkt_ref, hh).astype(BF16)
            dv_s[hh] = _blockdiag_in_out(wv_ref, hh).astype(BF16)

    u = u_ref[0]
    uc = _silu(_causal_conv(u.astype(F32), cw_ref, cb_ref)).astype(BF16)
    for hh in heads:
        q_s[hh] = jnp.dot(uc, dq_s[hh], preferred_element_type=F32).astype(BF16)
        kt = lax.dot_general(dkt_s[hh], uc, NT_DIMS, preferred_element_type=F32) * (ML_DH ** -0.5)
        for c in range(nc):
            kt_s[hh, c] = kt[:, c * L:(c + 1) * L].astype(BF16)
        v = jnp.dot(u, dv_s[hh], preferred_element_type=F32)
        vlane = lax.broadcasted_iota(jnp.int32, v.shape, 1)
        v_s[hh] = jnp.where(vlane == ML_DH, 1.0, v).astype(BF16)

    glane = lax.broadcasted_iota(jnp.int32, (nc, L), 1)
    grow = lax.broadcasted_iota(jnp.int32, (nc, L), 0)
    for hh in heads:
        ig = ig_ref[0, hh] + bi_ref[hh]
        lf = -_softplus(-(fg_ref[0, hh] + bf_ref[hh]))
        b = lf
        for k in range(int(math.log2(L))):
            sh = 1 << k
            b = b + jnp.where(glane >= sh, pltpu.roll(b, sh, axis=1), 0.0)
        g = jnp.broadcast_to(b[:, L - 1:L], (nc, L))
        a = g - b + ig
        mloc = jnp.broadcast_to(jnp.max(a, axis=1, keepdims=True), (nc, L))
        m = jnp.zeros((1, L), F32)
        m_prev = jnp.zeros((nc, L), F32)
        m_next = jnp.zeros((nc, L), F32)
        for c in range(nc):
            m_prev = jnp.where(grow == c, m, m_prev)
            m = jnp.maximum(g[c:c + 1, :] + m, mloc[c:c + 1, :])
            m_next = jnp.where(grow == c, m, m_next)
        ig_s[hh] = ig
        b_s[hh] = b
        w_s[hh] = jnp.exp(a - mloc)
        mp_s[hh] = m_prev
        so_s[hh] = jnp.exp(g + m_prev - m_next)
        sn_s[hh] = jnp.exp(mloc - m_next)

    for hh in heads:
        c_s[...] = jnp.zeros(c_s.shape, F32)

        def state(c, carry, hh=hh):
            r0 = pl.multiple_of(c * L, L)
            c_prev = c_s[...]
            cp_s[hh, c] = c_prev.astype(BF16)
            ktw = (kt_s[hh, c].astype(F32) * w_s[hh, pl.ds(c, 1), :]).astype(BF16)
            c_loc = jnp.dot(ktw, v_s[hh, pl.ds(r0, L), :], preferred_element_type=F32)
            c_s[...] = so_s[hh, pl.ds(c, 1), :][:, :1] * c_prev + sn_s[hh, pl.ds(c, 1), :][:, :1] * c_loc
            return carry

        lax.fori_loop(0, nc, state, 0, unroll=2)

    tri = (lax.broadcasted_iota(jnp.int32, (L, L), 0) >= lax.broadcasted_iota(jnp.int32, (L, L), 1))

    def chunk(c, carry):
        r0 = pl.multiple_of(c * L, L)
        og = og_ref[0, pl.ds(r0, L), :].astype(F32)
        zz = z_ref[0, pl.ds(r0, L), :].astype(F32)
        gate = _sigmoid(og)
        zs = _silu(zz) * ng_ref[...]
        outs = []
        for hh in heads:
            q_c = q_s[hh, pl.ds(r0, L), :]
            v_c = v_s[hh, pl.ds(r0, L), :]
            rb = jnp.broadcast_to(b_s[hh, pl.ds(c, 1), :], (L, L))
            cb = rb.T
            d = cb - rb + jnp.broadcast_to(ig_s[hh, pl.ds(c, 1), :], (L, L))
            d = jnp.where(tri, d, -jnp.inf)
            inter = cb[:, :1] + mp_s[hh, pl.ds(c, 1), :][:, :1]
            m_j = jnp.maximum(inter, jnp.max(d, axis=1, keepdims=True))
            s_mat = jnp.dot(q_c, kt_s[hh, c], preferred_element_type=F32) * jnp.exp(d - m_j)
            s_int = jnp.exp(inter - m_j)
            nd = (jnp.dot(s_mat.astype(BF16), v_c, preferred_element_type=F32)
                  + s_int * jnp.dot(q_c, cp_s[hh, c], preferred_element_type=F32))
            den = nd[:, ML_DH:ML_DH + 1]
            inv = 1.0 / jnp.maximum(jnp.abs(den), jnp.exp(-m_j))
            hg = gate[:, hh * ML_DH:(hh + 1) * ML_DH] * (nd[:, :ML_DH] * inv)
            mu = jnp.mean(hg, axis=1, keepdims=True)
            dev = hg - mu
            var = jnp.mean(dev * dev, axis=1, keepdims=True)
            outs.append(dev * lax.rsqrt(var + LN_EPS) * zs[:, hh * ML_DH:(hh + 1) * ML_DH])
        y_ref[0, pl.ds(r0, L), :] = jnp.concatenate(outs, axis=1).astype(y_ref.dtype)
        return carry

    lax.fori_loop(0, nc, chunk, 0, unroll=2)


def _mlstm_branch(p, gates, gbias, cw, cb, wq, wkt, wv, ng, l):
    b, t, _ = p.shape
    nc = t // ML_CHUNK
    ub, ob, zb = C_ML_U // ML_PW, C_ML_O // ML_PW, C_ML_Z // ML_PW
    seq = lambda base: pl.BlockSpec((1, t, ML_PW), lambda pr, bi: (bi, 0, base + pr))
    vec = pl.BlockSpec((None, 1, ML_PW), lambda pr, bi: (l, 0, pr))
    taps = pl.BlockSpec((None, ML_QKV_BLOCK, ML_PW), lambda pr, bi: (l, 0, pr))
    rows = lambda: pltpu.VMEM((2, nc, ML_CHUNK), F32)
    return pl.pallas_call(
        _mlstm_kernel,
        grid=(ML_PAIRS, b),
        in_specs=[
            seq(ub), seq(ob), seq(zb),
            pl.BlockSpec((1, 2, nc, ML_CHUNK), lambda pr, bi: (bi, pr, 0, 0)),
            pl.BlockSpec((1, 2, nc, ML_CHUNK), lambda pr, bi: (bi, ML_PAIRS + pr, 0, 0)),
            pl.BlockSpec((None, 2, 1, ML_CHUNK), lambda pr, bi: (l, pr, 0, 0)),
            pl.BlockSpec((None, 2, 1, ML_CHUNK), lambda pr, bi: (l, ML_PAIRS + pr, 0, 0)),
            pl.BlockSpec((None, cw.shape[1], ML_PW), lambda pr, bi: (l, 0, pr)),
            vec, taps, taps, taps, vec,
        ],
        out_specs=pl.BlockSpec((1, t, ML_PW), lambda pr, bi: (bi, 0, pr)),
        out_shape=jax.ShapeDtypeStruct((b, t, ML_W), BF16),
        scratch_shapes=[
            pltpu.VMEM((2, ML_PW, ML_DHP), BF16),
            pltpu.VMEM((2, ML_DHP, ML_PW), BF16),
            pltpu.VMEM((2, ML_PW, ML_DHP), BF16),
            pltpu.VMEM((2, t, ML_DHP), BF16),
            pltpu.VMEM((2, nc, ML_DHP, ML_CHUNK), BF16),
            pltpu.VMEM((2, t, ML_DHP), BF16),
            pltpu.VMEM((2, nc, ML_DHP, ML_DHP), BF16),
            pltpu.VMEM((ML_DHP, ML_DHP), F32),
            rows(), rows(), rows(), rows(), rows(), rows(),
        ],
        compiler_params=pltpu.CompilerParams(
            dimension_semantics=("arbitrary", "arbitrary"), vmem_limit_bytes=_vmem_limit(56 << 20)),
        name="mlstm",
    )(p, p, p, gates, gates, gbias, gbias, cw, cb, wq, wkt, wv, ng)


def _moba_kernel(q_ref, k_ref, v_ref, z_ref, slope_ref, o_ref, l_s):
    t = q_ref.shape[1]
    nb = t // MOBA_BLOCK
    bs = MOBA_BLOCK
    qscale = (MO_DH ** -0.5) * LOG2E
    slope2 = slope_ref[0][:, :1] * LOG2E

    krow = lax.broadcasted_iota(jnp.int32, (LANE, MO_DH), 0)
    kmean = jnp.zeros((LANE, MO_DH), F32)
    for n in range(nb):
        mean_n = jnp.sum(k_ref[0, n * bs:(n + 1) * bs, :].astype(F32), axis=0, keepdims=True) * (1.0 / bs)
        kmean = jnp.where(krow == n, mean_n, kmean)
    kmean = kmean.astype(BF16)

    lane = lax.broadcasted_iota(jnp.int32, (bs, LANE), 1)
    causal = (lax.broadcasted_iota(jnp.int32, (bs, bs), 0) >= lax.broadcasted_iota(jnp.int32, (bs, bs), 1))
    kpos = lax.broadcasted_iota(jnp.int32, (1, bs), 1).astype(F32)

    for qb in range(nb):
        q_b = q_ref[0, qb * bs:(qb + 1) * bs, :]
        q_sc = (q_b.astype(F32) * qscale).astype(BF16)
        if qb > MOBA_TOPK:
            gate = lax.dot_general(q_b, kmean, NT_DIMS, preferred_element_type=F32)
            beaten = jnp.zeros((bs, LANE), F32)
            for m in range(qb):
                gm = gate[:, m:m + 1]
                wins = (gm > gate) | ((gm == gate) & (lane > m))
                beaten = beaten + jnp.where(wins, 1.0, 0.0)
            keep = jnp.where(beaten < float(MOBA_TOPK), 1.0, 0.0)
        m_run = None
        for n in range(qb + 1):
            s = lax.dot_general(q_sc, k_ref[0, n * bs:(n + 1) * bs, :], NT_DIMS, preferred_element_type=F32)
            logit = s + slope2 * (kpos + float((n - qb) * bs))
            if n == qb:
                logit = jnp.where(causal, logit, NEG_INF)
            elif qb > MOBA_TOPK:
                logit = jnp.where(keep[:, n:n + 1] > 0.5, logit, NEG_INF)
            l_s[:, n * bs:(n + 1) * bs] = logit
            mx = jnp.max(logit, axis=1, keepdims=True)
            m_run = mx if m_run is None else jnp.maximum(m_run, mx)
        nk = (qb + 1) * bs
        pr = jnp.exp2(l_s[:, :nk] - m_run)
        den = jnp.sum(pr, axis=1, keepdims=True)
        acc = jnp.dot(pr.astype(BF16), v_ref[0, :nk, :], preferred_element_type=F32)
        z = z_ref[0, qb * bs:(qb + 1) * bs, :].astype(F32)
        o_ref[0, qb * bs:(qb + 1) * bs, :] = (acc * ((1.0 / den) * _silu(z))).astype(o_ref.dtype)


def _moba_branch(p, slopes):
    b, t, _ = p.shape
    qb, kb, vb, zb = (c // MO_DH for c in (C_MO_Q, C_MO_K, C_MO_V, C_MO_Z))
    seq = lambda base: pl.BlockSpec((1, t, MO_DH), lambda bi, h: (bi, 0, base + h))
    return pl.pallas_call(
        _moba_kernel,
        grid=(b, MO_HEADS),
        in_specs=[seq(qb), seq(kb), seq(vb), seq(zb),
                  pl.BlockSpec((1, 1, LANE), lambda bi, h: (h, 0, 0))],
        out_specs=pl.BlockSpec((1, t, MO_DH), lambda bi, h: (bi, 0, h)),
        out_shape=jax.ShapeDtypeStruct((b, t, MO_W), BF16),
        scratch_shapes=[pltpu.VMEM((MOBA_BLOCK, t), F32)],
        compiler_params=pltpu.CompilerParams(
            dimension_semantics=("arbitrary", "arbitrary"), vmem_limit_bytes=_vmem_limit(32 << 20)),
        name="moba",
    )(p, p, p, p, slopes)


def _out_proj_kernel(x_ref, yl_ref, ym_ref, yo_ref, w_ref, o_ref):
    acc = jnp.dot(yl_ref[...], w_ref[0:LRU_W, :], preferred_element_type=F32)
    acc = acc + jnp.dot(ym_ref[...], w_ref[LRU_W:LRU_W + ML_W, :], preferred_element_type=F32)
    acc = acc + jnp.dot(yo_ref[...], w_ref[LRU_W + ML_W:, :], preferred_element_type=F32)
    o_ref[...] = x_ref[...] + acc


def _out_proj(x2d, yl, ym, yo, w, l):
    m = x2d.shape[0]
    kdim = w.shape[1]
    rows = lambda width: pl.BlockSpec((OUT_TM, width), lambda i: (i, 0))
    vmem = 4 * OUT_TM * D_MODEL * 4 + 2 * kdim * D_MODEL * 2 + 2 * OUT_TM * kdim * 2 + 2 * OUT_TM * D_MODEL * 4
    return pl.pallas_call(
        _out_proj_kernel,
        grid=(m // OUT_TM,),
        in_specs=[rows(D_MODEL), rows(LRU_W), rows(ML_W), rows(MO_W),
                  pl.BlockSpec((None, kdim, D_MODEL), lambda i: (l, 0, 0))],
        out_specs=rows(D_MODEL),
        out_shape=jax.ShapeDtypeStruct((m, D_MODEL), F32),
        compiler_params=pltpu.CompilerParams(
            dimension_semantics=("arbitrary",), vmem_limit_bytes=_vmem_limit(vmem)),
        name="out_proj",
    )(x2d, yl, ym, yo, w)


def _mem_kv_kernel(mem_ref, g_ref, w_ref, o_ref):
    hm = _rmsnorm_rows(mem_ref[0], g_ref[...]).astype(BF16)
    o_ref[0] = jnp.dot(hm, w_ref[...], preferred_element_type=F32).astype(o_ref.dtype)


def _mem_kv(mem, g, wkv, l):
    b, m, _ = mem.shape
    return pl.pallas_call(
        _mem_kv_kernel,
        grid=(b,),
        in_specs=[pl.BlockSpec((1, m, D_MODEL), lambda bi: (bi, 0, 0)),
                  pl.BlockSpec((None, 1, D_MODEL), lambda bi: (l, 0, 0)),
                  pl.BlockSpec((None, D_MODEL, 2 * XA_W), lambda bi: (l, 0, 0))],
        out_specs=pl.BlockSpec((1, m, 2 * XA_W), lambda bi: (bi, 0, 0)),
        out_shape=jax.ShapeDtypeStruct((b, m, 2 * XA_W), BF16),
        compiler_params=pltpu.CompilerParams(
            dimension_semantics=("arbitrary",), vmem_limit_bytes=_vmem_limit(32 << 20)),
        name="mem_kv",
    )(mem, g, wkv)


def _xattn_kernel(x_ref, g_ref, wq_ref, kv_ref, wo_ref, fg_ref, o_ref, *, final_norm):
    x = x_ref[0]
    hx = _rmsnorm_rows(x, g_ref[...]).astype(BF16)
    q = jnp.dot(hx, wq_ref[...], preferred_element_type=F32).astype(BF16)
    heads = []
    for h in range(XA_HEADS):
        q_h = q[:, h * XA_DH:(h + 1) * XA_DH]
        k_h = kv_ref[0, :, h * XA_DH:(h + 1) * XA_DH]
        v_h = kv_ref[0, :, XA_W + h * XA_DH:XA_W + (h + 1) * XA_DH]
        s = lax.dot_general(q_h, k_h, NT_DIMS, preferred_element_type=F32) * (XA_DH ** -0.5)
        e = jnp.exp(s - jnp.max(s, axis=1, keepdims=True))
        pr = e / jnp.sum(e, axis=1, keepdims=True)
        heads.append(jnp.dot(pr.astype(BF16), v_h, preferred_element_type=F32).astype(BF16))
    o = jnp.concatenate(heads, axis=1)
    y = x + jnp.dot(o, wo_ref[...], preferred_element_type=F32)
    if final_norm:
        y = _rmsnorm_rows(y, fg_ref[...])
    o_ref[0] = y


def _xattn(x, g, wq, kv, wo, fg, l, final_norm):
    b, t, _ = x.shape
    m = kv.shape[1]
    return pl.pallas_call(
        functools.partial(_xattn_kernel, final_norm=final_norm),
        grid=(b, t // XA_TM),
        in_specs=[pl.BlockSpec((1, XA_TM, D_MODEL), lambda bi, i: (bi, i, 0)),
                  pl.BlockSpec((None, 1, D_MODEL), lambda bi, i: (l, 0, 0)),
                  pl.BlockSpec((None, D_MODEL, XA_W), lambda bi, i: (l, 0, 0)),
                  pl.BlockSpec((1, m, 2 * XA_W), lambda bi, i: (bi, 0, 0)),
                  pl.BlockSpec((None, XA_W, D_MODEL), lambda bi, i: (l, 0, 0)),
                  pl.BlockSpec((1, D_MODEL), lambda bi, i: (0, 0))],
        out_specs=pl.BlockSpec((1, XA_TM, D_MODEL), lambda bi, i: (bi, i, 0)),
        out_shape=jax.ShapeDtypeStruct((b, t, D_MODEL), F32),
        compiler_params=pltpu.CompilerParams(
            dimension_semantics=("arbitrary", "arbitrary"), vmem_limit_bytes=_vmem_limit(40 << 20)),
        name="mem_xattn",
    )(x, g, wq, kv, wo, fg)


def _alibi_slopes(n):
    def pow2(m):
        start = 2.0 ** (-8.0 / m)
        return [start ** (i + 1) for i in range(m)]
    if math.log2(n).is_integer():
        s = pow2(n)
    else:
        c = 2 ** int(math.floor(math.log2(n)))
        s = pow2(c) + pow2(2 * c)[0::2][:n - c]
    return np.asarray(s, dtype=np.float32)


def _taps(w, perm):
    depth = w.shape[0]
    return jnp.transpose(w, perm).reshape(depth, ML_QKV_BLOCK, ML_W)


def kernel(x, mem, mix_norm_g, w_in, lru_conv_w, lru_conv_b, lru_wa, lru_ba, lru_wx, lru_bx, lru_lambda,
           ml_conv_w, ml_conv_b, ml_wq, ml_wk, ml_wv, ml_bi, ml_bf, ml_norm_g, w_out, xa_norm_g, mem_norm_g,
           xa_wq, xa_wkv, xa_wo, final_norm_g):
    b, t, d = x.shape
    depth = w_in.shape[0]
    nc = t // ML_CHUNK
    row = lambda v: v[:, None, :]
    slopes = jnp.broadcast_to(jnp.asarray(_alibi_slopes(MO_HEADS))[:, None, None], (MO_HEADS, 1, LANE))
    w_in_p = _w_in_prep(w_in)
    w_out_b = w_out.astype(BF16)
    wq_b, wkv_b, wo_b = xa_wq.astype(BF16), xa_wkv.astype(BF16), xa_wo.astype(BF16)
    lru_wax = jnp.concatenate([lru_wa, lru_wx], axis=-1).astype(BF16)
    ml_wq_t = _taps(ml_wq, (0, 2, 1, 3))
    ml_wkt_t = _taps(ml_wk, (0, 3, 1, 2))
    ml_wv_t = _taps(ml_wv, (0, 2, 1, 3))
    ml_gbias = jnp.broadcast_to(jnp.concatenate([ml_bi, ml_bf], axis=1)[:, :, None, None],
                                (depth, 2 * ML_HEADS, 1, ML_CHUNK))
    mix_g, xa_g, mem_g = row(mix_norm_g), row(xa_norm_g), row(mem_norm_g)
    lru_cb, lru_ba_r, lru_bx_r, lru_lam = row(lru_conv_b), row(lru_ba), row(lru_bx), row(lru_lambda)
    ml_cb, ml_ng = row(ml_conv_b), row(ml_norm_g)
    for l in range(depth):
        p = _in_proj(x.reshape(b * t, d), mix_g, w_in_p, l).reshape(b, t, IN_COLS_P)
        gates = p[:, :, C_GATE:C_GATE + 2 * ML_HEADS].astype(F32)
        gates = jnp.swapaxes(gates, 1, 2).reshape(b, 2 * ML_HEADS, nc, ML_CHUNK)
        y_lru = _lru_branch(p, lru_conv_w, lru_cb, lru_wax, lru_ba_r, lru_bx_r, lru_lam, l)
        y_ml = _mlstm_branch(p, gates, ml_gbias, ml_conv_w, ml_cb, ml_wq_t, ml_wkt_t, ml_wv_t, ml_ng, l)
        y_mo = _moba_branch(p, slopes)
        x1 = _out_proj(x.reshape(b * t, d), y_lru.reshape(b * t, LRU_W), y_ml.reshape(b * t, ML_W),
                       y_mo.reshape(b * t, MO_W), w_out_b, l).reshape(b, t, d)
        kv = _mem_kv(mem, mem_g, wkv_b, l)
        x = _xattn(x1, xa_g, wq_b, kv, wo_b, final_norm_g[None, :], l, final_norm=(l == depth - 1))
    return x
```

```python
import functools
import math

import jax
import jax.numpy as jnp
import numpy as np
from jax import lax
from jax.experimental import pallas as pl
from jax.experimental.pallas import tpu as pltpu

LANE = 128
SUBLANE = 8
V7X_VMEM_BYTES = 64 * 1024 * 1024

D_MODEL = 2048
LRU_W = 512
LRU_BLOCKS = 4
LRU_BW = LRU_W // LRU_BLOCKS
LRU_C = 8.0
ML_W = 768
ML_HEADS = 4
ML_DH = 192
ML_DHP = 256
ML_PAIRS = 2
ML_PW = 2 * ML_DH
ML_QKV_BLOCK = 4
ML_QKV_SHIFT = 2
ML_CHUNK = 128
MO_W = 768
MO_HEADS = 6
MO_DH = 128
MOBA_BLOCK = 256
MOBA_TOPK = 3
XA_HEADS = 4
XA_DH = 128
XA_W = XA_HEADS * XA_DH
RMS_EPS = 1e-6
LN_EPS = 1e-5
NEG_INF = -1e30
LOG2E = 1.4426950408889634

REF_LRU_END = 2 * LRU_W
REF_ML_END = REF_LRU_END + 3 * ML_W
REF_GATE_END = REF_ML_END + 2 * ML_HEADS

C_ML_U = 0
C_ML_O = C_ML_U + ML_W
C_ML_Z = C_ML_O + ML_W
C_LRU_X = C_ML_Z + ML_W
C_LRU_Z = C_LRU_X + LRU_W
C_MO_Q = C_LRU_Z + LRU_W
C_MO_K = C_MO_Q + MO_W
C_MO_V = C_MO_K + MO_W
C_MO_Z = C_MO_V + MO_W
C_GATE = C_MO_Z + MO_W
IN_TM = 1024
IN_TN = 1664
IN_COLS_P = 6656
PREP_TR = 256
OUT_TM = 512
XA_TM = 512

BF16 = jnp.bfloat16
F32 = jnp.float32
NT_DIMS = (((1,), (1,)), ((), ()))


def _vmem_limit(nbytes):
    return int(min(V7X_VMEM_BYTES - (4 << 20), max(32 << 20, nbytes)))


def _rmsnorm_rows(x, g):
    ms = jnp.mean(x * x, axis=-1, keepdims=True)
    return x * lax.rsqrt(ms + RMS_EPS) * g


def _sigmoid(x):
    return jax.nn.sigmoid(x)


def _silu(x):
    return x * jax.nn.sigmoid(x)


def _softplus(x):
    return jnp.maximum(x, 0.0) + jnp.log1p(jnp.exp(-jnp.abs(x)))


def _shift_rows(x, s):
    rolled = pltpu.roll(x, s, axis=0)
    row = lax.broadcasted_iota(jnp.int32, x.shape, 0)
    return jnp.where(row >= s, rolled, 0.0)


def _causal_conv(x, w_ref, b_ref):
    k = w_ref.shape[0]
    acc = x * w_ref[k - 1:k, :]
    for j in range(k - 1):
        acc = acc + _shift_rows(x, k - 1 - j) * w_ref[j:j + 1, :]
    return acc + b_ref[...]


def _w_in_prep_kernel(w_ref, o_ref):
    last = pl.program_id(1) == pl.num_programs(1) - 1
    valid = jnp.where(last, 2 * ML_HEADS, PREP_TR)
    row = lax.broadcasted_iota(jnp.int32, w_ref.shape, 0)
    o_ref[...] = jnp.where(row < valid, w_ref[...], 0.0).astype(BF16)


def _w_in_prep_src_row(i):
    n_ml = (C_LRU_X - C_ML_U) // PREP_TR
    n_lru = (C_MO_Q - C_LRU_X) // PREP_TR
    n_mo = (C_GATE - C_MO_Q) // PREP_TR
    tile, unit = PREP_TR // SUBLANE, SUBLANE
    ml = REF_LRU_END // unit + i * tile
    lru = (i - n_ml) * tile
    mo = REF_GATE_END // unit + (i - n_ml - n_lru) * tile
    gate = REF_ML_END // unit
    return unit * jnp.where(i < n_ml, ml, jnp.where(i < n_ml + n_lru, lru, jnp.where(i < n_ml + n_lru + n_mo, mo, gate)))


def _w_in_prep(w_in_t):
    depth, _, k = w_in_t.shape
    return pl.pallas_call(
        _w_in_prep_kernel,
        grid=(depth, IN_COLS_P // PREP_TR),
        in_specs=[pl.BlockSpec((pl.Squeezed(), pl.Element(PREP_TR), pl.Element(k)),
                               lambda l, i: (l, _w_in_prep_src_row(i), 0))],
        out_specs=pl.BlockSpec((None, PREP_TR, k), lambda l, i: (l, i, 0)),
        out_shape=jax.ShapeDtypeStruct((depth, IN_COLS_P, k), BF16),
        compiler_params=pltpu.CompilerParams(
            dimension_semantics=("arbitrary", "arbitrary"), vmem_limit_bytes=_vmem_limit(32 << 20)),
        name="w_in_prep",
    )(w_in_t)


def _in_proj_kernel(x_ref, g_ref, w_ref, o_ref, xn_ref):
    @pl.when(pl.program_id(1) == 0)
    def _():
        xn_ref[...] = _rmsnorm_rows(x_ref[...], g_ref[...]).astype(BF16)

    o_ref[...] = lax.dot_general(xn_ref[...], w_ref[...], NT_DIMS, preferred_element_type=F32).astype(o_ref.dtype)


def _in_proj(x2d, g, w, l):
    m = x2d.shape[0]
    n = w.shape[1]
    vmem = (2 * IN_TM * D_MODEL * 4 + IN_TM * D_MODEL * 2 + 2 * D_MODEL * IN_TN * 2 + 2 * IN_TM * IN_TN * 2
            + 2 * IN_TM * IN_TN * 4)
    return pl.pallas_call(
        _in_proj_kernel,
        grid=(m // IN_TM, n // IN_TN),
        in_specs=[
            pl.BlockSpec((IN_TM, D_MODEL), lambda i, j: (i, 0)),
            pl.BlockSpec((None, 1, D_MODEL), lambda i, j: (l, 0, 0)),
            pl.BlockSpec((None, IN_TN, D_MODEL), lambda i, j: (l, j, 0)),
        ],
        out_specs=pl.BlockSpec((IN_TM, IN_TN), lambda i, j: (i, j)),
        out_shape=jax.ShapeDtypeStruct((m, n), BF16),
        scratch_shapes=[pltpu.VMEM((IN_TM, D_MODEL), BF16)],
        compiler_params=pltpu.CompilerParams(
            dimension_semantics=("arbitrary", "arbitrary"), vmem_limit_bytes=_vmem_limit(vmem)),
        name="in_proj",
    )(x2d, g, w)


def _lru_kernel(x_ref, z_ref, cw_ref, cb_ref, wax_ref, ba_ref, bx_ref, lam_ref, o_ref, a_s, u_s):
    t = x_ref.shape[1]
    x = x_ref[0].astype(F32)
    xc = _causal_conv(x, cw_ref, cb_ref)
    pre = jnp.dot(xc.astype(BF16), wax_ref[0], preferred_element_type=F32)
    r = _sigmoid(pre[:, :LRU_BW] + ba_ref[...])
    i = _sigmoid(pre[:, LRU_BW:] + bx_ref[...])
    log_a = (-LRU_C) * r * _softplus(-lam_ref[...])
    a = jnp.exp(log_a)
    a_s[...] = a
    u_s[...] = jnp.sqrt(-jnp.tanh(log_a) * (1.0 + a * a)) * (i * xc)

    row = lax.broadcasted_iota(jnp.int32, (SUBLANE, LRU_BW), 0)

    def block(blk, h_prev):
        r0 = pl.multiple_of(blk * SUBLANE, SUBLANE)
        a_b = a_s[pl.ds(r0, SUBLANE), :]
        u_b = u_s[pl.ds(r0, SUBLANE), :]
        for s in (1, 2, 4):
            a_sh = jnp.where(row >= s, pltpu.roll(a_b, s, axis=0), 1.0)
            u_sh = jnp.where(row >= s, pltpu.roll(u_b, s, axis=0), 0.0)
            u_b = a_b * u_sh + u_b
            a_b = a_b * a_sh
        h = a_b * h_prev + u_b
        u_s[pl.ds(r0, SUBLANE), :] = h
        return jnp.broadcast_to(h[SUBLANE - 1:SUBLANE, :], (SUBLANE, LRU_BW))

    lax.fori_loop(0, t // SUBLANE, block, jnp.zeros((SUBLANE, LRU_BW), F32), unroll=4)
    z = z_ref[0].astype(F32)
    o_ref[0] = (u_s[...] * _silu(z)).astype(o_ref.dtype)


def _lru_branch(p, cw, cb, wax, ba, bx, lam, l):
    b, t, _ = p.shape
    xb, zb = C_LRU_X // LRU_BW, C_LRU_Z // LRU_BW
    vec = pl.BlockSpec((None, 1, LRU_BW), lambda bi, g: (l, 0, g))
    return pl.pallas_call(
        _lru_kernel,
        grid=(b, LRU_BLOCKS),
        in_specs=[
            pl.BlockSpec((1, t, LRU_BW), lambda bi, g: (bi, 0, xb + g)),
            pl.BlockSpec((1, t, LRU_BW), lambda bi, g: (bi, 0, zb + g)),
            pl.BlockSpec((None, cw.shape[1], LRU_BW), lambda bi, g: (l, 0, g)),
            vec,
            pl.BlockSpec((None, 1, LRU_BW, 2 * LRU_BW), lambda bi, g: (l, g, 0, 0)),
            vec, vec, vec,
        ],
        out_specs=pl.BlockSpec((1, t, LRU_BW), lambda bi, g: (bi, 0, g)),
        out_shape=jax.ShapeDtypeStruct((b, t, LRU_W), BF16),
        scratch_shapes=[pltpu.VMEM((t, LRU_BW), F32), pltpu.VMEM((t, LRU_BW), F32)],
        compiler_params=pltpu.CompilerParams(
            dimension_semantics=("arbitrary", "arbitrary"), vmem_limit_bytes=_vmem_limit(24 * t * LRU_BW * 4)),
        name="rg_lru",
    )(p, p, cw, cb, wax, ba, bx, lam)


def _blockdiag_in_out(w_ref, hh):
    wh = w_ref[...][:, hh * ML_DH:(hh + 1) * ML_DH]
    wh = jnp.concatenate([wh, jnp.zeros((ML_QKV_BLOCK, ML_DHP - ML_DH), F32)], axis=1)
    r = lax.broadcasted_iota(jnp.int32, (ML_PW, ML_DHP), 0)
    c = lax.broadcasted_iota(jnp.int32, (ML_PW, ML_DHP), 1)
    d = jnp.zeros((ML_PW, ML_DHP), F32)
    for i in range(ML_QKV_BLOCK):
        d = jnp.where((r & (ML_QKV_BLOCK - 1)) == i, wh[i:i + 1, :], d)
    keep = (((r >> ML_QKV_SHIFT) - hh * (ML_DH // ML_QKV_BLOCK)) == (c >> ML_QKV_SHIFT)) & (c < ML_DH)
    return jnp.where(keep, d, 0.0)


def _blockdiag_out_in(w_ref, hh):
    wk = w_ref[...]
    o = lax.broadcasted_iota(jnp.int32, (ML_DHP, ML_PW), 0)
    r = lax.broadcasted_iota(jnp.int32, (ML_DHP, ML_PW), 1)
    d = jnp.zeros((ML_DHP, ML_PW), F32)
    for j in range(ML_QKV_BLOCK):
        d = jnp.where((o & (ML_QKV_BLOCK - 1)) == j, wk[j:j + 1, :], d)
    keep = ((o >> ML_QKV_SHIFT) == ((r >> ML_QKV_SHIFT) - hh * (ML_DH // ML_QKV_BLOCK))) & (o < ML_DH)
    return jnp.where(keep, d, 0.0)


def _mlstm_kernel(u_ref, og_ref, z_ref, ig_ref, fg_ref, bi_ref, bf_ref, cw_ref, cb_ref,
                  wq_ref, wkt_ref, wv_ref, ng_ref, y_ref,
                  dq_s, dkt_s, dv_s, q_s, kt_s, v_s, cp_s, c_s, ig_s, b_s, w_s, mp_s, so_s, sn_s):
    t = u_ref.shape[1]
    nc = t // ML_CHUNK
    L = ML_CHUNK
    heads = range(2)

    @pl.when(pl.program_id(1) == 0)
    def _():
        for hh in heads:
            dq_s[hh] = _blockdiag_in_out(wq_ref, hh).astype(BF16)
            dkt_s[hh] = _blockdiag_out_in(wkt_ref, hh).astype(BF16)
            dv_s[hh] = _blockdiag_in_out(wv_ref, hh).astype(BF16)

    u = u_ref[0]
    uc = _silu(_causal_conv(u.astype(F32), cw_ref, cb_ref)).astype(BF16)
    for hh in heads:
        q_s[hh] = jnp.dot(uc, dq_s[hh], preferred_element_type=F32).astype(BF16)
        kt = lax.dot_general(dkt_s[hh], uc, NT_DIMS, preferred_element_type=F32) * (ML_DH ** -0.5)
        for c in range(nc):
            kt_s[hh, c] = kt[:, c * L:(c + 1) * L].astype(BF16)
        v = jnp.dot(u, dv_s[hh], preferred_element_type=F32)
        vlane = lax.broadcasted_iota(jnp.int32, v.shape, 1)
        v_s[hh] = jnp.where(vlane == ML_DH, 1.0, v).astype(BF16)

    glane = lax.broadcasted_iota(jnp.int32, (nc, L), 1)
    grow = lax.broadcasted_iota(jnp.int32, (nc, L), 0)
    for hh in heads:
        ig = ig_ref[0, hh] + bi_ref[hh]
        lf = -_softplus(-(fg_ref[0, hh] + bf_ref[hh]))
        b = lf
        for k in range(int(math.log2(L))):
            sh = 1 << k
            b = b + jnp.where(glane >= sh, pltpu.roll(b, sh, axis=1), 0.0)
        g = jnp.broadcast_to(b[:, L - 1:L], (nc, L))
        a = g - b + ig
        mloc = jnp.broadcast_to(jnp.max(a, axis=1, keepdims=True), (nc, L))
        m = jnp.zeros((1, L), F32)
        m_prev = jnp.zeros((nc, L), F32)
        m_next = jnp.zeros((nc, L), F32)
        for c in range(nc):
            m_prev = jnp.where(grow == c, m, m_prev)
            m = jnp.maximum(g[c:c + 1, :] + m, mloc[c:c + 1, :])
            m_next = jnp.where(grow == c, m, m_next)
        ig_s[hh] = ig
        b_s[hh] = b
        w_s[hh] = jnp.exp(a - mloc)
        mp_s[hh] = m_prev
        so_s[hh] = jnp.exp(g + m_prev - m_next)
        sn_s[hh] = jnp.exp(mloc - m_next)

    for hh in heads:
        c_s[...] = jnp.zeros(c_s.shape, F32)

        def state(c, carry, hh=hh):
            r0 = pl.multiple_of(c * L, L)
            c_prev = c_s[...]
            cp_s[hh, c] = c_prev.astype(BF16)
            ktw = (kt_s[hh, c].astype(F32) * w_s[hh, pl.ds(c, 1), :]).astype(BF16)
            c_loc = jnp.dot(ktw, v_s[hh, pl.ds(r0, L), :], preferred_element_type=F32)
            c_s[...] = so_s[hh, pl.ds(c, 1), :][:, :1] * c_prev + sn_s[hh, pl.ds(c, 1), :][:, :1] * c_loc
            return carry

        lax.fori_loop(0, nc, state, 0, unroll=2)

    tri = (lax.broadcasted_iota(jnp.int32, (L, L), 0) >= lax.broadcasted_iota(jnp.int32, (L, L), 1))

    def chunk(c, carry):
        r0 = pl.multiple_of(c * L, L)
        og = og_ref[0, pl.ds(r0, L), :].astype(F32)
        zz = z_ref[0, pl.ds(r0, L), :].astype(F32)
        gate = _sigmoid(og)
        zs = _silu(zz) * ng_ref[...]
        outs = []
        for hh in heads:
            q_c = q_s[hh, pl.ds(r0, L), :]
            v_c = v_s[hh, pl.ds(r0, L), :]
            rb = jnp.broadcast_to(b_s[hh, pl.ds(c, 1), :], (L, L))
            cb = rb.T
            d = cb - rb + jnp.broadcast_to(ig_s[hh, pl.ds(c, 1), :], (L, L))
            d = jnp.where(tri, d, -jnp.inf)
            inter = cb[:, :1] + mp_s[hh, pl.ds(c, 1), :][:, :1]
            m_j = jnp.maximum(inter, jnp.max(d, axis=1, keepdims=True))
            s_mat = jnp.dot(q_c, kt_s[hh, c], preferred_element_type=F32) * jnp.exp(d - m_j)
            s_int = jnp.exp(inter - m_j)
            nd = (jnp.dot(s_mat.astype(BF16), v_c, preferred_element_type=F32)
                  + s_int * jnp.dot(q_c, cp_s[hh, c], preferred_element_type=F32))
            den = nd[:, ML_DH:ML_DH + 1]
            inv = 1.0 / jnp.maximum(jnp.abs(den), jnp.exp(-m_j))
            hg = gate[:, hh * ML_DH:(hh + 1) * ML_DH] * (nd[:, :ML_DH] * inv)
            mu = jnp.mean(hg, axis=1, keepdims=True)
            dev = hg - mu
            var = jnp.mean(dev * dev, axis=1, keepdims=True)
            outs.append(dev * lax.rsqrt(var + LN_EPS) * zs[:, hh * ML_DH:(hh + 1) * ML_DH])
        y_ref[0, pl.ds(r0, L), :] = jnp.concatenate(outs, axis=1).astype(y_ref.dtype)
        return carry

    lax.fori_loop(0, nc, chunk, 0, unroll=2)


def _mlstm_branch(p, gates, gbias, cw, cb, wq, wkt, wv, ng, l):
    b, t, _ = p.shape
    nc = t // ML_CHUNK
    ub, ob, zb = C_ML_U // ML_PW, C_ML_O // ML_PW, C_ML_Z // ML_PW
    seq = lambda base: pl.BlockSpec((1, t, ML_PW), lambda pr, bi: (bi, 0, base + pr))
    vec = pl.BlockSpec((None, 1, ML_PW), lambda pr, bi: (l, 0, pr))
    taps = pl.BlockSpec((None, ML_QKV_BLOCK, ML_PW), lambda pr, bi: (l, 0, pr))
    rows = lambda: pltpu.VMEM((2, nc, ML_CHUNK), F32)
    return pl.pallas_call(
        _mlstm_kernel,
        grid=(ML_PAIRS, b),
        in_specs=[
            seq(ub), seq(ob), seq(zb),
            pl.BlockSpec((1, 2, nc, ML_CHUNK), lambda pr, bi: (bi, pr, 0, 0)),
            pl.BlockSpec((1, 2, nc, ML_CHUNK), lambda pr, bi: (bi, ML_PAIRS + pr, 0, 0)),
            pl.BlockSpec((None, 2, 1, ML_CHUNK), lambda pr, bi: (l, pr, 0, 0)),
            pl.BlockSpec((None, 2, 1, ML_CHUNK), lambda pr, bi: (l, ML_PAIRS + pr, 0, 0)),
            pl.BlockSpec((None, cw.shape[1], ML_PW), lambda pr, bi: (l, 0, pr)),
            vec, taps, taps, taps, vec,
        ],
        out_specs=pl.BlockSpec((1, t, ML_PW), lambda pr, bi: (bi, 0, pr)),
        out_shape=jax.ShapeDtypeStruct((b, t, ML_W), BF16),
        scratch_shapes=[
            pltpu.VMEM((2, ML_PW, ML_DHP), BF16),
            pltpu.VMEM((2, ML_DHP, ML_PW), BF16),
            pltpu.VMEM((2, ML_PW, ML_DHP), BF16),
            pltpu.VMEM((2, t, ML_DHP), BF16),
            pltpu.VMEM((2, nc, ML_DHP, ML_CHUNK), BF16),
            pltpu.VMEM((2, t, ML_DHP), BF16),
            pltpu.VMEM((2, nc, ML_DHP, ML_DHP), BF16),
            pltpu.VMEM((ML_DHP, ML_DHP), F32),
            rows(), rows(), rows(), rows(), rows(), rows(),
        ],
        compiler_params=pltpu.CompilerParams(
            dimension_semantics=("arbitrary", "arbitrary"), vmem_limit_bytes=_vmem_limit(56 << 20)),
        name="mlstm",
    )(p, p, p, gates, gates, gbias, gbias, cw, cb, wq, wkt, wv, ng)


def _moba_kernel(q_ref, k_ref, v_ref, z_ref, slope_ref, o_ref, l_s):
    t = q_ref.shape[1]
    nb = t // MOBA_BLOCK
    bs = MOBA_BLOCK
    qscale = (MO_DH ** -0.5) * LOG2E
    slope2 = slope_ref[0][:, :1] * LOG2E

    krow = lax.broadcasted_iota(jnp.int32, (LANE, MO_DH), 0)
    kmean = jnp.zeros((LANE, MO_DH), F32)
    for n in range(nb):
        mean_n = jnp.sum(k_ref[0, n * bs:(n + 1) * bs, :].astype(F32), axis=0, keepdims=True) * (1.0 / bs)
        kmean = jnp.where(krow == n, mean_n, kmean)
    kmean = kmean.astype(BF16)

    lane = lax.broadcasted_iota(jnp.int32, (bs, LANE), 1)
    causal = (lax.broadcasted_iota(jnp.int32, (bs, bs), 0) >= lax.broadcasted_iota(jnp.int32, (bs, bs), 1))
    kpos = lax.broadcasted_iota(jnp.int32, (1, bs), 1).astype(F32)

    for qb in range(nb):
        q_b = q_ref[0, qb * bs:(qb + 1) * bs, :]
        q_sc = (q_b.astype(F32) * qscale).astype(BF16)
        if qb > MOBA_TOPK:
            gate = lax.dot_general(q_b, kmean, NT_DIMS, preferred_element_type=F32)
            beaten = jnp.zeros((bs, LANE), F32)
            for m in range(qb):
                gm = gate[:, m:m + 1]
                wins = (gm > gate) | ((gm == gate) & (lane > m))
                beaten = beaten + jnp.where(wins, 1.0, 0.0)
            keep = jnp.where(beaten < float(MOBA_TOPK), 1.0, 0.0)
        m_run = None
        for n in range(qb + 1):
            s = lax.dot_general(q_sc, k_ref[0, n * bs:(n + 1) * bs, :], NT_DIMS, preferred_element_type=F32)
            logit = s + slope2 * (kpos + float((n - qb) * bs))
            if n == qb:
                logit = jnp.where(causal, logit, NEG_INF)
            elif qb > MOBA_TOPK:
                logit = jnp.where(keep[:, n:n + 1] > 0.5, logit, NEG_INF)
            l_s[:, n * bs:(n + 1) * bs] = logit
            mx = jnp.max(logit, axis=1, keepdims=True)
            m_run = mx if m_run is None else jnp.maximum(m_run, mx)
        nk = (qb + 1) * bs
        pr = jnp.exp2(l_s[:, :nk] - m_run)
        den = jnp.sum(pr, axis=1, keepdims=True)
        acc = jnp.dot(pr.astype(BF16), v_ref[0, :nk, :], preferred_element_type=F32)
        z = z_ref[0, qb * bs:(qb + 1) * bs, :].astype(F32)
        o_ref[0, qb * bs:(qb + 1) * bs, :] = (acc * ((1.0 / den) * _silu(z))).astype(o_ref.dtype)


def _moba_branch(p, slopes):
    b, t, _ = p.shape
    qb, kb, vb, zb = (c // MO_DH for c in (C_MO_Q, C_MO_K, C_MO_V, C_MO_Z))
    seq = lambda base: pl.BlockSpec((1, t, MO_DH), lambda bi, h: (bi, 0, base + h))
    return pl.pallas_call(
        _moba_kernel,
        grid=(b, MO_HEADS),
        in_specs=[seq(qb), seq(kb), seq(vb), seq(zb),
                  pl.BlockSpec((1, 1, LANE), lambda bi, h: (h, 0, 0))],
        out_specs=pl.BlockSpec((1, t, MO_DH), lambda bi, h: (bi, 0, h)),
        out_shape=jax.ShapeDtypeStruct((b, t, MO_W), BF16),
        scratch_shapes=[pltpu.VMEM((MOBA_BLOCK, t), F32)],
        compiler_params=pltpu.CompilerParams(
            dimension_semantics=("arbitrary", "arbitrary"), vmem_limit_bytes=_vmem_limit(32 << 20)),
        name="moba",
    )(p, p, p, p, slopes)


def _out_proj_kernel(x_ref, yl_ref, ym_ref, yo_ref, w_ref, o_ref):
    acc = jnp.dot(yl_ref[...], w_ref[0:LRU_W, :], preferred_element_type=F32)
    acc = acc + jnp.dot(ym_ref[...], w_ref[LRU_W:LRU_W + ML_W, :], preferred_element_type=F32)
    acc = acc + jnp.dot(yo_ref[...], w_ref[LRU_W + ML_W:, :], preferred_element_type=F32)
    o_ref[...] = x_ref[...] + acc


def _out_proj(x2d, yl, ym, yo, w, l):
    m = x2d.shape[0]
    kdim = w.shape[1]
    rows = lambda width: pl.BlockSpec((OUT_TM, width), lambda i: (i, 0))
    vmem = 4 * OUT_TM * D_MODEL * 4 + 2 * kdim * D_MODEL * 2 + 2 * OUT_TM * kdim * 2 + 2 * OUT_TM * D_MODEL * 4
    return pl.pallas_call(
        _out_proj_kernel,
        grid=(m // OUT_TM,),
        in_specs=[rows(D_MODEL), rows(LRU_W), rows(ML_W), rows(MO_W),
                  pl.BlockSpec((None, kdim, D_MODEL), lambda i: (l, 0, 0))],
        out_specs=rows(D_MODEL),
        out_shape=jax.ShapeDtypeStruct((m, D_MODEL), F32),
        compiler_params=pltpu.CompilerParams(
            dimension_semantics=("arbitrary",), vmem_limit_bytes=_vmem_limit(vmem)),
        name="out_proj",
    )(x2d, yl, ym, yo, w)


def _mem_kv_kernel(mem_ref, g_ref, w_ref, o_ref):
    hm = _rmsnorm_rows(mem_ref[0], g_ref[...]).astype(BF16)
    o_ref[0] = jnp.dot(hm, w_ref[...], preferred_element_type=F32).astype(o_ref.dtype)


def _mem_kv(mem, g, wkv, l):
    b, m, _ = mem.shape
    return pl.pallas_call(
        _mem_kv_kernel,
        grid=(b,),
        in_specs=[pl.BlockSpec((1, m, D_MODEL), lambda bi: (bi, 0, 0)),
                  pl.BlockSpec((None, 1, D_MODEL), lambda bi: (l, 0, 0)),
                  pl.BlockSpec((None, D_MODEL, 2 * XA_W), lambda bi: (l, 0, 0))],
        out_specs=pl.BlockSpec((1, m, 2 * XA_W), lambda bi: (bi, 0, 0)),
        out_shape=jax.ShapeDtypeStruct((b, m, 2 * XA_W), BF16),
        compiler_params=pltpu.CompilerParams(
            dimension_semantics=("arbitrary",), vmem_limit_bytes=_vmem_limit(32 << 20)),
        name="mem_kv",
    )(mem, g, wkv)


def _xattn_kernel(x_ref, g_ref, wq_ref, kv_ref, wo_ref, fg_ref, o_ref, *, final_norm):
    x = x_ref[0]
    hx = _rmsnorm_rows(x, g_ref[...]).astype(BF16)
    q = jnp.dot(hx, wq_ref[...], preferred_element_type=F32).astype(BF16)
    heads = []
    for h in range(XA_HEADS):
        q_h = q[:, h * XA_DH:(h + 1) * XA_DH]
        k_h = kv_ref[0, :, h * XA_DH:(h + 1) * XA_DH]
        v_h = kv_ref[0, :, XA_W + h * XA_DH:XA_W + (h + 1) * XA_DH]
        s = lax.dot_general(q_h, k_h, NT_DIMS, preferred_element_type=F32) * (XA_DH ** -0.5)
        e = jnp.exp(s - jnp.max(s, axis=1, keepdims=True))
        pr = e / jnp.sum(e, axis=1, keepdims=True)
        heads.append(jnp.dot(pr.astype(BF16), v_h, preferred_element_type=F32).astype(BF16))
    o = jnp.concatenate(heads, axis=1)
    y = x + jnp.dot(o, wo_ref[...], preferred_element_type=F32)
    if final_norm:
        y = _rmsnorm_rows(y, fg_ref[...])
    o_ref[0] = y


def _xattn(x, g, wq, kv, wo, fg, l, final_norm):
    b, t, _ = x.shape
    m = kv.shape[1]
    return pl.pallas_call(
        functools.partial(_xattn_kernel, final_norm=final_norm),
        grid=(b, t // XA_TM),
        in_specs=[pl.BlockSpec((1, XA_TM, D_MODEL), lambda bi, i: (bi, i, 0)),
                  pl.BlockSpec((None, 1, D_MODEL), lambda bi, i: (l, 0, 0)),
                  pl.BlockSpec((None, D_MODEL, XA_W), lambda bi, i: (l, 0, 0)),
                  pl.BlockSpec((1, m, 2 * XA_W), lambda bi, i: (bi, 0, 0)),
                  pl.BlockSpec((None, XA_W, D_MODEL), lambda bi, i: (l, 0, 0)),
                  pl.BlockSpec((1, D_MODEL), lambda bi, i: (0, 0))],
        out_specs=pl.BlockSpec((1, XA_TM, D_MODEL), lambda bi, i: (bi, i, 0)),
        out_shape=jax.ShapeDtypeStruct((b, t, D_MODEL), F32),
        compiler_params=pltpu.CompilerParams(
            dimension_semantics=("arbitrary", "arbitrary"), vmem_limit_bytes=_vmem_limit(40 << 20)),
        name="mem_xattn",
    )(x, g, wq, kv, wo, fg)


def _alibi_slopes(n):
    def pow2(m):
        start = 2.0 ** (-8.0 / m)
        return [start ** (i + 1) for i in range(m)]
    if math.log2(n).is_integer():
        s = pow2(n)
    else:
        c = 2 ** int(math.floor(math.log2(n)))
        s = pow2(c) + pow2(2 * c)[0::2][:n - c]
    return np.asarray(s, dtype=np.float32)


def _taps(w, perm):
    depth = w.shape[0]
    return jnp.transpose(w, perm).reshape(depth, ML_QKV_BLOCK, ML_W)


def kernel(x, mem, mix_norm_g, w_in, lru_conv_w, lru_conv_b, lru_wa, lru_ba, lru_wx, lru_bx, lru_lambda,
           ml_conv_w, ml_conv_b, ml_wq, ml_wk, ml_wv, ml_bi, ml_bf, ml_norm_g, w_out, xa_norm_g, mem_norm_g,
           xa_wq, xa_wkv, xa_wo, final_norm_g):
    b, t, d = x.shape
    depth = w_in.shape[0]
    nc = t // ML_CHUNK
    row = lambda v: v[:, None, :]
    slopes = jnp.broadcast_to(jnp.asarray(_alibi_slopes(MO_HEADS))[:, None, None], (MO_HEADS, 1, LANE))
    w_in_p = _w_in_prep(jnp.swapaxes(w_in, 1, 2))
    w_out_b = w_out.astype(BF16)
    wq_b, wkv_b, wo_b = xa_wq.astype(BF16), xa_wkv.astype(BF16), xa_wo.astype(BF16)
    lru_wax = jnp.concatenate([lru_wa, lru_wx], axis=-1).astype(BF16)
    ml_wq_t = _taps(ml_wq, (0, 2, 1, 3))
    ml_wkt_t = _taps(ml_wk, (0, 3, 1, 2))
    ml_wv_t = _taps(ml_wv, (0, 2, 1, 3))
    ml_gbias = jnp.broadcast_to(jnp.concatenate([ml_bi, ml_bf], axis=1)[:, :, None, None],
                                (depth, 2 * ML_HEADS, 1, ML_CHUNK))
    mix_g, xa_g, mem_g = row(mix_norm_g), row(xa_norm_g), row(mem_norm_g)
    lru_cb, lru_ba_r, lru_bx_r, lru_lam = row(lru_conv_b), row(lru_ba), row(lru_bx), row(lru_lambda)
    ml_cb, ml_ng = row(ml_conv_b), row(ml_norm_g)
    for l in range(depth):
        p = _in_proj(x.reshape(b * t, d), mix_g, w_in_p, l).reshape(b, t, IN_COLS_P)
        gates = p[:, :, C_GATE:C_GATE + 2 * ML_HEADS].astype(F32)
        gates = jnp.swapaxes(gates, 1, 2).reshape(b, 2 * ML_HEADS, nc, ML_CHUNK)
        y_lru = _lru_branch(p, lru_conv_w, lru_cb, lru_wax, lru_ba_r, lru_bx_r, lru_lam, l)
        y_ml = _mlstm_branch(p, gates, ml_gbias, ml_conv_w, ml_cb, ml_wq_t, ml_wkt_t, ml_wv_t, ml_ng, l)
        y_mo = _moba_branch(p, slopes)
        x1 = _out_proj(x.reshape(b * t, d), y_lru.reshape(b * t, LRU_W), y_ml.reshape(b * t, ML_W),
                       y_mo.reshape(b * t, MO_W), w_out_b, l).reshape(b, t, d)
        kv = _mem_kv(mem, mem_g, wkv_b, l)
        x = _xattn(x1, xa_g, wq_b, kv, wo_b, final_norm_g[None, :], l, final_norm=(l == depth - 1))
    return x
```

```python
import functools
import math

import jax
import jax.numpy as jnp
import numpy as np
from jax import lax
from jax.experimental import pallas as pl
from jax.experimental.pallas import tpu as pltpu

LANE = 128
SUBLANE = 8
V7X_VMEM_BYTES = 64 * 1024 * 1024

D_MODEL = 2048
LRU_W = 512
LRU_BLOCKS = 4
LRU_BW = LRU_W // LRU_BLOCKS
LRU_C = 8.0
ML_W = 768
ML_HEADS = 4
ML_DH = 192
ML_DHP = 256
ML_PAIRS = 2
ML_PW = 2 * ML_DH
ML_QKV_BLOCK = 4
ML_QKV_SHIFT = 2
ML_CHUNK = 128
MO_W = 768
MO_HEADS = 6
MO_DH = 128
MOBA_BLOCK = 256
MOBA_TOPK = 3
XA_HEADS = 4
XA_DH = 128
XA_W = XA_HEADS * XA_DH
RMS_EPS = 1e-6
LN_EPS = 1e-5
NEG_INF = -1e30
LOG2E = 1.4426950408889634

REF_LRU_END = 2 * LRU_W
REF_ML_END = REF_LRU_END + 3 * ML_W
REF_GATE_END = REF_ML_END + 2 * ML_HEADS

C_ML_U = 0
C_ML_O = C_ML_U + ML_W
C_ML_Z = C_ML_O + ML_W
C_LRU_X = C_ML_Z + ML_W
C_LRU_Z = C_LRU_X + LRU_W
C_MO_Q = C_LRU_Z + LRU_W
C_MO_K = C_MO_Q + MO_W
C_MO_V = C_MO_K + MO_W
C_MO_Z = C_MO_V + MO_W
C_GATE = C_MO_Z + MO_W
IN_TM = 1024
IN_TN = 1664
IN_COLS_P = 6656
PREP_TR = 256
OUT_TM = 512
XA_TM = 512

BF16 = jnp.bfloat16
F32 = jnp.float32
NT_DIMS = (((1,), (1,)), ((), ()))


def _vmem_limit(nbytes):
    return int(min(V7X_VMEM_BYTES - (4 << 20), max(32 << 20, nbytes)))


def _rmsnorm_rows(x, g):
    ms = jnp.mean(x * x, axis=-1, keepdims=True)
    return x * lax.rsqrt(ms + RMS_EPS) * g


def _sigmoid(x):
    return jax.nn.sigmoid(x)


def _silu(x):
    return x * jax.nn.sigmoid(x)


def _softplus(x):
    return jnp.maximum(x, 0.0) + jnp.log1p(jnp.exp(-jnp.abs(x)))


def _shift_rows(x, s):
    rolled = pltpu.roll(x, s, axis=0)
    row = lax.broadcasted_iota(jnp.int32, x.shape, 0)
    return jnp.where(row >= s, rolled, 0.0)


def _causal_conv(x, w_ref, b_ref):
    k = w_ref.shape[0]
    acc = x * w_ref[k - 1:k, :]
    head = x[:SUBLANE]
    acc_head = head * w_ref[k - 1:k, :]
    for j in range(k - 1):
        acc = acc + pltpu.roll(x, k - 1 - j, axis=0) * w_ref[j:j + 1, :]
        acc_head = acc_head + _shift_rows(head, k - 1 - j) * w_ref[j:j + 1, :]
    return jnp.concatenate([acc_head, acc[SUBLANE:]], axis=0) + b_ref[...]


def _w_in_prep_kernel(w_ref, o_ref):
    last = pl.program_id(1) == pl.num_programs(1) - 1
    valid = jnp.where(last, 2 * ML_HEADS, PREP_TR)
    row = lax.broadcasted_iota(jnp.int32, w_ref.shape, 0)
    o_ref[...] = jnp.where(row < valid, w_ref[...], 0.0).astype(BF16)


def _w_in_prep_src_row(i):
    n_ml = (C_LRU_X - C_ML_U) // PREP_TR
    n_lru = (C_MO_Q - C_LRU_X) // PREP_TR
    n_mo = (C_GATE - C_MO_Q) // PREP_TR
    tile, unit = PREP_TR // SUBLANE, SUBLANE
    ml = REF_LRU_END // unit + i * tile
    lru = (i - n_ml) * tile
    mo = REF_GATE_END // unit + (i - n_ml - n_lru) * tile
    gate = REF_ML_END // unit
    return unit * jnp.where(i < n_ml, ml, jnp.where(i < n_ml + n_lru, lru, jnp.where(i < n_ml + n_lru + n_mo, mo, gate)))


def _w_in_prep(w_in_t):
    depth, _, k = w_in_t.shape
    return pl.pallas_call(
        _w_in_prep_kernel,
        grid=(depth, IN_COLS_P // PREP_TR),
        in_specs=[pl.BlockSpec((pl.Squeezed(), pl.Element(PREP_TR), pl.Element(k)),
                               lambda l, i: (l, _w_in_prep_src_row(i), 0))],
        out_specs=pl.BlockSpec((None, PREP_TR, k), lambda l, i: (l, i, 0)),
        out_shape=jax.ShapeDtypeStruct((depth, IN_COLS_P, k), BF16),
        compiler_params=pltpu.CompilerParams(
            dimension_semantics=("arbitrary", "arbitrary"), vmem_limit_bytes=_vmem_limit(32 << 20)),
        name="w_in_prep",
    )(w_in_t)


def _in_proj_kernel(x_ref, g_ref, w_ref, o_ref, xn_ref):
    @pl.when(pl.program_id(1) == 0)
    def _():
        xn_ref[...] = _rmsnorm_rows(x_ref[...], g_ref[...]).astype(BF16)

    o_ref[...] = lax.dot_general(xn_ref[...], w_ref[...], NT_DIMS, preferred_element_type=F32).astype(o_ref.dtype)


def _in_proj(x2d, g, w, l):
    m = x2d.shape[0]
    n = w.shape[1]
    vmem = (2 * IN_TM * D_MODEL * 4 + IN_TM * D_MODEL * 2 + 2 * D_MODEL * IN_TN * 2 + 2 * IN_TM * IN_TN * 2
            + 2 * IN_TM * IN_TN * 4)
    return pl.pallas_call(
        _in_proj_kernel,
        grid=(m // IN_TM, n // IN_TN),
        in_specs=[
            pl.BlockSpec((IN_TM, D_MODEL), lambda i, j: (i, 0)),
            pl.BlockSpec((None, 1, D_MODEL), lambda i, j: (l, 0, 0)),
            pl.BlockSpec((None, IN_TN, D_MODEL), lambda i, j: (l, j, 0)),
        ],
        out_specs=pl.BlockSpec((IN_TM, IN_TN), lambda i, j: (i, j)),
        out_shape=jax.ShapeDtypeStruct((m, n), BF16),
        scratch_shapes=[pltpu.VMEM((IN_TM, D_MODEL), BF16)],
        compiler_params=pltpu.CompilerParams(
            dimension_semantics=("arbitrary", "arbitrary"), vmem_limit_bytes=_vmem_limit(vmem)),
        name="in_proj",
    )(x2d, g, w)


def _lru_kernel(x_ref, z_ref, cw_ref, cb_ref, wax_ref, ba_ref, bx_ref, lam_ref, o_ref, a_s, u_s):
    t = x_ref.shape[1]
    x = x_ref[0].astype(F32)
    xc = _causal_conv(x, cw_ref, cb_ref)
    pre = jnp.dot(xc.astype(BF16), wax_ref[0], preferred_element_type=F32)
    r = _sigmoid(pre[:, :LRU_BW] + ba_ref[...])
    i = _sigmoid(pre[:, LRU_BW:] + bx_ref[...])
    log_a = (-LRU_C) * r * _softplus(-lam_ref[...])
    a = jnp.exp(log_a)
    a_s[...] = a
    u_s[...] = jnp.sqrt(-jnp.tanh(log_a) * (1.0 + a * a)) * (i * xc)

    row = lax.broadcasted_iota(jnp.int32, (SUBLANE, LRU_BW), 0)

    def block(blk, h_prev):
        r0 = pl.multiple_of(blk * SUBLANE, SUBLANE)
        a_b = a_s[pl.ds(r0, SUBLANE), :]
        u_b = u_s[pl.ds(r0, SUBLANE), :]
        for s in (1, 2, 4):
            a_sh = jnp.where(row >= s, pltpu.roll(a_b, s, axis=0), 1.0)
            u_sh = jnp.where(row >= s, pltpu.roll(u_b, s, axis=0), 0.0)
            u_b = a_b * u_sh + u_b
            a_b = a_b * a_sh
        h = a_b * h_prev + u_b
        u_s[pl.ds(r0, SUBLANE), :] = h
        return jnp.broadcast_to(h[SUBLANE - 1:SUBLANE, :], (SUBLANE, LRU_BW))

    lax.fori_loop(0, t // SUBLANE, block, jnp.zeros((SUBLANE, LRU_BW), F32), unroll=4)
    z = z_ref[0].astype(F32)
    o_ref[0] = (u_s[...] * _silu(z)).astype(o_ref.dtype)


def _lru_branch(p, cw, cb, wax, ba, bx, lam, l):
    b, t, _ = p.shape
    xb, zb = C_LRU_X // LRU_BW, C_LRU_Z // LRU_BW
    vec = pl.BlockSpec((None, 1, LRU_BW), lambda bi, g: (l, 0, g))
    return pl.pallas_call(
        _lru_kernel,
        grid=(b, LRU_BLOCKS),
        in_specs=[
            pl.BlockSpec((1, t, LRU_BW), lambda bi, g: (bi, 0, xb + g)),
            pl.BlockSpec((1, t, LRU_BW), lambda bi, g: (bi, 0, zb + g)),
            pl.BlockSpec((None, cw.shape[1], LRU_BW), lambda bi, g: (l, 0, g)),
            vec,
            pl.BlockSpec((None, 1, LRU_BW, 2 * LRU_BW), lambda bi, g: (l, g, 0, 0)),
            vec, vec, vec,
        ],
        out_specs=pl.BlockSpec((1, t, LRU_BW), lambda bi, g: (bi, 0, g)),
        out_shape=jax.ShapeDtypeStruct((b, t, LRU_W), BF16),
        scratch_shapes=[pltpu.VMEM((t, LRU_BW), F32), pltpu.VMEM((t, LRU_BW), F32)],
        compiler_params=pltpu.CompilerParams(
            dimension_semantics=("arbitrary", "arbitrary"), vmem_limit_bytes=_vmem_limit(24 * t * LRU_BW * 4)),
        name="rg_lru",
    )(p, p, cw, cb, wax, ba, bx, lam)


def _blockdiag_in_out(w_ref, hh):
    wh = w_ref[...][:, hh * ML_DH:(hh + 1) * ML_DH]
    wh = jnp.concatenate([wh, jnp.zeros((ML_QKV_BLOCK, ML_DHP - ML_DH), F32)], axis=1)
    r = lax.broadcasted_iota(jnp.int32, (ML_PW, ML_DHP), 0)
    c = lax.broadcasted_iota(jnp.int32, (ML_PW, ML_DHP), 1)
    d = jnp.zeros((ML_PW, ML_DHP), F32)
    for i in range(ML_QKV_BLOCK):
        d = jnp.where((r & (ML_QKV_BLOCK - 1)) == i, wh[i:i + 1, :], d)
    keep = (((r >> ML_QKV_SHIFT) - hh * (ML_DH // ML_QKV_BLOCK)) == (c >> ML_QKV_SHIFT)) & (c < ML_DH)
    return jnp.where(keep, d, 0.0)


def _blockdiag_in_pair(w_ref, hh):
    wh = w_ref[...]
    r = lax.broadcasted_iota(jnp.int32, (ML_PW, ML_PW), 0)
    c = lax.broadcasted_iota(jnp.int32, (ML_PW, ML_PW), 1)
    d = jnp.zeros((ML_PW, ML_PW), F32)
    for i in range(ML_QKV_BLOCK):
        d = jnp.where((r & (ML_QKV_BLOCK - 1)) == i, wh[i:i + 1, :], d)
    keep = ((r >> ML_QKV_SHIFT) == (c >> ML_QKV_SHIFT)) & (c >= hh * ML_DH) & (c < (hh + 1) * ML_DH)
    return jnp.where(keep, d, 0.0)


def _blockdiag_out_in(w_ref, hh):
    wk = w_ref[...]
    o = lax.broadcasted_iota(jnp.int32, (ML_DHP, ML_PW), 0)
    r = lax.broadcasted_iota(jnp.int32, (ML_DHP, ML_PW), 1)
    d = jnp.zeros((ML_DHP, ML_PW), F32)
    for j in range(ML_QKV_BLOCK):
        d = jnp.where((o & (ML_QKV_BLOCK - 1)) == j, wk[j:j + 1, :], d)
    keep = ((o >> ML_QKV_SHIFT) == ((r >> ML_QKV_SHIFT) - hh * (ML_DH // ML_QKV_BLOCK))) & (o < ML_DH)
    return jnp.where(keep, d, 0.0)


ML_DEN_LANE = (ML_DH, 0)


def _mlstm_kernel(u_ref, og_ref, z_ref, ig_ref, fg_ref, bi_ref, bf_ref, cw_ref, cb_ref,
                  wq_ref, wkt_ref, wv_ref, ng_ref, y_ref,
                  dq_s, dkt_s, dv_s, q_s, kt_s, v_s, cp_s, c_s, r_s, w_s, col_s, so_s, sn_s):
    t = u_ref.shape[1]
    nc = t // ML_CHUNK
    L = ML_CHUNK
    heads = range(2)

    @pl.when(pl.program_id(1) == 0)
    def _():
        for hh in heads:
            dq_s[hh] = _blockdiag_in_out(wq_ref, hh).astype(BF16)
            dkt_s[hh] = _blockdiag_out_in(wkt_ref, hh).astype(BF16)
            dv_s[hh] = _blockdiag_in_pair(wv_ref, hh).astype(BF16)

    u = u_ref[0]
    uc = _silu(_causal_conv(u.astype(F32), cw_ref, cb_ref)).astype(BF16)
    for hh in heads:
        w0 = hh * (ML_PW - ML_DHP)
        uc_w, u_w = uc[:, w0:w0 + ML_DHP], u[:, w0:w0 + ML_DHP]
        q_s[hh] = jnp.dot(uc_w, dq_s[hh, w0:w0 + ML_DHP, :], preferred_element_type=F32).astype(BF16)
        kt = lax.dot_general(dkt_s[hh, :, w0:w0 + ML_DHP], uc_w, NT_DIMS, preferred_element_type=F32)
        kt = kt * (ML_DH ** -0.5)
        for c in range(nc):
            kt_s[hh, c] = kt[:, c * L:(c + 1) * L].astype(BF16)
        v = jnp.dot(u_w, dv_s[hh, w0:w0 + ML_DHP, :], preferred_element_type=F32)
        vlane = lax.broadcasted_iota(jnp.int32, v.shape, 1)
        v_s[hh] = jnp.where(vlane == ML_DEN_LANE[hh], 1.0, v).astype(BF16)

    glane = lax.broadcasted_iota(jnp.int32, (nc, L), 1)
    grow = lax.broadcasted_iota(jnp.int32, (nc, L), 0)
    row8 = lax.broadcasted_iota(jnp.int32, (SUBLANE, L), 0)
    for hh in heads:
        ig = ig_ref[0, hh] + bi_ref[hh]
        lf = -_softplus(-(fg_ref[0, hh] + bf_ref[hh]))
        b = lf
        for k in range(int(math.log2(L))):
            sh = 1 << k
            b = b + jnp.where(glane >= sh, pltpu.roll(b, sh, axis=1), 0.0)
        g = jnp.broadcast_to(b[:, L - 1:L], (nc, L))
        a = g - b + ig
        mloc = jnp.broadcast_to(jnp.max(a, axis=1, keepdims=True), (nc, L))
        m = jnp.zeros((1, L), F32)
        m_prev = jnp.zeros((nc, L), F32)
        m_next = jnp.zeros((nc, L), F32)
        for c in range(nc):
            m_prev = jnp.where(grow == c, m, m_prev)
            m = jnp.maximum(g[c:c + 1, :] + m, mloc[c:c + 1, :])
            m_next = jnp.where(grow == c, m, m_next)
        r = ig - b
        cmx = r
        for k in range(int(math.log2(L))):
            sh = 1 << k
            cmx = jnp.maximum(cmx, jnp.where(glane >= sh, pltpu.roll(cmx, sh, axis=1), -jnp.inf))
        mm = jnp.maximum(m_prev, cmx)
        s_int = jnp.exp(m_prev - mm)
        clamp = jnp.exp(-(b + mm))
        r_s[hh] = r
        w_s[hh] = jnp.exp(a - mloc)
        so_s[hh] = jnp.exp(g + m_prev - m_next)
        sn_s[hh] = jnp.exp(mloc - m_next)
        for c in range(nc):
            col_s[hh, c] = jnp.where(row8 == 0, mm[c:c + 1, :],
                                     jnp.where(row8 == 1, s_int[c:c + 1, :],
                                               jnp.where(row8 == 2, clamp[c:c + 1, :], 0.0)))

    zero_rows = jnp.zeros((ML_DHP - ML_DH, ML_PW), BF16)
    c_s[...] = jnp.zeros(c_s.shape, F32)

    def state(c, carry):
        r0 = pl.multiple_of(c * L, L)
        for hh in heads:
            c_prev = c_s[hh]
            cp_s[hh, c] = jnp.concatenate([c_prev.astype(BF16), zero_rows], axis=0)
            ktw = (kt_s[hh, c, :ML_DH, :].astype(F32) * w_s[hh, pl.ds(c, 1), :]).astype(BF16)
            c_loc = jnp.dot(ktw, v_s[hh, pl.ds(r0, L), :], preferred_element_type=F32)
            c_s[hh] = so_s[hh, pl.ds(c, 1), :][:, :1] * c_prev + sn_s[hh, pl.ds(c, 1), :][:, :1] * c_loc
        return carry

    lax.fori_loop(0, nc, state, 0, unroll=2)

    tri = (lax.broadcasted_iota(jnp.int32, (L, L), 0) >= lax.broadcasted_iota(jnp.int32, (L, L), 1))
    first = lax.broadcasted_iota(jnp.int32, (L, ML_PW), 1) < ML_DH
    pad_rows = jnp.zeros((L - SUBLANE, L), F32)

    def chunk(c, carry):
        r0 = pl.multiple_of(c * L, L)
        nds, invs = [], []
        for hh in heads:
            q_c = q_s[hh, pl.ds(r0, L), :]
            cols = jnp.concatenate([col_s[hh, c], pad_rows], axis=0).T
            rb = jnp.broadcast_to(r_s[hh, pl.ds(c, 1), :], (L, L))
            decay = jnp.exp(jnp.where(tri, rb - cols[:, 0:1], -jnp.inf))
            s_mat = jnp.dot(q_c, kt_s[hh, c], preferred_element_type=F32) * decay
            nd = (jnp.dot(s_mat.astype(BF16), v_s[hh, pl.ds(r0, L), :], preferred_element_type=F32)
                  + cols[:, 1:2] * jnp.dot(q_c, cp_s[hh, c], preferred_element_type=F32))
            den = nd[:, ML_DEN_LANE[hh]:ML_DEN_LANE[hh] + 1]
            nds.append(nd)
            invs.append(1.0 / jnp.maximum(jnp.abs(den), cols[:, 2:3]))
        x = _sigmoid(og_ref[0, pl.ds(r0, L), :].astype(F32)) * jnp.where(first, nds[0], nds[1])
        mu0 = jnp.sum(jnp.where(first, x, 0.0), axis=1, keepdims=True) * (1.0 / ML_DH)
        mu1 = jnp.sum(jnp.where(first, 0.0, x), axis=1, keepdims=True) * (1.0 / ML_DH)
        dev = x - jnp.where(first, mu0, mu1)
        sq = dev * dev
        var0 = jnp.sum(jnp.where(first, sq, 0.0), axis=1, keepdims=True) * (1.0 / ML_DH)
        var1 = jnp.sum(jnp.where(first, 0.0, sq), axis=1, keepdims=True) * (1.0 / ML_DH)
        f0 = invs[0] * lax.rsqrt(invs[0] * invs[0] * var0 + LN_EPS)
        f1 = invs[1] * lax.rsqrt(invs[1] * invs[1] * var1 + LN_EPS)
        zs = _silu(z_ref[0, pl.ds(r0, L), :].astype(F32)) * ng_ref[...]
        y_ref[0, pl.ds(r0, L), :] = (dev * jnp.where(first, f0, f1) * zs).astype(y_ref.dtype)
        return carry

    lax.fori_loop(0, nc, chunk, 0, unroll=4)


def _mlstm_branch(p, gates, gbias, cw, cb, wq, wkt, wv, ng, l):
    b, t, _ = p.shape
    nc = t // ML_CHUNK
    ub, ob, zb = C_ML_U // ML_PW, C_ML_O // ML_PW, C_ML_Z // ML_PW
    seq = lambda base: pl.BlockSpec((1, t, ML_PW), lambda pr, bi: (bi, 0, base + pr))
    vec = pl.BlockSpec((None, 1, ML_PW), lambda pr, bi: (l, 0, pr))
    taps = pl.BlockSpec((None, ML_QKV_BLOCK, ML_PW), lambda pr, bi: (l, 0, pr))
    rows = lambda: pltpu.VMEM((2, nc, ML_CHUNK), F32)
    return pl.pallas_call(
        _mlstm_kernel,
        grid=(ML_PAIRS, b),
        in_specs=[
            seq(ub), seq(ob), seq(zb),
            pl.BlockSpec((1, 2, nc, ML_CHUNK), lambda pr, bi: (bi, pr, 0, 0)),
            pl.BlockSpec((1, 2, nc, ML_CHUNK), lambda pr, bi: (bi, ML_PAIRS + pr, 0, 0)),
            pl.BlockSpec((None, 2, 1, ML_CHUNK), lambda pr, bi: (l, pr, 0, 0)),
            pl.BlockSpec((None, 2, 1, ML_CHUNK), lambda pr, bi: (l, ML_PAIRS + pr, 0, 0)),
            pl.BlockSpec((None, cw.shape[1], ML_PW), lambda pr, bi: (l, 0, pr)),
            vec, taps, taps, taps, vec,
        ],
        out_specs=pl.BlockSpec((1, t, ML_PW), lambda pr, bi: (bi, 0, pr)),
        out_shape=jax.ShapeDtypeStruct((b, t, ML_W), BF16),
        scratch_shapes=[
            pltpu.VMEM((2, ML_PW, ML_DHP), BF16),
            pltpu.VMEM((2, ML_DHP, ML_PW), BF16),
            pltpu.VMEM((2, ML_PW, ML_PW), BF16),
            pltpu.VMEM((2, t, ML_DHP), BF16),
            pltpu.VMEM((2, nc, ML_DHP, ML_CHUNK), BF16),
            pltpu.VMEM((2, t, ML_PW), BF16),
            pltpu.VMEM((2, nc, ML_DHP, ML_PW), BF16),
            pltpu.VMEM((2, ML_DH, ML_PW), F32),
            rows(), rows(),
            pltpu.VMEM((2, nc, SUBLANE, ML_CHUNK), F32),
            rows(), rows(),
        ],
        compiler_params=pltpu.CompilerParams(
            dimension_semantics=("arbitrary", "arbitrary"), vmem_limit_bytes=_vmem_limit(56 << 20)),
        name="mlstm",
    )(p, p, p, gates, gates, gbias, gbias, cw, cb, wq, wkt, wv, ng)


def _moba_kernel(q_ref, k_ref, v_ref, z_ref, slope_ref, o_ref, l_s):
    t = q_ref.shape[1]
    nb = t // MOBA_BLOCK
    bs = MOBA_BLOCK
    qscale = (MO_DH ** -0.5) * LOG2E
    slope2 = slope_ref[0][:, :1] * LOG2E

    krow = lax.broadcasted_iota(jnp.int32, (LANE, MO_DH), 0)
    kmean = jnp.zeros((LANE, MO_DH), F32)
    for n in range(nb):
        mean_n = jnp.sum(k_ref[0, n * bs:(n + 1) * bs, :].astype(F32), axis=0, keepdims=True) * (1.0 / bs)
        kmean = jnp.where(krow == n, mean_n, kmean)
    kmean = kmean.astype(BF16)

    lane = lax.broadcasted_iota(jnp.int32, (bs, LANE), 1)
    causal = (lax.broadcasted_iota(jnp.int32, (bs, bs), 0) >= lax.broadcasted_iota(jnp.int32, (bs, bs), 1))
    kpos = lax.broadcasted_iota(jnp.int32, (1, bs), 1).astype(F32)

    for qb in range(nb):
        q_b = q_ref[0, qb * bs:(qb + 1) * bs, :]
        q_sc = (q_b.astype(F32) * qscale).astype(BF16)
        if qb > MOBA_TOPK:
            gate = lax.dot_general(q_b, kmean, NT_DIMS, preferred_element_type=F32)
            beaten = jnp.zeros((bs, LANE), F32)
            for m in range(qb):
                gm = gate[:, m:m + 1]
                wins = (gm > gate) | ((gm == gate) & (lane > m))
                beaten = beaten + jnp.where(wins, 1.0, 0.0)
            keep = jnp.where(beaten < float(MOBA_TOPK), 1.0, 0.0)
        m_run = None
        for n in range(qb + 1):
            s = lax.dot_general(q_sc, k_ref[0, n * bs:(n + 1) * bs, :], NT_DIMS, preferred_element_type=F32)
            logit = s + slope2 * (kpos + float((n - qb) * bs))
            if n == qb:
                logit = jnp.where(causal, logit, NEG_INF)
            elif qb > MOBA_TOPK:
                logit = jnp.where(keep[:, n:n + 1] > 0.5, logit, NEG_INF)
            l_s[:, n * bs:(n + 1) * bs] = logit
            mx = jnp.max(logit, axis=1, keepdims=True)
            m_run = mx if m_run is None else jnp.maximum(m_run, mx)
        nk = (qb + 1) * bs
        pr = jnp.exp2(l_s[:, :nk] - m_run)
        den = jnp.sum(pr, axis=1, keepdims=True)
        acc = jnp.dot(pr.astype(BF16), v_ref[0, :nk, :], preferred_element_type=F32)
        z = z_ref[0, qb * bs:(qb + 1) * bs, :].astype(F32)
        o_ref[0, qb * bs:(qb + 1) * bs, :] = (acc * ((1.0 / den) * _silu(z))).astype(o_ref.dtype)


def _moba_branch(p, slopes):
    b, t, _ = p.shape
    qb, kb, vb, zb = (c // MO_DH for c in (C_MO_Q, C_MO_K, C_MO_V, C_MO_Z))
    seq = lambda base: pl.BlockSpec((1, t, MO_DH), lambda bi, h: (bi, 0, base + h))
    return pl.pallas_call(
        _moba_kernel,
        grid=(b, MO_HEADS),
        in_specs=[seq(qb), seq(kb), seq(vb), seq(zb),
                  pl.BlockSpec((1, 1, LANE), lambda bi, h: (h, 0, 0))],
        out_specs=pl.BlockSpec((1, t, MO_DH), lambda bi, h: (bi, 0, h)),
        out_shape=jax.ShapeDtypeStruct((b, t, MO_W), BF16),
        scratch_shapes=[pltpu.VMEM((MOBA_BLOCK, t), F32)],
        compiler_params=pltpu.CompilerParams(
            dimension_semantics=("arbitrary", "arbitrary"), vmem_limit_bytes=_vmem_limit(32 << 20)),
        name="moba",
    )(p, p, p, p, slopes)


def _out_proj_kernel(x_ref, yl_ref, ym_ref, yo_ref, w_ref, o_ref):
    acc = jnp.dot(yl_ref[...], w_ref[0:LRU_W, :], preferred_element_type=F32)
    acc = acc + jnp.dot(ym_ref[...], w_ref[LRU_W:LRU_W + ML_W, :], preferred_element_type=F32)
    acc = acc + jnp.dot(yo_ref[...], w_ref[LRU_W + ML_W:, :], preferred_element_type=F32)
    o_ref[...] = x_ref[...] + acc


def _out_proj(x2d, yl, ym, yo, w, l):
    m = x2d.shape[0]
    kdim = w.shape[1]
    rows = lambda width: pl.BlockSpec((OUT_TM, width), lambda i: (i, 0))
    vmem = 4 * OUT_TM * D_MODEL * 4 + 2 * kdim * D_MODEL * 2 + 2 * OUT_TM * kdim * 2 + 2 * OUT_TM * D_MODEL * 4
    return pl.pallas_call(
        _out_proj_kernel,
        grid=(m // OUT_TM,),
        in_specs=[rows(D_MODEL), rows(LRU_W), rows(ML_W), rows(MO_W),
                  pl.BlockSpec((None, kdim, D_MODEL), lambda i: (l, 0, 0))],
        out_specs=rows(D_MODEL),
        out_shape=jax.ShapeDtypeStruct((m, D_MODEL), F32),
        compiler_params=pltpu.CompilerParams(
            dimension_semantics=("arbitrary",), vmem_limit_bytes=_vmem_limit(vmem)),
        name="out_proj",
    )(x2d, yl, ym, yo, w)


def _mem_kv_kernel(mem_ref, g_ref, w_ref, o_ref):
    hm = _rmsnorm_rows(mem_ref[0], g_ref[...]).astype(BF16)
    o_ref[0] = jnp.dot(hm, w_ref[...], preferred_element_type=F32).astype(o_ref.dtype)


def _mem_kv(mem, g, wkv, l):
    b, m, _ = mem.shape
    return pl.pallas_call(
        _mem_kv_kernel,
        grid=(b,),
        in_specs=[pl.BlockSpec((1, m, D_MODEL), lambda bi: (bi, 0, 0)),
                  pl.BlockSpec((None, 1, D_MODEL), lambda bi: (l, 0, 0)),
                  pl.BlockSpec((None, D_MODEL, 2 * XA_W), lambda bi: (l, 0, 0))],
        out_specs=pl.BlockSpec((1, m, 2 * XA_W), lambda bi: (bi, 0, 0)),
        out_shape=jax.ShapeDtypeStruct((b, m, 2 * XA_W), BF16),
        compiler_params=pltpu.CompilerParams(
            dimension_semantics=("arbitrary",), vmem_limit_bytes=_vmem_limit(32 << 20)),
        name="mem_kv",
    )(mem, g, wkv)


def _xattn_kernel(x_ref, g_ref, wq_ref, kv_ref, wo_ref, fg_ref, o_ref, *, final_norm):
    x = x_ref[0]
    hx = _rmsnorm_rows(x, g_ref[...]).astype(BF16)
    q = jnp.dot(hx, wq_ref[...], preferred_element_type=F32).astype(BF16)
    heads = []
    for h in range(XA_HEADS):
        q_h = q[:, h * XA_DH:(h + 1) * XA_DH]
        k_h = kv_ref[0, :, h * XA_DH:(h + 1) * XA_DH]
        v_h = kv_ref[0, :, XA_W + h * XA_DH:XA_W + (h + 1) * XA_DH]
        s = lax.dot_general(q_h, k_h, NT_DIMS, preferred_element_type=F32) * (XA_DH ** -0.5)
        e = jnp.exp(s - jnp.max(s, axis=1, keepdims=True))
        pr = e / jnp.sum(e, axis=1, keepdims=True)
        heads.append(jnp.dot(pr.astype(BF16), v_h, preferred_element_type=F32).astype(BF16))
    o = jnp.concatenate(heads, axis=1)
    y = x + jnp.dot(o, wo_ref[...], preferred_element_type=F32)
    if final_norm:
        y = _rmsnorm_rows(y, fg_ref[...])
    o_ref[0] = y


def _xattn(x, g, wq, kv, wo, fg, l, final_norm):
    b, t, _ = x.shape
    m = kv.shape[1]
    return pl.pallas_call(
        functools.partial(_xattn_kernel, final_norm=final_norm),
        grid=(b, t // XA_TM),
        in_specs=[pl.BlockSpec((1, XA_TM, D_MODEL), lambda bi, i: (bi, i, 0)),
                  pl.BlockSpec((None, 1, D_MODEL), lambda bi, i: (l, 0, 0)),
                  pl.BlockSpec((None, D_MODEL, XA_W), lambda bi, i: (l, 0, 0)),
                  pl.BlockSpec((1, m, 2 * XA_W), lambda bi, i: (bi, 0, 0)),
                  pl.BlockSpec((None, XA_W, D_MODEL), lambda bi, i: (l, 0, 0)),
                  pl.BlockSpec((1, D_MODEL), lambda bi, i: (0, 0))],
        out_specs=pl.BlockSpec((1, XA_TM, D_MODEL), lambda bi, i: (bi, i, 0)),
        out_shape=jax.ShapeDtypeStruct((b, t, D_MODEL), F32),
        compiler_params=pltpu.CompilerParams(
            dimension_semantics=("arbitrary", "arbitrary"), vmem_limit_bytes=_vmem_limit(40 << 20)),
        name="mem_xattn",
    )(x, g, wq, kv, wo, fg)


def _alibi_slopes(n):
    def pow2(m):
        start = 2.0 ** (-8.0 / m)
        return [start ** (i + 1) for i in range(m)]
    if math.log2(n).is_integer():
        s = pow2(n)
    else:
        c = 2 ** int(math.floor(math.log2(n)))
        s = pow2(c) + pow2(2 * c)[0::2][:n - c]
    return np.asarray(s, dtype=np.float32)


def _taps(w, perm):
    depth = w.shape[0]
    return jnp.transpose(w, perm).reshape(depth, ML_QKV_BLOCK, ML_W)


def kernel(x, mem, mix_norm_g, w_in, lru_conv_w, lru_conv_b, lru_wa, lru_ba, lru_wx, lru_bx, lru_lambda,
           ml_conv_w, ml_conv_b, ml_wq, ml_wk, ml_wv, ml_bi, ml_bf, ml_norm_g, w_out, xa_norm_g, mem_norm_g,
           xa_wq, xa_wkv, xa_wo, final_norm_g):
    b, t, d = x.shape
    depth = w_in.shape[0]
    nc = t // ML_CHUNK
    row = lambda v: v[:, None, :]
    slopes = jnp.broadcast_to(jnp.asarray(_alibi_slopes(MO_HEADS))[:, None, None], (MO_HEADS, 1, LANE))
    w_in_p = _w_in_prep(jnp.swapaxes(w_in, 1, 2))
    w_out_b = w_out.astype(BF16)
    wq_b, wkv_b, wo_b = xa_wq.astype(BF16), xa_wkv.astype(BF16), xa_wo.astype(BF16)
    lru_wax = jnp.concatenate([lru_wa, lru_wx], axis=-1).astype(BF16)
    ml_wq_t = _taps(ml_wq, (0, 2, 1, 3))
    ml_wkt_t = _taps(ml_wk, (0, 3, 1, 2))
    ml_wv_t = _taps(ml_wv, (0, 2, 1, 3))
    ml_gbias = jnp.broadcast_to(jnp.concatenate([ml_bi, ml_bf], axis=1)[:, :, None, None],
                                (depth, 2 * ML_HEADS, 1, ML_CHUNK))
    mix_g, xa_g, mem_g = row(mix_norm_g), row(xa_norm_g), row(mem_norm_g)
    lru_cb, lru_ba_r, lru_bx_r, lru_lam = row(lru_conv_b), row(lru_ba), row(lru_bx), row(lru_lambda)
    ml_cb, ml_ng = row(ml_conv_b), row(ml_norm_g)
    for l in range(depth):
        p = _in_proj(x.reshape(b * t, d), mix_g, w_in_p, l).reshape(b, t, IN_COLS_P)
        gates = p[:, :, C_GATE:C_GATE + 2 * ML_HEADS].astype(F32)
        gates = jnp.swapaxes(gates, 1, 2).reshape(b, 2 * ML_HEADS, nc, ML_CHUNK)
        y_lru = _lru_branch(p, lru_conv_w, lru_cb, lru_wax, lru_ba_r, lru_bx_r, lru_lam, l)
        y_ml = _mlstm_branch(p, gates, ml_gbias, ml_conv_w, ml_cb, ml_wq_t, ml_wkt_t, ml_wv_t, ml_ng, l)
        y_mo = _moba_branch(p, slopes)
        x1 = _out_proj(x.reshape(b * t, d), y_lru.reshape(b * t, LRU_W), y_ml.reshape(b * t, ML_W),
                       y_mo.reshape(b * t, MO_W), w_out_b, l).reshape(b, t, d)
        kv = _mem_kv(mem, mem_g, wkv_b, l)
        x = _xattn(x1, xa_g, wq_b, kv, wo_b, final_norm_g[None, :], l, final_norm=(l == depth - 1))
    return x
```

```python
import functools
import math

import jax
import jax.numpy as jnp
import numpy as np
from jax import lax
from jax.experimental import pallas as pl
from jax.experimental.pallas import tpu as pltpu

LANE = 128
SUBLANE = 8
V7X_VMEM_BYTES = 64 * 1024 * 1024

D_MODEL = 2048
LRU_W = 512
LRU_BLOCKS = 4
LRU_BW = LRU_W // LRU_BLOCKS
LRU_C = 8.0
ML_W = 768
ML_HEADS = 4
ML_DH = 192
ML_DHP = 256
ML_PAIRS = 2
ML_PW = 2 * ML_DH
ML_QKV_BLOCK = 4
ML_QKV_SHIFT = 2
ML_CHUNK = 128
MO_W = 768
MO_HEADS = 6
MO_DH = 128
MOBA_BLOCK = 256
MOBA_TOPK = 3
XA_HEADS = 4
XA_DH = 128
XA_W = XA_HEADS * XA_DH
RMS_EPS = 1e-6
LN_EPS = 1e-5
NEG_INF = -1e30
LOG2E = 1.4426950408889634

REF_LRU_END = 2 * LRU_W
REF_ML_END = REF_LRU_END + 3 * ML_W
REF_GATE_END = REF_ML_END + 2 * ML_HEADS

C_ML_U = 0
C_ML_O = C_ML_U + ML_W
C_ML_Z = C_ML_O + ML_W
C_LRU_X = C_ML_Z + ML_W
C_LRU_Z = C_LRU_X + LRU_W
C_MO_Q = C_LRU_Z + LRU_W
C_MO_K = C_MO_Q + MO_W
C_MO_V = C_MO_K + MO_W
C_MO_Z = C_MO_V + MO_W
C_GATE = C_MO_Z + MO_W
IN_TM = 1024
IN_TN = 1664
IN_COLS_P = 6656
PREP_TR = 256
OUT_TM = 512
XA_TM = 512

BF16 = jnp.bfloat16
F32 = jnp.float32
NT_DIMS = (((1,), (1,)), ((), ()))


def _vmem_limit(nbytes):
    return int(min(V7X_VMEM_BYTES - (4 << 20), max(32 << 20, nbytes)))


def _rmsnorm_rows(x, g):
    ms = jnp.mean(x * x, axis=-1, keepdims=True)
    return x * lax.rsqrt(ms + RMS_EPS) * g


def _sigmoid(x):
    return jax.nn.sigmoid(x)


def _silu(x):
    return x * jax.nn.sigmoid(x)


def _softplus(x):
    return jnp.maximum(x, 0.0) + jnp.log1p(jnp.exp(-jnp.abs(x)))


def _shift_rows(x, s):
    rolled = pltpu.roll(x, s, axis=0)
    row = lax.broadcasted_iota(jnp.int32, x.shape, 0)
    return jnp.where(row >= s, rolled, 0.0)


def _causal_conv(x, w_ref, b_ref):
    k = w_ref.shape[0]
    acc = x * w_ref[k - 1:k, :]
    head = x[:SUBLANE]
    acc_head = head * w_ref[k - 1:k, :]
    for j in range(k - 1):
        acc = acc + pltpu.roll(x, k - 1 - j, axis=0) * w_ref[j:j + 1, :]
        acc_head = acc_head + _shift_rows(head, k - 1 - j) * w_ref[j:j + 1, :]
    return jnp.concatenate([acc_head, acc[SUBLANE:]], axis=0) + b_ref[...]


def _w_in_prep_kernel(w_ref, o_ref):
    last = pl.program_id(1) == pl.num_programs(1) - 1
    valid = jnp.where(last, 2 * ML_HEADS, PREP_TR)
    row = lax.broadcasted_iota(jnp.int32, w_ref.shape, 0)
    o_ref[...] = jnp.where(row < valid, w_ref[...], 0.0).astype(BF16)


def _w_in_prep_src_row(i):
    n_ml = (C_LRU_X - C_ML_U) // PREP_TR
    n_lru = (C_MO_Q - C_LRU_X) // PREP_TR
    n_mo = (C_GATE - C_MO_Q) // PREP_TR
    tile, unit = PREP_TR // SUBLANE, SUBLANE
    ml = REF_LRU_END // unit + i * tile
    lru = (i - n_ml) * tile
    mo = REF_GATE_END // unit + (i - n_ml - n_lru) * tile
    gate = REF_ML_END // unit
    return unit * jnp.where(i < n_ml, ml, jnp.where(i < n_ml + n_lru, lru, jnp.where(i < n_ml + n_lru + n_mo, mo, gate)))


def _w_in_prep(w_in_t):
    depth, _, k = w_in_t.shape
    return pl.pallas_call(
        _w_in_prep_kernel,
        grid=(depth, IN_COLS_P // PREP_TR),
        in_specs=[pl.BlockSpec((pl.Squeezed(), pl.Element(PREP_TR), pl.Element(k)),
                               lambda l, i: (l, _w_in_prep_src_row(i), 0))],
        out_specs=pl.BlockSpec((None, PREP_TR, k), lambda l, i: (l, i, 0)),
        out_shape=jax.ShapeDtypeStruct((depth, IN_COLS_P, k), BF16),
        compiler_params=pltpu.CompilerParams(
            dimension_semantics=("arbitrary", "arbitrary"), vmem_limit_bytes=_vmem_limit(32 << 20)),
        name="w_in_prep",
    )(w_in_t)


def _in_proj_kernel(x_ref, g_ref, w_ref, o_ref, xn_ref):
    @pl.when(pl.program_id(1) == 0)
    def _():
        xn_ref[...] = _rmsnorm_rows(x_ref[...], g_ref[...]).astype(BF16)

    o_ref[...] = lax.dot_general(xn_ref[...], w_ref[...], NT_DIMS, preferred_element_type=F32).astype(o_ref.dtype)


def _in_proj(x2d, g, w, l):
    m = x2d.shape[0]
    n = w.shape[1]
    vmem = (2 * IN_TM * D_MODEL * 4 + IN_TM * D_MODEL * 2 + 2 * D_MODEL * IN_TN * 2 + 2 * IN_TM * IN_TN * 2
            + 2 * IN_TM * IN_TN * 4)
    return pl.pallas_call(
        _in_proj_kernel,
        grid=(m // IN_TM, n // IN_TN),
        in_specs=[
            pl.BlockSpec((IN_TM, D_MODEL), lambda i, j: (i, 0)),
            pl.BlockSpec((None, 1, D_MODEL), lambda i, j: (l, 0, 0)),
            pl.BlockSpec((None, IN_TN, D_MODEL), lambda i, j: (l, j, 0)),
        ],
        out_specs=pl.BlockSpec((IN_TM, IN_TN), lambda i, j: (i, j)),
        out_shape=jax.ShapeDtypeStruct((m, n), BF16),
        scratch_shapes=[pltpu.VMEM((IN_TM, D_MODEL), BF16)],
        compiler_params=pltpu.CompilerParams(
            dimension_semantics=("arbitrary", "arbitrary"), vmem_limit_bytes=_vmem_limit(vmem)),
        name="in_proj",
    )(x2d, g, w)


def _lru_kernel(x_ref, z_ref, cw_ref, cb_ref, wax_ref, ba_ref, bx_ref, lam_ref, o_ref, a_s, u_s):
    t = x_ref.shape[1]
    x = x_ref[0].astype(F32)
    xc = _causal_conv(x, cw_ref, cb_ref)
    pre = jnp.dot(xc.astype(BF16), wax_ref[0], preferred_element_type=F32)
    r = _sigmoid(pre[:, :LRU_BW] + ba_ref[...])
    i = _sigmoid(pre[:, LRU_BW:] + bx_ref[...])
    log_a = (-LRU_C) * r * _softplus(-lam_ref[...])
    a = jnp.exp(log_a)
    a_s[...] = a
    u_s[...] = jnp.sqrt(-jnp.tanh(log_a) * (1.0 + a * a)) * (i * xc)

    row = lax.broadcasted_iota(jnp.int32, (SUBLANE, LRU_BW), 0)

    def block(blk, h_prev):
        r0 = pl.multiple_of(blk * SUBLANE, SUBLANE)
        a_b = a_s[pl.ds(r0, SUBLANE), :]
        u_b = u_s[pl.ds(r0, SUBLANE), :]
        for s in (1, 2, 4):
            a_sh = jnp.where(row >= s, pltpu.roll(a_b, s, axis=0), 1.0)
            u_sh = jnp.where(row >= s, pltpu.roll(u_b, s, axis=0), 0.0)
            u_b = a_b * u_sh + u_b
            a_b = a_b * a_sh
        h = a_b * h_prev + u_b
        u_s[pl.ds(r0, SUBLANE), :] = h
        return jnp.broadcast_to(h[SUBLANE - 1:SUBLANE, :], (SUBLANE, LRU_BW))

    lax.fori_loop(0, t // SUBLANE, block, jnp.zeros((SUBLANE, LRU_BW), F32), unroll=4)
    z = z_ref[0].astype(F32)
    o_ref[0] = (u_s[...] * _silu(z)).astype(o_ref.dtype)


def _lru_branch(p, cw, cb, wax, ba, bx, lam, l):
    b, t, _ = p.shape
    xb, zb = C_LRU_X // LRU_BW, C_LRU_Z // LRU_BW
    vec = pl.BlockSpec((None, 1, LRU_BW), lambda bi, g: (l, 0, g))
    return pl.pallas_call(
        _lru_kernel,
        grid=(b, LRU_BLOCKS),
        in_specs=[
            pl.BlockSpec((1, t, LRU_BW), lambda bi, g: (bi, 0, xb + g)),
            pl.BlockSpec((1, t, LRU_BW), lambda bi, g: (bi, 0, zb + g)),
            pl.BlockSpec((None, cw.shape[1], LRU_BW), lambda bi, g: (l, 0, g)),
            vec,
            pl.BlockSpec((None, 1, LRU_BW, 2 * LRU_BW), lambda bi, g: (l, g, 0, 0)),
            vec, vec, vec,
        ],
        out_specs=pl.BlockSpec((1, t, LRU_BW), lambda bi, g: (bi, 0, g)),
        out_shape=jax.ShapeDtypeStruct((b, t, LRU_W), BF16),
        scratch_shapes=[pltpu.VMEM((t, LRU_BW), F32), pltpu.VMEM((t, LRU_BW), F32)],
        compiler_params=pltpu.CompilerParams(
            dimension_semantics=("arbitrary", "arbitrary"), vmem_limit_bytes=_vmem_limit(24 * t * LRU_BW * 4)),
        name="rg_lru",
    )(p, p, cw, cb, wax, ba, bx, lam)


def _blockdiag_in_out(w_ref, hh):
    wh = w_ref[...][:, hh * ML_DH:(hh + 1) * ML_DH]
    wh = jnp.concatenate([wh, jnp.zeros((ML_QKV_BLOCK, ML_DHP - ML_DH), F32)], axis=1)
    r = lax.broadcasted_iota(jnp.int32, (ML_PW, ML_DHP), 0)
    c = lax.broadcasted_iota(jnp.int32, (ML_PW, ML_DHP), 1)
    d = jnp.zeros((ML_PW, ML_DHP), F32)
    for i in range(ML_QKV_BLOCK):
        d = jnp.where((r & (ML_QKV_BLOCK - 1)) == i, wh[i:i + 1, :], d)
    keep = (((r >> ML_QKV_SHIFT) - hh * (ML_DH // ML_QKV_BLOCK)) == (c >> ML_QKV_SHIFT)) & (c < ML_DH)
    return jnp.where(keep, d, 0.0)


def _blockdiag_in_pair(w_ref, hh):
    wh = w_ref[...]
    r = lax.broadcasted_iota(jnp.int32, (ML_PW, ML_PW), 0)
    c = lax.broadcasted_iota(jnp.int32, (ML_PW, ML_PW), 1)
    d = jnp.zeros((ML_PW, ML_PW), F32)
    for i in range(ML_QKV_BLOCK):
        d = jnp.where((r & (ML_QKV_BLOCK - 1)) == i, wh[i:i + 1, :], d)
    keep = ((r >> ML_QKV_SHIFT) == (c >> ML_QKV_SHIFT)) & (c >= hh * ML_DH) & (c < (hh + 1) * ML_DH)
    return jnp.where(keep, d, 0.0)


def _blockdiag_out_in(w_ref, hh):
    wk = w_ref[...]
    o = lax.broadcasted_iota(jnp.int32, (ML_DHP, ML_PW), 0)
    r = lax.broadcasted_iota(jnp.int32, (ML_DHP, ML_PW), 1)
    d = jnp.zeros((ML_DHP, ML_PW), F32)
    for j in range(ML_QKV_BLOCK):
        d = jnp.where((o & (ML_QKV_BLOCK - 1)) == j, wk[j:j + 1, :], d)
    keep = ((o >> ML_QKV_SHIFT) == ((r >> ML_QKV_SHIFT) - hh * (ML_DH // ML_QKV_BLOCK))) & (o < ML_DH)
    return jnp.where(keep, d, 0.0)


ML_DEN_LANE = (ML_DH, 0)


def _mlstm_kernel(u_ref, og_ref, z_ref, ig_ref, fg_ref, bi_ref, bf_ref, cw_ref, cb_ref,
                  wq_ref, wkt_ref, wv_ref, ng_ref, y_ref,
                  dq_s, dkt_s, dv_s, q_s, kt_s, v_s, cp_s, c_s, r_s, w_s, col_s, so_s, sn_s):
    t = u_ref.shape[1]
    nc = t // ML_CHUNK
    L = ML_CHUNK
    heads = range(2)

    @pl.when(pl.program_id(1) == 0)
    def _():
        for hh in heads:
            dq_s[hh] = _blockdiag_in_out(wq_ref, hh).astype(BF16)
            dkt_s[hh] = _blockdiag_out_in(wkt_ref, hh).astype(BF16)
            dv_s[hh] = _blockdiag_in_pair(wv_ref, hh).astype(BF16)

    u = u_ref[0]
    uc = _silu(_causal_conv(u.astype(F32), cw_ref, cb_ref)).astype(BF16)
    for hh in heads:
        w0 = hh * (ML_PW - ML_DHP)
        uc_w, u_w = uc[:, w0:w0 + ML_DHP], u[:, w0:w0 + ML_DHP]
        q_s[hh] = jnp.dot(uc_w, dq_s[hh, w0:w0 + ML_DHP, :], preferred_element_type=F32).astype(BF16)
        kt = lax.dot_general(dkt_s[hh, :, w0:w0 + ML_DHP], uc_w, NT_DIMS, preferred_element_type=F32)
        kt = kt * (ML_DH ** -0.5)
        for c in range(nc):
            kt_s[hh, c] = kt[:, c * L:(c + 1) * L].astype(BF16)
        v = jnp.dot(u_w, dv_s[hh, w0:w0 + ML_DHP, :], preferred_element_type=F32)
        vlane = lax.broadcasted_iota(jnp.int32, v.shape, 1)
        v_s[hh] = jnp.where(vlane == ML_DEN_LANE[hh], 1.0, v).astype(BF16)

    glane = lax.broadcasted_iota(jnp.int32, (nc, L), 1)
    grow = lax.broadcasted_iota(jnp.int32, (nc, L), 0)
    row8 = lax.broadcasted_iota(jnp.int32, (SUBLANE, L), 0)
    for hh in heads:
        ig = ig_ref[0, hh] + bi_ref[hh]
        lf = -_softplus(-(fg_ref[0, hh] + bf_ref[hh]))
        b = lf
        for k in range(int(math.log2(L))):
            sh = 1 << k
            b = b + jnp.where(glane >= sh, pltpu.roll(b, sh, axis=1), 0.0)
        g = jnp.broadcast_to(b[:, L - 1:L], (nc, L))
        a = g - b + ig
        mloc = jnp.broadcast_to(jnp.max(a, axis=1, keepdims=True), (nc, L))
        m = jnp.zeros((1, L), F32)
        m_prev = jnp.zeros((nc, L), F32)
        m_next = jnp.zeros((nc, L), F32)
        for c in range(nc):
            m_prev = jnp.where(grow == c, m, m_prev)
            m = jnp.maximum(g[c:c + 1, :] + m, mloc[c:c + 1, :])
            m_next = jnp.where(grow == c, m, m_next)
        r = ig - b
        cmx = r
        for k in range(int(math.log2(L))):
            sh = 1 << k
            cmx = jnp.maximum(cmx, jnp.where(glane >= sh, pltpu.roll(cmx, sh, axis=1), -jnp.inf))
        mm = jnp.maximum(m_prev, cmx)
        s_int = jnp.exp(m_prev - mm)
        clamp = jnp.exp(-(b + mm))
        r_s[hh] = r
        w_s[hh] = jnp.exp(a - mloc)
        so_s[hh] = jnp.exp(g + m_prev - m_next)
        sn_s[hh] = jnp.exp(mloc - m_next)
        for c in range(nc):
            col_s[hh, c] = jnp.where(row8 == 0, mm[c:c + 1, :],
                                     jnp.where(row8 == 1, s_int[c:c + 1, :],
                                               jnp.where(row8 == 2, clamp[c:c + 1, :], 0.0)))

    zero_rows = jnp.zeros((ML_DHP - ML_DH, ML_PW), BF16)
    c_s[...] = jnp.zeros(c_s.shape, F32)

    def state(c, carry):
        r0 = pl.multiple_of(c * L, L)
        for hh in heads:
            c_prev = c_s[hh]
            cp_s[hh, c] = jnp.concatenate([c_prev.astype(BF16), zero_rows], axis=0)
            ktw = (kt_s[hh, c, :ML_DH, :].astype(F32) * w_s[hh, pl.ds(c, 1), :]).astype(BF16)
            c_loc = jnp.dot(ktw, v_s[hh, pl.ds(r0, L), :], preferred_element_type=F32)
            c_s[hh] = so_s[hh, pl.ds(c, 1), :][:, :1] * c_prev + sn_s[hh, pl.ds(c, 1), :][:, :1] * c_loc
        return carry

    lax.fori_loop(0, nc, state, 0, unroll=2)

    tri = (lax.broadcasted_iota(jnp.int32, (L, L), 0) >= lax.broadcasted_iota(jnp.int32, (L, L), 1))
    first = lax.broadcasted_iota(jnp.int32, (L, ML_PW), 1) < ML_DH
    pad_rows = jnp.zeros((L - SUBLANE, L), F32)

    def chunk(c, carry):
        r0 = pl.multiple_of(c * L, L)
        nds, invs = [], []
        for hh in heads:
            q_c = q_s[hh, pl.ds(r0, L), :]
            cols = jnp.concatenate([col_s[hh, c], pad_rows], axis=0).T
            rb = jnp.broadcast_to(r_s[hh, pl.ds(c, 1), :], (L, L))
            decay = jnp.exp(jnp.where(tri, rb - cols[:, 0:1], -jnp.inf))
            s_mat = jnp.dot(q_c, kt_s[hh, c], preferred_element_type=F32) * decay
            nd = (jnp.dot(s_mat.astype(BF16), v_s[hh, pl.ds(r0, L), :], preferred_element_type=F32)
                  + cols[:, 1:2] * jnp.dot(q_c, cp_s[hh, c], preferred_element_type=F32))
            den = nd[:, ML_DEN_LANE[hh]:ML_DEN_LANE[hh] + 1]
            nds.append(nd)
            invs.append(1.0 / jnp.maximum(jnp.abs(den), cols[:, 2:3]))
        x = _sigmoid(og_ref[0, pl.ds(r0, L), :].astype(F32)) * jnp.where(first, nds[0], nds[1])
        mu0 = jnp.sum(jnp.where(first, x, 0.0), axis=1, keepdims=True) * (1.0 / ML_DH)
        mu1 = jnp.sum(jnp.where(first, 0.0, x), axis=1, keepdims=True) * (1.0 / ML_DH)
        dev = x - jnp.where(first, mu0, mu1)
        sq = dev * dev
        var0 = jnp.sum(jnp.where(first, sq, 0.0), axis=1, keepdims=True) * (1.0 / ML_DH)
        var1 = jnp.sum(jnp.where(first, 0.0, sq), axis=1, keepdims=True) * (1.0 / ML_DH)
        f0 = invs[0] * lax.rsqrt(invs[0] * invs[0] * var0 + LN_EPS)
        f1 = invs[1] * lax.rsqrt(invs[1] * invs[1] * var1 + LN_EPS)
        zs = _silu(z_ref[0, pl.ds(r0, L), :].astype(F32)) * ng_ref[...]
        y_ref[0, pl.ds(r0, L), :] = (dev * jnp.where(first, f0, f1) * zs).astype(y_ref.dtype)
        return carry

    lax.fori_loop(0, nc, chunk, 0, unroll=4)


def _mlstm_branch(p, gates, gbias, cw, cb, wq, wkt, wv, ng, l):
    b, t, _ = p.shape
    nc = t // ML_CHUNK
    ub, ob, zb = C_ML_U // ML_PW, C_ML_O // ML_PW, C_ML_Z // ML_PW
    seq = lambda base: pl.BlockSpec((1, t, ML_PW), lambda pr, bi: (bi, 0, base + pr))
    vec = pl.BlockSpec((None, 1, ML_PW), lambda pr, bi: (l, 0, pr))
    taps = pl.BlockSpec((None, ML_QKV_BLOCK, ML_PW), lambda pr, bi: (l, 0, pr))
    rows = lambda: pltpu.VMEM((2, nc, ML_CHUNK), F32)
    return pl.pallas_call(
        _mlstm_kernel,
        grid=(ML_PAIRS, b),
        in_specs=[
            seq(ub), seq(ob), seq(zb),
            pl.BlockSpec((1, 2, nc, ML_CHUNK), lambda pr, bi: (bi, pr, 0, 0)),
            pl.BlockSpec((1, 2, nc, ML_CHUNK), lambda pr, bi: (bi, ML_PAIRS + pr, 0, 0)),
            pl.BlockSpec((None, 2, 1, ML_CHUNK), lambda pr, bi: (l, pr, 0, 0)),
            pl.BlockSpec((None, 2, 1, ML_CHUNK), lambda pr, bi: (l, ML_PAIRS + pr, 0, 0)),
            pl.BlockSpec((None, cw.shape[1], ML_PW), lambda pr, bi: (l, 0, pr)),
            vec, taps, taps, taps, vec,
        ],
        out_specs=pl.BlockSpec((1, t, ML_PW), lambda pr, bi: (bi, 0, pr)),
        out_shape=jax.ShapeDtypeStruct((b, t, ML_W), BF16),
        scratch_shapes=[
            pltpu.VMEM((2, ML_PW, ML_DHP), BF16),
            pltpu.VMEM((2, ML_DHP, ML_PW), BF16),
            pltpu.VMEM((2, ML_PW, ML_PW), BF16),
            pltpu.VMEM((2, t, ML_DHP), BF16),
            pltpu.VMEM((2, nc, ML_DHP, ML_CHUNK), BF16),
            pltpu.VMEM((2, t, ML_PW), BF16),
            pltpu.VMEM((2, nc, ML_DHP, ML_PW), BF16),
            pltpu.VMEM((2, ML_DH, ML_PW), F32),
            rows(), rows(),
            pltpu.VMEM((2, nc, SUBLANE, ML_CHUNK), F32),
            rows(), rows(),
        ],
        compiler_params=pltpu.CompilerParams(
            dimension_semantics=("arbitrary", "arbitrary"), vmem_limit_bytes=_vmem_limit(56 << 20)),
        name="mlstm",
    )(p, p, p, gates, gates, gbias, gbias, cw, cb, wq, wkt, wv, ng)


MOBA_SEL_LANE = 6


def _moba_key_ext(nb):
    ext = np.zeros((nb, MOBA_BLOCK, MO_DH), np.float32)
    for n in range(nb):
        ext[n, :, 0:3] = n * MOBA_BLOCK
        ext[n, :, 3:6] = np.arange(MOBA_BLOCK, dtype=np.float32)[:, None]
        ext[n, :, MOBA_SEL_LANE + n] = 1.0
    return jnp.asarray(ext, dtype=BF16)


def _moba_slope_rows(slopes):
    rows = np.zeros((len(slopes), 1, MO_DH), np.float32)
    for h, s in enumerate(slopes):
        rest = np.float32(np.float32(s) * np.float32(LOG2E))
        for i in range(3):
            piece = np.float32(rest).astype(BF16).astype(np.float32)
            rows[h, 0, i] = rows[h, 0, 3 + i] = piece
            rest = np.float32(rest - piece)
    return jnp.asarray(rows)


def _moba_kernel(q_ref, k_ref, v_ref, z_ref, srow_ref, kext_ref, o_ref, l_s, qa_s):
    t = q_ref.shape[1]
    nb = t // MOBA_BLOCK
    bs = MOBA_BLOCK
    qscale = (MO_DH ** -0.5) * LOG2E

    krow = lax.broadcasted_iota(jnp.int32, (LANE, MO_DH), 0)
    kmean = jnp.zeros((LANE, MO_DH), F32)
    for n in range(nb):
        mean_n = jnp.sum(k_ref[0, n * bs:(n + 1) * bs, :].astype(F32), axis=0, keepdims=True) * (1.0 / bs)
        kmean = jnp.where(krow == MOBA_SEL_LANE + n, mean_n, kmean)
    kmean = kmean.astype(BF16)

    lane = lax.broadcasted_iota(jnp.int32, (bs, MO_DH), 1)
    causal = (lax.broadcasted_iota(jnp.int32, (bs, bs), 0) >= lax.broadcasted_iota(jnp.int32, (bs, bs), 1))
    ones_col = jnp.where(lane == 0, 1.0, 0.0).astype(BF16)
    slope_cols = jnp.where(lane < MOBA_SEL_LANE, srow_ref[0], 0.0)

    def prepare(qb):
        q_b = q_ref[0, qb * bs:(qb + 1) * bs, :]
        q_ext = slope_cols
        if qb > MOBA_TOPK:
            gate = lax.dot_general(q_b, kmean, NT_DIMS, preferred_element_type=F32)
            beaten = jnp.zeros((bs, MO_DH), F32)
            for m in range(qb):
                gm = gate[:, MOBA_SEL_LANE + m:MOBA_SEL_LANE + m + 1]
                wins = (gm > gate) | ((gm == gate) & (lane > MOBA_SEL_LANE + m))
                beaten = beaten + jnp.where(wins, 1.0, 0.0)
            past = (lane >= MOBA_SEL_LANE) & (lane < MOBA_SEL_LANE + qb)
            q_ext = jnp.where(past & (beaten >= float(MOBA_TOPK)), NEG_INF, slope_cols)
        qa_s[qb] = jnp.concatenate([(q_b.astype(F32) * qscale).astype(BF16), q_ext.astype(BF16)], axis=1)

    def scores(qb):
        q_aug = qa_s[qb]
        l_q = l_s.at[qb % 2]
        mx = None
        for n in range(qb + 1):
            k_aug = jnp.concatenate([k_ref[0, n * bs:(n + 1) * bs, :], kext_ref[n]], axis=1)
            logit = lax.dot_general(q_aug, k_aug, NT_DIMS, preferred_element_type=F32)
            if n == qb:
                logit = jnp.where(causal, logit, NEG_INF)
            l_q[:, n * bs:(n + 1) * bs] = logit
            half = jnp.maximum(logit[:, :LANE], logit[:, LANE:])
            mx = half if mx is None else jnp.maximum(mx, half)
            yield None
        yield jnp.max(mx, axis=1, keepdims=True)

    def outputs(qb, m_row):
        l_q = l_s.at[qb % 2]
        acc = jnp.zeros((bs, 2 * MO_DH), F32)
        for n in range(qb + 1):
            pr = jnp.exp2(l_q[:, n * bs:(n + 1) * bs] - m_row).astype(BF16)
            v_aug = jnp.concatenate([v_ref[0, n * bs:(n + 1) * bs, :], ones_col], axis=1)
            acc = acc + jnp.dot(pr, v_aug, preferred_element_type=F32)
            yield None
        z = z_ref[0, qb * bs:(qb + 1) * bs, :].astype(F32)
        inv = 1.0 / acc[:, MO_DH:MO_DH + 1]
        o_ref[0, qb * bs:(qb + 1) * bs, :] = (acc[:, :MO_DH] * (inv * _silu(z))).astype(o_ref.dtype)

    prepare(0)
    prepare(1)
    m_row = list(scores(0))[-1]
    for qb in range(nb):
        if qb + 2 < nb:
            prepare(qb + 2)
        m_next = list(scores(qb + 1))[-1] if qb + 1 < nb else None
        list(outputs(qb, m_row))
        m_row = m_next


def _moba_branch(p, srows, kext):
    b, t, _ = p.shape
    nb = t // MOBA_BLOCK
    qb, kb, vb, zb = (c // MO_DH for c in (C_MO_Q, C_MO_K, C_MO_V, C_MO_Z))
    seq = lambda base: pl.BlockSpec((1, t, MO_DH), lambda bi, h: (bi, 0, base + h))
    return pl.pallas_call(
        _moba_kernel,
        grid=(b, MO_HEADS),
        in_specs=[seq(qb), seq(kb), seq(vb), seq(zb),
                  pl.BlockSpec((1, 1, MO_DH), lambda bi, h: (h, 0, 0)),
                  pl.BlockSpec((nb, MOBA_BLOCK, MO_DH), lambda bi, h: (0, 0, 0))],
        out_specs=pl.BlockSpec((1, t, MO_DH), lambda bi, h: (bi, 0, h)),
        out_shape=jax.ShapeDtypeStruct((b, t, MO_W), BF16),
        scratch_shapes=[pltpu.VMEM((2, MOBA_BLOCK, t), F32), pltpu.VMEM((nb, MOBA_BLOCK, 2 * MO_DH), BF16)],
        compiler_params=pltpu.CompilerParams(
            dimension_semantics=("arbitrary", "arbitrary"), vmem_limit_bytes=_vmem_limit(32 << 20)),
        name="moba",
    )(p, p, p, p, srows, kext)


def _out_proj_kernel(x_ref, yl_ref, ym_ref, yo_ref, w_ref, o_ref):
    acc = jnp.dot(yl_ref[...], w_ref[0:LRU_W, :], preferred_element_type=F32)
    acc = acc + jnp.dot(ym_ref[...], w_ref[LRU_W:LRU_W + ML_W, :], preferred_element_type=F32)
    acc = acc + jnp.dot(yo_ref[...], w_ref[LRU_W + ML_W:, :], preferred_element_type=F32)
    o_ref[...] = x_ref[...] + acc


def _out_proj(x2d, yl, ym, yo, w, l):
    m = x2d.shape[0]
    kdim = w.shape[1]
    rows = lambda width: pl.BlockSpec((OUT_TM, width), lambda i: (i, 0))
    vmem = 4 * OUT_TM * D_MODEL * 4 + 2 * kdim * D_MODEL * 2 + 2 * OUT_TM * kdim * 2 + 2 * OUT_TM * D_MODEL * 4
    return pl.pallas_call(
        _out_proj_kernel,
        grid=(m // OUT_TM,),
        in_specs=[rows(D_MODEL), rows(LRU_W), rows(ML_W), rows(MO_W),
                  pl.BlockSpec((None, kdim, D_MODEL), lambda i: (l, 0, 0))],
        out_specs=rows(D_MODEL),
        out_shape=jax.ShapeDtypeStruct((m, D_MODEL), F32),
        compiler_params=pltpu.CompilerParams(
            dimension_semantics=("arbitrary",), vmem_limit_bytes=_vmem_limit(vmem)),
        name="out_proj",
    )(x2d, yl, ym, yo, w)


def _mem_kv_kernel(mem_ref, g_ref, w_ref, o_ref):
    hm = _rmsnorm_rows(mem_ref[0], g_ref[...]).astype(BF16)
    o_ref[0] = jnp.dot(hm, w_ref[...], preferred_element_type=F32).astype(o_ref.dtype)


def _mem_kv(mem, g, wkv, l):
    b, m, _ = mem.shape
    return pl.pallas_call(
        _mem_kv_kernel,
        grid=(b,),
        in_specs=[pl.BlockSpec((1, m, D_MODEL), lambda bi: (bi, 0, 0)),
                  pl.BlockSpec((None, 1, D_MODEL), lambda bi: (l, 0, 0)),
                  pl.BlockSpec((None, D_MODEL, 2 * XA_W), lambda bi: (l, 0, 0))],
        out_specs=pl.BlockSpec((1, m, 2 * XA_W), lambda bi: (bi, 0, 0)),
        out_shape=jax.ShapeDtypeStruct((b, m, 2 * XA_W), BF16),
        compiler_params=pltpu.CompilerParams(
            dimension_semantics=("arbitrary",), vmem_limit_bytes=_vmem_limit(32 << 20)),
        name="mem_kv",
    )(mem, g, wkv)


def _xattn_kernel(x_ref, g_ref, wq_ref, kv_ref, wo_ref, fg_ref, o_ref, *, final_norm):
    x = x_ref[0]
    hx = _rmsnorm_rows(x, g_ref[...]).astype(BF16)
    q = jnp.dot(hx, wq_ref[...], preferred_element_type=F32).astype(BF16)
    heads = []
    for h in range(XA_HEADS):
        q_h = q[:, h * XA_DH:(h + 1) * XA_DH]
        k_h = kv_ref[0, :, h * XA_DH:(h + 1) * XA_DH]
        v_h = kv_ref[0, :, XA_W + h * XA_DH:XA_W + (h + 1) * XA_DH]
        s = lax.dot_general(q_h, k_h, NT_DIMS, preferred_element_type=F32) * (XA_DH ** -0.5)
        e = jnp.exp(s - jnp.max(s, axis=1, keepdims=True))
        pr = e / jnp.sum(e, axis=1, keepdims=True)
        heads.append(jnp.dot(pr.astype(BF16), v_h, preferred_element_type=F32).astype(BF16))
    o = jnp.concatenate(heads, axis=1)
    y = x + jnp.dot(o, wo_ref[...], preferred_element_type=F32)
    if final_norm:
        y = _rmsnorm_rows(y, fg_ref[...])
    o_ref[0] = y


def _xattn(x, g, wq, kv, wo, fg, l, final_norm):
    b, t, _ = x.shape
    m = kv.shape[1]
    return pl.pallas_call(
        functools.partial(_xattn_kernel, final_norm=final_norm),
        grid=(b, t // XA_TM),
        in_specs=[pl.BlockSpec((1, XA_TM, D_MODEL), lambda bi, i: (bi, i, 0)),
                  pl.BlockSpec((None, 1, D_MODEL), lambda bi, i: (l, 0, 0)),
                  pl.BlockSpec((None, D_MODEL, XA_W), lambda bi, i: (l, 0, 0)),
                  pl.BlockSpec((1, m, 2 * XA_W), lambda bi, i: (bi, 0, 0)),
                  pl.BlockSpec((None, XA_W, D_MODEL), lambda bi, i: (l, 0, 0)),
                  pl.BlockSpec((1, D_MODEL), lambda bi, i: (0, 0))],
        out_specs=pl.BlockSpec((1, XA_TM, D_MODEL), lambda bi, i: (bi, i, 0)),
        out_shape=jax.ShapeDtypeStruct((b, t, D_MODEL), F32),
        compiler_params=pltpu.CompilerParams(
            dimension_semantics=("arbitrary", "arbitrary"), vmem_limit_bytes=_vmem_limit(40 << 20)),
        name="mem_xattn",
    )(x, g, wq, kv, wo, fg)


def _alibi_slopes(n):
    def pow2(m):
        start = 2.0 ** (-8.0 / m)
        return [start ** (i + 1) for i in range(m)]
    if math.log2(n).is_integer():
        s = pow2(n)
    else:
        c = 2 ** int(math.floor(math.log2(n)))
        s = pow2(c) + pow2(2 * c)[0::2][:n - c]
    return np.asarray(s, dtype=np.float32)


def _taps(w, perm):
    depth = w.shape[0]
    return jnp.transpose(w, perm).reshape(depth, ML_QKV_BLOCK, ML_W)


def kernel(x, mem, mix_norm_g, w_in, lru_conv_w, lru_conv_b, lru_wa, lru_ba, lru_wx, lru_bx, lru_lambda,
           ml_conv_w, ml_conv_b, ml_wq, ml_wk, ml_wv, ml_bi, ml_bf, ml_norm_g, w_out, xa_norm_g, mem_norm_g,
           xa_wq, xa_wkv, xa_wo, final_norm_g):
    b, t, d = x.shape
    depth = w_in.shape[0]
    nc = t // ML_CHUNK
    row = lambda v: v[:, None, :]
    moba_srows = _moba_slope_rows(_alibi_slopes(MO_HEADS))
    moba_kext = _moba_key_ext(t // MOBA_BLOCK)
    w_in_p = _w_in_prep(jnp.swapaxes(w_in, 1, 2))
    w_out_b = w_out.astype(BF16)
    wq_b, wkv_b, wo_b = xa_wq.astype(BF16), xa_wkv.astype(BF16), xa_wo.astype(BF16)
    lru_wax = jnp.concatenate([lru_wa, lru_wx], axis=-1).astype(BF16)
    ml_wq_t = _taps(ml_wq, (0, 2, 1, 3))
    ml_wkt_t = _taps(ml_wk, (0, 3, 1, 2))
    ml_wv_t = _taps(ml_wv, (0, 2, 1, 3))
    ml_gbias = jnp.broadcast_to(jnp.concatenate([ml_bi, ml_bf], axis=1)[:, :, None, None],
                                (depth, 2 * ML_HEADS, 1, ML_CHUNK))
    mix_g, xa_g, mem_g = row(mix_norm_g), row(xa_norm_g), row(mem_norm_g)
    lru_cb, lru_ba_r, lru_bx_r, lru_lam = row(lru_conv_b), row(lru_ba), row(lru_bx), row(lru_lambda)
    ml_cb, ml_ng = row(ml_conv_b), row(ml_norm_g)
    for l in range(depth):
        p = _in_proj(x.reshape(b * t, d), mix_g, w_in_p, l).reshape(b, t, IN_COLS_P)
        gates = p[:, :, C_GATE:C_GATE + 2 * ML_HEADS].astype(F32)
        gates = jnp.swapaxes(gates, 1, 2).reshape(b, 2 * ML_HEADS, nc, ML_CHUNK)
        y_lru = _lru_branch(p, lru_conv_w, lru_cb, lru_wax, lru_ba_r, lru_bx_r, lru_lam, l)
        y_ml = _mlstm_branch(p, gates, ml_gbias, ml_conv_w, ml_cb, ml_wq_t, ml_wkt_t, ml_wv_t, ml_ng, l)
        y_mo = _moba_branch(p, moba_srows, moba_kext)
        x1 = _out_proj(x.reshape(b * t, d), y_lru.reshape(b * t, LRU_W), y_ml.reshape(b * t, ML_W),
                       y_mo.reshape(b * t, MO_W), w_out_b, l).reshape(b, t, d)
        kv = _mem_kv(mem, mem_g, wkv_b, l)
        x = _xattn(x1, xa_g, wq_b, kv, wo_b, final_norm_g[None, :], l, final_norm=(l == depth - 1))
    return x
```

```python
import functools
import math

import jax
import jax.numpy as jnp
import numpy as np
from jax import lax
from jax.experimental import pallas as pl
from jax.experimental.pallas import tpu as pltpu

LANE = 128
SUBLANE = 8
V7X_VMEM_BYTES = 64 * 1024 * 1024

D_MODEL = 2048
LRU_W = 512
LRU_BLOCKS = 4
LRU_BW = LRU_W // LRU_BLOCKS
LRU_C = 8.0
ML_W = 768
ML_HEADS = 4
ML_DH = 192
ML_DHP = 256
ML_PAIRS = 2
ML_PW = 2 * ML_DH
ML_QKV_BLOCK = 4
ML_QKV_SHIFT = 2
ML_CHUNK = 128
MO_W = 768
MO_HEADS = 6
MO_DH = 128
MOBA_BLOCK = 256
MOBA_TOPK = 3
XA_HEADS = 4
XA_DH = 128
XA_W = XA_HEADS * XA_DH
RMS_EPS = 1e-6
LN_EPS = 1e-5
NEG_INF = -1e30
LOG2E = 1.4426950408889634

REF_LRU_END = 2 * LRU_W
REF_ML_END = REF_LRU_END + 3 * ML_W
REF_GATE_END = REF_ML_END + 2 * ML_HEADS

C_ML_U = 0
C_ML_O = C_ML_U + ML_W
C_ML_Z = C_ML_O + ML_W
C_LRU_X = C_ML_Z + ML_W
C_LRU_Z = C_LRU_X + LRU_W
C_MO_Q = C_LRU_Z + LRU_W
C_MO_K = C_MO_Q + MO_W
C_MO_V = C_MO_K + MO_W
C_MO_Z = C_MO_V + MO_W
C_GATE = C_MO_Z + MO_W
IN_TM = 512
IN_TN = 3328
IN_COLS_P = 6656
PREP_TR = 256
OUT_TM = 512
XA_TM = 512

BF16 = jnp.bfloat16
F32 = jnp.float32
NT_DIMS = (((1,), (1,)), ((), ()))


def _vmem_limit(nbytes):
    return int(min(V7X_VMEM_BYTES - (4 << 20), max(32 << 20, nbytes)))


def _rmsnorm_rows(x, g):
    ms = jnp.mean(x * x, axis=-1, keepdims=True)
    return x * lax.rsqrt(ms + RMS_EPS) * g


def _sigmoid(x):
    return jax.nn.sigmoid(x)


def _silu(x):
    return x * jax.nn.sigmoid(x)


def _softplus(x):
    return jnp.maximum(x, 0.0) + jnp.log1p(jnp.exp(-jnp.abs(x)))


def _shift_rows(x, s):
    rolled = pltpu.roll(x, s, axis=0)
    row = lax.broadcasted_iota(jnp.int32, x.shape, 0)
    return jnp.where(row >= s, rolled, 0.0)


def _causal_conv(x, w_ref, b_ref):
    k = w_ref.shape[0]
    acc = x * w_ref[k - 1:k, :]
    head = x[:SUBLANE]
    acc_head = head * w_ref[k - 1:k, :]
    for j in range(k - 1):
        acc = acc + pltpu.roll(x, k - 1 - j, axis=0) * w_ref[j:j + 1, :]
        acc_head = acc_head + _shift_rows(head, k - 1 - j) * w_ref[j:j + 1, :]
    return jnp.concatenate([acc_head, acc[SUBLANE:]], axis=0) + b_ref[...]


def _w_in_prep_kernel(w_ref, o_ref):
    last = pl.program_id(1) == pl.num_programs(1) - 1
    valid = jnp.where(last, 2 * ML_HEADS, PREP_TR)
    row = lax.broadcasted_iota(jnp.int32, w_ref.shape, 0)
    o_ref[...] = jnp.where(row < valid, w_ref[...], 0.0).astype(BF16)


def _w_in_prep_src_row(i):
    n_ml = (C_LRU_X - C_ML_U) // PREP_TR
    n_lru = (C_MO_Q - C_LRU_X) // PREP_TR
    n_mo = (C_GATE - C_MO_Q) // PREP_TR
    tile, unit = PREP_TR // SUBLANE, SUBLANE
    ml = REF_LRU_END // unit + i * tile
    lru = (i - n_ml) * tile
    mo = REF_GATE_END // unit + (i - n_ml - n_lru) * tile
    gate = REF_ML_END // unit
    return unit * jnp.where(i < n_ml, ml, jnp.where(i < n_ml + n_lru, lru, jnp.where(i < n_ml + n_lru + n_mo, mo, gate)))


def _w_in_prep(w_in_t):
    depth, _, k = w_in_t.shape
    return pl.pallas_call(
        _w_in_prep_kernel,
        grid=(depth, IN_COLS_P // PREP_TR),
        in_specs=[pl.BlockSpec((pl.Squeezed(), pl.Element(PREP_TR), pl.Element(k)),
                               lambda l, i: (l, _w_in_prep_src_row(i), 0))],
        out_specs=pl.BlockSpec((None, PREP_TR, k), lambda l, i: (l, i, 0)),
        out_shape=jax.ShapeDtypeStruct((depth, IN_COLS_P, k), BF16),
        compiler_params=pltpu.CompilerParams(
            dimension_semantics=("arbitrary", "arbitrary"), vmem_limit_bytes=_vmem_limit(32 << 20)),
        name="w_in_prep",
    )(w_in_t)


def _in_proj_kernel(x_ref, g_ref, w_ref, o_ref, xn_ref):
    @pl.when(pl.program_id(1) == 0)
    def _():
        xn_ref[...] = _rmsnorm_rows(x_ref[...], g_ref[...]).astype(BF16)

    o_ref[...] = lax.dot_general(xn_ref[...], w_ref[...], NT_DIMS, preferred_element_type=F32).astype(o_ref.dtype)


def _in_proj(x2d, g, w, l):
    m = x2d.shape[0]
    n = w.shape[1]
    vmem = (2 * IN_TM * D_MODEL * 4 + IN_TM * D_MODEL * 2 + 2 * D_MODEL * IN_TN * 2 + 2 * IN_TM * IN_TN * 2
            + 2 * IN_TM * IN_TN * 4)
    return pl.pallas_call(
        _in_proj_kernel,
        grid=(m // IN_TM, n // IN_TN),
        in_specs=[
            pl.BlockSpec((IN_TM, D_MODEL), lambda i, j: (i, 0)),
            pl.BlockSpec((None, 1, D_MODEL), lambda i, j: (l, 0, 0)),
            pl.BlockSpec((None, IN_TN, D_MODEL), lambda i, j: (l, j, 0)),
        ],
        out_specs=pl.BlockSpec((IN_TM, IN_TN), lambda i, j: (i, j)),
        out_shape=jax.ShapeDtypeStruct((m, n), BF16),
        scratch_shapes=[pltpu.VMEM((IN_TM, D_MODEL), BF16)],
        compiler_params=pltpu.CompilerParams(
            dimension_semantics=("arbitrary", "arbitrary"), vmem_limit_bytes=_vmem_limit(vmem)),
        name="in_proj",
    )(x2d, g, w)


def _lru_kernel(x_ref, z_ref, cw_ref, cb_ref, wax_ref, ba_ref, bx_ref, lam_ref, o_ref, a_s, u_s):
    t = x_ref.shape[1]
    x = x_ref[0].astype(F32)
    xc = _causal_conv(x, cw_ref, cb_ref)
    pre = jnp.dot(xc.astype(BF16), wax_ref[0], preferred_element_type=F32)
    r = _sigmoid(pre[:, :LRU_BW] + ba_ref[...])
    i = _sigmoid(pre[:, LRU_BW:] + bx_ref[...])
    log_a = (-LRU_C) * r * _softplus(-lam_ref[...])
    a = jnp.exp(log_a)
    a_s[...] = a
    u_s[...] = jnp.sqrt(-jnp.tanh(log_a) * (1.0 + a * a)) * (i * xc)

    row = lax.broadcasted_iota(jnp.int32, (SUBLANE, LRU_BW), 0)

    def block(blk, h_prev):
        r0 = pl.multiple_of(blk * SUBLANE, SUBLANE)
        a_b = a_s[pl.ds(r0, SUBLANE), :]
        u_b = u_s[pl.ds(r0, SUBLANE), :]
        for s in (1, 2, 4):
            a_sh = jnp.where(row >= s, pltpu.roll(a_b, s, axis=0), 1.0)
            u_sh = jnp.where(row >= s, pltpu.roll(u_b, s, axis=0), 0.0)
            u_b = a_b * u_sh + u_b
            a_b = a_b * a_sh
        h = a_b * h_prev + u_b
        u_s[pl.ds(r0, SUBLANE), :] = h
        return jnp.broadcast_to(h[SUBLANE - 1:SUBLANE, :], (SUBLANE, LRU_BW))

    lax.fori_loop(0, t // SUBLANE, block, jnp.zeros((SUBLANE, LRU_BW), F32), unroll=4)
    z = z_ref[0].astype(F32)
    o_ref[0] = (u_s[...] * _silu(z)).astype(o_ref.dtype)


def _lru_branch(p, cw, cb, wax, ba, bx, lam, l):
    b, t, _ = p.shape
    xb, zb = C_LRU_X // LRU_BW, C_LRU_Z // LRU_BW
    vec = pl.BlockSpec((None, 1, LRU_BW), lambda bi, g: (l, 0, g))
    return pl.pallas_call(
        _lru_kernel,
        grid=(b, LRU_BLOCKS),
        in_specs=[
            pl.BlockSpec((1, t, LRU_BW), lambda bi, g: (bi, 0, xb + g)),
            pl.BlockSpec((1, t, LRU_BW), lambda bi, g: (bi, 0, zb + g)),
            pl.BlockSpec((None, cw.shape[1], LRU_BW), lambda bi, g: (l, 0, g)),
            vec,
            pl.BlockSpec((None, 1, LRU_BW, 2 * LRU_BW), lambda bi, g: (l, g, 0, 0)),
            vec, vec, vec,
        ],
        out_specs=pl.BlockSpec((1, t, LRU_BW), lambda bi, g: (bi, 0, g)),
        out_shape=jax.ShapeDtypeStruct((b, t, LRU_W), BF16),
        scratch_shapes=[pltpu.VMEM((t, LRU_BW), F32), pltpu.VMEM((t, LRU_BW), F32)],
        compiler_params=pltpu.CompilerParams(
            dimension_semantics=("arbitrary", "arbitrary"), vmem_limit_bytes=_vmem_limit(24 * t * LRU_BW * 4)),
        name="rg_lru",
    )(p, p, cw, cb, wax, ba, bx, lam)


def _blockdiag_in_out(w_ref, hh):
    wh = w_ref[...][:, hh * ML_DH:(hh + 1) * ML_DH]
    wh = jnp.concatenate([wh, jnp.zeros((ML_QKV_BLOCK, ML_DHP - ML_DH), F32)], axis=1)
    r = lax.broadcasted_iota(jnp.int32, (ML_PW, ML_DHP), 0)
    c = lax.broadcasted_iota(jnp.int32, (ML_PW, ML_DHP), 1)
    d = jnp.zeros((ML_PW, ML_DHP), F32)
    for i in range(ML_QKV_BLOCK):
        d = jnp.where((r & (ML_QKV_BLOCK - 1)) == i, wh[i:i + 1, :], d)
    keep = (((r >> ML_QKV_SHIFT) - hh * (ML_DH // ML_QKV_BLOCK)) == (c >> ML_QKV_SHIFT)) & (c < ML_DH)
    return jnp.where(keep, d, 0.0)


def _blockdiag_in_pair(w_ref, hh):
    wh = w_ref[...]
    r = lax.broadcasted_iota(jnp.int32, (ML_PW, ML_PW), 0)
    c = lax.broadcasted_iota(jnp.int32, (ML_PW, ML_PW), 1)
    d = jnp.zeros((ML_PW, ML_PW), F32)
    for i in range(ML_QKV_BLOCK):
        d = jnp.where((r & (ML_QKV_BLOCK - 1)) == i, wh[i:i + 1, :], d)
    keep = ((r >> ML_QKV_SHIFT) == (c >> ML_QKV_SHIFT)) & (c >= hh * ML_DH) & (c < (hh + 1) * ML_DH)
    return jnp.where(keep, d, 0.0)


def _blockdiag_out_in(w_ref, hh):
    wk = w_ref[...]
    o = lax.broadcasted_iota(jnp.int32, (ML_DHP, ML_PW), 0)
    r = lax.broadcasted_iota(jnp.int32, (ML_DHP, ML_PW), 1)
    d = jnp.zeros((ML_DHP, ML_PW), F32)
    for j in range(ML_QKV_BLOCK):
        d = jnp.where((o & (ML_QKV_BLOCK - 1)) == j, wk[j:j + 1, :], d)
    keep = ((o >> ML_QKV_SHIFT) == ((r >> ML_QKV_SHIFT) - hh * (ML_DH // ML_QKV_BLOCK))) & (o < ML_DH)
    return jnp.where(keep, d, 0.0)


ML_DEN_LANE = (ML_DH, 0)


def _mlstm_kernel(u_ref, og_ref, z_ref, ig_ref, fg_ref, bi_ref, bf_ref, cw_ref, cb_ref,
                  wq_ref, wkt_ref, wv_ref, ng_ref, y_ref,
                  dq_s, dkt_s, dv_s, q_s, kt_s, v_s, cp_s, c_s, r_s, w_s, col_s, so_s, sn_s):
    t = u_ref.shape[1]
    nc = t // ML_CHUNK
    L = ML_CHUNK
    heads = range(2)

    @pl.when(pl.program_id(1) == 0)
    def _():
        for hh in heads:
            dq_s[hh] = _blockdiag_in_out(wq_ref, hh).astype(BF16)
            dkt_s[hh] = _blockdiag_out_in(wkt_ref, hh).astype(BF16)
            dv_s[hh] = _blockdiag_in_pair(wv_ref, hh).astype(BF16)

    u = u_ref[0]
    uc = _silu(_causal_conv(u.astype(F32), cw_ref, cb_ref)).astype(BF16)
    for hh in heads:
        w0 = hh * (ML_PW - ML_DHP)
        uc_w, u_w = uc[:, w0:w0 + ML_DHP], u[:, w0:w0 + ML_DHP]
        q_s[hh] = jnp.dot(uc_w, dq_s[hh, w0:w0 + ML_DHP, :], preferred_element_type=F32).astype(BF16)
        kt = lax.dot_general(dkt_s[hh, :, w0:w0 + ML_DHP], uc_w, NT_DIMS, preferred_element_type=F32)
        kt = kt * (ML_DH ** -0.5)
        for c in range(nc):
            kt_s[hh, c] = kt[:, c * L:(c + 1) * L].astype(BF16)
        v = jnp.dot(u_w, dv_s[hh, w0:w0 + ML_DHP, :], preferred_element_type=F32)
        vlane = lax.broadcasted_iota(jnp.int32, v.shape, 1)
        v_s[hh] = jnp.where(vlane == ML_DEN_LANE[hh], 1.0, v).astype(BF16)

    glane = lax.broadcasted_iota(jnp.int32, (nc, L), 1)
    grow = lax.broadcasted_iota(jnp.int32, (nc, L), 0)
    row8 = lax.broadcasted_iota(jnp.int32, (SUBLANE, L), 0)
    for hh in heads:
        ig = ig_ref[0, hh] + bi_ref[hh]
        lf = -_softplus(-(fg_ref[0, hh] + bf_ref[hh]))
        b = lf
        for k in range(int(math.log2(L))):
            sh = 1 << k
            b = b + jnp.where(glane >= sh, pltpu.roll(b, sh, axis=1), 0.0)
        g = jnp.broadcast_to(b[:, L - 1:L], (nc, L))
        a = g - b + ig
        mloc = jnp.broadcast_to(jnp.max(a, axis=1, keepdims=True), (nc, L))
        m = jnp.zeros((1, L), F32)
        m_prev = jnp.zeros((nc, L), F32)
        m_next = jnp.zeros((nc, L), F32)
        for c in range(nc):
            m_prev = jnp.where(grow == c, m, m_prev)
            m = jnp.maximum(g[c:c + 1, :] + m, mloc[c:c + 1, :])
            m_next = jnp.where(grow == c, m, m_next)
        r = ig - b
        cmx = r
        for k in range(int(math.log2(L))):
            sh = 1 << k
            cmx = jnp.maximum(cmx, jnp.where(glane >= sh, pltpu.roll(cmx, sh, axis=1), -jnp.inf))
        mm = jnp.maximum(m_prev, cmx)
        s_int = jnp.exp(m_prev - mm)
        clamp = jnp.exp(-(b + mm))
        r_s[hh] = r
        w_s[hh] = jnp.exp(a - mloc)
        so_s[hh] = jnp.exp(g + m_prev - m_next)
        sn_s[hh] = jnp.exp(mloc - m_next)
        for c in range(nc):
            col_s[hh, c] = jnp.where(row8 == 0, mm[c:c + 1, :],
                                     jnp.where(row8 == 1, s_int[c:c + 1, :],
                                               jnp.where(row8 == 2, clamp[c:c + 1, :], 0.0)))

    zero_rows = jnp.zeros((ML_DHP - ML_DH, ML_PW), BF16)
    c_s[...] = jnp.zeros(c_s.shape, F32)

    def state(c, carry):
        r0 = pl.multiple_of(c * L, L)
        for hh in heads:
            c_prev = c_s[hh]
            cp_s[hh, c] = jnp.concatenate([c_prev.astype(BF16), zero_rows], axis=0)
            ktw = (kt_s[hh, c, :ML_DH, :].astype(F32) * w_s[hh, pl.ds(c, 1), :]).astype(BF16)
            c_loc = jnp.dot(ktw, v_s[hh, pl.ds(r0, L), :], preferred_element_type=F32)
            c_s[hh] = so_s[hh, pl.ds(c, 1), :][:, :1] * c_prev + sn_s[hh, pl.ds(c, 1), :][:, :1] * c_loc
        return carry

    lax.fori_loop(0, nc, state, 0, unroll=2)

    tri = (lax.broadcasted_iota(jnp.int32, (L, L), 0) >= lax.broadcasted_iota(jnp.int32, (L, L), 1))
    first = lax.broadcasted_iota(jnp.int32, (L, ML_PW), 1) < ML_DH
    pad_rows = jnp.zeros((L - SUBLANE, L), F32)

    def chunk(c, carry):
        r0 = pl.multiple_of(c * L, L)
        nds, invs = [], []
        for hh in heads:
            q_c = q_s[hh, pl.ds(r0, L), :]
            cols = jnp.concatenate([col_s[hh, c], pad_rows], axis=0).T
            rb = jnp.broadcast_to(r_s[hh, pl.ds(c, 1), :], (L, L))
            decay = jnp.exp(jnp.where(tri, rb - cols[:, 0:1], -jnp.inf))
            s_mat = jnp.dot(q_c, kt_s[hh, c], preferred_element_type=F32) * decay
            nd = (jnp.dot(s_mat.astype(BF16), v_s[hh, pl.ds(r0, L), :], preferred_element_type=F32)
                  + cols[:, 1:2] * jnp.dot(q_c, cp_s[hh, c], preferred_element_type=F32))
            den = nd[:, ML_DEN_LANE[hh]:ML_DEN_LANE[hh] + 1]
            nds.append(nd)
            invs.append(1.0 / jnp.maximum(jnp.abs(den), cols[:, 2:3]))
        x = _sigmoid(og_ref[0, pl.ds(r0, L), :].astype(F32)) * jnp.where(first, nds[0], nds[1])
        mu0 = jnp.sum(jnp.where(first, x, 0.0), axis=1, keepdims=True) * (1.0 / ML_DH)
        mu1 = jnp.sum(jnp.where(first, 0.0, x), axis=1, keepdims=True) * (1.0 / ML_DH)
        dev = x - jnp.where(first, mu0, mu1)
        sq = dev * dev
        var0 = jnp.sum(jnp.where(first, sq, 0.0), axis=1, keepdims=True) * (1.0 / ML_DH)
        var1 = jnp.sum(jnp.where(first, 0.0, sq), axis=1, keepdims=True) * (1.0 / ML_DH)
        f0 = invs[0] * lax.rsqrt(invs[0] * invs[0] * var0 + LN_EPS)
        f1 = invs[1] * lax.rsqrt(invs[1] * invs[1] * var1 + LN_EPS)
        zs = _silu(z_ref[0, pl.ds(r0, L), :].astype(F32)) * ng_ref[...]
        y_ref[0, pl.ds(r0, L), :] = (dev * jnp.where(first, f0, f1) * zs).astype(y_ref.dtype)
        return carry

    lax.fori_loop(0, nc, chunk, 0, unroll=4)


def _mlstm_branch(p, gates, gbias, cw, cb, wq, wkt, wv, ng, l):
    b, t, _ = p.shape
    nc = t // ML_CHUNK
    ub, ob, zb = C_ML_U // ML_PW, C_ML_O // ML_PW, C_ML_Z // ML_PW
    seq = lambda base: pl.BlockSpec((1, t, ML_PW), lambda pr, bi: (bi, 0, base + pr))
    vec = pl.BlockSpec((None, 1, ML_PW), lambda pr, bi: (l, 0, pr))
    taps = pl.BlockSpec((None, ML_QKV_BLOCK, ML_PW), lambda pr, bi: (l, 0, pr))
    rows = lambda: pltpu.VMEM((2, nc, ML_CHUNK), F32)
    return pl.pallas_call(
        _mlstm_kernel,
        grid=(ML_PAIRS, b),
        in_specs=[
            seq(ub), seq(ob), seq(zb),
            pl.BlockSpec((1, 2, nc, ML_CHUNK), lambda pr, bi: (bi, pr, 0, 0)),
            pl.BlockSpec((1, 2, nc, ML_CHUNK), lambda pr, bi: (bi, ML_PAIRS + pr, 0, 0)),
            pl.BlockSpec((None, 2, 1, ML_CHUNK), lambda pr, bi: (l, pr, 0, 0)),
            pl.BlockSpec((None, 2, 1, ML_CHUNK), lambda pr, bi: (l, ML_PAIRS + pr, 0, 0)),
            pl.BlockSpec((None, cw.shape[1], ML_PW), lambda pr, bi: (l, 0, pr)),
            vec, taps, taps, taps, vec,
        ],
        out_specs=pl.BlockSpec((1, t, ML_PW), lambda pr, bi: (bi, 0, pr)),
        out_shape=jax.ShapeDtypeStruct((b, t, ML_W), BF16),
        scratch_shapes=[
            pltpu.VMEM((2, ML_PW, ML_DHP), BF16),
            pltpu.VMEM((2, ML_DHP, ML_PW), BF16),
            pltpu.VMEM((2, ML_PW, ML_PW), BF16),
            pltpu.VMEM((2, t, ML_DHP), BF16),
            pltpu.VMEM((2, nc, ML_DHP, ML_CHUNK), BF16),
            pltpu.VMEM((2, t, ML_PW), BF16),
            pltpu.VMEM((2, nc, ML_DHP, ML_PW), BF16),
            pltpu.VMEM((2, ML_DH, ML_PW), F32),
            rows(), rows(),
            pltpu.VMEM((2, nc, SUBLANE, ML_CHUNK), F32),
            rows(), rows(),
        ],
        compiler_params=pltpu.CompilerParams(
            dimension_semantics=("arbitrary", "arbitrary"), vmem_limit_bytes=_vmem_limit(56 << 20)),
        name="mlstm",
    )(p, p, p, gates, gates, gbias, gbias, cw, cb, wq, wkt, wv, ng)


MOBA_SEL_LANE = 6


def _moba_key_ext(nb):
    ext = np.zeros((nb, MOBA_BLOCK, MO_DH), np.float32)
    for n in range(nb):
        ext[n, :, 0:3] = n * MOBA_BLOCK
        ext[n, :, 3:6] = np.arange(MOBA_BLOCK, dtype=np.float32)[:, None]
        ext[n, :, MOBA_SEL_LANE + n] = 1.0
    return jnp.asarray(ext, dtype=BF16)


def _moba_slope_rows(slopes):
    rows = np.zeros((len(slopes), 1, MO_DH), np.float32)
    for h, s in enumerate(slopes):
        rest = np.float32(np.float32(s) * np.float32(LOG2E))
        for i in range(3):
            piece = np.float32(rest).astype(BF16).astype(np.float32)
            rows[h, 0, i] = rows[h, 0, 3 + i] = piece
            rest = np.float32(rest - piece)
    return jnp.asarray(rows)


def _moba_kernel(q_ref, k_ref, v_ref, z_ref, srow_ref, kext_ref, o_ref, l_s, qa_s):
    t = q_ref.shape[1]
    nb = t // MOBA_BLOCK
    bs = MOBA_BLOCK
    qscale = (MO_DH ** -0.5) * LOG2E

    krow = lax.broadcasted_iota(jnp.int32, (LANE, MO_DH), 0)
    kmean = jnp.zeros((LANE, MO_DH), F32)
    for n in range(nb):
        mean_n = jnp.sum(k_ref[0, n * bs:(n + 1) * bs, :].astype(F32), axis=0, keepdims=True) * (1.0 / bs)
        kmean = jnp.where(krow == MOBA_SEL_LANE + n, mean_n, kmean)
    kmean = kmean.astype(BF16)

    lane = lax.broadcasted_iota(jnp.int32, (bs, MO_DH), 1)
    causal = (lax.broadcasted_iota(jnp.int32, (bs, bs), 0) >= lax.broadcasted_iota(jnp.int32, (bs, bs), 1))
    ones_col = jnp.where(lane == 0, 1.0, 0.0).astype(BF16)
    slope_cols = jnp.where(lane < MOBA_SEL_LANE, srow_ref[0], 0.0)

    def prepare(qb):
        q_b = q_ref[0, qb * bs:(qb + 1) * bs, :]
        q_ext = slope_cols
        if qb > MOBA_TOPK:
            gate = lax.dot_general(q_b, kmean, NT_DIMS, preferred_element_type=F32)
            beaten = jnp.zeros((bs, MO_DH), F32)
            for m in range(qb):
                gm = gate[:, MOBA_SEL_LANE + m:MOBA_SEL_LANE + m + 1]
                wins = (gm > gate) | ((gm == gate) & (lane > MOBA_SEL_LANE + m))
                beaten = beaten + jnp.where(wins, 1.0, 0.0)
            past = (lane >= MOBA_SEL_LANE) & (lane < MOBA_SEL_LANE + qb)
            q_ext = jnp.where(past & (beaten >= float(MOBA_TOPK)), NEG_INF, slope_cols)
        qa_s[qb] = jnp.concatenate([(q_b.astype(F32) * qscale).astype(BF16), q_ext.astype(BF16)], axis=1)

    def scores(qb):
        q_aug = qa_s[qb]
        l_q = l_s.at[qb % 2]
        mx = None
        for n in range(qb + 1):
            k_aug = jnp.concatenate([k_ref[0, n * bs:(n + 1) * bs, :], kext_ref[n]], axis=1)
            logit = lax.dot_general(q_aug, k_aug, NT_DIMS, preferred_element_type=F32)
            if n == qb:
                logit = jnp.where(causal, logit, NEG_INF)
            l_q[:, n * bs:(n + 1) * bs] = logit
            half = jnp.maximum(logit[:, :LANE], logit[:, LANE:])
            mx = half if mx is None else jnp.maximum(mx, half)
            yield None
        yield jnp.max(mx, axis=1, keepdims=True)

    def outputs(qb, m_row):
        l_q = l_s.at[qb % 2]
        acc = jnp.zeros((bs, 2 * MO_DH), F32)
        for n in range(qb + 1):
            pr = jnp.exp2(l_q[:, n * bs:(n + 1) * bs] - m_row).astype(BF16)
            v_aug = jnp.concatenate([v_ref[0, n * bs:(n + 1) * bs, :], ones_col], axis=1)
            acc = acc + jnp.dot(pr, v_aug, preferred_element_type=F32)
            yield None
        z = z_ref[0, qb * bs:(qb + 1) * bs, :].astype(F32)
        inv = 1.0 / acc[:, MO_DH:MO_DH + 1]
        o_ref[0, qb * bs:(qb + 1) * bs, :] = (acc[:, :MO_DH] * (inv * _silu(z))).astype(o_ref.dtype)

    prepare(0)
    prepare(1)
    m_row = list(scores(0))[-1]
    for qb in range(nb):
        if qb + 2 < nb:
            prepare(qb + 2)
        m_next = list(scores(qb + 1))[-1] if qb + 1 < nb else None
        list(outputs(qb, m_row))
        m_row = m_next


def _moba_branch(p, srows, kext):
    b, t, _ = p.shape
    nb = t // MOBA_BLOCK
    qb, kb, vb, zb = (c // MO_DH for c in (C_MO_Q, C_MO_K, C_MO_V, C_MO_Z))
    seq = lambda base: pl.BlockSpec((1, t, MO_DH), lambda bi, h: (bi, 0, base + h))
    return pl.pallas_call(
        _moba_kernel,
        grid=(b, MO_HEADS),
        in_specs=[seq(qb), seq(kb), seq(vb), seq(zb),
                  pl.BlockSpec((1, 1, MO_DH), lambda bi, h: (h, 0, 0)),
                  pl.BlockSpec((nb, MOBA_BLOCK, MO_DH), lambda bi, h: (0, 0, 0))],
        out_specs=pl.BlockSpec((1, t, MO_DH), lambda bi, h: (bi, 0, h)),
        out_shape=jax.ShapeDtypeStruct((b, t, MO_W), BF16),
        scratch_shapes=[pltpu.VMEM((2, MOBA_BLOCK, t), F32), pltpu.VMEM((nb, MOBA_BLOCK, 2 * MO_DH), BF16)],
        compiler_params=pltpu.CompilerParams(
            dimension_semantics=("arbitrary", "arbitrary"), vmem_limit_bytes=_vmem_limit(32 << 20)),
        name="moba",
    )(p, p, p, p, srows, kext)


def _out_proj_kernel(x_ref, yl_ref, ym_ref, yo_ref, w_ref, o_ref):
    acc = jnp.dot(yl_ref[...], w_ref[0:LRU_W, :], preferred_element_type=F32)
    acc = acc + jnp.dot(ym_ref[...], w_ref[LRU_W:LRU_W + ML_W, :], preferred_element_type=F32)
    acc = acc + jnp.dot(yo_ref[...], w_ref[LRU_W + ML_W:, :], preferred_element_type=F32)
    o_ref[...] = x_ref[...] + acc


def _out_proj(x2d, yl, ym, yo, w, l):
    m = x2d.shape[0]
    kdim = w.shape[1]
    rows = lambda width: pl.BlockSpec((OUT_TM, width), lambda i: (i, 0))
    vmem = 4 * OUT_TM * D_MODEL * 4 + 2 * kdim * D_MODEL * 2 + 2 * OUT_TM * kdim * 2 + 2 * OUT_TM * D_MODEL * 4
    return pl.pallas_call(
        _out_proj_kernel,
        grid=(m // OUT_TM,),
        in_specs=[rows(D_MODEL), rows(LRU_W), rows(ML_W), rows(MO_W),
                  pl.BlockSpec((None, kdim, D_MODEL), lambda i: (l, 0, 0))],
        out_specs=rows(D_MODEL),
        out_shape=jax.ShapeDtypeStruct((m, D_MODEL), F32),
        compiler_params=pltpu.CompilerParams(
            dimension_semantics=("arbitrary",), vmem_limit_bytes=_vmem_limit(vmem)),
        name="out_proj",
    )(x2d, yl, ym, yo, w)


def _mem_kv_kernel(mem_ref, g_ref, w_ref, o_ref):
    hm = _rmsnorm_rows(mem_ref[0], g_ref[...]).astype(BF16)
    o_ref[0] = jnp.dot(hm, w_ref[...], preferred_element_type=F32).astype(o_ref.dtype)


def _mem_kv(mem, g, wkv, l):
    b, m, _ = mem.shape
    return pl.pallas_call(
        _mem_kv_kernel,
        grid=(b,),
        in_specs=[pl.BlockSpec((1, m, D_MODEL), lambda bi: (bi, 0, 0)),
                  pl.BlockSpec((None, 1, D_MODEL), lambda bi: (l, 0, 0)),
                  pl.BlockSpec((None, D_MODEL, 2 * XA_W), lambda bi: (l, 0, 0))],
        out_specs=pl.BlockSpec((1, m, 2 * XA_W), lambda bi: (bi, 0, 0)),
        out_shape=jax.ShapeDtypeStruct((b, m, 2 * XA_W), BF16),
        compiler_params=pltpu.CompilerParams(
            dimension_semantics=("arbitrary",), vmem_limit_bytes=_vmem_limit(32 << 20)),
        name="mem_kv",
    )(mem, g, wkv)


def _xattn_kernel(x_ref, g_ref, wq_ref, kv_ref, wo_ref, fg_ref, o_ref, *, final_norm):
    x = x_ref[0]
    hx = _rmsnorm_rows(x, g_ref[...]).astype(BF16)
    q = jnp.dot(hx, wq_ref[...], preferred_element_type=F32).astype(BF16)
    heads = []
    for h in range(XA_HEADS):
        q_h = q[:, h * XA_DH:(h + 1) * XA_DH]
        k_h = kv_ref[0, :, h * XA_DH:(h + 1) * XA_DH]
        v_h = kv_ref[0, :, XA_W + h * XA_DH:XA_W + (h + 1) * XA_DH]
        s = lax.dot_general(q_h, k_h, NT_DIMS, preferred_element_type=F32) * (XA_DH ** -0.5)
        e = jnp.exp(s - jnp.max(s, axis=1, keepdims=True))
        pr = e / jnp.sum(e, axis=1, keepdims=True)
        heads.append(jnp.dot(pr.astype(BF16), v_h, preferred_element_type=F32).astype(BF16))
    o = jnp.concatenate(heads, axis=1)
    y = x + jnp.dot(o, wo_ref[...], preferred_element_type=F32)
    if final_norm:
        y = _rmsnorm_rows(y, fg_ref[...])
    o_ref[0] = y


def _xattn(x, g, wq, kv, wo, fg, l, final_norm):
    b, t, _ = x.shape
    m = kv.shape[1]
    return pl.pallas_call(
        functools.partial(_xattn_kernel, final_norm=final_norm),
        grid=(b, t // XA_TM),
        in_specs=[pl.BlockSpec((1, XA_TM, D_MODEL), lambda bi, i: (bi, i, 0)),
                  pl.BlockSpec((None, 1, D_MODEL), lambda bi, i: (l, 0, 0)),
                  pl.BlockSpec((None, D_MODEL, XA_W), lambda bi, i: (l, 0, 0)),
                  pl.BlockSpec((1, m, 2 * XA_W), lambda bi, i: (bi, 0, 0)),
                  pl.BlockSpec((None, XA_W, D_MODEL), lambda bi, i: (l, 0, 0)),
                  pl.BlockSpec((1, D_MODEL), lambda bi, i: (0, 0))],
        out_specs=pl.BlockSpec((1, XA_TM, D_MODEL), lambda bi, i: (bi, i, 0)),
        out_shape=jax.ShapeDtypeStruct((b, t, D_MODEL), F32),
        compiler_params=pltpu.CompilerParams(
            dimension_semantics=("arbitrary", "arbitrary"), vmem_limit_bytes=_vmem_limit(40 << 20)),
        name="mem_xattn",
    )(x, g, wq, kv, wo, fg)


def _alibi_slopes(n):
    def pow2(m):
        start = 2.0 ** (-8.0 / m)
        return [start ** (i + 1) for i in range(m)]
    if math.log2(n).is_integer():
        s = pow2(n)
    else:
        c = 2 ** int(math.floor(math.log2(n)))
        s = pow2(c) + pow2(2 * c)[0::2][:n - c]
    return np.asarray(s, dtype=np.float32)


def _taps(w, perm):
    depth = w.shape[0]
    return jnp.transpose(w, perm).reshape(depth, ML_QKV_BLOCK, ML_W)


def kernel(x, mem, mix_norm_g, w_in, lru_conv_w, lru_conv_b, lru_wa, lru_ba, lru_wx, lru_bx, lru_lambda,
           ml_conv_w, ml_conv_b, ml_wq, ml_wk, ml_wv, ml_bi, ml_bf, ml_norm_g, w_out, xa_norm_g, mem_norm_g,
           xa_wq, xa_wkv, xa_wo, final_norm_g):
    b, t, d = x.shape
    depth = w_in.shape[0]
    nc = t // ML_CHUNK
    row = lambda v: v[:, None, :]
    moba_srows = _moba_slope_rows(_alibi_slopes(MO_HEADS))
    moba_kext = _moba_key_ext(t // MOBA_BLOCK)
    w_in_p = _w_in_prep(jnp.swapaxes(w_in, 1, 2))
    w_out_b = w_out.astype(BF16)
    wq_b, wkv_b, wo_b = xa_wq.astype(BF16), xa_wkv.astype(BF16), xa_wo.astype(BF16)
    lru_wax = jnp.concatenate([lru_wa, lru_wx], axis=-1).astype(BF16)
    ml_wq_t = _taps(ml_wq, (0, 2, 1, 3))
    ml_wkt_t = _taps(ml_wk, (0, 3, 1, 2))
    ml_wv_t = _taps(ml_wv, (0, 2, 1, 3))
    ml_gbias = jnp.broadcast_to(jnp.concatenate([ml_bi, ml_bf], axis=1)[:, :, None, None],
                                (depth, 2 * ML_HEADS, 1, ML_CHUNK))
    mix_g, xa_g, mem_g = row(mix_norm_g), row(xa_norm_g), row(mem_norm_g)
    lru_cb, lru_ba_r, lru_bx_r, lru_lam = row(lru_conv_b), row(lru_ba), row(lru_bx), row(lru_lambda)
    ml_cb, ml_ng = row(ml_conv_b), row(ml_norm_g)
    for l in range(depth):
        p = _in_proj(x.reshape(b * t, d), mix_g, w_in_p, l).reshape(b, t, IN_COLS_P)
        gates = p[:, :, C_GATE:C_GATE + 2 * ML_HEADS].astype(F32)
        gates = jnp.swapaxes(gates, 1, 2).reshape(b, 2 * ML_HEADS, nc, ML_CHUNK)
        y_lru = _lru_branch(p, lru_conv_w, lru_cb, lru_wax, lru_ba_r, lru_bx_r, lru_lam, l)
        y_ml = _mlstm_branch(p, gates, ml_gbias, ml_conv_w, ml_cb, ml_wq_t, ml_wkt_t, ml_wv_t, ml_ng, l)
        y_mo = _moba_branch(p, moba_srows, moba_kext)
        x1 = _out_proj(x.reshape(b * t, d), y_lru.reshape(b * t, LRU_W), y_ml.reshape(b * t, ML_W),
                       y_mo.reshape(b * t, MO_W), w_out_b, l).reshape(b, t, d)
        kv = _mem_kv(mem, mem_g, wkv_b, l)
        x = _xattn(x1, xa_g, wq_b, kv, wo_b, final_norm_g[None, :], l, final_norm=(l == depth - 1))
    return x
```

```python
import functools
import math

import jax
import jax.numpy as jnp
import numpy as np
from jax import lax
from jax.experimental import pallas as pl
from jax.experimental.pallas import tpu as pltpu

LANE = 128
SUBLANE = 8
V7X_VMEM_BYTES = 64 * 1024 * 1024

D_MODEL = 2048
LRU_W = 512
LRU_BLOCKS = 4
LRU_BW = LRU_W // LRU_BLOCKS
LRU_C = 8.0
ML_W = 768
ML_HEADS = 4
ML_DH = 192
ML_DHP = 256
ML_PAIRS = 2
ML_PW = 2 * ML_DH
ML_QKV_BLOCK = 4
ML_QKV_SHIFT = 2
ML_CHUNK = 128
MO_W = 768
MO_HEADS = 6
MO_DH = 128
MOBA_BLOCK = 256
MOBA_TOPK = 3
XA_HEADS = 4
XA_DH = 128
XA_W = XA_HEADS * XA_DH
RMS_EPS = 1e-6
LN_EPS = 1e-5
NEG_INF = -1e30
LOG2E = 1.4426950408889634

REF_LRU_END = 2 * LRU_W
REF_ML_END = REF_LRU_END + 3 * ML_W
REF_GATE_END = REF_ML_END + 2 * ML_HEADS

C_ML_U = 0
C_ML_O = C_ML_U + ML_W
C_ML_Z = C_ML_O + ML_W
C_LRU_X = C_ML_Z + ML_W
C_LRU_Z = C_LRU_X + LRU_W
C_MO_Q = C_LRU_Z + LRU_W
C_MO_K = C_MO_Q + MO_W
C_MO_V = C_MO_K + MO_W
C_MO_Z = C_MO_V + MO_W
C_GATE = C_MO_Z + MO_W
IN_TM = 512
IN_TN = 3328
IN_COLS_P = 6656
PREP_TR = 256
POST_TM = 512
POST_SUB = 2

BF16 = jnp.bfloat16
F32 = jnp.float32
NT_DIMS = (((1,), (1,)), ((), ()))


def _vmem_limit(nbytes):
    return int(min(V7X_VMEM_BYTES - (4 << 20), max(32 << 20, nbytes)))


def _rmsnorm_rows(x, g):
    ms = jnp.mean(x * x, axis=-1, keepdims=True)
    return x * lax.rsqrt(ms + RMS_EPS) * g


def _sigmoid(x):
    return jax.nn.sigmoid(x)


def _silu(x):
    return x * jax.nn.sigmoid(x)


def _softplus(x):
    return jnp.maximum(x, 0.0) + jnp.log1p(jnp.exp(-jnp.abs(x)))


def _shift_rows(x, s):
    rolled = pltpu.roll(x, s, axis=0)
    row = lax.broadcasted_iota(jnp.int32, x.shape, 0)
    return jnp.where(row >= s, rolled, 0.0)


def _causal_conv(x, w_ref, b_ref):
    k = w_ref.shape[0]
    acc = x * w_ref[k - 1:k, :]
    head = x[:SUBLANE]
    acc_head = head * w_ref[k - 1:k, :]
    for j in range(k - 1):
        acc = acc + pltpu.roll(x, k - 1 - j, axis=0) * w_ref[j:j + 1, :]
        acc_head = acc_head + _shift_rows(head, k - 1 - j) * w_ref[j:j + 1, :]
    return jnp.concatenate([acc_head, acc[SUBLANE:]], axis=0) + b_ref[...]


def _w_in_prep_kernel(w_ref, o_ref):
    last = pl.program_id(1) == pl.num_programs(1) - 1
    valid = jnp.where(last, 2 * ML_HEADS, PREP_TR)
    row = lax.broadcasted_iota(jnp.int32, w_ref.shape, 0)
    o_ref[...] = jnp.where(row < valid, w_ref[...], 0.0).astype(BF16)


def _w_in_prep_src_row(i):
    n_ml = (C_LRU_X - C_ML_U) // PREP_TR
    n_lru = (C_MO_Q - C_LRU_X) // PREP_TR
    n_mo = (C_GATE - C_MO_Q) // PREP_TR
    tile, unit = PREP_TR // SUBLANE, SUBLANE
    ml = REF_LRU_END // unit + i * tile
    lru = (i - n_ml) * tile
    mo = REF_GATE_END // unit + (i - n_ml - n_lru) * tile
    gate = REF_ML_END // unit
    return unit * jnp.where(i < n_ml, ml, jnp.where(i < n_ml + n_lru, lru, jnp.where(i < n_ml + n_lru + n_mo, mo, gate)))


def _w_in_prep(w_in_t):
    depth, _, k = w_in_t.shape
    return pl.pallas_call(
        _w_in_prep_kernel,
        grid=(depth, IN_COLS_P // PREP_TR),
        in_specs=[pl.BlockSpec((pl.Squeezed(), pl.Element(PREP_TR), pl.Element(k)),
                               lambda l, i: (l, _w_in_prep_src_row(i), 0))],
        out_specs=pl.BlockSpec((None, PREP_TR, k), lambda l, i: (l, i, 0)),
        out_shape=jax.ShapeDtypeStruct((depth, IN_COLS_P, k), BF16),
        compiler_params=pltpu.CompilerParams(
            dimension_semantics=("arbitrary", "arbitrary"), vmem_limit_bytes=_vmem_limit(32 << 20)),
        name="w_in_prep",
    )(w_in_t)


def _in_proj_kernel(x_ref, g_ref, w_ref, o_ref, xn_ref):
    @pl.when(pl.program_id(1) == 0)
    def _():
        xn_ref[...] = _rmsnorm_rows(x_ref[...], g_ref[...]).astype(BF16)

    o_ref[...] = lax.dot_general(xn_ref[...], w_ref[...], NT_DIMS, preferred_element_type=F32).astype(o_ref.dtype)


def _in_proj(x2d, g, w, l):
    m = x2d.shape[0]
    n = w.shape[1]
    vmem = (2 * IN_TM * D_MODEL * 4 + IN_TM * D_MODEL * 2 + 2 * D_MODEL * IN_TN * 2 + 2 * IN_TM * IN_TN * 2
            + 2 * IN_TM * IN_TN * 4)
    return pl.pallas_call(
        _in_proj_kernel,
        grid=(m // IN_TM, n // IN_TN),
        in_specs=[
            pl.BlockSpec((IN_TM, D_MODEL), lambda i, j: (i, 0)),
            pl.BlockSpec((None, 1, D_MODEL), lambda i, j: (l, 0, 0)),
            pl.BlockSpec((None, IN_TN, D_MODEL), lambda i, j: (l, j, 0)),
        ],
        out_specs=pl.BlockSpec((IN_TM, IN_TN), lambda i, j: (i, j)),
        out_shape=jax.ShapeDtypeStruct((m, n), BF16),
        scratch_shapes=[pltpu.VMEM((IN_TM, D_MODEL), BF16)],
        compiler_params=pltpu.CompilerParams(
            dimension_semantics=("arbitrary", "arbitrary"), vmem_limit_bytes=_vmem_limit(vmem)),
        name="in_proj",
    )(x2d, g, w)


def _lru_kernel(x_ref, z_ref, cw_ref, cb_ref, wax_ref, ba_ref, bx_ref, lam_ref, o_ref, a_s, u_s):
    t = x_ref.shape[1]
    x = x_ref[0].astype(F32)
    xc = _causal_conv(x, cw_ref, cb_ref)
    pre = jnp.dot(xc.astype(BF16), wax_ref[0], preferred_element_type=F32)
    r = _sigmoid(pre[:, :LRU_BW] + ba_ref[...])
    i = _sigmoid(pre[:, LRU_BW:] + bx_ref[...])
    log_a = (-LRU_C) * r * _softplus(-lam_ref[...])
    a = jnp.exp(log_a)
    a_s[...] = a
    u_s[...] = jnp.sqrt(-jnp.tanh(log_a) * (1.0 + a * a)) * (i * xc)

    row = lax.broadcasted_iota(jnp.int32, (SUBLANE, LRU_BW), 0)

    def block(blk, h_prev):
        r0 = pl.multiple_of(blk * SUBLANE, SUBLANE)
        a_b = a_s[pl.ds(r0, SUBLANE), :]
        u_b = u_s[pl.ds(r0, SUBLANE), :]
        for s in (1, 2, 4):
            a_sh = jnp.where(row >= s, pltpu.roll(a_b, s, axis=0), 1.0)
            u_sh = jnp.where(row >= s, pltpu.roll(u_b, s, axis=0), 0.0)
            u_b = a_b * u_sh + u_b
            a_b = a_b * a_sh
        h = a_b * h_prev + u_b
        u_s[pl.ds(r0, SUBLANE), :] = h
        return jnp.broadcast_to(h[SUBLANE - 1:SUBLANE, :], (SUBLANE, LRU_BW))

    lax.fori_loop(0, t // SUBLANE, block, jnp.zeros((SUBLANE, LRU_BW), F32), unroll=4)
    z = z_ref[0].astype(F32)
    o_ref[0] = (u_s[...] * _silu(z)).astype(o_ref.dtype)


def _lru_branch(p, cw, cb, wax, ba, bx, lam, l):
    b, t, _ = p.shape
    xb, zb = C_LRU_X // LRU_BW, C_LRU_Z // LRU_BW
    vec = pl.BlockSpec((None, 1, LRU_BW), lambda bi, g: (l, 0, g))
    return pl.pallas_call(
        _lru_kernel,
        grid=(b, LRU_BLOCKS),
        in_specs=[
            pl.BlockSpec((1, t, LRU_BW), lambda bi, g: (bi, 0, xb + g)),
            pl.BlockSpec((1, t, LRU_BW), lambda bi, g: (bi, 0, zb + g)),
            pl.BlockSpec((None, cw.shape[1], LRU_BW), lambda bi, g: (l, 0, g)),
            vec,
            pl.BlockSpec((None, 1, LRU_BW, 2 * LRU_BW), lambda bi, g: (l, g, 0, 0)),
            vec, vec, vec,
        ],
        out_specs=pl.BlockSpec((1, t, LRU_BW), lambda bi, g: (bi, 0, g)),
        out_shape=jax.ShapeDtypeStruct((b, t, LRU_W), BF16),
        scratch_shapes=[pltpu.VMEM((t, LRU_BW), F32), pltpu.VMEM((t, LRU_BW), F32)],
        compiler_params=pltpu.CompilerParams(
            dimension_semantics=("arbitrary", "arbitrary"), vmem_limit_bytes=_vmem_limit(24 * t * LRU_BW * 4)),
        name="rg_lru",
    )(p, p, cw, cb, wax, ba, bx, lam)


def _blockdiag_in_out(w_ref, hh):
    wh = w_ref[...][:, hh * ML_DH:(hh + 1) * ML_DH]
    wh = jnp.concatenate([wh, jnp.zeros((ML_QKV_BLOCK, ML_DHP - ML_DH), F32)], axis=1)
    r = lax.broadcasted_iota(jnp.int32, (ML_PW, ML_DHP), 0)
    c = lax.broadcasted_iota(jnp.int32, (ML_PW, ML_DHP), 1)
    d = jnp.zeros((ML_PW, ML_DHP), F32)
    for i in range(ML_QKV_BLOCK):
        d = jnp.where((r & (ML_QKV_BLOCK - 1)) == i, wh[i:i + 1, :], d)
    keep = (((r >> ML_QKV_SHIFT) - hh * (ML_DH // ML_QKV_BLOCK)) == (c >> ML_QKV_SHIFT)) & (c < ML_DH)
    return jnp.where(keep, d, 0.0)


def _blockdiag_in_pair(w_ref, hh):
    wh = w_ref[...]
    r = lax.broadcasted_iota(jnp.int32, (ML_PW, ML_PW), 0)
    c = lax.broadcasted_iota(jnp.int32, (ML_PW, ML_PW), 1)
    d = jnp.zeros((ML_PW, ML_PW), F32)
    for i in range(ML_QKV_BLOCK):
        d = jnp.where((r & (ML_QKV_BLOCK - 1)) == i, wh[i:i + 1, :], d)
    keep = ((r >> ML_QKV_SHIFT) == (c >> ML_QKV_SHIFT)) & (c >= hh * ML_DH) & (c < (hh + 1) * ML_DH)
    return jnp.where(keep, d, 0.0)


def _blockdiag_out_in(w_ref, hh):
    wk = w_ref[...]
    o = lax.broadcasted_iota(jnp.int32, (ML_DHP, ML_PW), 0)
    r = lax.broadcasted_iota(jnp.int32, (ML_DHP, ML_PW), 1)
    d = jnp.zeros((ML_DHP, ML_PW), F32)
    for j in range(ML_QKV_BLOCK):
        d = jnp.where((o & (ML_QKV_BLOCK - 1)) == j, wk[j:j + 1, :], d)
    keep = ((o >> ML_QKV_SHIFT) == ((r >> ML_QKV_SHIFT) - hh * (ML_DH // ML_QKV_BLOCK))) & (o < ML_DH)
    return jnp.where(keep, d, 0.0)


ML_DEN_LANE = (ML_DH, 0)


def _mlstm_kernel(u_ref, og_ref, z_ref, ig_ref, fg_ref, bi_ref, bf_ref, cw_ref, cb_ref,
                  wq_ref, wkt_ref, wv_ref, ng_ref, y_ref,
                  dq_s, dkt_s, dv_s, q_s, kt_s, v_s, cp_s, c_s, r_s, w_s, col_s, so_s, sn_s):
    t = u_ref.shape[1]
    nc = t // ML_CHUNK
    L = ML_CHUNK
    heads = range(2)

    @pl.when(pl.program_id(1) == 0)
    def _():
        for hh in heads:
            dq_s[hh] = _blockdiag_in_out(wq_ref, hh).astype(BF16)
            dkt_s[hh] = _blockdiag_out_in(wkt_ref, hh).astype(BF16)
            dv_s[hh] = _blockdiag_in_pair(wv_ref, hh).astype(BF16)

    u = u_ref[0]
    uc = _silu(_causal_conv(u.astype(F32), cw_ref, cb_ref)).astype(BF16)
    for hh in heads:
        w0 = hh * (ML_PW - ML_DHP)
        uc_w, u_w = uc[:, w0:w0 + ML_DHP], u[:, w0:w0 + ML_DHP]
        q_s[hh] = jnp.dot(uc_w, dq_s[hh, w0:w0 + ML_DHP, :], preferred_element_type=F32).astype(BF16)
        kt = lax.dot_general(dkt_s[hh, :, w0:w0 + ML_DHP], uc_w, NT_DIMS, preferred_element_type=F32)
        kt = kt * (ML_DH ** -0.5)
        for c in range(nc):
            kt_s[hh, c] = kt[:, c * L:(c + 1) * L].astype(BF16)
        v = jnp.dot(u_w, dv_s[hh, w0:w0 + ML_DHP, :], preferred_element_type=F32)
        vlane = lax.broadcasted_iota(jnp.int32, v.shape, 1)
        v_s[hh] = jnp.where(vlane == ML_DEN_LANE[hh], 1.0, v).astype(BF16)

    glane = lax.broadcasted_iota(jnp.int32, (nc, L), 1)
    grow = lax.broadcasted_iota(jnp.int32, (nc, L), 0)
    row8 = lax.broadcasted_iota(jnp.int32, (SUBLANE, L), 0)
    for hh in heads:
        ig = ig_ref[0, hh] + bi_ref[hh]
        lf = -_softplus(-(fg_ref[0, hh] + bf_ref[hh]))
        b = lf
        for k in range(int(math.log2(L))):
            sh = 1 << k
            b = b + jnp.where(glane >= sh, pltpu.roll(b, sh, axis=1), 0.0)
        g = jnp.broadcast_to(b[:, L - 1:L], (nc, L))
        a = g - b + ig
        mloc = jnp.broadcast_to(jnp.max(a, axis=1, keepdims=True), (nc, L))
        m = jnp.zeros((1, L), F32)
        m_prev = jnp.zeros((nc, L), F32)
        m_next = jnp.zeros((nc, L), F32)
        for c in range(nc):
            m_prev = jnp.where(grow == c, m, m_prev)
            m = jnp.maximum(g[c:c + 1, :] + m, mloc[c:c + 1, :])
            m_next = jnp.where(grow == c, m, m_next)
        r = ig - b
        cmx = r
        for k in range(int(math.log2(L))):
            sh = 1 << k
            cmx = jnp.maximum(cmx, jnp.where(glane >= sh, pltpu.roll(cmx, sh, axis=1), -jnp.inf))
        mm = jnp.maximum(m_prev, cmx)
        s_int = jnp.exp(m_prev - mm)
        clamp = jnp.exp(-(b + mm))
        r_s[hh] = r
        w_s[hh] = jnp.exp(a - mloc)
        so_s[hh] = jnp.exp(g + m_prev - m_next)
        sn_s[hh] = jnp.exp(mloc - m_next)
        for c in range(nc):
            col_s[hh, c] = jnp.where(row8 == 0, mm[c:c + 1, :],
                                     jnp.where(row8 == 1, s_int[c:c + 1, :],
                                               jnp.where(row8 == 2, clamp[c:c + 1, :], 0.0)))

    zero_rows = jnp.zeros((ML_DHP - ML_DH, ML_PW), BF16)
    c_s[...] = jnp.zeros(c_s.shape, F32)

    def state(c, carry):
        r0 = pl.multiple_of(c * L, L)
        for hh in heads:
            c_prev = c_s[hh]
            cp_s[hh, c] = jnp.concatenate([c_prev.astype(BF16), zero_rows], axis=0)
            ktw = (kt_s[hh, c, :ML_DH, :].astype(F32) * w_s[hh, pl.ds(c, 1), :]).astype(BF16)
            c_loc = jnp.dot(ktw, v_s[hh, pl.ds(r0, L), :], preferred_element_type=F32)
            c_s[hh] = so_s[hh, pl.ds(c, 1), :][:, :1] * c_prev + sn_s[hh, pl.ds(c, 1), :][:, :1] * c_loc
        return carry

    lax.fori_loop(0, nc, state, 0, unroll=2)

    tri = (lax.broadcasted_iota(jnp.int32, (L, L), 0) >= lax.broadcasted_iota(jnp.int32, (L, L), 1))
    first = lax.broadcasted_iota(jnp.int32, (L, ML_PW), 1) < ML_DH
    pad_rows = jnp.zeros((L - SUBLANE, L), F32)

    def chunk(c, carry):
        r0 = pl.multiple_of(c * L, L)
        nds, invs = [], []
        for hh in heads:
            q_c = q_s[hh, pl.ds(r0, L), :]
            cols = jnp.concatenate([col_s[hh, c], pad_rows], axis=0).T
            rb = jnp.broadcast_to(r_s[hh, pl.ds(c, 1), :], (L, L))
            decay = jnp.exp(jnp.where(tri, rb - cols[:, 0:1], -jnp.inf))
            s_mat = jnp.dot(q_c, kt_s[hh, c], preferred_element_type=F32) * decay
            nd = (jnp.dot(s_mat.astype(BF16), v_s[hh, pl.ds(r0, L), :], preferred_element_type=F32)
                  + cols[:, 1:2] * jnp.dot(q_c, cp_s[hh, c], preferred_element_type=F32))
            den = nd[:, ML_DEN_LANE[hh]:ML_DEN_LANE[hh] + 1]
            nds.append(nd)
            invs.append(1.0 / jnp.maximum(jnp.abs(den), cols[:, 2:3]))
        x = _sigmoid(og_ref[0, pl.ds(r0, L), :].astype(F32)) * jnp.where(first, nds[0], nds[1])
        mu0 = jnp.sum(jnp.where(first, x, 0.0), axis=1, keepdims=True) * (1.0 / ML_DH)
        mu1 = jnp.sum(jnp.where(first, 0.0, x), axis=1, keepdims=True) * (1.0 / ML_DH)
        dev = x - jnp.where(first, mu0, mu1)
        sq = dev * dev
        var0 = jnp.sum(jnp.where(first, sq, 0.0), axis=1, keepdims=True) * (1.0 / ML_DH)
        var1 = jnp.sum(jnp.where(first, 0.0, sq), axis=1, keepdims=True) * (1.0 / ML_DH)
        f0 = invs[0] * lax.rsqrt(invs[0] * invs[0] * var0 + LN_EPS)
        f1 = invs[1] * lax.rsqrt(invs[1] * invs[1] * var1 + LN_EPS)
        zs = _silu(z_ref[0, pl.ds(r0, L), :].astype(F32)) * ng_ref[...]
        y_ref[0, pl.ds(r0, L), :] = (dev * jnp.where(first, f0, f1) * zs).astype(y_ref.dtype)
        return carry

    lax.fori_loop(0, nc, chunk, 0, unroll=4)


def _mlstm_branch(p, gates, gbias, cw, cb, wq, wkt, wv, ng, l):
    b, t, _ = p.shape
    nc = t // ML_CHUNK
    ub, ob, zb = C_ML_U // ML_PW, C_ML_O // ML_PW, C_ML_Z // ML_PW
    seq = lambda base: pl.BlockSpec((1, t, ML_PW), lambda pr, bi: (bi, 0, base + pr))
    vec = pl.BlockSpec((None, 1, ML_PW), lambda pr, bi: (l, 0, pr))
    taps = pl.BlockSpec((None, ML_QKV_BLOCK, ML_PW), lambda pr, bi: (l, 0, pr))
    rows = lambda: pltpu.VMEM((2, nc, ML_CHUNK), F32)
    return pl.pallas_call(
        _mlstm_kernel,
        grid=(ML_PAIRS, b),
        in_specs=[
            seq(ub), seq(ob), seq(zb),
            pl.BlockSpec((1, 2, nc, ML_CHUNK), lambda pr, bi: (bi, pr, 0, 0)),
            pl.BlockSpec((1, 2, nc, ML_CHUNK), lambda pr, bi: (bi, ML_PAIRS + pr, 0, 0)),
            pl.BlockSpec((None, 2, 1, ML_CHUNK), lambda pr, bi: (l, pr, 0, 0)),
            pl.BlockSpec((None, 2, 1, ML_CHUNK), lambda pr, bi: (l, ML_PAIRS + pr, 0, 0)),
            pl.BlockSpec((None, cw.shape[1], ML_PW), lambda pr, bi: (l, 0, pr)),
            vec, taps, taps, taps, vec,
        ],
        out_specs=pl.BlockSpec((1, t, ML_PW), lambda pr, bi: (bi, 0, pr)),
        out_shape=jax.ShapeDtypeStruct((b, t, ML_W), BF16),
        scratch_shapes=[
            pltpu.VMEM((2, ML_PW, ML_DHP), BF16),
            pltpu.VMEM((2, ML_DHP, ML_PW), BF16),
            pltpu.VMEM((2, ML_PW, ML_PW), BF16),
            pltpu.VMEM((2, t, ML_DHP), BF16),
            pltpu.VMEM((2, nc, ML_DHP, ML_CHUNK), BF16),
            pltpu.VMEM((2, t, ML_PW), BF16),
            pltpu.VMEM((2, nc, ML_DHP, ML_PW), BF16),
            pltpu.VMEM((2, ML_DH, ML_PW), F32),
            rows(), rows(),
            pltpu.VMEM((2, nc, SUBLANE, ML_CHUNK), F32),
            rows(), rows(),
        ],
        compiler_params=pltpu.CompilerParams(
            dimension_semantics=("arbitrary", "arbitrary"), vmem_limit_bytes=_vmem_limit(56 << 20)),
        name="mlstm",
    )(p, p, p, gates, gates, gbias, gbias, cw, cb, wq, wkt, wv, ng)


MOBA_SEL_LANE = 6


def _moba_key_ext(nb):
    ext = np.zeros((nb, MOBA_BLOCK, MO_DH), np.float32)
    for n in range(nb):
        ext[n, :, 0:3] = n * MOBA_BLOCK
        ext[n, :, 3:6] = np.arange(MOBA_BLOCK, dtype=np.float32)[:, None]
        ext[n, :, MOBA_SEL_LANE + n] = 1.0
    return jnp.asarray(ext, dtype=BF16)


def _moba_slope_rows(slopes):
    rows = np.zeros((len(slopes), 1, MO_DH), np.float32)
    for h, s in enumerate(slopes):
        rest = np.float32(np.float32(s) * np.float32(LOG2E))
        for i in range(3):
            piece = np.float32(rest).astype(BF16).astype(np.float32)
            rows[h, 0, i] = rows[h, 0, 3 + i] = piece
            rest = np.float32(rest - piece)
    return jnp.asarray(rows)


def _moba_kernel(q_ref, k_ref, v_ref, z_ref, srow_ref, kext_ref, o_ref, l_s, qa_s):
    t = q_ref.shape[1]
    nb = t // MOBA_BLOCK
    bs = MOBA_BLOCK
    qscale = (MO_DH ** -0.5) * LOG2E

    krow = lax.broadcasted_iota(jnp.int32, (LANE, MO_DH), 0)
    kmean = jnp.zeros((LANE, MO_DH), F32)
    for n in range(nb):
        mean_n = jnp.sum(k_ref[0, n * bs:(n + 1) * bs, :].astype(F32), axis=0, keepdims=True) * (1.0 / bs)
        kmean = jnp.where(krow == MOBA_SEL_LANE + n, mean_n, kmean)
    kmean = kmean.astype(BF16)

    lane = lax.broadcasted_iota(jnp.int32, (bs, MO_DH), 1)
    causal = (lax.broadcasted_iota(jnp.int32, (bs, bs), 0) >= lax.broadcasted_iota(jnp.int32, (bs, bs), 1))
    ones_col = jnp.where(lane == 0, 1.0, 0.0).astype(BF16)
    slope_cols = jnp.where(lane < MOBA_SEL_LANE, srow_ref[0], 0.0)

    def prepare(qb):
        q_b = q_ref[0, qb * bs:(qb + 1) * bs, :]
        q_ext = slope_cols
        if qb > MOBA_TOPK:
            gate = lax.dot_general(q_b, kmean, NT_DIMS, preferred_element_type=F32)
            beaten = jnp.zeros((bs, MO_DH), F32)
            for m in range(qb):
                gm = gate[:, MOBA_SEL_LANE + m:MOBA_SEL_LANE + m + 1]
                wins = (gm > gate) | ((gm == gate) & (lane > MOBA_SEL_LANE + m))
                beaten = beaten + jnp.where(wins, 1.0, 0.0)
            past = (lane >= MOBA_SEL_LANE) & (lane < MOBA_SEL_LANE + qb)
            q_ext = jnp.where(past & (beaten >= float(MOBA_TOPK)), NEG_INF, slope_cols)
        qa_s[qb] = jnp.concatenate([(q_b.astype(F32) * qscale).astype(BF16), q_ext.astype(BF16)], axis=1)

    def scores(qb):
        q_aug = qa_s[qb]
        l_q = l_s.at[qb % 2]
        mx = None
        for n in range(qb + 1):
            k_aug = jnp.concatenate([k_ref[0, n * bs:(n + 1) * bs, :], kext_ref[n]], axis=1)
            logit = lax.dot_general(q_aug, k_aug, NT_DIMS, preferred_element_type=F32)
            if n == qb:
                logit = jnp.where(causal, logit, NEG_INF)
            l_q[:, n * bs:(n + 1) * bs] = logit
            half = jnp.maximum(logit[:, :LANE], logit[:, LANE:])
            mx = half if mx is None else jnp.maximum(mx, half)
            yield None
        yield jnp.max(mx, axis=1, keepdims=True)

    def outputs(qb, m_row):
        l_q = l_s.at[qb % 2]
        acc = jnp.zeros((bs, 2 * MO_DH), F32)
        for n in range(qb + 1):
            pr = jnp.exp2(l_q[:, n * bs:(n + 1) * bs] - m_row).astype(BF16)
            v_aug = jnp.concatenate([v_ref[0, n * bs:(n + 1) * bs, :], ones_col], axis=1)
            acc = acc + jnp.dot(pr, v_aug, preferred_element_type=F32)
            yield None
        z = z_ref[0, qb * bs:(qb + 1) * bs, :].astype(F32)
        inv = 1.0 / acc[:, MO_DH:MO_DH + 1]
        o_ref[0, qb * bs:(qb + 1) * bs, :] = (acc[:, :MO_DH] * (inv * _silu(z))).astype(o_ref.dtype)

    prepare(0)
    prepare(1)
    m_row = list(scores(0))[-1]
    for qb in range(nb):
        if qb + 2 < nb:
            prepare(qb + 2)
        m_next = list(scores(qb + 1))[-1] if qb + 1 < nb else None
        list(outputs(qb, m_row))
        m_row = m_next


def _moba_branch(p, srows, kext):
    b, t, _ = p.shape
    nb = t // MOBA_BLOCK
    qb, kb, vb, zb = (c // MO_DH for c in (C_MO_Q, C_MO_K, C_MO_V, C_MO_Z))
    seq = lambda base: pl.BlockSpec((1, t, MO_DH), lambda bi, h: (bi, 0, base + h))
    return pl.pallas_call(
        _moba_kernel,
        grid=(b, MO_HEADS),
        in_specs=[seq(qb), seq(kb), seq(vb), seq(zb),
                  pl.BlockSpec((1, 1, MO_DH), lambda bi, h: (h, 0, 0)),
                  pl.BlockSpec((nb, MOBA_BLOCK, MO_DH), lambda bi, h: (0, 0, 0))],
        out_specs=pl.BlockSpec((1, t, MO_DH), lambda bi, h: (bi, 0, h)),
        out_shape=jax.ShapeDtypeStruct((b, t, MO_W), BF16),
        scratch_shapes=[pltpu.VMEM((2, MOBA_BLOCK, t), F32), pltpu.VMEM((nb, MOBA_BLOCK, 2 * MO_DH), BF16)],
        compiler_params=pltpu.CompilerParams(
            dimension_semantics=("arbitrary", "arbitrary"), vmem_limit_bytes=_vmem_limit(32 << 20)),
        name="moba",
    )(p, p, p, p, srows, kext)


def _mem_kv_kernel(mem_ref, g_ref, w_ref, o_ref):
    hm = _rmsnorm_rows(mem_ref[0], g_ref[...]).astype(BF16)
    o_ref[0] = jnp.dot(hm, w_ref[...], preferred_element_type=F32).astype(o_ref.dtype)


def _mem_kv(mem, g, wkv, l):
    b, m, _ = mem.shape
    return pl.pallas_call(
        _mem_kv_kernel,
        grid=(b,),
        in_specs=[pl.BlockSpec((1, m, D_MODEL), lambda bi: (bi, 0, 0)),
                  pl.BlockSpec((None, 1, D_MODEL), lambda bi: (l, 0, 0)),
                  pl.BlockSpec((None, D_MODEL, 2 * XA_W), lambda bi: (l, 0, 0))],
        out_specs=pl.BlockSpec((1, m, 2 * XA_W), lambda bi: (bi, 0, 0)),
        out_shape=jax.ShapeDtypeStruct((b, m, 2 * XA_W), BF16),
        compiler_params=pltpu.CompilerParams(
            dimension_semantics=("arbitrary",), vmem_limit_bytes=_vmem_limit(32 << 20)),
        name="mem_kv",
    )(mem, g, wkv)


def _post_kernel(x_ref, yl_ref, ym_ref, yo_ref, wout_ref, g_ref, wq_ref, kv_ref, wo_ref, fg_ref, o_ref, x1_s,
                 *, final_norm):
    sub = x_ref.shape[1] // POST_SUB
    groups = [pl.ds(i * sub, sub) for i in range(POST_SUB)]

    for r in groups:
        acc = jnp.dot(yl_ref[0, r, :], wout_ref[0:LRU_W, :], preferred_element_type=F32)
        acc = acc + jnp.dot(ym_ref[0, r, :], wout_ref[LRU_W:LRU_W + ML_W, :], preferred_element_type=F32)
        acc = acc + jnp.dot(yo_ref[0, r, :], wout_ref[LRU_W + ML_W:, :], preferred_element_type=F32)
        x1_s[r, :] = x_ref[0, r, :] + acc
    qs = []
    for r in groups:
        hx = _rmsnorm_rows(x1_s[r, :], g_ref[...]).astype(BF16)
        qs.append(jnp.dot(hx, wq_ref[...], preferred_element_type=F32).astype(BF16))
    scores = []
    for q in qs:
        scores.append([
            lax.dot_general(q[:, h * XA_DH:(h + 1) * XA_DH], kv_ref[0, :, h * XA_DH:(h + 1) * XA_DH], NT_DIMS,
                            preferred_element_type=F32) * (XA_DH ** -0.5)
            for h in range(XA_HEADS)])
    outs = []
    for sc in scores:
        heads = []
        for h, s in enumerate(sc):
            e = jnp.exp(s - jnp.max(s, axis=1, keepdims=True))
            pr = e / jnp.sum(e, axis=1, keepdims=True)
            v_h = kv_ref[0, :, XA_W + h * XA_DH:XA_W + (h + 1) * XA_DH]
            heads.append(jnp.dot(pr.astype(BF16), v_h, preferred_element_type=F32).astype(BF16))
        outs.append(jnp.concatenate(heads, axis=1))
    for r, o in zip(groups, outs):
        y = x1_s[r, :] + jnp.dot(o, wo_ref[...], preferred_element_type=F32)
        if final_norm:
            y = _rmsnorm_rows(y, fg_ref[...])
        o_ref[0, r, :] = y


def _post_mix(x, yl, ym, yo, wout, g, wq, kv, wo, fg, l, final_norm):
    b, t, _ = x.shape
    m = kv.shape[1]
    rows = lambda width: pl.BlockSpec((1, POST_TM, width), lambda bi, i: (bi, i, 0))
    resident = lambda *shape: pl.BlockSpec((None,) + shape, lambda bi, i: (l,) + (0,) * len(shape),
                                           pipeline_mode=pl.Buffered(1))
    weights = (D_MODEL * D_MODEL + D_MODEL * XA_W + XA_W * D_MODEL) * 2
    tiles = 2 * POST_TM * (2 * D_MODEL * 4 + D_MODEL * 2) + 2 * m * 2 * XA_W * 2
    vmem = weights + tiles + 6 * POST_TM * D_MODEL * 4
    return pl.pallas_call(
        functools.partial(_post_kernel, final_norm=final_norm),
        grid=(b, t // POST_TM),
        in_specs=[rows(D_MODEL), rows(LRU_W), rows(ML_W), rows(MO_W),
                  resident(D_MODEL, D_MODEL),
                  pl.BlockSpec((None, 1, D_MODEL), lambda bi, i: (l, 0, 0)),
                  resident(D_MODEL, XA_W),
                  pl.BlockSpec((1, m, 2 * XA_W), lambda bi, i: (bi, 0, 0)),
                  resident(XA_W, D_MODEL),
                  pl.BlockSpec((1, D_MODEL), lambda bi, i: (0, 0))],
        out_specs=rows(D_MODEL),
        out_shape=jax.ShapeDtypeStruct((b, t, D_MODEL), F32),
        scratch_shapes=[pltpu.VMEM((POST_TM, D_MODEL), F32)],
        compiler_params=pltpu.CompilerParams(
            dimension_semantics=("arbitrary", "arbitrary"), vmem_limit_bytes=_vmem_limit(vmem)),
        name="post_mix",
    )(x, yl, ym, yo, wout, g, wq, kv, wo, fg)


def _alibi_slopes(n):
    def pow2(m):
        start = 2.0 ** (-8.0 / m)
        return [start ** (i + 1) for i in range(m)]
    if math.log2(n).is_integer():
        s = pow2(n)
    else:
        c = 2 ** int(math.floor(math.log2(n)))
        s = pow2(c) + pow2(2 * c)[0::2][:n - c]
    return np.asarray(s, dtype=np.float32)


def _taps(w, perm):
    depth = w.shape[0]
    return jnp.transpose(w, perm).reshape(depth, ML_QKV_BLOCK, ML_W)


def kernel(x, mem, mix_norm_g, w_in, lru_conv_w, lru_conv_b, lru_wa, lru_ba, lru_wx, lru_bx, lru_lambda,
           ml_conv_w, ml_conv_b, ml_wq, ml_wk, ml_wv, ml_bi, ml_bf, ml_norm_g, w_out, xa_norm_g, mem_norm_g,
           xa_wq, xa_wkv, xa_wo, final_norm_g):
    b, t, d = x.shape
    depth = w_in.shape[0]
    nc = t // ML_CHUNK
    row = lambda v: v[:, None, :]
    moba_srows = _moba_slope_rows(_alibi_slopes(MO_HEADS))
    moba_kext = _moba_key_ext(t // MOBA_BLOCK)
    w_in_p = _w_in_prep(jnp.swapaxes(w_in, 1, 2))
    w_out_b = w_out.astype(BF16)
    wq_b, wkv_b, wo_b = xa_wq.astype(BF16), xa_wkv.astype(BF16), xa_wo.astype(BF16)
    lru_wax = jnp.concatenate([lru_wa, lru_wx], axis=-1).astype(BF16)
    ml_wq_t = _taps(ml_wq, (0, 2, 1, 3))
    ml_wkt_t = _taps(ml_wk, (0, 3, 1, 2))
    ml_wv_t = _taps(ml_wv, (0, 2, 1, 3))
    ml_gbias = jnp.broadcast_to(jnp.concatenate([ml_bi, ml_bf], axis=1)[:, :, None, None],
                                (depth, 2 * ML_HEADS, 1, ML_CHUNK))
    mix_g, xa_g, mem_g = row(mix_norm_g), row(xa_norm_g), row(mem_norm_g)
    lru_cb, lru_ba_r, lru_bx_r, lru_lam = row(lru_conv_b), row(lru_ba), row(lru_bx), row(lru_lambda)
    ml_cb, ml_ng = row(ml_conv_b), row(ml_norm_g)
    for l in range(depth):
        p = _in_proj(x.reshape(b * t, d), mix_g, w_in_p, l).reshape(b, t, IN_COLS_P)
        gates = p[:, :, C_GATE:C_GATE + 2 * ML_HEADS].astype(F32)
        gates = jnp.swapaxes(gates, 1, 2).reshape(b, 2 * ML_HEADS, nc, ML_CHUNK)
        y_lru = _lru_branch(p, lru_conv_w, lru_cb, lru_wax, lru_ba_r, lru_bx_r, lru_lam, l)
        y_ml = _mlstm_branch(p, gates, ml_gbias, ml_conv_w, ml_cb, ml_wq_t, ml_wkt_t, ml_wv_t, ml_ng, l)
        y_mo = _moba_branch(p, moba_srows, moba_kext)
        kv = _mem_kv(mem, mem_g, wkv_b, l)
        x = _post_mix(x, y_lru, y_ml, y_mo, w_out_b, xa_g, wq_b, kv, wo_b, final_norm_g[None, :], l,
                      final_norm=(l == depth - 1))
    return x
```

```python
import functools
import math

import jax
import jax.numpy as jnp
import numpy as np
from jax import lax
from jax.experimental import pallas as pl
from jax.experimental.pallas import tpu as pltpu

LANE = 128
SUBLANE = 8
V7X_VMEM_BYTES = 64 * 1024 * 1024

D_MODEL = 2048
LRU_W = 512
LRU_BLOCKS = 4
LRU_BW = LRU_W // LRU_BLOCKS
LRU_C = 8.0
ML_W = 768
ML_HEADS = 4
ML_DH = 192
ML_DHP = 256
ML_PAIRS = 2
ML_PW = 2 * ML_DH
ML_QKV_BLOCK = 4
ML_QKV_SHIFT = 2
ML_CHUNK = 128
MO_W = 768
MO_HEADS = 6
MO_DH = 128
MOBA_BLOCK = 256
MOBA_TOPK = 3
XA_HEADS = 4
XA_DH = 128
XA_W = XA_HEADS * XA_DH
RMS_EPS = 1e-6
LN_EPS = 1e-5
NEG_INF = -1e30
LOG2E = 1.4426950408889634

REF_LRU_END = 2 * LRU_W
REF_ML_END = REF_LRU_END + 3 * ML_W
REF_GATE_END = REF_ML_END + 2 * ML_HEADS

C_ML_U = 0
C_ML_O = C_ML_U + ML_W
C_ML_Z = C_ML_O + ML_W
C_LRU_X = C_ML_Z + ML_W
C_LRU_Z = C_LRU_X + LRU_W
C_MO_Q = C_LRU_Z + LRU_W
C_MO_K = C_MO_Q + MO_W
C_MO_V = C_MO_K + MO_W
C_MO_Z = C_MO_V + MO_W
C_GATE = C_MO_Z + MO_W
IN_TM = 512
IN_TN = 3328
IN_COLS_P = 6656
PREP_TR = 256
POST_TM = 512
POST_SUB = 2

BF16 = jnp.bfloat16
F32 = jnp.float32
NT_DIMS = (((1,), (1,)), ((), ()))


def _vmem_limit(nbytes):
    return int(min(V7X_VMEM_BYTES - (4 << 20), max(32 << 20, nbytes)))


def _rmsnorm_rows(x, g):
    ms = jnp.mean(x * x, axis=-1, keepdims=True)
    return x * lax.rsqrt(ms + RMS_EPS) * g


def _sigmoid(x):
    return jax.nn.sigmoid(x)


def _silu(x):
    return x * jax.nn.sigmoid(x)


def _softplus(x):
    return jnp.maximum(x, 0.0) + jnp.log1p(jnp.exp(-jnp.abs(x)))


def _shift_rows(x, s):
    rolled = pltpu.roll(x, s, axis=0)
    row = lax.broadcasted_iota(jnp.int32, x.shape, 0)
    return jnp.where(row >= s, rolled, 0.0)


def _causal_conv(x, w_ref, b_ref):
    k = w_ref.shape[0]
    acc = x * w_ref[k - 1:k, :]
    head = x[:SUBLANE]
    acc_head = head * w_ref[k - 1:k, :]
    for j in range(k - 1):
        acc = acc + pltpu.roll(x, k - 1 - j, axis=0) * w_ref[j:j + 1, :]
        acc_head = acc_head + _shift_rows(head, k - 1 - j) * w_ref[j:j + 1, :]
    return jnp.concatenate([acc_head, acc[SUBLANE:]], axis=0) + b_ref[...]


def _w_in_prep_kernel(w_ref, o_ref):
    last = pl.program_id(1) == pl.num_programs(1) - 1
    valid = jnp.where(last, 2 * ML_HEADS, PREP_TR)
    row = lax.broadcasted_iota(jnp.int32, w_ref.shape, 0)
    o_ref[...] = jnp.where(row < valid, w_ref[...], 0.0).astype(BF16)


def _w_in_prep_src_row(i):
    n_ml = (C_LRU_X - C_ML_U) // PREP_TR
    n_lru = (C_MO_Q - C_LRU_X) // PREP_TR
    n_mo = (C_GATE - C_MO_Q) // PREP_TR
    tile, unit = PREP_TR // SUBLANE, SUBLANE
    ml = REF_LRU_END // unit + i * tile
    lru = (i - n_ml) * tile
    mo = REF_GATE_END // unit + (i - n_ml - n_lru) * tile
    gate = REF_ML_END // unit
    return unit * jnp.where(i < n_ml, ml, jnp.where(i < n_ml + n_lru, lru, jnp.where(i < n_ml + n_lru + n_mo, mo, gate)))


def _w_in_prep(w_in_t):
    depth, _, k = w_in_t.shape
    return pl.pallas_call(
        _w_in_prep_kernel,
        grid=(depth, IN_COLS_P // PREP_TR),
        in_specs=[pl.BlockSpec((pl.Squeezed(), pl.Element(PREP_TR), pl.Element(k)),
                               lambda l, i: (l, _w_in_prep_src_row(i), 0))],
        out_specs=pl.BlockSpec((None, PREP_TR, k), lambda l, i: (l, i, 0)),
        out_shape=jax.ShapeDtypeStruct((depth, IN_COLS_P, k), BF16),
        compiler_params=pltpu.CompilerParams(
            dimension_semantics=("arbitrary", "arbitrary"), vmem_limit_bytes=_vmem_limit(32 << 20)),
        name="w_in_prep",
    )(w_in_t)


def _in_proj_kernel(x_ref, g_ref, w_ref, o_ref):
    xn = _rmsnorm_rows(x_ref[...], g_ref[...]).astype(BF16)
    for j in range(w_ref.shape[0] // IN_TN):
        cols = pl.ds(j * IN_TN, IN_TN)
        o_ref[:, cols] = lax.dot_general(xn, w_ref[cols, :], NT_DIMS, preferred_element_type=F32).astype(o_ref.dtype)


def _in_proj(x2d, g, w, l):
    m = x2d.shape[0]
    n = w.shape[1]
    vmem = (2 * IN_TM * D_MODEL * 4 + 2 * IN_TM * D_MODEL * 2 + D_MODEL * n * 2 + 2 * IN_TM * n * 2
            + 2 * IN_TM * IN_TN * 4)
    return pl.pallas_call(
        _in_proj_kernel,
        grid=(m // IN_TM,),
        in_specs=[
            pl.BlockSpec((IN_TM, D_MODEL), lambda i: (i, 0)),
            pl.BlockSpec((None, 1, D_MODEL), lambda i: (l, 0, 0)),
            pl.BlockSpec((None, n, D_MODEL), lambda i: (l, 0, 0), pipeline_mode=pl.Buffered(1)),
        ],
        out_specs=pl.BlockSpec((IN_TM, n), lambda i: (i, 0)),
        out_shape=jax.ShapeDtypeStruct((m, n), BF16),
        compiler_params=pltpu.CompilerParams(
            dimension_semantics=("arbitrary",), vmem_limit_bytes=_vmem_limit(vmem)),
        name="in_proj",
    )(x2d, g, w)


def _lru_kernel(x_ref, z_ref, cw_ref, cb_ref, wax_ref, ba_ref, bx_ref, lam_ref, o_ref, a_s, u_s):
    t = x_ref.shape[1]
    x = x_ref[0].astype(F32)
    xc = _causal_conv(x, cw_ref, cb_ref)
    pre = jnp.dot(xc.astype(BF16), wax_ref[0], preferred_element_type=F32)
    r = _sigmoid(pre[:, :LRU_BW] + ba_ref[...])
    i = _sigmoid(pre[:, LRU_BW:] + bx_ref[...])
    log_a = (-LRU_C) * r * _softplus(-lam_ref[...])
    a = jnp.exp(log_a)
    a_s[...] = a
    u_s[...] = jnp.sqrt(-jnp.tanh(log_a) * (1.0 + a * a)) * (i * xc)

    row = lax.broadcasted_iota(jnp.int32, (SUBLANE, LRU_BW), 0)

    def block(blk, h_prev):
        r0 = pl.multiple_of(blk * SUBLANE, SUBLANE)
        a_b = a_s[pl.ds(r0, SUBLANE), :]
        u_b = u_s[pl.ds(r0, SUBLANE), :]
        for s in (1, 2, 4):
            a_sh = jnp.where(row >= s, pltpu.roll(a_b, s, axis=0), 1.0)
            u_sh = jnp.where(row >= s, pltpu.roll(u_b, s, axis=0), 0.0)
            u_b = a_b * u_sh + u_b
            a_b = a_b * a_sh
        h = a_b * h_prev + u_b
        u_s[pl.ds(r0, SUBLANE), :] = h
        return jnp.broadcast_to(h[SUBLANE - 1:SUBLANE, :], (SUBLANE, LRU_BW))

    lax.fori_loop(0, t // SUBLANE, block, jnp.zeros((SUBLANE, LRU_BW), F32), unroll=4)
    z = z_ref[0].astype(F32)
    o_ref[0] = (u_s[...] * _silu(z)).astype(o_ref.dtype)


def _lru_branch(p, cw, cb, wax, ba, bx, lam, l):
    b, t, _ = p.shape
    xb, zb = C_LRU_X // LRU_BW, C_LRU_Z // LRU_BW
    vec = pl.BlockSpec((None, 1, LRU_BW), lambda bi, g: (l, 0, g))
    return pl.pallas_call(
        _lru_kernel,
        grid=(b, LRU_BLOCKS),
        in_specs=[
            pl.BlockSpec((1, t, LRU_BW), lambda bi, g: (bi, 0, xb + g)),
            pl.BlockSpec((1, t, LRU_BW), lambda bi, g: (bi, 0, zb + g)),
            pl.BlockSpec((None, cw.shape[1], LRU_BW), lambda bi, g: (l, 0, g)),
            vec,
            pl.BlockSpec((None, 1, LRU_BW, 2 * LRU_BW), lambda bi, g: (l, g, 0, 0)),
            vec, vec, vec,
        ],
        out_specs=pl.BlockSpec((1, t, LRU_BW), lambda bi, g: (bi, 0, g)),
        out_shape=jax.ShapeDtypeStruct((b, t, LRU_W), BF16),
        scratch_shapes=[pltpu.VMEM((t, LRU_BW), F32), pltpu.VMEM((t, LRU_BW), F32)],
        compiler_params=pltpu.CompilerParams(
            dimension_semantics=("arbitrary", "arbitrary"), vmem_limit_bytes=_vmem_limit(24 * t * LRU_BW * 4)),
        name="rg_lru",
    )(p, p, cw, cb, wax, ba, bx, lam)


def _blockdiag_in_out(w_ref, hh):
    wh = w_ref[...][:, hh * ML_DH:(hh + 1) * ML_DH]
    wh = jnp.concatenate([wh, jnp.zeros((ML_QKV_BLOCK, ML_DHP - ML_DH), F32)], axis=1)
    r = lax.broadcasted_iota(jnp.int32, (ML_PW, ML_DHP), 0)
    c = lax.broadcasted_iota(jnp.int32, (ML_PW, ML_DHP), 1)
    d = jnp.zeros((ML_PW, ML_DHP), F32)
    for i in range(ML_QKV_BLOCK):
        d = jnp.where((r & (ML_QKV_BLOCK - 1)) == i, wh[i:i + 1, :], d)
    keep = (((r >> ML_QKV_SHIFT) - hh * (ML_DH // ML_QKV_BLOCK)) == (c >> ML_QKV_SHIFT)) & (c < ML_DH)
    return jnp.where(keep, d, 0.0)


def _blockdiag_in_pair(w_ref, hh):
    wh = w_ref[...]
    r = lax.broadcasted_iota(jnp.int32, (ML_PW, ML_PW), 0)
    c = lax.broadcasted_iota(jnp.int32, (ML_PW, ML_PW), 1)
    d = jnp.zeros((ML_PW, ML_PW), F32)
    for i in range(ML_QKV_BLOCK):
        d = jnp.where((r & (ML_QKV_BLOCK - 1)) == i, wh[i:i + 1, :], d)
    keep = ((r >> ML_QKV_SHIFT) == (c >> ML_QKV_SHIFT)) & (c >= hh * ML_DH) & (c < (hh + 1) * ML_DH)
    return jnp.where(keep, d, 0.0)


def _blockdiag_out_in(w_ref, hh):
    wk = w_ref[...]
    o = lax.broadcasted_iota(jnp.int32, (ML_DHP, ML_PW), 0)
    r = lax.broadcasted_iota(jnp.int32, (ML_DHP, ML_PW), 1)
    d = jnp.zeros((ML_DHP, ML_PW), F32)
    for j in range(ML_QKV_BLOCK):
        d = jnp.where((o & (ML_QKV_BLOCK - 1)) == j, wk[j:j + 1, :], d)
    keep = ((o >> ML_QKV_SHIFT) == ((r >> ML_QKV_SHIFT) - hh * (ML_DH // ML_QKV_BLOCK))) & (o < ML_DH)
    return jnp.where(keep, d, 0.0)


ML_DEN_LANE = (ML_DH, 0)


def _mlstm_kernel(u_ref, og_ref, z_ref, ig_ref, fg_ref, bi_ref, bf_ref, cw_ref, cb_ref,
                  wq_ref, wkt_ref, wv_ref, ng_ref, y_ref,
                  dq_s, dkt_s, dv_s, q_s, kt_s, v_s, cp_s, c_s, r_s, w_s, col_s, so_s, sn_s):
    t = u_ref.shape[1]
    nc = t // ML_CHUNK
    L = ML_CHUNK
    heads = range(2)

    @pl.when(pl.program_id(1) == 0)
    def _():
        for hh in heads:
            dq_s[hh] = _blockdiag_in_out(wq_ref, hh).astype(BF16)
            dkt_s[hh] = _blockdiag_out_in(wkt_ref, hh).astype(BF16)
            dv_s[hh] = _blockdiag_in_pair(wv_ref, hh).astype(BF16)

    u = u_ref[0]
    uc = _silu(_causal_conv(u.astype(F32), cw_ref, cb_ref)).astype(BF16)
    for hh in heads:
        w0 = hh * (ML_PW - ML_DHP)
        uc_w, u_w = uc[:, w0:w0 + ML_DHP], u[:, w0:w0 + ML_DHP]
        q_s[hh] = jnp.dot(uc_w, dq_s[hh, w0:w0 + ML_DHP, :], preferred_element_type=F32).astype(BF16)
        kt = lax.dot_general(dkt_s[hh, :, w0:w0 + ML_DHP], uc_w, NT_DIMS, preferred_element_type=F32)
        kt = kt * (ML_DH ** -0.5)
        for c in range(nc):
            kt_s[hh, c] = kt[:, c * L:(c + 1) * L].astype(BF16)
        v = jnp.dot(u_w, dv_s[hh, w0:w0 + ML_DHP, :], preferred_element_type=F32)
        vlane = lax.broadcasted_iota(jnp.int32, v.shape, 1)
        v_s[hh] = jnp.where(vlane == ML_DEN_LANE[hh], 1.0, v).astype(BF16)

    glane = lax.broadcasted_iota(jnp.int32, (nc, L), 1)
    grow = lax.broadcasted_iota(jnp.int32, (nc, L), 0)
    row8 = lax.broadcasted_iota(jnp.int32, (SUBLANE, L), 0)
    for hh in heads:
        ig = ig_ref[0, hh] + bi_ref[hh]
        lf = -_softplus(-(fg_ref[0, hh] + bf_ref[hh]))
        b = lf
        for k in range(int(math.log2(L))):
            sh = 1 << k
            b = b + jnp.where(glane >= sh, pltpu.roll(b, sh, axis=1), 0.0)
        g = jnp.broadcast_to(b[:, L - 1:L], (nc, L))
        a = g - b + ig
        mloc = jnp.broadcast_to(jnp.max(a, axis=1, keepdims=True), (nc, L))
        m = jnp.zeros((1, L), F32)
        m_prev = jnp.zeros((nc, L), F32)
        m_next = jnp.zeros((nc, L), F32)
        for c in range(nc):
            m_prev = jnp.where(grow == c, m, m_prev)
            m = jnp.maximum(g[c:c + 1, :] + m, mloc[c:c + 1, :])
            m_next = jnp.where(grow == c, m, m_next)
        r = ig - b
        cmx = r
        for k in range(int(math.log2(L))):
            sh = 1 << k
            cmx = jnp.maximum(cmx, jnp.where(glane >= sh, pltpu.roll(cmx, sh, axis=1), -jnp.inf))
        mm = jnp.maximum(m_prev, cmx)
        s_int = jnp.exp(m_prev - mm)
        clamp = jnp.exp(-(b + mm))
        r_s[hh] = r
        w_s[hh] = jnp.exp(a - mloc)
        so_s[hh] = jnp.exp(g + m_prev - m_next)
        sn_s[hh] = jnp.exp(mloc - m_next)
        for c in range(nc):
            col_s[hh, c] = jnp.where(row8 == 0, mm[c:c + 1, :],
                                     jnp.where(row8 == 1, s_int[c:c + 1, :],
                                               jnp.where(row8 == 2, clamp[c:c + 1, :], 0.0)))

    zero_rows = jnp.zeros((ML_DHP - ML_DH, ML_PW), BF16)
    c_s[...] = jnp.zeros(c_s.shape, F32)

    def state(c, carry):
        r0 = pl.multiple_of(c * L, L)
        for hh in heads:
            c_prev = c_s[hh]
            cp_s[hh, c] = jnp.concatenate([c_prev.astype(BF16), zero_rows], axis=0)
            ktw = (kt_s[hh, c, :ML_DH, :].astype(F32) * w_s[hh, pl.ds(c, 1), :]).astype(BF16)
            c_loc = jnp.dot(ktw, v_s[hh, pl.ds(r0, L), :], preferred_element_type=F32)
            c_s[hh] = so_s[hh, pl.ds(c, 1), :][:, :1] * c_prev + sn_s[hh, pl.ds(c, 1), :][:, :1] * c_loc
        return carry

    lax.fori_loop(0, nc, state, 0, unroll=2)

    tri = (lax.broadcasted_iota(jnp.int32, (L, L), 0) >= lax.broadcasted_iota(jnp.int32, (L, L), 1))
    first = lax.broadcasted_iota(jnp.int32, (L, ML_PW), 1) < ML_DH
    pad_rows = jnp.zeros((L - SUBLANE, L), F32)

    def chunk(c, carry):
        r0 = pl.multiple_of(c * L, L)
        nds, invs = [], []
        for hh in heads:
            q_c = q_s[hh, pl.ds(r0, L), :]
            cols = jnp.concatenate([col_s[hh, c], pad_rows], axis=0).T
            rb = jnp.broadcast_to(r_s[hh, pl.ds(c, 1), :], (L, L))
            decay = jnp.exp(jnp.where(tri, rb - cols[:, 0:1], -jnp.inf))
            s_mat = jnp.dot(q_c, kt_s[hh, c], preferred_element_type=F32) * decay
            nd = (jnp.dot(s_mat.astype(BF16), v_s[hh, pl.ds(r0, L), :], preferred_element_type=F32)
                  + cols[:, 1:2] * jnp.dot(q_c, cp_s[hh, c], preferred_element_type=F32))
            den = nd[:, ML_DEN_LANE[hh]:ML_DEN_LANE[hh] + 1]
            nds.append(nd)
            invs.append(1.0 / jnp.maximum(jnp.abs(den), cols[:, 2:3]))
        x = _sigmoid(og_ref[0, pl.ds(r0, L), :].astype(F32)) * jnp.where(first, nds[0], nds[1])
        mu0 = jnp.sum(jnp.where(first, x, 0.0), axis=1, keepdims=True) * (1.0 / ML_DH)
        mu1 = jnp.sum(jnp.where(first, 0.0, x), axis=1, keepdims=True) * (1.0 / ML_DH)
        dev = x - jnp.where(first, mu0, mu1)
        sq = dev * dev
        var0 = jnp.sum(jnp.where(first, sq, 0.0), axis=1, keepdims=True) * (1.0 / ML_DH)
        var1 = jnp.sum(jnp.where(first, 0.0, sq), axis=1, keepdims=True) * (1.0 / ML_DH)
        f0 = invs[0] * lax.rsqrt(invs[0] * invs[0] * var0 + LN_EPS)
        f1 = invs[1] * lax.rsqrt(invs[1] * invs[1] * var1 + LN_EPS)
        zs = _silu(z_ref[0, pl.ds(r0, L), :].astype(F32)) * ng_ref[...]
        y_ref[0, pl.ds(r0, L), :] = (dev * jnp.where(first, f0, f1) * zs).astype(y_ref.dtype)
        return carry

    lax.fori_loop(0, nc, chunk, 0, unroll=4)


def _mlstm_branch(p, gates, gbias, cw, cb, wq, wkt, wv, ng, l):
    b, t, _ = p.shape
    nc = t // ML_CHUNK
    ub, ob, zb = C_ML_U // ML_PW, C_ML_O // ML_PW, C_ML_Z // ML_PW
    seq = lambda base: pl.BlockSpec((1, t, ML_PW), lambda pr, bi: (bi, 0, base + pr))
    vec = pl.BlockSpec((None, 1, ML_PW), lambda pr, bi: (l, 0, pr))
    taps = pl.BlockSpec((None, ML_QKV_BLOCK, ML_PW), lambda pr, bi: (l, 0, pr))
    rows = lambda: pltpu.VMEM((2, nc, ML_CHUNK), F32)
    return pl.pallas_call(
        _mlstm_kernel,
        grid=(ML_PAIRS, b),
        in_specs=[
            seq(ub), seq(ob), seq(zb),
            pl.BlockSpec((1, 2, nc, ML_CHUNK), lambda pr, bi: (bi, pr, 0, 0)),
            pl.BlockSpec((1, 2, nc, ML_CHUNK), lambda pr, bi: (bi, ML_PAIRS + pr, 0, 0)),
            pl.BlockSpec((None, 2, 1, ML_CHUNK), lambda pr, bi: (l, pr, 0, 0)),
            pl.BlockSpec((None, 2, 1, ML_CHUNK), lambda pr, bi: (l, ML_PAIRS + pr, 0, 0)),
            pl.BlockSpec((None, cw.shape[1], ML_PW), lambda pr, bi: (l, 0, pr)),
            vec, taps, taps, taps, vec,
        ],
        out_specs=pl.BlockSpec((1, t, ML_PW), lambda pr, bi: (bi, 0, pr)),
        out_shape=jax.ShapeDtypeStruct((b, t, ML_W), BF16),
        scratch_shapes=[
            pltpu.VMEM((2, ML_PW, ML_DHP), BF16),
            pltpu.VMEM((2, ML_DHP, ML_PW), BF16),
            pltpu.VMEM((2, ML_PW, ML_PW), BF16),
            pltpu.VMEM((2, t, ML_DHP), BF16),
            pltpu.VMEM((2, nc, ML_DHP, ML_CHUNK), BF16),
            pltpu.VMEM((2, t, ML_PW), BF16),
            pltpu.VMEM((2, nc, ML_DHP, ML_PW), BF16),
            pltpu.VMEM((2, ML_DH, ML_PW), F32),
            rows(), rows(),
            pltpu.VMEM((2, nc, SUBLANE, ML_CHUNK), F32),
            rows(), rows(),
        ],
        compiler_params=pltpu.CompilerParams(
            dimension_semantics=("arbitrary", "arbitrary"), vmem_limit_bytes=_vmem_limit(56 << 20)),
        name="mlstm",
    )(p, p, p, gates, gates, gbias, gbias, cw, cb, wq, wkt, wv, ng)


MOBA_SEL_LANE = 6


def _moba_key_ext(nb):
    ext = np.zeros((nb, MOBA_BLOCK, MO_DH), np.float32)
    for n in range(nb):
        ext[n, :, 0:3] = n * MOBA_BLOCK
        ext[n, :, 3:6] = np.arange(MOBA_BLOCK, dtype=np.float32)[:, None]
        ext[n, :, MOBA_SEL_LANE + n] = 1.0
    return jnp.asarray(ext, dtype=BF16)


def _moba_slope_rows(slopes):
    rows = np.zeros((len(slopes), 1, MO_DH), np.float32)
    for h, s in enumerate(slopes):
        rest = np.float32(np.float32(s) * np.float32(LOG2E))
        for i in range(3):
            piece = np.float32(rest).astype(BF16).astype(np.float32)
            rows[h, 0, i] = rows[h, 0, 3 + i] = piece
            rest = np.float32(rest - piece)
    return jnp.asarray(rows)


def _moba_kernel(q_ref, k_ref, v_ref, z_ref, srow_ref, kext_ref, o_ref, l_s, qa_s):
    t = q_ref.shape[1]
    nb = t // MOBA_BLOCK
    bs = MOBA_BLOCK
    qscale = (MO_DH ** -0.5) * LOG2E

    krow = lax.broadcasted_iota(jnp.int32, (LANE, MO_DH), 0)
    kmean = jnp.zeros((LANE, MO_DH), F32)
    for n in range(nb):
        mean_n = jnp.sum(k_ref[0, n * bs:(n + 1) * bs, :].astype(F32), axis=0, keepdims=True) * (1.0 / bs)
        kmean = jnp.where(krow == MOBA_SEL_LANE + n, mean_n, kmean)
    kmean = kmean.astype(BF16)

    lane = lax.broadcasted_iota(jnp.int32, (bs, MO_DH), 1)
    causal = (lax.broadcasted_iota(jnp.int32, (bs, bs), 0) >= lax.broadcasted_iota(jnp.int32, (bs, bs), 1))
    ones_col = jnp.where(lane == 0, 1.0, 0.0).astype(BF16)
    slope_cols = jnp.where(lane < MOBA_SEL_LANE, srow_ref[0], 0.0)

    def prepare(qb):
        q_b = q_ref[0, qb * bs:(qb + 1) * bs, :]
        q_ext = slope_cols
        if qb > MOBA_TOPK:
            gate = lax.dot_general(q_b, kmean, NT_DIMS, preferred_element_type=F32)
            beaten = jnp.zeros((bs, MO_DH), F32)
            for m in range(qb):
                gm = gate[:, MOBA_SEL_LANE + m:MOBA_SEL_LANE + m + 1]
                wins = (gm > gate) | ((gm == gate) & (lane > MOBA_SEL_LANE + m))
                beaten = beaten + jnp.where(wins, 1.0, 0.0)
            past = (lane >= MOBA_SEL_LANE) & (lane < MOBA_SEL_LANE + qb)
            q_ext = jnp.where(past & (beaten >= float(MOBA_TOPK)), NEG_INF, slope_cols)
        qa_s[qb] = jnp.concatenate([(q_b.astype(F32) * qscale).astype(BF16), q_ext.astype(BF16)], axis=1)

    def scores(qb):
        q_aug = qa_s[qb]
        l_q = l_s.at[qb % 2]
        mx = None
        for n in range(qb + 1):
            k_aug = jnp.concatenate([k_ref[0, n * bs:(n + 1) * bs, :], kext_ref[n]], axis=1)
            logit = lax.dot_general(q_aug, k_aug, NT_DIMS, preferred_element_type=F32)
            if n == qb:
                logit = jnp.where(causal, logit, NEG_INF)
            l_q[:, n * bs:(n + 1) * bs] = logit
            half = jnp.maximum(logit[:, :LANE], logit[:, LANE:])
            mx = half if mx is None else jnp.maximum(mx, half)
        return jnp.max(mx, axis=1, keepdims=True)

    def outputs(qb, m_row):
        l_q = l_s.at[qb % 2]
        acc = jnp.zeros((bs, 2 * MO_DH), F32)
        for n in range(qb + 1):
            pr = jnp.exp2(l_q[:, n * bs:(n + 1) * bs] - m_row).astype(BF16)
            v_aug = jnp.concatenate([v_ref[0, n * bs:(n + 1) * bs, :], ones_col], axis=1)
            acc = acc + jnp.dot(pr, v_aug, preferred_element_type=F32)
        z = z_ref[0, qb * bs:(qb + 1) * bs, :].astype(F32)
        inv = 1.0 / acc[:, MO_DH:MO_DH + 1]
        o_ref[0, qb * bs:(qb + 1) * bs, :] = (acc[:, :MO_DH] * (inv * _silu(z))).astype(o_ref.dtype)

    prepare(0)
    prepare(1)
    m_row = scores(0)
    for qb in range(nb):
        if qb + 2 < nb:
            prepare(qb + 2)
        m_next = scores(qb + 1) if qb + 1 < nb else None
        outputs(qb, m_row)
        m_row = m_next


def _moba_branch(p, srows, kext):
    b, t, _ = p.shape
    nb = t // MOBA_BLOCK
    qb, kb, vb, zb = (c // MO_DH for c in (C_MO_Q, C_MO_K, C_MO_V, C_MO_Z))
    seq = lambda base: pl.BlockSpec((1, t, MO_DH), lambda bi, h: (bi, 0, base + h))
    return pl.pallas_call(
        _moba_kernel,
        grid=(b, MO_HEADS),
        in_specs=[seq(qb), seq(kb), seq(vb), seq(zb),
                  pl.BlockSpec((1, 1, MO_DH), lambda bi, h: (h, 0, 0)),
                  pl.BlockSpec((nb, MOBA_BLOCK, MO_DH), lambda bi, h: (0, 0, 0))],
        out_specs=pl.BlockSpec((1, t, MO_DH), lambda bi, h: (bi, 0, h)),
        out_shape=jax.ShapeDtypeStruct((b, t, MO_W), BF16),
        scratch_shapes=[pltpu.VMEM((2, MOBA_BLOCK, t), F32), pltpu.VMEM((nb, MOBA_BLOCK, 2 * MO_DH), BF16)],
        compiler_params=pltpu.CompilerParams(
            dimension_semantics=("arbitrary", "arbitrary"), vmem_limit_bytes=_vmem_limit(32 << 20)),
        name="moba",
    )(p, p, p, p, srows, kext)


def _mem_kv_kernel(mem_ref, g_ref, w_ref, o_ref):
    hm = _rmsnorm_rows(mem_ref[0], g_ref[...]).astype(BF16)
    o_ref[0] = jnp.dot(hm, w_ref[...], preferred_element_type=F32).astype(o_ref.dtype)


def _mem_kv(mem, g, wkv, l):
    b, m, _ = mem.shape
    return pl.pallas_call(
        _mem_kv_kernel,
        grid=(b,),
        in_specs=[pl.BlockSpec((1, m, D_MODEL), lambda bi: (bi, 0, 0)),
                  pl.BlockSpec((None, 1, D_MODEL), lambda bi: (l, 0, 0)),
                  pl.BlockSpec((None, D_MODEL, 2 * XA_W), lambda bi: (l, 0, 0))],
        out_specs=pl.BlockSpec((1, m, 2 * XA_W), lambda bi: (bi, 0, 0)),
        out_shape=jax.ShapeDtypeStruct((b, m, 2 * XA_W), BF16),
        compiler_params=pltpu.CompilerParams(
            dimension_semantics=("arbitrary",), vmem_limit_bytes=_vmem_limit(32 << 20)),
        name="mem_kv",
    )(mem, g, wkv)


def _post_kernel(x_ref, yl_ref, ym_ref, yo_ref, wout_ref, g_ref, wq_ref, kv_ref, wo_ref, fg_ref, o_ref, x1_s,
                 *, final_norm):
    sub = x_ref.shape[1] // POST_SUB
    groups = [pl.ds(i * sub, sub) for i in range(POST_SUB)]

    for r in groups:
        acc = jnp.dot(yl_ref[0, r, :], wout_ref[0:LRU_W, :], preferred_element_type=F32)
        acc = acc + jnp.dot(ym_ref[0, r, :], wout_ref[LRU_W:LRU_W + ML_W, :], preferred_element_type=F32)
        acc = acc + jnp.dot(yo_ref[0, r, :], wout_ref[LRU_W + ML_W:, :], preferred_element_type=F32)
        x1_s[r, :] = x_ref[0, r, :] + acc
    qs = []
    for r in groups:
        hx = _rmsnorm_rows(x1_s[r, :], g_ref[...]).astype(BF16)
        qs.append(jnp.dot(hx, wq_ref[...], preferred_element_type=F32).astype(BF16))
    scores = []
    for q in qs:
        scores.append([
            lax.dot_general(q[:, h * XA_DH:(h + 1) * XA_DH], kv_ref[0, :, h * XA_DH:(h + 1) * XA_DH], NT_DIMS,
                            preferred_element_type=F32) * (XA_DH ** -0.5)
            for h in range(XA_HEADS)])
    outs = []
    for sc in scores:
        heads = []
        for h, s in enumerate(sc):
            e = jnp.exp(s - jnp.max(s, axis=1, keepdims=True))
            pr = e / jnp.sum(e, axis=1, keepdims=True)
            v_h = kv_ref[0, :, XA_W + h * XA_DH:XA_W + (h + 1) * XA_DH]
            heads.append(jnp.dot(pr.astype(BF16), v_h, preferred_element_type=F32).astype(BF16))
        outs.append(jnp.concatenate(heads, axis=1))
    for r, o in zip(groups, outs):
        y = x1_s[r, :] + jnp.dot(o, wo_ref[...], preferred_element_type=F32)
        if final_norm:
            y = _rmsnorm_rows(y, fg_ref[...])
        o_ref[0, r, :] = y


def _post_mix(x, yl, ym, yo, wout, g, wq, kv, wo, fg, l, final_norm):
    b, t, _ = x.shape
    m = kv.shape[1]
    rows = lambda width: pl.BlockSpec((1, POST_TM, width), lambda bi, i: (bi, i, 0))
    resident = lambda *shape: pl.BlockSpec((None,) + shape, lambda bi, i: (l,) + (0,) * len(shape),
                                           pipeline_mode=pl.Buffered(1))
    weights = (D_MODEL * D_MODEL + D_MODEL * XA_W + XA_W * D_MODEL) * 2
    tiles = 2 * POST_TM * (2 * D_MODEL * 4 + D_MODEL * 2) + 2 * m * 2 * XA_W * 2
    vmem = weights + tiles + 6 * POST_TM * D_MODEL * 4
    return pl.pallas_call(
        functools.partial(_post_kernel, final_norm=final_norm),
        grid=(b, t // POST_TM),
        in_specs=[rows(D_MODEL), rows(LRU_W), rows(ML_W), rows(MO_W),
                  resident(D_MODEL, D_MODEL),
                  pl.BlockSpec((None, 1, D_MODEL), lambda bi, i: (l, 0, 0)),
                  resident(D_MODEL, XA_W),
                  pl.BlockSpec((1, m, 2 * XA_W), lambda bi, i: (bi, 0, 0)),
                  resident(XA_W, D_MODEL),
                  pl.BlockSpec((1, D_MODEL), lambda bi, i: (0, 0))],
        out_specs=rows(D_MODEL),
        out_shape=jax.ShapeDtypeStruct((b, t, D_MODEL), F32),
        scratch_shapes=[pltpu.VMEM((POST_TM, D_MODEL), F32)],
        compiler_params=pltpu.CompilerParams(
            dimension_semantics=("arbitrary", "arbitrary"), vmem_limit_bytes=_vmem_limit(vmem)),
        name="post_mix",
    )(x, yl, ym, yo, wout, g, wq, kv, wo, fg)


def _alibi_slopes(n):
    def pow2(m):
        start = 2.0 ** (-8.0 / m)
        return [start ** (i + 1) for i in range(m)]
    if math.log2(n).is_integer():
        s = pow2(n)
    else:
        c = 2 ** int(math.floor(math.log2(n)))
        s = pow2(c) + pow2(2 * c)[0::2][:n - c]
    return np.asarray(s, dtype=np.float32)


def _taps(w, perm):
    depth = w.shape[0]
    return jnp.transpose(w, perm).reshape(depth, ML_QKV_BLOCK, ML_W)


def kernel(x, mem, mix_norm_g, w_in, lru_conv_w, lru_conv_b, lru_wa, lru_ba, lru_wx, lru_bx, lru_lambda,
           ml_conv_w, ml_conv_b, ml_wq, ml_wk, ml_wv, ml_bi, ml_bf, ml_norm_g, w_out, xa_norm_g, mem_norm_g,
           xa_wq, xa_wkv, xa_wo, final_norm_g):
    b, t, d = x.shape
    depth = w_in.shape[0]
    nc = t // ML_CHUNK
    row = lambda v: v[:, None, :]
    moba_srows = _moba_slope_rows(_alibi_slopes(MO_HEADS))
    moba_kext = _moba_key_ext(t // MOBA_BLOCK)
    w_in_p = _w_in_prep(jnp.swapaxes(w_in, 1, 2))
    w_out_b = w_out.astype(BF16)
    wq_b, wkv_b, wo_b = xa_wq.astype(BF16), xa_wkv.astype(BF16), xa_wo.astype(BF16)
    lru_wax = jnp.concatenate([lru_wa, lru_wx], axis=-1).astype(BF16)
    ml_wq_t = _taps(ml_wq, (0, 2, 1, 3))
    ml_wkt_t = _taps(ml_wk, (0, 3, 1, 2))
    ml_wv_t = _taps(ml_wv, (0, 2, 1, 3))
    ml_gbias = jnp.broadcast_to(jnp.concatenate([ml_bi, ml_bf], axis=1)[:, :, None, None],
                                (depth, 2 * ML_HEADS, 1, ML_CHUNK))
    mix_g, xa_g, mem_g = row(mix_norm_g), row(xa_norm_g), row(mem_norm_g)
    lru_cb, lru_ba_r, lru_bx_r, lru_lam = row(lru_conv_b), row(lru_ba), row(lru_bx), row(lru_lambda)
    ml_cb, ml_ng = row(ml_conv_b), row(ml_norm_g)
    for l in range(depth):
        p = _in_proj(x.reshape(b * t, d), mix_g, w_in_p, l).reshape(b, t, IN_COLS_P)
        gates = p[:, :, C_GATE:C_GATE + 2 * ML_HEADS].astype(F32)
        gates = jnp.swapaxes(gates, 1, 2).reshape(b, 2 * ML_HEADS, nc, ML_CHUNK)
        y_lru = _lru_branch(p, lru_conv_w, lru_cb, lru_wax, lru_ba_r, lru_bx_r, lru_lam, l)
        y_ml = _mlstm_branch(p, gates, ml_gbias, ml_conv_w, ml_cb, ml_wq_t, ml_wkt_t, ml_wv_t, ml_ng, l)
        y_mo = _moba_branch(p, moba_srows, moba_kext)
        kv = _mem_kv(mem, mem_g, wkv_b, l)
        x = _post_mix(x, y_lru, y_ml, y_mo, w_out_b, xa_g, wq_b, kv, wo_b, final_norm_g[None, :], l,
                      final_norm=(l == depth - 1))
    return x
```

```python
import functools
import math

import jax
import jax.numpy as jnp
import numpy as np
from jax import lax
from jax.experimental import pallas as pl
from jax.experimental.pallas import tpu as pltpu

LANE = 128
SUBLANE = 8
V7X_VMEM_BYTES = 64 * 1024 * 1024

D_MODEL = 2048
LRU_W = 512
LRU_BLOCKS = 4
LRU_BW = LRU_W // LRU_BLOCKS
LRU_C = 8.0
ML_W = 768
ML_HEADS = 4
ML_DH = 192
ML_DHP = 256
ML_PAIRS = 2
ML_PW = 2 * ML_DH
ML_QKV_BLOCK = 4
ML_QKV_SHIFT = 2
ML_CHUNK = 128
MO_W = 768
MO_HEADS = 6
MO_DH = 128
MOBA_BLOCK = 256
MOBA_TOPK = 3
XA_HEADS = 4
XA_DH = 128
XA_W = XA_HEADS * XA_DH
RMS_EPS = 1e-6
LN_EPS = 1e-5
NEG_INF = -1e30
LOG2E = 1.4426950408889634

REF_LRU_END = 2 * LRU_W
REF_ML_END = REF_LRU_END + 3 * ML_W
REF_GATE_END = REF_ML_END + 2 * ML_HEADS

C_ML_U = 0
C_ML_O = C_ML_U + ML_W
C_ML_Z = C_ML_O + ML_W
C_LRU_X = C_ML_Z + ML_W
C_LRU_Z = C_LRU_X + LRU_W
C_MO_Q = C_LRU_Z + LRU_W
C_MO_K = C_MO_Q + MO_W
C_MO_V = C_MO_K + MO_W
C_MO_Z = C_MO_V + MO_W
C_GATE = C_MO_Z + MO_W
R_MIX_G = 0
R_XA_G = R_MIX_G + D_MODEL
R_MEM_G = R_XA_G + D_MODEL
R_ML_CB = R_MEM_G + D_MODEL
R_ML_NG = R_ML_CB + ML_W
R_LRU_CB = R_ML_NG + ML_W
R_LRU_BA = R_LRU_CB + LRU_W
R_LRU_BX = R_LRU_BA + LRU_W
R_LRU_LAM = R_LRU_BX + LRU_W
IN_TM = 512
IN_TN = 3328
IN_COLS_P = 6656
PREP_TR = 256
POST_TM = 512
POST_SUB = 2

BF16 = jnp.bfloat16
F32 = jnp.float32
NT_DIMS = (((1,), (1,)), ((), ()))


def _vmem_limit(nbytes):
    return int(min(V7X_VMEM_BYTES - (4 << 20), max(32 << 20, nbytes)))


def _rmsnorm_rows(x, g):
    ms = jnp.mean(x * x, axis=-1, keepdims=True)
    return x * lax.rsqrt(ms + RMS_EPS) * g


def _sigmoid(x):
    return jax.nn.sigmoid(x)


def _silu(x):
    return x * jax.nn.sigmoid(x)


def _softplus(x):
    return jnp.maximum(x, 0.0) + jnp.log1p(jnp.exp(-jnp.abs(x)))


def _shift_rows(x, s):
    rolled = pltpu.roll(x, s, axis=0)
    row = lax.broadcasted_iota(jnp.int32, x.shape, 0)
    return jnp.where(row >= s, rolled, 0.0)


def _causal_conv(x, w_ref, b_ref):
    k = w_ref.shape[0]
    acc = x * w_ref[k - 1:k, :]
    head = x[:SUBLANE]
    acc_head = head * w_ref[k - 1:k, :]
    for j in range(k - 1):
        acc = acc + pltpu.roll(x, k - 1 - j, axis=0) * w_ref[j:j + 1, :]
        acc_head = acc_head + _shift_rows(head, k - 1 - j) * w_ref[j:j + 1, :]
    return jnp.concatenate([acc_head, acc[SUBLANE:]], axis=0) + b_ref[...]


def _w_in_prep_kernel(w_ref, o_ref):
    last = pl.program_id(1) == pl.num_programs(1) - 1
    valid = jnp.where(last, 2 * ML_HEADS, PREP_TR)
    row = lax.broadcasted_iota(jnp.int32, w_ref.shape, 0)
    o_ref[...] = jnp.where(row < valid, w_ref[...], 0.0).astype(BF16)


def _w_in_prep_src_row(i):
    n_ml = (C_LRU_X - C_ML_U) // PREP_TR
    n_lru = (C_MO_Q - C_LRU_X) // PREP_TR
    n_mo = (C_GATE - C_MO_Q) // PREP_TR
    tile, unit = PREP_TR // SUBLANE, SUBLANE
    ml = REF_LRU_END // unit + i * tile
    lru = (i - n_ml) * tile
    mo = REF_GATE_END // unit + (i - n_ml - n_lru) * tile
    gate = REF_ML_END // unit
    return unit * jnp.where(i < n_ml, ml, jnp.where(i < n_ml + n_lru, lru, jnp.where(i < n_ml + n_lru + n_mo, mo, gate)))


def _w_in_prep(w_in_t):
    depth, _, k = w_in_t.shape
    return pl.pallas_call(
        _w_in_prep_kernel,
        grid=(depth, IN_COLS_P // PREP_TR),
        in_specs=[pl.BlockSpec((pl.Squeezed(), pl.Element(PREP_TR), pl.Element(k)),
                               lambda l, i: (l, _w_in_prep_src_row(i), 0))],
        out_specs=pl.BlockSpec((None, PREP_TR, k), lambda l, i: (l, i, 0)),
        out_shape=jax.ShapeDtypeStruct((depth, IN_COLS_P, k), BF16),
        compiler_params=pltpu.CompilerParams(
            dimension_semantics=("arbitrary", "arbitrary"), vmem_limit_bytes=_vmem_limit(32 << 20)),
        name="w_in_prep",
    )(w_in_t)


def _in_proj_kernel(x_ref, g_ref, w_ref, o_ref, gt_ref):
    xn = _rmsnorm_rows(x_ref[...], g_ref[...]).astype(BF16)
    n_chunks = w_ref.shape[0] // IN_TN
    for j in range(n_chunks):
        cols = pl.ds(j * IN_TN, IN_TN)
        res = lax.dot_general(xn, w_ref[cols, :], NT_DIMS, preferred_element_type=F32)
        o_ref[:, cols] = res.astype(o_ref.dtype)
        if j == C_GATE // IN_TN:
            g0 = C_GATE - j * IN_TN
            gt_ref[...] = res[:, g0:g0 + LANE].T[:2 * ML_HEADS, :]


def _in_proj(x2d, g, w, l):
    m = x2d.shape[0]
    n = w.shape[1]
    vmem = (2 * IN_TM * D_MODEL * 4 + 2 * IN_TM * D_MODEL * 2 + D_MODEL * n * 2 + 2 * IN_TM * n * 2
            + 2 * IN_TM * IN_TN * 4)
    return pl.pallas_call(
        _in_proj_kernel,
        grid=(m // IN_TM,),
        in_specs=[
            pl.BlockSpec((IN_TM, D_MODEL), lambda i: (i, 0)),
            pl.BlockSpec((None, 1, D_MODEL), lambda i: (l, 0, R_MIX_G // D_MODEL)),
            pl.BlockSpec((None, n, D_MODEL), lambda i: (l, 0, 0), pipeline_mode=pl.Buffered(1)),
        ],
        out_specs=[pl.BlockSpec((IN_TM, n), lambda i: (i, 0)),
                   pl.BlockSpec((2 * ML_HEADS, IN_TM), lambda i: (0, i))],
        out_shape=[jax.ShapeDtypeStruct((m, n), BF16), jax.ShapeDtypeStruct((2 * ML_HEADS, m), F32)],
        compiler_params=pltpu.CompilerParams(
            dimension_semantics=("arbitrary",), vmem_limit_bytes=_vmem_limit(vmem)),
        name="in_proj",
    )(x2d, g, w)


def _lru_kernel(x_ref, z_ref, cw_ref, cb_ref, wax_ref, ba_ref, bx_ref, lam_ref, o_ref, a_s, u_s):
    t = x_ref.shape[1]
    x = x_ref[0].astype(F32)
    xc = _causal_conv(x, cw_ref, cb_ref)
    pre = jnp.dot(xc.astype(BF16), wax_ref[0], preferred_element_type=F32)
    r = _sigmoid(pre[:, :LRU_BW] + ba_ref[...])
    i = _sigmoid(pre[:, LRU_BW:] + bx_ref[...])
    log_a = (-LRU_C) * r * _softplus(-lam_ref[...])
    a = jnp.exp(log_a)
    a_s[...] = a
    u_s[...] = jnp.sqrt(-jnp.tanh(log_a) * (1.0 + a * a)) * (i * xc)

    row = lax.broadcasted_iota(jnp.int32, (SUBLANE, LRU_BW), 0)

    def block(blk, h_prev):
        r0 = pl.multiple_of(blk * SUBLANE, SUBLANE)
        a_b = a_s[pl.ds(r0, SUBLANE), :]
        u_b = u_s[pl.ds(r0, SUBLANE), :]
        for s in (1, 2, 4):
            a_sh = jnp.where(row >= s, pltpu.roll(a_b, s, axis=0), 1.0)
            u_sh = jnp.where(row >= s, pltpu.roll(u_b, s, axis=0), 0.0)
            u_b = a_b * u_sh + u_b
            a_b = a_b * a_sh
        h = a_b * h_prev + u_b
        u_s[pl.ds(r0, SUBLANE), :] = h
        return jnp.broadcast_to(h[SUBLANE - 1:SUBLANE, :], (SUBLANE, LRU_BW))

    lax.fori_loop(0, t // SUBLANE, block, jnp.zeros((SUBLANE, LRU_BW), F32), unroll=4)
    z = z_ref[0].astype(F32)
    o_ref[0] = (u_s[...] * _silu(z)).astype(o_ref.dtype)


def _lru_branch(p, cw, wax, rows, l):
    b, t, _ = p.shape
    xb, zb = C_LRU_X // LRU_BW, C_LRU_Z // LRU_BW
    vec = lambda off: pl.BlockSpec((None, 1, LRU_BW), lambda bi, g: (l, 0, off // LRU_BW + g))
    return pl.pallas_call(
        _lru_kernel,
        grid=(b, LRU_BLOCKS),
        in_specs=[
            pl.BlockSpec((1, t, LRU_BW), lambda bi, g: (bi, 0, xb + g)),
            pl.BlockSpec((1, t, LRU_BW), lambda bi, g: (bi, 0, zb + g)),
            pl.BlockSpec((None, cw.shape[1], LRU_BW), lambda bi, g: (l, 0, g)),
            vec(R_LRU_CB),
            pl.BlockSpec((None, 1, LRU_BW, 2 * LRU_BW), lambda bi, g: (l, g, 0, 0)),
            vec(R_LRU_BA), vec(R_LRU_BX), vec(R_LRU_LAM),
        ],
        out_specs=pl.BlockSpec((1, t, LRU_BW), lambda bi, g: (bi, 0, g)),
        out_shape=jax.ShapeDtypeStruct((b, t, LRU_W), BF16),
        scratch_shapes=[pltpu.VMEM((t, LRU_BW), F32), pltpu.VMEM((t, LRU_BW), F32)],
        compiler_params=pltpu.CompilerParams(
            dimension_semantics=("arbitrary", "arbitrary"), vmem_limit_bytes=_vmem_limit(24 * t * LRU_BW * 4)),
        name="rg_lru",
    )(p, p, cw, rows, wax, rows, rows, rows)


def _blockdiag_in_out(w_ref, hh):
    wh = w_ref[...][:, hh * ML_DH:(hh + 1) * ML_DH]
    wh = jnp.concatenate([wh, jnp.zeros((ML_QKV_BLOCK, ML_DHP - ML_DH), F32)], axis=1)
    r = lax.broadcasted_iota(jnp.int32, (ML_PW, ML_DHP), 0)
    c = lax.broadcasted_iota(jnp.int32, (ML_PW, ML_DHP), 1)
    d = jnp.zeros((ML_PW, ML_DHP), F32)
    for i in range(ML_QKV_BLOCK):
        d = jnp.where((r & (ML_QKV_BLOCK - 1)) == i, wh[i:i + 1, :], d)
    keep = (((r >> ML_QKV_SHIFT) - hh * (ML_DH // ML_QKV_BLOCK)) == (c >> ML_QKV_SHIFT)) & (c < ML_DH)
    return jnp.where(keep, d, 0.0)


def _blockdiag_in_pair(w_ref, hh):
    wh = w_ref[...]
    r = lax.broadcasted_iota(jnp.int32, (ML_PW, ML_PW), 0)
    c = lax.broadcasted_iota(jnp.int32, (ML_PW, ML_PW), 1)
    d = jnp.zeros((ML_PW, ML_PW), F32)
    for i in range(ML_QKV_BLOCK):
        d = jnp.where((r & (ML_QKV_BLOCK - 1)) == i, wh[i:i + 1, :], d)
    keep = ((r >> ML_QKV_SHIFT) == (c >> ML_QKV_SHIFT)) & (c >= hh * ML_DH) & (c < (hh + 1) * ML_DH)
    return jnp.where(keep, d, 0.0)


def _blockdiag_out_in(w_ref, hh):
    wk = w_ref[...]
    o = lax.broadcasted_iota(jnp.int32, (ML_DHP, ML_PW), 0)
    r = lax.broadcasted_iota(jnp.int32, (ML_DHP, ML_PW), 1)
    d = jnp.zeros((ML_DHP, ML_PW), F32)
    for j in range(ML_QKV_BLOCK):
        d = jnp.where((o & (ML_QKV_BLOCK - 1)) == j, wk[j:j + 1, :], d)
    keep = ((o >> ML_QKV_SHIFT) == ((r >> ML_QKV_SHIFT) - hh * (ML_DH // ML_QKV_BLOCK))) & (o < ML_DH)
    return jnp.where(keep, d, 0.0)


ML_DEN_LANE = (ML_DH, 0)


def _mlstm_kernel(u_ref, og_ref, z_ref, ig_ref, fg_ref, bi_ref, bf_ref, cw_ref, cb_ref,
                  wq_ref, wkt_ref, wv_ref, ng_ref, y_ref,
                  dq_s, dkt_s, dv_s, q_s, kt_s, v_s, cp_s, c_s, r_s, w_s, col_s, so_s, sn_s):
    t = u_ref.shape[1]
    nc = t // ML_CHUNK
    L = ML_CHUNK
    heads = range(2)

    @pl.when(pl.program_id(1) == 0)
    def _():
        for hh in heads:
            dq_s[hh] = _blockdiag_in_out(wq_ref, hh).astype(BF16)
            dkt_s[hh] = _blockdiag_out_in(wkt_ref, hh).astype(BF16)
            dv_s[hh] = _blockdiag_in_pair(wv_ref, hh).astype(BF16)

    u = u_ref[0]
    uc = _silu(_causal_conv(u.astype(F32), cw_ref, cb_ref)).astype(BF16)
    for hh in heads:
        w0 = hh * (ML_PW - ML_DHP)
        uc_w, u_w = uc[:, w0:w0 + ML_DHP], u[:, w0:w0 + ML_DHP]
        q_s[hh] = jnp.dot(uc_w, dq_s[hh, w0:w0 + ML_DHP, :], preferred_element_type=F32).astype(BF16)
        kt = lax.dot_general(dkt_s[hh, :, w0:w0 + ML_DHP], uc_w, NT_DIMS, preferred_element_type=F32)
        kt = kt * (ML_DH ** -0.5)
        for c in range(nc):
            kt_s[hh, c] = kt[:, c * L:(c + 1) * L].astype(BF16)
        v = jnp.dot(u_w, dv_s[hh, w0:w0 + ML_DHP, :], preferred_element_type=F32)
        vlane = lax.broadcasted_iota(jnp.int32, v.shape, 1)
        v_s[hh] = jnp.where(vlane == ML_DEN_LANE[hh], 1.0, v).astype(BF16)

    glane = lax.broadcasted_iota(jnp.int32, (nc, L), 1)
    grow = lax.broadcasted_iota(jnp.int32, (nc, L), 0)
    row8 = lax.broadcasted_iota(jnp.int32, (SUBLANE, L), 0)
    for hh in heads:
        ig = ig_ref[hh, 0] + bi_ref[hh]
        lf = -_softplus(-(fg_ref[hh, 0] + bf_ref[hh]))
        b = lf
        for k in range(int(math.log2(L))):
            sh = 1 << k
            b = b + jnp.where(glane >= sh, pltpu.roll(b, sh, axis=1), 0.0)
        g = jnp.broadcast_to(b[:, L - 1:L], (nc, L))
        a = g - b + ig
        mloc = jnp.broadcast_to(jnp.max(a, axis=1, keepdims=True), (nc, L))
        m = jnp.zeros((1, L), F32)
        m_prev = jnp.zeros((nc, L), F32)
        m_next = jnp.zeros((nc, L), F32)
        for c in range(nc):
            m_prev = jnp.where(grow == c, m, m_prev)
            m = jnp.maximum(g[c:c + 1, :] + m, mloc[c:c + 1, :])
            m_next = jnp.where(grow == c, m, m_next)
        r = ig - b
        cmx = r
        for k in range(int(math.log2(L))):
            sh = 1 << k
            cmx = jnp.maximum(cmx, jnp.where(glane >= sh, pltpu.roll(cmx, sh, axis=1), -jnp.inf))
        mm = jnp.maximum(m_prev, cmx)
        s_int = jnp.exp(m_prev - mm)
        clamp = jnp.exp(-(b + mm))
        r_s[hh] = r
        w_s[hh] = jnp.exp(a - mloc)
        so_s[hh] = jnp.exp(g + m_prev - m_next)
        sn_s[hh] = jnp.exp(mloc - m_next)
        for c in range(nc):
            col_s[hh, c] = jnp.where(row8 == 0, mm[c:c + 1, :],
                                     jnp.where(row8 == 1, s_int[c:c + 1, :],
                                               jnp.where(row8 == 2, clamp[c:c + 1, :], 0.0)))

    zero_rows = jnp.zeros((ML_DHP - ML_DH, ML_PW), BF16)
    c_s[...] = jnp.zeros(c_s.shape, F32)

    def state(c, carry):
        r0 = pl.multiple_of(c * L, L)
        for hh in heads:
            c_prev = c_s[hh]
            cp_s[hh, c] = jnp.concatenate([c_prev.astype(BF16), zero_rows], axis=0)
            ktw = (kt_s[hh, c, :ML_DH, :].astype(F32) * w_s[hh, pl.ds(c, 1), :]).astype(BF16)
            c_loc = jnp.dot(ktw, v_s[hh, pl.ds(r0, L), :], preferred_element_type=F32)
            c_s[hh] = so_s[hh, pl.ds(c, 1), :][:, :1] * c_prev + sn_s[hh, pl.ds(c, 1), :][:, :1] * c_loc
        return carry

    lax.fori_loop(0, nc, state, 0, unroll=2)

    tri = (lax.broadcasted_iota(jnp.int32, (L, L), 0) >= lax.broadcasted_iota(jnp.int32, (L, L), 1))
    first = lax.broadcasted_iota(jnp.int32, (L, ML_PW), 1) < ML_DH
    pad_rows = jnp.zeros((L - SUBLANE, L), F32)

    def chunk(c, carry):
        r0 = pl.multiple_of(c * L, L)
        nds, invs = [], []
        for hh in heads:
            q_c = q_s[hh, pl.ds(r0, L), :]
            cols = jnp.concatenate([col_s[hh, c], pad_rows], axis=0).T
            rb = jnp.broadcast_to(r_s[hh, pl.ds(c, 1), :], (L, L))
            decay = jnp.exp(jnp.where(tri, rb - cols[:, 0:1], -jnp.inf))
            s_mat = jnp.dot(q_c, kt_s[hh, c], preferred_element_type=F32) * decay
            nd = (jnp.dot(s_mat.astype(BF16), v_s[hh, pl.ds(r0, L), :], preferred_element_type=F32)
                  + cols[:, 1:2] * jnp.dot(q_c, cp_s[hh, c], preferred_element_type=F32))
            den = nd[:, ML_DEN_LANE[hh]:ML_DEN_LANE[hh] + 1]
            nds.append(nd)
            invs.append(1.0 / jnp.maximum(jnp.abs(den), cols[:, 2:3]))
        x = _sigmoid(og_ref[0, pl.ds(r0, L), :].astype(F32)) * jnp.where(first, nds[0], nds[1])
        mu0 = jnp.sum(jnp.where(first, x, 0.0), axis=1, keepdims=True) * (1.0 / ML_DH)
        mu1 = jnp.sum(jnp.where(first, 0.0, x), axis=1, keepdims=True) * (1.0 / ML_DH)
        dev = x - jnp.where(first, mu0, mu1)
        sq = dev * dev
        var0 = jnp.sum(jnp.where(first, sq, 0.0), axis=1, keepdims=True) * (1.0 / ML_DH)
        var1 = jnp.sum(jnp.where(first, 0.0, sq), axis=1, keepdims=True) * (1.0 / ML_DH)
        f0 = invs[0] * lax.rsqrt(invs[0] * invs[0] * var0 + LN_EPS)
        f1 = invs[1] * lax.rsqrt(invs[1] * invs[1] * var1 + LN_EPS)
        zs = _silu(z_ref[0, pl.ds(r0, L), :].astype(F32)) * ng_ref[...]
        y_ref[0, pl.ds(r0, L), :] = (dev * jnp.where(first, f0, f1) * zs).astype(y_ref.dtype)
        return carry

    lax.fori_loop(0, nc, chunk, 0, unroll=4)


def _mlstm_branch(p, gates, gbias, cw, wq, wkt, wv, rows, l):
    b, t, _ = p.shape
    nc = t // ML_CHUNK
    ub, ob, zb = C_ML_U // ML_PW, C_ML_O // ML_PW, C_ML_Z // ML_PW
    seq = lambda base: pl.BlockSpec((1, t, ML_PW), lambda pr, bi: (bi, 0, base + pr))
    vec = lambda off: pl.BlockSpec((None, 1, ML_PW), lambda pr, bi: (l, 0, off // ML_PW + pr))
    taps = pl.BlockSpec((None, ML_QKV_BLOCK, ML_PW), lambda pr, bi: (l, 0, pr))
    gate_rows = lambda: pltpu.VMEM((2, nc, ML_CHUNK), F32)
    return pl.pallas_call(
        _mlstm_kernel,
        grid=(ML_PAIRS, b),
        in_specs=[
            seq(ub), seq(ob), seq(zb),
            pl.BlockSpec((2, 1, nc, ML_CHUNK), lambda pr, bi: (pr, bi, 0, 0)),
            pl.BlockSpec((2, 1, nc, ML_CHUNK), lambda pr, bi: (ML_PAIRS + pr, bi, 0, 0)),
            pl.BlockSpec((None, 2, 1, ML_CHUNK), lambda pr, bi: (l, pr, 0, 0)),
            pl.BlockSpec((None, 2, 1, ML_CHUNK), lambda pr, bi: (l, ML_PAIRS + pr, 0, 0)),
            pl.BlockSpec((None, cw.shape[1], ML_PW), lambda pr, bi: (l, 0, pr)),
            vec(R_ML_CB), taps, taps, taps, vec(R_ML_NG),
        ],
        out_specs=pl.BlockSpec((1, t, ML_PW), lambda pr, bi: (bi, 0, pr)),
        out_shape=jax.ShapeDtypeStruct((b, t, ML_W), BF16),
        scratch_shapes=[
            pltpu.VMEM((2, ML_PW, ML_DHP), BF16),
            pltpu.VMEM((2, ML_DHP, ML_PW), BF16),
            pltpu.VMEM((2, ML_PW, ML_PW), BF16),
            pltpu.VMEM((2, t, ML_DHP), BF16),
            pltpu.VMEM((2, nc, ML_DHP, ML_CHUNK), BF16),
            pltpu.VMEM((2, t, ML_PW), BF16),
            pltpu.VMEM((2, nc, ML_DHP, ML_PW), BF16),
            pltpu.VMEM((2, ML_DH, ML_PW), F32),
            gate_rows(), gate_rows(),
            pltpu.VMEM((2, nc, SUBLANE, ML_CHUNK), F32),
            gate_rows(), gate_rows(),
        ],
        compiler_params=pltpu.CompilerParams(
            dimension_semantics=("arbitrary", "arbitrary"), vmem_limit_bytes=_vmem_limit(56 << 20)),
        name="mlstm",
    )(p, p, p, gates, gates, gbias, gbias, cw, rows, wq, wkt, wv, rows)


MOBA_SEL_LANE = 6


def _moba_key_ext(nb):
    ext = np.zeros((nb, MOBA_BLOCK, MO_DH), np.float32)
    for n in range(nb):
        ext[n, :, 0:3] = n * MOBA_BLOCK
        ext[n, :, 3:6] = np.arange(MOBA_BLOCK, dtype=np.float32)[:, None]
        ext[n, :, MOBA_SEL_LANE + n] = 1.0
    return jnp.asarray(ext, dtype=BF16)


def _moba_slope_rows(slopes):
    rows = np.zeros((len(slopes), 1, MO_DH), np.float32)
    for h, s in enumerate(slopes):
        rest = np.float32(np.float32(s) * np.float32(LOG2E))
        for i in range(3):
            piece = np.float32(rest).astype(BF16).astype(np.float32)
            rows[h, 0, i] = rows[h, 0, 3 + i] = piece
            rest = np.float32(rest - piece)
    return jnp.asarray(rows)


def _moba_kernel(q_ref, k_ref, v_ref, z_ref, srow_ref, kext_ref, o_ref, l_s, qa_s):
    t = q_ref.shape[1]
    nb = t // MOBA_BLOCK
    bs = MOBA_BLOCK
    qscale = (MO_DH ** -0.5) * LOG2E

    krow = lax.broadcasted_iota(jnp.int32, (LANE, MO_DH), 0)
    kmean = jnp.zeros((LANE, MO_DH), F32)
    for n in range(nb):
        mean_n = jnp.sum(k_ref[0, n * bs:(n + 1) * bs, :].astype(F32), axis=0, keepdims=True) * (1.0 / bs)
        kmean = jnp.where(krow == MOBA_SEL_LANE + n, mean_n, kmean)
    kmean = kmean.astype(BF16)

    lane = lax.broadcasted_iota(jnp.int32, (bs, MO_DH), 1)
    causal = (lax.broadcasted_iota(jnp.int32, (bs, bs), 0) >= lax.broadcasted_iota(jnp.int32, (bs, bs), 1))
    ones_col = jnp.where(lane == 0, 1.0, 0.0).astype(BF16)
    slope_cols = jnp.where(lane < MOBA_SEL_LANE, srow_ref[0], 0.0)

    def prepare(qb):
        q_b = q_ref[0, qb * bs:(qb + 1) * bs, :]
        q_ext = slope_cols
        if qb > MOBA_TOPK:
            gate = lax.dot_general(q_b, kmean, NT_DIMS, preferred_element_type=F32)
            beaten = jnp.zeros((bs, MO_DH), F32)
            for m in range(qb):
                gm = gate[:, MOBA_SEL_LANE + m:MOBA_SEL_LANE + m + 1]
                wins = (gm > gate) | ((gm == gate) & (lane > MOBA_SEL_LANE + m))
                beaten = beaten + jnp.where(wins, 1.0, 0.0)
            past = (lane >= MOBA_SEL_LANE) & (lane < MOBA_SEL_LANE + qb)
            q_ext = jnp.where(past & (beaten >= float(MOBA_TOPK)), NEG_INF, slope_cols)
        qa_s[qb] = jnp.concatenate([(q_b.astype(F32) * qscale).astype(BF16), q_ext.astype(BF16)], axis=1)

    def scores(qb):
        q_aug = qa_s[qb]
        l_q = l_s.at[qb % 2]
        mx = None
        for n in range(qb + 1):
            k_aug = jnp.concatenate([k_ref[0, n * bs:(n + 1) * bs, :], kext_ref[n]], axis=1)
            logit = lax.dot_general(q_aug, k_aug, NT_DIMS, preferred_element_type=F32)
            if n == qb:
                logit = jnp.where(causal, logit, NEG_INF)
            l_q[:, n * bs:(n + 1) * bs] = logit
            half = jnp.maximum(logit[:, :LANE], logit[:, LANE:])
            mx = half if mx is None else jnp.maximum(mx, half)
        return jnp.max(mx, axis=1, keepdims=True)

    def outputs(qb, m_row):
        l_q = l_s.at[qb % 2]
        acc = jnp.zeros((bs, 2 * MO_DH), F32)
        for n in range(qb + 1):
            pr = jnp.exp2(l_q[:, n * bs:(n + 1) * bs] - m_row).astype(BF16)
            v_aug = jnp.concatenate([v_ref[0, n * bs:(n + 1) * bs, :], ones_col], axis=1)
            acc = acc + jnp.dot(pr, v_aug, preferred_element_type=F32)
        z = z_ref[0, qb * bs:(qb + 1) * bs, :].astype(F32)
        inv = 1.0 / acc[:, MO_DH:MO_DH + 1]
        o_ref[0, qb * bs:(qb + 1) * bs, :] = (acc[:, :MO_DH] * (inv * _silu(z))).astype(o_ref.dtype)

    prepare(0)
    prepare(1)
    m_row = scores(0)
    for qb in range(nb):
        if qb + 2 < nb:
            prepare(qb + 2)
        m_next = scores(qb + 1) if qb + 1 < nb else None
        outputs(qb, m_row)
        m_row = m_next


def _moba_branch(p, srows, kext):
    b, t, _ = p.shape
    nb = t // MOBA_BLOCK
    qb, kb, vb, zb = (c // MO_DH for c in (C_MO_Q, C_MO_K, C_MO_V, C_MO_Z))
    seq = lambda base: pl.BlockSpec((1, t, MO_DH), lambda bi, h: (bi, 0, base + h))
    return pl.pallas_call(
        _moba_kernel,
        grid=(b, MO_HEADS),
        in_specs=[seq(qb), seq(kb), seq(vb), seq(zb),
                  pl.BlockSpec((1, 1, MO_DH), lambda bi, h: (h, 0, 0)),
                  pl.BlockSpec((nb, MOBA_BLOCK, MO_DH), lambda bi, h: (0, 0, 0))],
        out_specs=pl.BlockSpec((1, t, MO_DH), lambda bi, h: (bi, 0, h)),
        out_shape=jax.ShapeDtypeStruct((b, t, MO_W), BF16),
        scratch_shapes=[pltpu.VMEM((2, MOBA_BLOCK, t), F32), pltpu.VMEM((nb, MOBA_BLOCK, 2 * MO_DH), BF16)],
        compiler_params=pltpu.CompilerParams(
            dimension_semantics=("arbitrary", "arbitrary"), vmem_limit_bytes=_vmem_limit(32 << 20)),
        name="moba",
    )(p, p, p, p, srows, kext)


def _mem_kv_kernel(mem_ref, g_ref, w_ref, o_ref):
    hm = _rmsnorm_rows(mem_ref[0], g_ref[...]).astype(BF16)
    o_ref[0] = jnp.dot(hm, w_ref[...], preferred_element_type=F32).astype(o_ref.dtype)


def _mem_kv(mem, g, wkv, l):
    b, m, _ = mem.shape
    return pl.pallas_call(
        _mem_kv_kernel,
        grid=(b,),
        in_specs=[pl.BlockSpec((1, m, D_MODEL), lambda bi: (bi, 0, 0)),
                  pl.BlockSpec((None, 1, D_MODEL), lambda bi: (l, 0, R_MEM_G // D_MODEL)),
                  pl.BlockSpec((None, D_MODEL, 2 * XA_W), lambda bi: (l, 0, 0))],
        out_specs=pl.BlockSpec((1, m, 2 * XA_W), lambda bi: (bi, 0, 0)),
        out_shape=jax.ShapeDtypeStruct((b, m, 2 * XA_W), BF16),
        compiler_params=pltpu.CompilerParams(
            dimension_semantics=("arbitrary",), vmem_limit_bytes=_vmem_limit(32 << 20)),
        name="mem_kv",
    )(mem, g, wkv)


def _post_kernel(x_ref, yl_ref, ym_ref, yo_ref, wout_ref, g_ref, wq_ref, kv_ref, wo_ref, fg_ref, o_ref, x1_s,
                 *, final_norm):
    sub = x_ref.shape[1] // POST_SUB
    groups = [pl.ds(i * sub, sub) for i in range(POST_SUB)]

    for r in groups:
        acc = jnp.dot(yl_ref[0, r, :], wout_ref[0:LRU_W, :], preferred_element_type=F32)
        acc = acc + jnp.dot(ym_ref[0, r, :], wout_ref[LRU_W:LRU_W + ML_W, :], preferred_element_type=F32)
        acc = acc + jnp.dot(yo_ref[0, r, :], wout_ref[LRU_W + ML_W:, :], preferred_element_type=F32)
        x1_s[r, :] = x_ref[0, r, :] + acc
    qs = []
    for r in groups:
        hx = _rmsnorm_rows(x1_s[r, :], g_ref[...]).astype(BF16)
        qs.append(jnp.dot(hx, wq_ref[...], preferred_element_type=F32).astype(BF16))
    scores = []
    for q in qs:
        scores.append([
            lax.dot_general(q[:, h * XA_DH:(h + 1) * XA_DH], kv_ref[0, :, h * XA_DH:(h + 1) * XA_DH], NT_DIMS,
                            preferred_element_type=F32) * (XA_DH ** -0.5)
            for h in range(XA_HEADS)])
    outs = []
    for sc in scores:
        heads = []
        for h, s in enumerate(sc):
            e = jnp.exp(s - jnp.max(s, axis=1, keepdims=True))
            pr = e / jnp.sum(e, axis=1, keepdims=True)
            v_h = kv_ref[0, :, XA_W + h * XA_DH:XA_W + (h + 1) * XA_DH]
            heads.append(jnp.dot(pr.astype(BF16), v_h, preferred_element_type=F32).astype(BF16))
        outs.append(jnp.concatenate(heads, axis=1))
    for r, o in zip(groups, outs):
        y = x1_s[r, :] + jnp.dot(o, wo_ref[...], preferred_element_type=F32)
        if final_norm:
            y = _rmsnorm_rows(y, fg_ref[...])
        o_ref[0, r, :] = y


def _post_mix(x, yl, ym, yo, wout, g, wq, kv, wo, fg, l, final_norm):
    b, t, _ = x.shape
    m = kv.shape[1]
    rows = lambda width: pl.BlockSpec((1, POST_TM, width), lambda bi, i: (bi, i, 0))
    resident = lambda *shape: pl.BlockSpec((None,) + shape, lambda bi, i: (l,) + (0,) * len(shape),
                                           pipeline_mode=pl.Buffered(1))
    weights = (D_MODEL * D_MODEL + D_MODEL * XA_W + XA_W * D_MODEL) * 2
    tiles = 2 * POST_TM * (2 * D_MODEL * 4 + D_MODEL * 2) + 2 * m * 2 * XA_W * 2
    vmem = weights + tiles + 6 * POST_TM * D_MODEL * 4
    return pl.pallas_call(
        functools.partial(_post_kernel, final_norm=final_norm),
        grid=(b, t // POST_TM),
        in_specs=[rows(D_MODEL), rows(LRU_W), rows(ML_W), rows(MO_W),
                  resident(D_MODEL, D_MODEL),
                  pl.BlockSpec((None, 1, D_MODEL), lambda bi, i: (l, 0, R_XA_G // D_MODEL)),
                  resident(D_MODEL, XA_W),
                  pl.BlockSpec((1, m, 2 * XA_W), lambda bi, i: (bi, 0, 0)),
                  resident(XA_W, D_MODEL),
                  pl.BlockSpec((1, D_MODEL), lambda bi, i: (0, 0))],
        out_specs=rows(D_MODEL),
        out_shape=jax.ShapeDtypeStruct((b, t, D_MODEL), F32),
        scratch_shapes=[pltpu.VMEM((POST_TM, D_MODEL), F32)],
        compiler_params=pltpu.CompilerParams(
            dimension_semantics=("arbitrary", "arbitrary"), vmem_limit_bytes=_vmem_limit(vmem)),
        name="post_mix",
    )(x, yl, ym, yo, wout, g, wq, kv, wo, fg)


def _alibi_slopes(n):
    def pow2(m):
        start = 2.0 ** (-8.0 / m)
        return [start ** (i + 1) for i in range(m)]
    if math.log2(n).is_integer():
        s = pow2(n)
    else:
        c = 2 ** int(math.floor(math.log2(n)))
        s = pow2(c) + pow2(2 * c)[0::2][:n - c]
    return np.asarray(s, dtype=np.float32)


def _taps(w, perm):
    depth = w.shape[0]
    return jnp.transpose(w, perm).reshape(depth, ML_QKV_BLOCK, ML_W)


def kernel(x, mem, mix_norm_g, w_in, lru_conv_w, lru_conv_b, lru_wa, lru_ba, lru_wx, lru_bx, lru_lambda,
           ml_conv_w, ml_conv_b, ml_wq, ml_wk, ml_wv, ml_bi, ml_bf, ml_norm_g, w_out, xa_norm_g, mem_norm_g,
           xa_wq, xa_wkv, xa_wo, final_norm_g):
    b, t, d = x.shape
    depth = w_in.shape[0]
    nc = t // ML_CHUNK
    moba_srows = _moba_slope_rows(_alibi_slopes(MO_HEADS))
    moba_kext = _moba_key_ext(t // MOBA_BLOCK)
    w_in_p = _w_in_prep(jnp.swapaxes(w_in, 1, 2))
    w_out_b = w_out.astype(BF16)
    wq_b, wkv_b, wo_b = xa_wq.astype(BF16), xa_wkv.astype(BF16), xa_wo.astype(BF16)
    lru_wax = jnp.concatenate([lru_wa, lru_wx], axis=-1).astype(BF16)
    ml_wq_t = _taps(ml_wq, (0, 2, 1, 3))
    ml_wkt_t = _taps(ml_wk, (0, 3, 1, 2))
    ml_wv_t = _taps(ml_wv, (0, 2, 1, 3))
    ml_gbias = jnp.broadcast_to(jnp.concatenate([ml_bi, ml_bf], axis=1)[:, :, None, None],
                                (depth, 2 * ML_HEADS, 1, ML_CHUNK))
    rows = jnp.concatenate([mix_norm_g, xa_norm_g, mem_norm_g, ml_conv_b, ml_norm_g,
                            lru_conv_b, lru_ba, lru_bx, lru_lambda], axis=1)[:, None, :]
    for l in range(depth):
        p, gates = _in_proj(x.reshape(b * t, d), rows, w_in_p, l)
        p, gates = p.reshape(b, t, IN_COLS_P), gates.reshape(2 * ML_HEADS, b, nc, ML_CHUNK)
        y_lru = _lru_branch(p, lru_conv_w, lru_wax, rows, l)
        y_ml = _mlstm_branch(p, gates, ml_gbias, ml_conv_w, ml_wq_t, ml_wkt_t, ml_wv_t, rows, l)
        y_mo = _moba_branch(p, moba_srows, moba_kext)
        kv = _mem_kv(mem, rows, wkv_b, l)
        x = _post_mix(x, y_lru, y_ml, y_mo, w_out_b, rows, wq_b, kv, wo_b, final_norm_g[None, :], l,
                      final_norm=(l == depth - 1))
    return x
```

```python
import functools
import math

import jax
import jax.numpy as jnp
import numpy as np
from jax import lax
from jax.experimental import pallas as pl
from jax.experimental.pallas import tpu as pltpu

LANE = 128
SUBLANE = 8
V7X_VMEM_BYTES = 64 * 1024 * 1024

D_MODEL = 2048
LRU_W = 512
LRU_BLOCKS = 4
LRU_BW = LRU_W // LRU_BLOCKS
LRU_C = 8.0
ML_W = 768
ML_HEADS = 4
ML_DH = 192
ML_DHP = 256
ML_PAIRS = 2
ML_PW = 2 * ML_DH
ML_QKV_BLOCK = 4
ML_QKV_SHIFT = 2
ML_CHUNK = 128
MO_W = 768
MO_HEADS = 6
MO_DH = 128
MOBA_BLOCK = 256
MOBA_TOPK = 3
XA_HEADS = 4
XA_DH = 128
XA_W = XA_HEADS * XA_DH
RMS_EPS = 1e-6
LN_EPS = 1e-5
NEG_INF = -1e30
LOG2E = 1.4426950408889634

REF_LRU_END = 2 * LRU_W
REF_ML_END = REF_LRU_END + 3 * ML_W
REF_GATE_END = REF_ML_END + 2 * ML_HEADS

C_ML_U = 0
C_ML_O = C_ML_U + ML_W
C_ML_Z = C_ML_O + ML_W
C_LRU_X = C_ML_Z + ML_W
C_LRU_Z = C_LRU_X + LRU_W
C_MO_Q = C_LRU_Z + LRU_W
C_MO_K = C_MO_Q + MO_W
C_MO_V = C_MO_K + MO_W
C_MO_Z = C_MO_V + MO_W
C_GATE = C_MO_Z + MO_W
R_MIX_G = 0
R_XA_G = R_MIX_G + D_MODEL
R_MEM_G = R_XA_G + D_MODEL
R_ML_CB = R_MEM_G + D_MODEL
R_ML_NG = R_ML_CB + ML_W
R_LRU_CB = R_ML_NG + ML_W
R_LRU_BA = R_LRU_CB + LRU_W
R_LRU_BX = R_LRU_BA + LRU_W
R_LRU_LAM = R_LRU_BX + LRU_W
IN_TM = 512
IN_TN = 3328
IN_COLS_P = 6656
PREP_TR = 256
POST_TM = 512
POST_SUB = 2

BF16 = jnp.bfloat16
F32 = jnp.float32
NT_DIMS = (((1,), (1,)), ((), ()))


def _vmem_limit(nbytes):
    return int(min(V7X_VMEM_BYTES - (4 << 20), max(32 << 20, nbytes)))


def _rmsnorm_rows(x, g):
    ms = jnp.mean(x * x, axis=-1, keepdims=True)
    return x * lax.rsqrt(ms + RMS_EPS) * g


def _sigmoid(x):
    return jax.nn.sigmoid(x)


def _silu(x):
    return x * jax.nn.sigmoid(x)


def _softplus(x):
    return jnp.maximum(x, 0.0) + jnp.log1p(jnp.exp(-jnp.abs(x)))


def _shift_rows(x, s):
    rolled = pltpu.roll(x, s, axis=0)
    row = lax.broadcasted_iota(jnp.int32, x.shape, 0)
    return jnp.where(row >= s, rolled, 0.0)


def _causal_conv(x, w_ref, b_ref):
    k = w_ref.shape[0]
    acc = x * w_ref[k - 1:k, :]
    head = x[:SUBLANE]
    acc_head = head * w_ref[k - 1:k, :]
    for j in range(k - 1):
        acc = acc + pltpu.roll(x, k - 1 - j, axis=0) * w_ref[j:j + 1, :]
        acc_head = acc_head + _shift_rows(head, k - 1 - j) * w_ref[j:j + 1, :]
    return jnp.concatenate([acc_head, acc[SUBLANE:]], axis=0) + b_ref[...]


def _w_in_src_rows():
    n_ml = (C_LRU_X - C_ML_U) // PREP_TR
    n_lru = (C_MO_Q - C_LRU_X) // PREP_TR
    n_mo = (C_GATE - C_MO_Q) // PREP_TR
    return ([REF_LRU_END + i * PREP_TR for i in range(n_ml)] + [i * PREP_TR for i in range(n_lru)]
            + [REF_GATE_END + i * PREP_TR for i in range(n_mo)] + [REF_ML_END])


def _in_proj_kernel(x_ref, g_ref, w_hbm, o_ref, gt_ref, w_s, stage_s, sem, *, layer):
    @pl.when(pl.program_id(0) == 0)
    def _():
        src = _w_in_src_rows()

        def copy(k):
            return pltpu.make_async_copy(w_hbm.at[layer, pl.ds(src[k], PREP_TR), :], stage_s.at[k % 2], sem.at[k % 2])

        copy(0).start()
        for k in range(len(src)):
            if k + 1 < len(src):
                copy(k + 1).start()
            copy(k).wait()
            tile = stage_s[k % 2]
            if k == len(src) - 1:
                row = lax.broadcasted_iota(jnp.int32, tile.shape, 0)
                tile = jnp.where(row < 2 * ML_HEADS, tile, 0.0)
            w_s[k * PREP_TR:(k + 1) * PREP_TR, :] = tile.astype(BF16)

    xn = _rmsnorm_rows(x_ref[...], g_ref[...]).astype(BF16)
    n_chunks = w_s.shape[0] // IN_TN
    for j in range(n_chunks):
        cols = pl.ds(j * IN_TN, IN_TN)
        res = lax.dot_general(xn, w_s[cols, :], NT_DIMS, preferred_element_type=F32)
        o_ref[:, cols] = res.astype(o_ref.dtype)
        if j == C_GATE // IN_TN:
            g0 = C_GATE - j * IN_TN
            gt_ref[...] = res[:, g0:g0 + LANE].T[:2 * ML_HEADS, :]


def _in_proj(x2d, g, w_in_t, l):
    m = x2d.shape[0]
    n = IN_COLS_P
    vmem = (2 * IN_TM * D_MODEL * 4 + 2 * IN_TM * D_MODEL * 2 + D_MODEL * n * 2 + 2 * PREP_TR * D_MODEL * 4
            + 2 * IN_TM * n * 2 + 2 * IN_TM * IN_TN * 4)
    return pl.pallas_call(
        functools.partial(_in_proj_kernel, layer=l),
        grid=(m // IN_TM,),
        in_specs=[
            pl.BlockSpec((IN_TM, D_MODEL), lambda i: (i, 0)),
            pl.BlockSpec((None, 1, D_MODEL), lambda i: (l, 0, R_MIX_G // D_MODEL)),
            pl.BlockSpec(memory_space=pl.ANY),
        ],
        out_specs=[pl.BlockSpec((IN_TM, n), lambda i: (i, 0)),
                   pl.BlockSpec((2 * ML_HEADS, IN_TM), lambda i: (0, i))],
        out_shape=[jax.ShapeDtypeStruct((m, n), BF16), jax.ShapeDtypeStruct((2 * ML_HEADS, m), F32)],
        scratch_shapes=[pltpu.VMEM((n, D_MODEL), BF16),
                        pltpu.VMEM((2, PREP_TR, D_MODEL), F32),
                        pltpu.SemaphoreType.DMA((2,))],
        compiler_params=pltpu.CompilerParams(
            dimension_semantics=("arbitrary",), vmem_limit_bytes=_vmem_limit(vmem)),
        name="in_proj",
    )(x2d, g, w_in_t)


def _lru_kernel(x_ref, z_ref, cw_ref, cb_ref, wax_ref, ba_ref, bx_ref, lam_ref, o_ref, a_s, u_s):
    t = x_ref.shape[1]
    x = x_ref[0].astype(F32)
    xc = _causal_conv(x, cw_ref, cb_ref)
    pre = jnp.dot(xc.astype(BF16), wax_ref[0], preferred_element_type=F32)
    r = _sigmoid(pre[:, :LRU_BW] + ba_ref[...])
    i = _sigmoid(pre[:, LRU_BW:] + bx_ref[...])
    log_a = (-LRU_C) * r * _softplus(-lam_ref[...])
    a = jnp.exp(log_a)
    a_s[...] = a
    u_s[...] = jnp.sqrt(-jnp.tanh(log_a) * (1.0 + a * a)) * (i * xc)

    row = lax.broadcasted_iota(jnp.int32, (SUBLANE, LRU_BW), 0)

    def block(blk, h_prev):
        r0 = pl.multiple_of(blk * SUBLANE, SUBLANE)
        a_b = a_s[pl.ds(r0, SUBLANE), :]
        u_b = u_s[pl.ds(r0, SUBLANE), :]
        for s in (1, 2, 4):
            a_sh = jnp.where(row >= s, pltpu.roll(a_b, s, axis=0), 1.0)
            u_sh = jnp.where(row >= s, pltpu.roll(u_b, s, axis=0), 0.0)
            u_b = a_b * u_sh + u_b
            a_b = a_b * a_sh
        h = a_b * h_prev + u_b
        u_s[pl.ds(r0, SUBLANE), :] = h
        return jnp.broadcast_to(h[SUBLANE - 1:SUBLANE, :], (SUBLANE, LRU_BW))

    lax.fori_loop(0, t // SUBLANE, block, jnp.zeros((SUBLANE, LRU_BW), F32), unroll=4)
    z = z_ref[0].astype(F32)
    o_ref[0] = (u_s[...] * _silu(z)).astype(o_ref.dtype)


def _lru_branch(p, cw, wax, rows, l):
    b, t, _ = p.shape
    xb, zb = C_LRU_X // LRU_BW, C_LRU_Z // LRU_BW
    vec = lambda off: pl.BlockSpec((None, 1, LRU_BW), lambda bi, g: (l, 0, off // LRU_BW + g))
    return pl.pallas_call(
        _lru_kernel,
        grid=(b, LRU_BLOCKS),
        in_specs=[
            pl.BlockSpec((1, t, LRU_BW), lambda bi, g: (bi, 0, xb + g)),
            pl.BlockSpec((1, t, LRU_BW), lambda bi, g: (bi, 0, zb + g)),
            pl.BlockSpec((None, cw.shape[1], LRU_BW), lambda bi, g: (l, 0, g)),
            vec(R_LRU_CB),
            pl.BlockSpec((None, 1, LRU_BW, 2 * LRU_BW), lambda bi, g: (l, g, 0, 0)),
            vec(R_LRU_BA), vec(R_LRU_BX), vec(R_LRU_LAM),
        ],
        out_specs=pl.BlockSpec((1, t, LRU_BW), lambda bi, g: (bi, 0, g)),
        out_shape=jax.ShapeDtypeStruct((b, t, LRU_W), BF16),
        scratch_shapes=[pltpu.VMEM((t, LRU_BW), F32), pltpu.VMEM((t, LRU_BW), F32)],
        compiler_params=pltpu.CompilerParams(
            dimension_semantics=("arbitrary", "arbitrary"), vmem_limit_bytes=_vmem_limit(24 * t * LRU_BW * 4)),
        name="rg_lru",
    )(p, p, cw, rows, wax, rows, rows, rows)


def _blockdiag_in_out(w_ref, hh):
    wh = w_ref[...][:, hh * ML_DH:(hh + 1) * ML_DH]
    wh = jnp.concatenate([wh, jnp.zeros((ML_QKV_BLOCK, ML_DHP - ML_DH), F32)], axis=1)
    r = lax.broadcasted_iota(jnp.int32, (ML_PW, ML_DHP), 0)
    c = lax.broadcasted_iota(jnp.int32, (ML_PW, ML_DHP), 1)
    d = jnp.zeros((ML_PW, ML_DHP), F32)
    for i in range(ML_QKV_BLOCK):
        d = jnp.where((r & (ML_QKV_BLOCK - 1)) == i, wh[i:i + 1, :], d)
    keep = (((r >> ML_QKV_SHIFT) - hh * (ML_DH // ML_QKV_BLOCK)) == (c >> ML_QKV_SHIFT)) & (c < ML_DH)
    return jnp.where(keep, d, 0.0)


def _blockdiag_in_pair(w_ref, hh):
    wh = w_ref[...]
    r = lax.broadcasted_iota(jnp.int32, (ML_PW, ML_PW), 0)
    c = lax.broadcasted_iota(jnp.int32, (ML_PW, ML_PW), 1)
    d = jnp.zeros((ML_PW, ML_PW), F32)
    for i in range(ML_QKV_BLOCK):
        d = jnp.where((r & (ML_QKV_BLOCK - 1)) == i, wh[i:i + 1, :], d)
    keep = ((r >> ML_QKV_SHIFT) == (c >> ML_QKV_SHIFT)) & (c >= hh * ML_DH) & (c < (hh + 1) * ML_DH)
    return jnp.where(keep, d, 0.0)


def _blockdiag_out_in(w_ref, hh):
    wk = w_ref[...]
    o = lax.broadcasted_iota(jnp.int32, (ML_DHP, ML_PW), 0)
    r = lax.broadcasted_iota(jnp.int32, (ML_DHP, ML_PW), 1)
    d = jnp.zeros((ML_DHP, ML_PW), F32)
    for j in range(ML_QKV_BLOCK):
        d = jnp.where((o & (ML_QKV_BLOCK - 1)) == j, wk[j:j + 1, :], d)
    keep = ((o >> ML_QKV_SHIFT) == ((r >> ML_QKV_SHIFT) - hh * (ML_DH // ML_QKV_BLOCK))) & (o < ML_DH)
    return jnp.where(keep, d, 0.0)


ML_DEN_LANE = (ML_DH, 0)


def _mlstm_kernel(u_ref, og_ref, z_ref, ig_ref, fg_ref, bi_ref, bf_ref, cw_ref, cb_ref,
                  wq_ref, wkt_ref, wv_ref, ng_ref, y_ref,
                  dq_s, dkt_s, dv_s, q_s, kt_s, v_s, cp_s, c_s, r_s, w_s, col_s, so_s, sn_s):
    t = u_ref.shape[1]
    nc = t // ML_CHUNK
    L = ML_CHUNK
    heads = range(2)

    @pl.when(pl.program_id(1) == 0)
    def _():
        for hh in heads:
            dq_s[hh] = _blockdiag_in_out(wq_ref, hh).astype(BF16)
            dkt_s[hh] = _blockdiag_out_in(wkt_ref, hh).astype(BF16)
            dv_s[hh] = _blockdiag_in_pair(wv_ref, hh).astype(BF16)

    u = u_ref[0]
    uc = _silu(_causal_conv(u.astype(F32), cw_ref, cb_ref)).astype(BF16)
    for hh in heads:
        w0 = hh * (ML_PW - ML_DHP)
        uc_w, u_w = uc[:, w0:w0 + ML_DHP], u[:, w0:w0 + ML_DHP]
        q_s[hh] = jnp.dot(uc_w, dq_s[hh, w0:w0 + ML_DHP, :], preferred_element_type=F32).astype(BF16)
        kt = lax.dot_general(dkt_s[hh, :, w0:w0 + ML_DHP], uc_w, NT_DIMS, preferred_element_type=F32)
        kt = kt * (ML_DH ** -0.5)
        for c in range(nc):
            kt_s[hh, c] = kt[:, c * L:(c + 1) * L].astype(BF16)
        v = jnp.dot(u_w, dv_s[hh, w0:w0 + ML_DHP, :], preferred_element_type=F32)
        vlane = lax.broadcasted_iota(jnp.int32, v.shape, 1)
        v_s[hh] = jnp.where(vlane == ML_DEN_LANE[hh], 1.0, v).astype(BF16)

    glane = lax.broadcasted_iota(jnp.int32, (nc, L), 1)
    grow = lax.broadcasted_iota(jnp.int32, (nc, L), 0)
    row8 = lax.broadcasted_iota(jnp.int32, (SUBLANE, L), 0)
    for hh in heads:
        ig = ig_ref[hh, 0] + bi_ref[hh]
        lf = -_softplus(-(fg_ref[hh, 0] + bf_ref[hh]))
        b = lf
        for k in range(int(math.log2(L))):
            sh = 1 << k
            b = b + jnp.where(glane >= sh, pltpu.roll(b, sh, axis=1), 0.0)
        g = jnp.broadcast_to(b[:, L - 1:L], (nc, L))
        a = g - b + ig
        mloc = jnp.broadcast_to(jnp.max(a, axis=1, keepdims=True), (nc, L))
        m = jnp.zeros((1, L), F32)
        m_prev = jnp.zeros((nc, L), F32)
        m_next = jnp.zeros((nc, L), F32)
        for c in range(nc):
            m_prev = jnp.where(grow == c, m, m_prev)
            m = jnp.maximum(g[c:c + 1, :] + m, mloc[c:c + 1, :])
            m_next = jnp.where(grow == c, m, m_next)
        r = ig - b
        cmx = r
        for k in range(int(math.log2(L))):
            sh = 1 << k
            cmx = jnp.maximum(cmx, jnp.where(glane >= sh, pltpu.roll(cmx, sh, axis=1), -jnp.inf))
        mm = jnp.maximum(m_prev, cmx)
        s_int = jnp.exp(m_prev - mm)
        clamp = jnp.exp(-(b + mm))
        r_s[hh] = r
        w_s[hh] = jnp.exp(a - mloc)
        so_s[hh] = jnp.exp(g + m_prev - m_next)
        sn_s[hh] = jnp.exp(mloc - m_next)
        for c in range(nc):
            col_s[hh, c] = jnp.where(row8 == 0, mm[c:c + 1, :],
                                     jnp.where(row8 == 1, s_int[c:c + 1, :],
                                               jnp.where(row8 == 2, clamp[c:c + 1, :], 0.0)))

    zero_rows = jnp.zeros((ML_DHP - ML_DH, ML_PW), BF16)
    c_s[...] = jnp.zeros(c_s.shape, F32)

    def state(c, carry):
        r0 = pl.multiple_of(c * L, L)
        for hh in heads:
            c_prev = c_s[hh]
            cp_s[hh, c] = jnp.concatenate([c_prev.astype(BF16), zero_rows], axis=0)
            ktw = (kt_s[hh, c, :ML_DH, :].astype(F32) * w_s[hh, pl.ds(c, 1), :]).astype(BF16)
            c_loc = jnp.dot(ktw, v_s[hh, pl.ds(r0, L), :], preferred_element_type=F32)
            c_s[hh] = so_s[hh, pl.ds(c, 1), :][:, :1] * c_prev + sn_s[hh, pl.ds(c, 1), :][:, :1] * c_loc
        return carry

    lax.fori_loop(0, nc, state, 0, unroll=2)

    tri = (lax.broadcasted_iota(jnp.int32, (L, L), 0) >= lax.broadcasted_iota(jnp.int32, (L, L), 1))
    first = lax.broadcasted_iota(jnp.int32, (L, ML_PW), 1) < ML_DH
    pad_rows = jnp.zeros((L - SUBLANE, L), F32)

    def chunk(c, carry):
        r0 = pl.multiple_of(c * L, L)
        nds, invs = [], []
        for hh in heads:
            q_c = q_s[hh, pl.ds(r0, L), :]
            cols = jnp.concatenate([col_s[hh, c], pad_rows], axis=0).T
            rb = jnp.broadcast_to(r_s[hh, pl.ds(c, 1), :], (L, L))
            decay = jnp.exp(jnp.where(tri, rb - cols[:, 0:1], -jnp.inf))
            s_mat = jnp.dot(q_c, kt_s[hh, c], preferred_element_type=F32) * decay
            nd = (jnp.dot(s_mat.astype(BF16), v_s[hh, pl.ds(r0, L), :], preferred_element_type=F32)
                  + cols[:, 1:2] * jnp.dot(q_c, cp_s[hh, c], preferred_element_type=F32))
            den = nd[:, ML_DEN_LANE[hh]:ML_DEN_LANE[hh] + 1]
            nds.append(nd)
            invs.append(1.0 / jnp.maximum(jnp.abs(den), cols[:, 2:3]))
        x = _sigmoid(og_ref[0, pl.ds(r0, L), :].astype(F32)) * jnp.where(first, nds[0], nds[1])
        mu0 = jnp.sum(jnp.where(first, x, 0.0), axis=1, keepdims=True) * (1.0 / ML_DH)
        mu1 = jnp.sum(jnp.where(first, 0.0, x), axis=1, keepdims=True) * (1.0 / ML_DH)
        dev = x - jnp.where(first, mu0, mu1)
        sq = dev * dev
        var0 = jnp.sum(jnp.where(first, sq, 0.0), axis=1, keepdims=True) * (1.0 / ML_DH)
        var1 = jnp.sum(jnp.where(first, 0.0, sq), axis=1, keepdims=True) * (1.0 / ML_DH)
        f0 = invs[0] * lax.rsqrt(invs[0] * invs[0] * var0 + LN_EPS)
        f1 = invs[1] * lax.rsqrt(invs[1] * invs[1] * var1 + LN_EPS)
        zs = _silu(z_ref[0, pl.ds(r0, L), :].astype(F32)) * ng_ref[...]
        y_ref[0, pl.ds(r0, L), :] = (dev * jnp.where(first, f0, f1) * zs).astype(y_ref.dtype)
        return carry

    lax.fori_loop(0, nc, chunk, 0, unroll=4)


def _mlstm_branch(p, gates, gbias, cw, wq, wkt, wv, rows, l):
    b, t, _ = p.shape
    nc = t // ML_CHUNK
    ub, ob, zb = C_ML_U // ML_PW, C_ML_O // ML_PW, C_ML_Z // ML_PW
    seq = lambda base: pl.BlockSpec((1, t, ML_PW), lambda pr, bi: (bi, 0, base + pr))
    vec = lambda off: pl.BlockSpec((None, 1, ML_PW), lambda pr, bi: (l, 0, off // ML_PW + pr))
    taps = pl.BlockSpec((None, ML_QKV_BLOCK, ML_PW), lambda pr, bi: (l, 0, pr))
    gate_rows = lambda: pltpu.VMEM((2, nc, ML_CHUNK), F32)
    return pl.pallas_call(
        _mlstm_kernel,
        grid=(ML_PAIRS, b),
        in_specs=[
            seq(ub), seq(ob), seq(zb),
            pl.BlockSpec((2, 1, nc, ML_CHUNK), lambda pr, bi: (pr, bi, 0, 0)),
            pl.BlockSpec((2, 1, nc, ML_CHUNK), lambda pr, bi: (ML_PAIRS + pr, bi, 0, 0)),
            pl.BlockSpec((None, 2, 1, ML_CHUNK), lambda pr, bi: (l, pr, 0, 0)),
            pl.BlockSpec((None, 2, 1, ML_CHUNK), lambda pr, bi: (l, ML_PAIRS + pr, 0, 0)),
            pl.BlockSpec((None, cw.shape[1], ML_PW), lambda pr, bi: (l, 0, pr)),
            vec(R_ML_CB), taps, taps, taps, vec(R_ML_NG),
        ],
        out_specs=pl.BlockSpec((1, t, ML_PW), lambda pr, bi: (bi, 0, pr)),
        out_shape=jax.ShapeDtypeStruct((b, t, ML_W), BF16),
        scratch_shapes=[
            pltpu.VMEM((2, ML_PW, ML_DHP), BF16),
            pltpu.VMEM((2, ML_DHP, ML_PW), BF16),
            pltpu.VMEM((2, ML_PW, ML_PW), BF16),
            pltpu.VMEM((2, t, ML_DHP), BF16),
            pltpu.VMEM((2, nc, ML_DHP, ML_CHUNK), BF16),
            pltpu.VMEM((2, t, ML_PW), BF16),
            pltpu.VMEM((2, nc, ML_DHP, ML_PW), BF16),
            pltpu.VMEM((2, ML_DH, ML_PW), F32),
            gate_rows(), gate_rows(),
            pltpu.VMEM((2, nc, SUBLANE, ML_CHUNK), F32),
            gate_rows(), gate_rows(),
        ],
        compiler_params=pltpu.CompilerParams(
            dimension_semantics=("arbitrary", "arbitrary"), vmem_limit_bytes=_vmem_limit(56 << 20)),
        name="mlstm",
    )(p, p, p, gates, gates, gbias, gbias, cw, rows, wq, wkt, wv, rows)


MOBA_SEL_LANE = 6


def _moba_key_ext(nb):
    ext = np.zeros((nb, MOBA_BLOCK, MO_DH), np.float32)
    for n in range(nb):
        ext[n, :, 0:3] = n * MOBA_BLOCK
        ext[n, :, 3:6] = np.arange(MOBA_BLOCK, dtype=np.float32)[:, None]
        ext[n, :, MOBA_SEL_LANE + n] = 1.0
    return jnp.asarray(ext, dtype=BF16)


def _moba_slope_rows(slopes):
    rows = np.zeros((len(slopes), 1, MO_DH), np.float32)
    for h, s in enumerate(slopes):
        rest = np.float32(np.float32(s) * np.float32(LOG2E))
        for i in range(3):
            piece = np.float32(rest).astype(BF16).astype(np.float32)
            rows[h, 0, i] = rows[h, 0, 3 + i] = piece
            rest = np.float32(rest - piece)
    return jnp.asarray(rows)


def _moba_kernel(q_ref, k_ref, v_ref, z_ref, srow_ref, kext_ref, o_ref, l_s, qa_s):
    t = q_ref.shape[1]
    nb = t // MOBA_BLOCK
    bs = MOBA_BLOCK
    qscale = (MO_DH ** -0.5) * LOG2E

    krow = lax.broadcasted_iota(jnp.int32, (LANE, MO_DH), 0)
    kmean = jnp.zeros((LANE, MO_DH), F32)
    for n in range(nb):
        mean_n = jnp.sum(k_ref[0, n * bs:(n + 1) * bs, :].astype(F32), axis=0, keepdims=True) * (1.0 / bs)
        kmean = jnp.where(krow == MOBA_SEL_LANE + n, mean_n, kmean)
    kmean = kmean.astype(BF16)

    lane = lax.broadcasted_iota(jnp.int32, (bs, MO_DH), 1)
    causal = (lax.broadcasted_iota(jnp.int32, (bs, bs), 0) >= lax.broadcasted_iota(jnp.int32, (bs, bs), 1))
    ones_col = jnp.where(lane == 0, 1.0, 0.0).astype(BF16)
    slope_cols = jnp.where(lane < MOBA_SEL_LANE, srow_ref[0], 0.0)

    def prepare(qb):
        q_b = q_ref[0, qb * bs:(qb + 1) * bs, :]
        q_ext = slope_cols
        if qb > MOBA_TOPK:
            gate = lax.dot_general(q_b, kmean, NT_DIMS, preferred_element_type=F32)
            beaten = jnp.zeros((bs, MO_DH), F32)
            for m in range(qb):
                gm = gate[:, MOBA_SEL_LANE + m:MOBA_SEL_LANE + m + 1]
                wins = (gm > gate) | ((gm == gate) & (lane > MOBA_SEL_LANE + m))
                beaten = beaten + jnp.where(wins, 1.0, 0.0)
            past = (lane >= MOBA_SEL_LANE) & (lane < MOBA_SEL_LANE + qb)
            q_ext = jnp.where(past & (beaten >= float(MOBA_TOPK)), NEG_INF, slope_cols)
        qa_s[qb] = jnp.concatenate([(q_b.astype(F32) * qscale).astype(BF16), q_ext.astype(BF16)], axis=1)

    def scores(qb):
        q_aug = qa_s[qb]
        l_q = l_s.at[qb % 2]
        mx = None
        for n in range(qb + 1):
            k_aug = jnp.concatenate([k_ref[0, n * bs:(n + 1) * bs, :], kext_ref[n]], axis=1)
            logit = lax.dot_general(q_aug, k_aug, NT_DIMS, preferred_element_type=F32)
            if n == qb:
                logit = jnp.where(causal, logit, NEG_INF)
            l_q[:, n * bs:(n + 1) * bs] = logit
            half = jnp.maximum(logit[:, :LANE], logit[:, LANE:])
            mx = half if mx is None else jnp.maximum(mx, half)
        return jnp.max(mx, axis=1, keepdims=True)

    def outputs(qb, m_row):
        l_q = l_s.at[qb % 2]
        acc = jnp.zeros((bs, 2 * MO_DH), F32)
        for n in range(qb + 1):
            pr = jnp.exp2(l_q[:, n * bs:(n + 1) * bs] - m_row).astype(BF16)
            v_aug = jnp.concatenate([v_ref[0, n * bs:(n + 1) * bs, :], ones_col], axis=1)
            acc = acc + jnp.dot(pr, v_aug, preferred_element_type=F32)
        z = z_ref[0, qb * bs:(qb + 1) * bs, :].astype(F32)
        inv = 1.0 / acc[:, MO_DH:MO_DH + 1]
        o_ref[0, qb * bs:(qb + 1) * bs, :] = (acc[:, :MO_DH] * (inv * _silu(z))).astype(o_ref.dtype)

    prepare(0)
    prepare(1)
    m_row = scores(0)
    for qb in range(nb):
        if qb + 2 < nb:
            prepare(qb + 2)
        m_next = scores(qb + 1) if qb + 1 < nb else None
        outputs(qb, m_row)
        m_row = m_next


def _moba_branch(p, srows, kext):
    b, t, _ = p.shape
    nb = t // MOBA_BLOCK
    qb, kb, vb, zb = (c // MO_DH for c in (C_MO_Q, C_MO_K, C_MO_V, C_MO_Z))
    seq = lambda base: pl.BlockSpec((1, t, MO_DH), lambda bi, h: (bi, 0, base + h))
    return pl.pallas_call(
        _moba_kernel,
        grid=(b, MO_HEADS),
        in_specs=[seq(qb), seq(kb), seq(vb), seq(zb),
                  pl.BlockSpec((1, 1, MO_DH), lambda bi, h: (h, 0, 0)),
                  pl.BlockSpec((nb, MOBA_BLOCK, MO_DH), lambda bi, h: (0, 0, 0))],
        out_specs=pl.BlockSpec((1, t, MO_DH), lambda bi, h: (bi, 0, h)),
        out_shape=jax.ShapeDtypeStruct((b, t, MO_W), BF16),
        scratch_shapes=[pltpu.VMEM((2, MOBA_BLOCK, t), F32), pltpu.VMEM((nb, MOBA_BLOCK, 2 * MO_DH), BF16)],
        compiler_params=pltpu.CompilerParams(
            dimension_semantics=("arbitrary", "arbitrary"), vmem_limit_bytes=_vmem_limit(32 << 20)),
        name="moba",
    )(p, p, p, p, srows, kext)


def _mem_kv_kernel(mem_ref, g_ref, w_ref, o_ref):
    hm = _rmsnorm_rows(mem_ref[0], g_ref[...]).astype(BF16)
    o_ref[0] = jnp.dot(hm, w_ref[...], preferred_element_type=F32).astype(o_ref.dtype)


def _mem_kv(mem, g, wkv, l):
    b, m, _ = mem.shape
    return pl.pallas_call(
        _mem_kv_kernel,
        grid=(b,),
        in_specs=[pl.BlockSpec((1, m, D_MODEL), lambda bi: (bi, 0, 0)),
                  pl.BlockSpec((None, 1, D_MODEL), lambda bi: (l, 0, R_MEM_G // D_MODEL)),
                  pl.BlockSpec((None, D_MODEL, 2 * XA_W), lambda bi: (l, 0, 0))],
        out_specs=pl.BlockSpec((1, m, 2 * XA_W), lambda bi: (bi, 0, 0)),
        out_shape=jax.ShapeDtypeStruct((b, m, 2 * XA_W), BF16),
        compiler_params=pltpu.CompilerParams(
            dimension_semantics=("arbitrary",), vmem_limit_bytes=_vmem_limit(32 << 20)),
        name="mem_kv",
    )(mem, g, wkv)


def _post_kernel(x_ref, yl_ref, ym_ref, yo_ref, wout_ref, g_ref, wq_ref, kv_ref, wo_ref, fg_ref, o_ref, x1_s,
                 *, final_norm):
    sub = x_ref.shape[1] // POST_SUB
    groups = [pl.ds(i * sub, sub) for i in range(POST_SUB)]

    for r in groups:
        acc = jnp.dot(yl_ref[0, r, :], wout_ref[0:LRU_W, :], preferred_element_type=F32)
        acc = acc + jnp.dot(ym_ref[0, r, :], wout_ref[LRU_W:LRU_W + ML_W, :], preferred_element_type=F32)
        acc = acc + jnp.dot(yo_ref[0, r, :], wout_ref[LRU_W + ML_W:, :], preferred_element_type=F32)
        x1_s[r, :] = x_ref[0, r, :] + acc
    qs = []
    for r in groups:
        hx = _rmsnorm_rows(x1_s[r, :], g_ref[...]).astype(BF16)
        qs.append(jnp.dot(hx, wq_ref[...], preferred_element_type=F32).astype(BF16))
    scores = []
    for q in qs:
        scores.append([
            lax.dot_general(q[:, h * XA_DH:(h + 1) * XA_DH], kv_ref[0, :, h * XA_DH:(h + 1) * XA_DH], NT_DIMS,
                            preferred_element_type=F32) * (XA_DH ** -0.5)
            for h in range(XA_HEADS)])
    outs = []
    for sc in scores:
        heads = []
        for h, s in enumerate(sc):
            e = jnp.exp(s - jnp.max(s, axis=1, keepdims=True))
            pr = e / jnp.sum(e, axis=1, keepdims=True)
            v_h = kv_ref[0, :, XA_W + h * XA_DH:XA_W + (h + 1) * XA_DH]
            heads.append(jnp.dot(pr.astype(BF16), v_h, preferred_element_type=F32).astype(BF16))
        outs.append(jnp.concatenate(heads, axis=1))
    for r, o in zip(groups, outs):
        y = x1_s[r, :] + jnp.dot(o, wo_ref[...], preferred_element_type=F32)
        if final_norm:
            y = _rmsnorm_rows(y, fg_ref[...])
        o_ref[0, r, :] = y


def _post_mix(x, yl, ym, yo, wout, g, wq, kv, wo, fg, l, final_norm):
    b, t, _ = x.shape
    m = kv.shape[1]
    rows = lambda width: pl.BlockSpec((1, POST_TM, width), lambda bi, i: (bi, i, 0))
    resident = lambda *shape: pl.BlockSpec((None,) + shape, lambda bi, i: (l,) + (0,) * len(shape),
                                           pipeline_mode=pl.Buffered(1))
    weights = (D_MODEL * D_MODEL + D_MODEL * XA_W + XA_W * D_MODEL) * 2
    tiles = 2 * POST_TM * (2 * D_MODEL * 4 + D_MODEL * 2) + 2 * m * 2 * XA_W * 2
    vmem = weights + tiles + 6 * POST_TM * D_MODEL * 4
    return pl.pallas_call(
        functools.partial(_post_kernel, final_norm=final_norm),
        grid=(b, t // POST_TM),
        in_specs=[rows(D_MODEL), rows(LRU_W), rows(ML_W), rows(MO_W),
                  resident(D_MODEL, D_MODEL),
                  pl.BlockSpec((None, 1, D_MODEL), lambda bi, i: (l, 0, R_XA_G // D_MODEL)),
                  resident(D_MODEL, XA_W),
                  pl.BlockSpec((1, m, 2 * XA_W), lambda bi, i: (bi, 0, 0)),
                  resident(XA_W, D_MODEL),
                  pl.BlockSpec((1, D_MODEL), lambda bi, i: (0, 0))],
        out_specs=rows(D_MODEL),
        out_shape=jax.ShapeDtypeStruct((b, t, D_MODEL), F32),
        scratch_shapes=[pltpu.VMEM((POST_TM, D_MODEL), F32)],
        compiler_params=pltpu.CompilerParams(
            dimension_semantics=("arbitrary", "arbitrary"), vmem_limit_bytes=_vmem_limit(vmem)),
        name="post_mix",
    )(x, yl, ym, yo, wout, g, wq, kv, wo, fg)


def _alibi_slopes(n):
    def pow2(m):
        start = 2.0 ** (-8.0 / m)
        return [start ** (i + 1) for i in range(m)]
    if math.log2(n).is_integer():
        s = pow2(n)
    else:
        c = 2 ** int(math.floor(math.log2(n)))
        s = pow2(c) + pow2(2 * c)[0::2][:n - c]
    return np.asarray(s, dtype=np.float32)


def _taps(w, perm):
    depth = w.shape[0]
    return jnp.transpose(w, perm).reshape(depth, ML_QKV_BLOCK, ML_W)


def kernel(x, mem, mix_norm_g, w_in, lru_conv_w, lru_conv_b, lru_wa, lru_ba, lru_wx, lru_bx, lru_lambda,
           ml_conv_w, ml_conv_b, ml_wq, ml_wk, ml_wv, ml_bi, ml_bf, ml_norm_g, w_out, xa_norm_g, mem_norm_g,
           xa_wq, xa_wkv, xa_wo, final_norm_g):
    b, t, d = x.shape
    depth = w_in.shape[0]
    nc = t // ML_CHUNK
    moba_srows = _moba_slope_rows(_alibi_slopes(MO_HEADS))
    moba_kext = _moba_key_ext(t // MOBA_BLOCK)
    w_in_t = jnp.swapaxes(w_in, 1, 2)
    w_out_b = w_out.astype(BF16)
    wq_b, wkv_b, wo_b = xa_wq.astype(BF16), xa_wkv.astype(BF16), xa_wo.astype(BF16)
    lru_wax = jnp.concatenate([lru_wa, lru_wx], axis=-1).astype(BF16)
    ml_wq_t = _taps(ml_wq, (0, 2, 1, 3))
    ml_wkt_t = _taps(ml_wk, (0, 3, 1, 2))
    ml_wv_t = _taps(ml_wv, (0, 2, 1, 3))
    ml_gbias = jnp.broadcast_to(jnp.concatenate([ml_bi, ml_bf], axis=1)[:, :, None, None],
                                (depth, 2 * ML_HEADS, 1, ML_CHUNK))
    rows = jnp.concatenate([mix_norm_g, xa_norm_g, mem_norm_g, ml_conv_b, ml_norm_g,
                            lru_conv_b, lru_ba, lru_bx, lru_lambda], axis=1)[:, None, :]
    for l in range(depth):
        p, gates = _in_proj(x.reshape(b * t, d), rows, w_in_t, l)
        p, gates = p.reshape(b, t, IN_COLS_P), gates.reshape(2 * ML_HEADS, b, nc, ML_CHUNK)
        y_lru = _lru_branch(p, lru_conv_w, lru_wax, rows, l)
        y_ml = _mlstm_branch(p, gates, ml_gbias, ml_conv_w, ml_wq_t, ml_wkt_t, ml_wv_t, rows, l)
        y_mo = _moba_branch(p, moba_srows, moba_kext)
        kv = _mem_kv(mem, rows, wkv_b, l)
        x = _post_mix(x, y_lru, y_ml, y_mo, w_out_b, rows, wq_b, kv, wo_b, final_norm_g[None, :], l,
                      final_norm=(l == depth - 1))
    return x
```

```python
import functools
import math

import jax
import jax.numpy as jnp
import numpy as np
from jax import lax
from jax.experimental import pallas as pl
from jax.experimental.pallas import tpu as pltpu

LANE = 128
SUBLANE = 8
V7X_VMEM_BYTES = 64 * 1024 * 1024

D_MODEL = 2048
LRU_W = 512
LRU_BLOCKS = 4
LRU_BW = LRU_W // LRU_BLOCKS
LRU_C = 8.0
ML_W = 768
ML_HEADS = 4
ML_DH = 192
ML_DHP = 256
ML_PAIRS = 2
ML_PW = 2 * ML_DH
ML_QKV_BLOCK = 4
ML_QKV_SHIFT = 2
ML_CHUNK = 128
MO_W = 768
MO_HEADS = 6
MO_DH = 128
MOBA_BLOCK = 256
MOBA_TOPK = 3
XA_HEADS = 4
XA_DH = 128
XA_W = XA_HEADS * XA_DH
RMS_EPS = 1e-6
LN_EPS = 1e-5
NEG_INF = -1e30
LOG2E = 1.4426950408889634

REF_LRU_END = 2 * LRU_W
REF_ML_END = REF_LRU_END + 3 * ML_W
REF_GATE_END = REF_ML_END + 2 * ML_HEADS

C_ML_U = 0
C_ML_O = C_ML_U + ML_W
C_ML_Z = C_ML_O + ML_W
C_LRU_X = C_ML_Z + ML_W
C_LRU_Z = C_LRU_X + LRU_W
C_MO_Q = C_LRU_Z + LRU_W
C_MO_K = C_MO_Q + MO_W
C_MO_V = C_MO_K + MO_W
C_MO_Z = C_MO_V + MO_W
C_GATE = C_MO_Z + MO_W
R_MIX_G = 0
R_XA_G = R_MIX_G + D_MODEL
R_MEM_G = R_XA_G + D_MODEL
R_ML_CB = R_MEM_G + D_MODEL
R_ML_NG = R_ML_CB + ML_W
R_LRU_CB = R_ML_NG + ML_W
R_LRU_BA = R_LRU_CB + LRU_W
R_LRU_BX = R_LRU_BA + LRU_W
R_LRU_LAM = R_LRU_BX + LRU_W
IN_TM = 512
IN_TN = 3328
IN_COLS_P = 6656
PREP_TR = 256
STAGE_ROWS = 256
POST_TM = 512
POST_SUB = 2

BF16 = jnp.bfloat16
F32 = jnp.float32
NT_DIMS = (((1,), (1,)), ((), ()))


def _vmem_limit(nbytes):
    return int(min(V7X_VMEM_BYTES - (4 << 20), max(32 << 20, nbytes)))


def _rmsnorm_rows(x, g):
    ms = jnp.mean(x * x, axis=-1, keepdims=True)
    return x * lax.rsqrt(ms + RMS_EPS) * g


def _sigmoid(x):
    return jax.nn.sigmoid(x)


def _silu(x):
    return x * jax.nn.sigmoid(x)


def _softplus(x):
    return jnp.maximum(x, 0.0) + jnp.log1p(jnp.exp(-jnp.abs(x)))


def _shift_rows(x, s):
    rolled = pltpu.roll(x, s, axis=0)
    row = lax.broadcasted_iota(jnp.int32, x.shape, 0)
    return jnp.where(row >= s, rolled, 0.0)


def _causal_conv(x, w_ref, b_ref):
    k = w_ref.shape[0]
    acc = x * w_ref[k - 1:k, :]
    head = x[:SUBLANE]
    acc_head = head * w_ref[k - 1:k, :]
    for j in range(k - 1):
        acc = acc + pltpu.roll(x, k - 1 - j, axis=0) * w_ref[j:j + 1, :]
        acc_head = acc_head + _shift_rows(head, k - 1 - j) * w_ref[j:j + 1, :]
    return jnp.concatenate([acc_head, acc[SUBLANE:]], axis=0) + b_ref[...]


def _w_in_src_rows():
    n_ml = (C_LRU_X - C_ML_U) // PREP_TR
    n_lru = (C_MO_Q - C_LRU_X) // PREP_TR
    n_mo = (C_GATE - C_MO_Q) // PREP_TR
    return ([REF_LRU_END + i * PREP_TR for i in range(n_ml)] + [i * PREP_TR for i in range(n_lru)]
            + [REF_GATE_END + i * PREP_TR for i in range(n_mo)] + [REF_ML_END])


def _in_proj_kernel(x_ref, g_ref, w_hbm, o_ref, gt_ref, w_s, stage_s, sem, *, layer):
    @pl.when(pl.program_id(0) == 0)
    def _():
        src = _w_in_src_rows()

        def copy(k):
            return pltpu.make_async_copy(w_hbm.at[layer, pl.ds(src[k], PREP_TR), :], stage_s.at[k % 2], sem.at[k % 2])

        copy(0).start()
        for k in range(len(src)):
            if k + 1 < len(src):
                copy(k + 1).start()
            copy(k).wait()
            tile = stage_s[k % 2]
            if k == len(src) - 1:
                row = lax.broadcasted_iota(jnp.int32, tile.shape, 0)
                tile = jnp.where(row < 2 * ML_HEADS, tile, 0.0)
            w_s[k * PREP_TR:(k + 1) * PREP_TR, :] = tile.astype(BF16)

    xn = _rmsnorm_rows(x_ref[...], g_ref[...]).astype(BF16)
    n_chunks = w_s.shape[0] // IN_TN
    for j in range(n_chunks):
        cols = pl.ds(j * IN_TN, IN_TN)
        res = lax.dot_general(xn, w_s[cols, :], NT_DIMS, preferred_element_type=F32)
        o_ref[:, cols] = res.astype(o_ref.dtype)
        if j == C_GATE // IN_TN:
            g0 = C_GATE - j * IN_TN
            gt_ref[...] = res[:, g0:g0 + LANE].T[:2 * ML_HEADS, :]


def _in_proj(x2d, g, w_in_t, l):
    m = x2d.shape[0]
    n = IN_COLS_P
    vmem = (2 * IN_TM * D_MODEL * 4 + 2 * IN_TM * D_MODEL * 2 + D_MODEL * n * 2 + 2 * PREP_TR * D_MODEL * 4
            + 2 * IN_TM * n * 2 + 2 * IN_TM * IN_TN * 4)
    return pl.pallas_call(
        functools.partial(_in_proj_kernel, layer=l),
        grid=(m // IN_TM,),
        in_specs=[
            pl.BlockSpec((IN_TM, D_MODEL), lambda i: (i, 0)),
            pl.BlockSpec((None, 1, D_MODEL), lambda i: (l, 0, R_MIX_G // D_MODEL)),
            pl.BlockSpec(memory_space=pl.ANY),
        ],
        out_specs=[pl.BlockSpec((IN_TM, n), lambda i: (i, 0)),
                   pl.BlockSpec((2 * ML_HEADS, IN_TM), lambda i: (0, i))],
        out_shape=[jax.ShapeDtypeStruct((m, n), BF16), jax.ShapeDtypeStruct((2 * ML_HEADS, m), F32)],
        scratch_shapes=[pltpu.VMEM((n, D_MODEL), BF16),
                        pltpu.VMEM((2, PREP_TR, D_MODEL), F32),
                        pltpu.SemaphoreType.DMA((2,))],
        compiler_params=pltpu.CompilerParams(
            dimension_semantics=("arbitrary",), vmem_limit_bytes=_vmem_limit(vmem)),
        name="in_proj",
    )(x2d, g, w_in_t)


def _lru_kernel(x_ref, z_ref, cw_ref, cb_ref, wax_ref, ba_ref, bx_ref, lam_ref, o_ref, a_s, u_s):
    t = x_ref.shape[1]
    x = x_ref[0].astype(F32)
    xc = _causal_conv(x, cw_ref, cb_ref)
    pre = jnp.dot(xc.astype(BF16), wax_ref[0], preferred_element_type=F32)
    r = _sigmoid(pre[:, :LRU_BW] + ba_ref[...])
    i = _sigmoid(pre[:, LRU_BW:] + bx_ref[...])
    log_a = (-LRU_C) * r * _softplus(-lam_ref[...])
    a = jnp.exp(log_a)
    a_s[...] = a
    u_s[...] = jnp.sqrt(-jnp.tanh(log_a) * (1.0 + a * a)) * (i * xc)

    row = lax.broadcasted_iota(jnp.int32, (SUBLANE, LRU_BW), 0)

    def block(blk, h_prev):
        r0 = pl.multiple_of(blk * SUBLANE, SUBLANE)
        a_b = a_s[pl.ds(r0, SUBLANE), :]
        u_b = u_s[pl.ds(r0, SUBLANE), :]
        for s in (1, 2, 4):
            a_sh = jnp.where(row >= s, pltpu.roll(a_b, s, axis=0), 1.0)
            u_sh = jnp.where(row >= s, pltpu.roll(u_b, s, axis=0), 0.0)
            u_b = a_b * u_sh + u_b
            a_b = a_b * a_sh
        h = a_b * h_prev + u_b
        u_s[pl.ds(r0, SUBLANE), :] = h
        return jnp.broadcast_to(h[SUBLANE - 1:SUBLANE, :], (SUBLANE, LRU_BW))

    lax.fori_loop(0, t // SUBLANE, block, jnp.zeros((SUBLANE, LRU_BW), F32), unroll=4)
    z = z_ref[0].astype(F32)
    o_ref[0] = (u_s[...] * _silu(z)).astype(o_ref.dtype)


def _lru_branch(p, cw, wax, rows, l):
    b, t, _ = p.shape
    xb, zb = C_LRU_X // LRU_BW, C_LRU_Z // LRU_BW
    vec = lambda off: pl.BlockSpec((None, 1, LRU_BW), lambda bi, g: (l, 0, off // LRU_BW + g))
    return pl.pallas_call(
        _lru_kernel,
        grid=(b, LRU_BLOCKS),
        in_specs=[
            pl.BlockSpec((1, t, LRU_BW), lambda bi, g: (bi, 0, xb + g)),
            pl.BlockSpec((1, t, LRU_BW), lambda bi, g: (bi, 0, zb + g)),
            pl.BlockSpec((None, cw.shape[1], LRU_BW), lambda bi, g: (l, 0, g)),
            vec(R_LRU_CB),
            pl.BlockSpec((None, 1, LRU_BW, 2 * LRU_BW), lambda bi, g: (l, g, 0, 0)),
            vec(R_LRU_BA), vec(R_LRU_BX), vec(R_LRU_LAM),
        ],
        out_specs=pl.BlockSpec((1, t, LRU_BW), lambda bi, g: (bi, 0, g)),
        out_shape=jax.ShapeDtypeStruct((b, t, LRU_W), BF16),
        scratch_shapes=[pltpu.VMEM((t, LRU_BW), F32), pltpu.VMEM((t, LRU_BW), F32)],
        compiler_params=pltpu.CompilerParams(
            dimension_semantics=("arbitrary", "arbitrary"), vmem_limit_bytes=_vmem_limit(24 * t * LRU_BW * 4)),
        name="rg_lru",
    )(p, p, cw, rows, wax, rows, rows, rows)


def _blockdiag_in_out(w_ref, hh):
    wh = w_ref[...][:, hh * ML_DH:(hh + 1) * ML_DH]
    wh = jnp.concatenate([wh, jnp.zeros((ML_QKV_BLOCK, ML_DHP - ML_DH), F32)], axis=1)
    r = lax.broadcasted_iota(jnp.int32, (ML_PW, ML_DHP), 0)
    c = lax.broadcasted_iota(jnp.int32, (ML_PW, ML_DHP), 1)
    d = jnp.zeros((ML_PW, ML_DHP), F32)
    for i in range(ML_QKV_BLOCK):
        d = jnp.where((r & (ML_QKV_BLOCK - 1)) == i, wh[i:i + 1, :], d)
    keep = (((r >> ML_QKV_SHIFT) - hh * (ML_DH // ML_QKV_BLOCK)) == (c >> ML_QKV_SHIFT)) & (c < ML_DH)
    return jnp.where(keep, d, 0.0)


def _blockdiag_in_pair(w_ref, hh):
    wh = w_ref[...]
    r = lax.broadcasted_iota(jnp.int32, (ML_PW, ML_PW), 0)
    c = lax.broadcasted_iota(jnp.int32, (ML_PW, ML_PW), 1)
    d = jnp.zeros((ML_PW, ML_PW), F32)
    for i in range(ML_QKV_BLOCK):
        d = jnp.where((r & (ML_QKV_BLOCK - 1)) == i, wh[i:i + 1, :], d)
    keep = ((r >> ML_QKV_SHIFT) == (c >> ML_QKV_SHIFT)) & (c >= hh * ML_DH) & (c < (hh + 1) * ML_DH)
    return jnp.where(keep, d, 0.0)


def _blockdiag_out_in(w_ref, hh):
    wk = w_ref[...]
    o = lax.broadcasted_iota(jnp.int32, (ML_DHP, ML_PW), 0)
    r = lax.broadcasted_iota(jnp.int32, (ML_DHP, ML_PW), 1)
    d = jnp.zeros((ML_DHP, ML_PW), F32)
    for j in range(ML_QKV_BLOCK):
        d = jnp.where((o & (ML_QKV_BLOCK - 1)) == j, wk[j:j + 1, :], d)
    keep = ((o >> ML_QKV_SHIFT) == ((r >> ML_QKV_SHIFT) - hh * (ML_DH // ML_QKV_BLOCK))) & (o < ML_DH)
    return jnp.where(keep, d, 0.0)


ML_DEN_LANE = (ML_DH, 0)


def _mlstm_kernel(u_ref, og_ref, z_ref, ig_ref, fg_ref, bi_ref, bf_ref, cw_ref, cb_ref,
                  wq_ref, wkt_ref, wv_ref, ng_ref, y_ref,
                  dq_s, dkt_s, dv_s, q_s, kt_s, v_s, cp_s, c_s, r_s, w_s, col_s, so_s, sn_s):
    t = u_ref.shape[1]
    nc = t // ML_CHUNK
    L = ML_CHUNK
    heads = range(2)

    @pl.when(pl.program_id(1) == 0)
    def _():
        for hh in heads:
            dq_s[hh] = _blockdiag_in_out(wq_ref, hh).astype(BF16)
            dkt_s[hh] = _blockdiag_out_in(wkt_ref, hh).astype(BF16)
            dv_s[hh] = _blockdiag_in_pair(wv_ref, hh).astype(BF16)

    u = u_ref[0]
    uc = _silu(_causal_conv(u.astype(F32), cw_ref, cb_ref)).astype(BF16)
    for hh in heads:
        w0 = hh * (ML_PW - ML_DHP)
        uc_w, u_w = uc[:, w0:w0 + ML_DHP], u[:, w0:w0 + ML_DHP]
        q_s[hh] = jnp.dot(uc_w, dq_s[hh, w0:w0 + ML_DHP, :], preferred_element_type=F32).astype(BF16)
        kt = lax.dot_general(dkt_s[hh, :, w0:w0 + ML_DHP], uc_w, NT_DIMS, preferred_element_type=F32)
        kt = kt * (ML_DH ** -0.5)
        for c in range(nc):
            kt_s[hh, c] = kt[:, c * L:(c + 1) * L].astype(BF16)
        v = jnp.dot(u_w, dv_s[hh, w0:w0 + ML_DHP, :], preferred_element_type=F32)
        vlane = lax.broadcasted_iota(jnp.int32, v.shape, 1)
        v_s[hh] = jnp.where(vlane == ML_DEN_LANE[hh], 1.0, v).astype(BF16)

    glane = lax.broadcasted_iota(jnp.int32, (nc, L), 1)
    grow = lax.broadcasted_iota(jnp.int32, (nc, L), 0)
    row8 = lax.broadcasted_iota(jnp.int32, (SUBLANE, L), 0)
    for hh in heads:
        ig = ig_ref[hh, 0] + bi_ref[hh]
        lf = -_softplus(-(fg_ref[hh, 0] + bf_ref[hh]))
        b = lf
        for k in range(int(math.log2(L))):
            sh = 1 << k
            b = b + jnp.where(glane >= sh, pltpu.roll(b, sh, axis=1), 0.0)
        g = jnp.broadcast_to(b[:, L - 1:L], (nc, L))
        a = g - b + ig
        mloc = jnp.broadcast_to(jnp.max(a, axis=1, keepdims=True), (nc, L))
        m = jnp.zeros((1, L), F32)
        m_prev = jnp.zeros((nc, L), F32)
        m_next = jnp.zeros((nc, L), F32)
        for c in range(nc):
            m_prev = jnp.where(grow == c, m, m_prev)
            m = jnp.maximum(g[c:c + 1, :] + m, mloc[c:c + 1, :])
            m_next = jnp.where(grow == c, m, m_next)
        r = ig - b
        cmx = r
        for k in range(int(math.log2(L))):
            sh = 1 << k
            cmx = jnp.maximum(cmx, jnp.where(glane >= sh, pltpu.roll(cmx, sh, axis=1), -jnp.inf))
        mm = jnp.maximum(m_prev, cmx)
        s_int = jnp.exp(m_prev - mm)
        clamp = jnp.exp(-(b + mm))
        r_s[hh] = r
        w_s[hh] = jnp.exp(a - mloc)
        so_s[hh] = jnp.exp(g + m_prev - m_next)
        sn_s[hh] = jnp.exp(mloc - m_next)
        for c in range(nc):
            col_s[hh, c] = jnp.where(row8 == 0, mm[c:c + 1, :],
                                     jnp.where(row8 == 1, s_int[c:c + 1, :],
                                               jnp.where(row8 == 2, clamp[c:c + 1, :], 0.0)))

    zero_rows = jnp.zeros((ML_DHP - ML_DH, ML_PW), BF16)
    c_s[...] = jnp.zeros(c_s.shape, F32)

    def state(c, carry):
        r0 = pl.multiple_of(c * L, L)
        for hh in heads:
            c_prev = c_s[hh]
            cp_s[hh, c] = jnp.concatenate([c_prev.astype(BF16), zero_rows], axis=0)
            ktw = (kt_s[hh, c, :ML_DH, :].astype(F32) * w_s[hh, pl.ds(c, 1), :]).astype(BF16)
            c_loc = jnp.dot(ktw, v_s[hh, pl.ds(r0, L), :], preferred_element_type=F32)
            c_s[hh] = so_s[hh, pl.ds(c, 1), :][:, :1] * c_prev + sn_s[hh, pl.ds(c, 1), :][:, :1] * c_loc
        return carry

    lax.fori_loop(0, nc, state, 0, unroll=2)

    tri = (lax.broadcasted_iota(jnp.int32, (L, L), 0) >= lax.broadcasted_iota(jnp.int32, (L, L), 1))
    first = lax.broadcasted_iota(jnp.int32, (L, ML_PW), 1) < ML_DH
    pad_rows = jnp.zeros((L - SUBLANE, L), F32)

    def chunk(c, carry):
        r0 = pl.multiple_of(c * L, L)
        nds, invs = [], []
        for hh in heads:
            q_c = q_s[hh, pl.ds(r0, L), :]
            cols = jnp.concatenate([col_s[hh, c], pad_rows], axis=0).T
            rb = jnp.broadcast_to(r_s[hh, pl.ds(c, 1), :], (L, L))
            decay = jnp.exp(jnp.where(tri, rb - cols[:, 0:1], -jnp.inf))
            s_mat = jnp.dot(q_c, kt_s[hh, c], preferred_element_type=F32) * decay
            nd = (jnp.dot(s_mat.astype(BF16), v_s[hh, pl.ds(r0, L), :], preferred_element_type=F32)
                  + cols[:, 1:2] * jnp.dot(q_c, cp_s[hh, c], preferred_element_type=F32))
            den = nd[:, ML_DEN_LANE[hh]:ML_DEN_LANE[hh] + 1]
            nds.append(nd)
            invs.append(1.0 / jnp.maximum(jnp.abs(den), cols[:, 2:3]))
        x = _sigmoid(og_ref[0, pl.ds(r0, L), :].astype(F32)) * jnp.where(first, nds[0], nds[1])
        mu0 = jnp.sum(jnp.where(first, x, 0.0), axis=1, keepdims=True) * (1.0 / ML_DH)
        mu1 = jnp.sum(jnp.where(first, 0.0, x), axis=1, keepdims=True) * (1.0 / ML_DH)
        dev = x - jnp.where(first, mu0, mu1)
        sq = dev * dev
        var0 = jnp.sum(jnp.where(first, sq, 0.0), axis=1, keepdims=True) * (1.0 / ML_DH)
        var1 = jnp.sum(jnp.where(first, 0.0, sq), axis=1, keepdims=True) * (1.0 / ML_DH)
        f0 = invs[0] * lax.rsqrt(invs[0] * invs[0] * var0 + LN_EPS)
        f1 = invs[1] * lax.rsqrt(invs[1] * invs[1] * var1 + LN_EPS)
        zs = _silu(z_ref[0, pl.ds(r0, L), :].astype(F32)) * ng_ref[...]
        y_ref[0, pl.ds(r0, L), :] = (dev * jnp.where(first, f0, f1) * zs).astype(y_ref.dtype)
        return carry

    lax.fori_loop(0, nc, chunk, 0, unroll=4)


def _mlstm_branch(p, gates, gbias, cw, wq, wkt, wv, rows, l):
    b, t, _ = p.shape
    nc = t // ML_CHUNK
    ub, ob, zb = C_ML_U // ML_PW, C_ML_O // ML_PW, C_ML_Z // ML_PW
    seq = lambda base: pl.BlockSpec((1, t, ML_PW), lambda pr, bi: (bi, 0, base + pr))
    vec = lambda off: pl.BlockSpec((None, 1, ML_PW), lambda pr, bi: (l, 0, off // ML_PW + pr))
    taps = pl.BlockSpec((None, ML_QKV_BLOCK, ML_PW), lambda pr, bi: (l, 0, pr))
    gate_rows = lambda: pltpu.VMEM((2, nc, ML_CHUNK), F32)
    return pl.pallas_call(
        _mlstm_kernel,
        grid=(ML_PAIRS, b),
        in_specs=[
            seq(ub), seq(ob), seq(zb),
            pl.BlockSpec((2, 1, nc, ML_CHUNK), lambda pr, bi: (pr, bi, 0, 0)),
            pl.BlockSpec((2, 1, nc, ML_CHUNK), lambda pr, bi: (ML_PAIRS + pr, bi, 0, 0)),
            pl.BlockSpec((None, 2, 1, ML_CHUNK), lambda pr, bi: (l, pr, 0, 0)),
            pl.BlockSpec((None, 2, 1, ML_CHUNK), lambda pr, bi: (l, ML_PAIRS + pr, 0, 0)),
            pl.BlockSpec((None, cw.shape[1], ML_PW), lambda pr, bi: (l, 0, pr)),
            vec(R_ML_CB), taps, taps, taps, vec(R_ML_NG),
        ],
        out_specs=pl.BlockSpec((1, t, ML_PW), lambda pr, bi: (bi, 0, pr)),
        out_shape=jax.ShapeDtypeStruct((b, t, ML_W), BF16),
        scratch_shapes=[
            pltpu.VMEM((2, ML_PW, ML_DHP), BF16),
            pltpu.VMEM((2, ML_DHP, ML_PW), BF16),
            pltpu.VMEM((2, ML_PW, ML_PW), BF16),
            pltpu.VMEM((2, t, ML_DHP), BF16),
            pltpu.VMEM((2, nc, ML_DHP, ML_CHUNK), BF16),
            pltpu.VMEM((2, t, ML_PW), BF16),
            pltpu.VMEM((2, nc, ML_DHP, ML_PW), BF16),
            pltpu.VMEM((2, ML_DH, ML_PW), F32),
            gate_rows(), gate_rows(),
            pltpu.VMEM((2, nc, SUBLANE, ML_CHUNK), F32),
            gate_rows(), gate_rows(),
        ],
        compiler_params=pltpu.CompilerParams(
            dimension_semantics=("arbitrary", "arbitrary"), vmem_limit_bytes=_vmem_limit(56 << 20)),
        name="mlstm",
    )(p, p, p, gates, gates, gbias, gbias, cw, rows, wq, wkt, wv, rows)


MOBA_SEL_LANE = 6


def _moba_key_ext(nb):
    ext = np.zeros((nb, MOBA_BLOCK, MO_DH), np.float32)
    for n in range(nb):
        ext[n, :, 0:3] = n * MOBA_BLOCK
        ext[n, :, 3:6] = np.arange(MOBA_BLOCK, dtype=np.float32)[:, None]
        ext[n, :, MOBA_SEL_LANE + n] = 1.0
    return jnp.asarray(ext, dtype=BF16)


def _moba_slope_rows(slopes):
    rows = np.zeros((len(slopes), 1, MO_DH), np.float32)
    for h, s in enumerate(slopes):
        rest = np.float32(np.float32(s) * np.float32(LOG2E))
        for i in range(3):
            piece = np.float32(rest).astype(BF16).astype(np.float32)
            rows[h, 0, i] = rows[h, 0, 3 + i] = piece
            rest = np.float32(rest - piece)
    return jnp.asarray(rows)


def _moba_kernel(q_ref, k_ref, v_ref, z_ref, srow_ref, kext_ref, o_ref, l_s, qa_s):
    t = q_ref.shape[1]
    nb = t // MOBA_BLOCK
    bs = MOBA_BLOCK
    qscale = (MO_DH ** -0.5) * LOG2E

    krow = lax.broadcasted_iota(jnp.int32, (LANE, MO_DH), 0)
    kmean = jnp.zeros((LANE, MO_DH), F32)
    for n in range(nb):
        mean_n = jnp.sum(k_ref[0, n * bs:(n + 1) * bs, :].astype(F32), axis=0, keepdims=True) * (1.0 / bs)
        kmean = jnp.where(krow == MOBA_SEL_LANE + n, mean_n, kmean)
    kmean = kmean.astype(BF16)

    lane = lax.broadcasted_iota(jnp.int32, (bs, MO_DH), 1)
    causal = (lax.broadcasted_iota(jnp.int32, (bs, bs), 0) >= lax.broadcasted_iota(jnp.int32, (bs, bs), 1))
    ones_col = jnp.where(lane == 0, 1.0, 0.0).astype(BF16)
    slope_cols = jnp.where(lane < MOBA_SEL_LANE, srow_ref[0], 0.0)

    def prepare(qb):
        q_b = q_ref[0, qb * bs:(qb + 1) * bs, :]
        q_ext = slope_cols
        if qb > MOBA_TOPK:
            gate = lax.dot_general(q_b, kmean, NT_DIMS, preferred_element_type=F32)
            beaten = jnp.zeros((bs, MO_DH), F32)
            for m in range(qb):
                gm = gate[:, MOBA_SEL_LANE + m:MOBA_SEL_LANE + m + 1]
                wins = (gm > gate) | ((gm == gate) & (lane > MOBA_SEL_LANE + m))
                beaten = beaten + jnp.where(wins, 1.0, 0.0)
            past = (lane >= MOBA_SEL_LANE) & (lane < MOBA_SEL_LANE + qb)
            q_ext = jnp.where(past & (beaten >= float(MOBA_TOPK)), NEG_INF, slope_cols)
        qa_s[qb] = jnp.concatenate([(q_b.astype(F32) * qscale).astype(BF16), q_ext.astype(BF16)], axis=1)

    def scores(qb):
        q_aug = qa_s[qb]
        l_q = l_s.at[qb % 2]
        mx = None
        for n in range(qb + 1):
            k_aug = jnp.concatenate([k_ref[0, n * bs:(n + 1) * bs, :], kext_ref[n]], axis=1)
            logit = lax.dot_general(q_aug, k_aug, NT_DIMS, preferred_element_type=F32)
            if n == qb:
                logit = jnp.where(causal, logit, NEG_INF)
            l_q[:, n * bs:(n + 1) * bs] = logit
            half = jnp.maximum(logit[:, :LANE], logit[:, LANE:])
            mx = half if mx is None else jnp.maximum(mx, half)
        return jnp.max(mx, axis=1, keepdims=True)

    def outputs(qb, m_row):
        l_q = l_s.at[qb % 2]
        acc = jnp.zeros((bs, 2 * MO_DH), F32)
        for n in range(qb + 1):
            pr = jnp.exp2(l_q[:, n * bs:(n + 1) * bs] - m_row).astype(BF16)
            v_aug = jnp.concatenate([v_ref[0, n * bs:(n + 1) * bs, :], ones_col], axis=1)
            acc = acc + jnp.dot(pr, v_aug, preferred_element_type=F32)
        z = z_ref[0, qb * bs:(qb + 1) * bs, :].astype(F32)
        inv = 1.0 / acc[:, MO_DH:MO_DH + 1]
        o_ref[0, qb * bs:(qb + 1) * bs, :] = (acc[:, :MO_DH] * (inv * _silu(z))).astype(o_ref.dtype)

    prepare(0)
    prepare(1)
    m_row = scores(0)
    for qb in range(nb):
        if qb + 2 < nb:
            prepare(qb + 2)
        m_next = scores(qb + 1) if qb + 1 < nb else None
        outputs(qb, m_row)
        m_row = m_next


def _moba_branch(p, srows, kext):
    b, t, _ = p.shape
    nb = t // MOBA_BLOCK
    qb, kb, vb, zb = (c // MO_DH for c in (C_MO_Q, C_MO_K, C_MO_V, C_MO_Z))
    seq = lambda base: pl.BlockSpec((1, t, MO_DH), lambda bi, h: (bi, 0, base + h))
    return pl.pallas_call(
        _moba_kernel,
        grid=(b, MO_HEADS),
        in_specs=[seq(qb), seq(kb), seq(vb), seq(zb),
                  pl.BlockSpec((1, 1, MO_DH), lambda bi, h: (h, 0, 0)),
                  pl.BlockSpec((nb, MOBA_BLOCK, MO_DH), lambda bi, h: (0, 0, 0))],
        out_specs=pl.BlockSpec((1, t, MO_DH), lambda bi, h: (bi, 0, h)),
        out_shape=jax.ShapeDtypeStruct((b, t, MO_W), BF16),
        scratch_shapes=[pltpu.VMEM((2, MOBA_BLOCK, t), F32), pltpu.VMEM((nb, MOBA_BLOCK, 2 * MO_DH), BF16)],
        compiler_params=pltpu.CompilerParams(
            dimension_semantics=("arbitrary", "arbitrary"), vmem_limit_bytes=_vmem_limit(32 << 20)),
        name="moba",
    )(p, p, p, p, srows, kext)


def _stage_weights(w_hbm, layer, dst_ref, stage_ref, sem):
    tile_rows = stage_ref.shape[1]
    n_tiles = dst_ref.shape[0] // tile_rows

    def copy(k):
        return pltpu.make_async_copy(w_hbm.at[layer, pl.ds(k * tile_rows, tile_rows), :], stage_ref.at[k % 2],
                                     sem.at[k % 2])

    copy(0).start()
    for k in range(n_tiles):
        if k + 1 < n_tiles:
            copy(k + 1).start()
        copy(k).wait()
        dst_ref[k * tile_rows:(k + 1) * tile_rows, :] = stage_ref[k % 2].astype(BF16)


def _mem_kv_kernel(mem_ref, g_ref, w_hbm, o_ref, w_s, stage_s, sem, *, layer):
    @pl.when(pl.program_id(0) == 0)
    def _():
        _stage_weights(w_hbm, layer, w_s, stage_s, sem)

    hm = _rmsnorm_rows(mem_ref[0], g_ref[...]).astype(BF16)
    o_ref[0] = jnp.dot(hm, w_s[...], preferred_element_type=F32).astype(o_ref.dtype)


def _mem_kv(mem, g, wkv, l):
    b, m, _ = mem.shape
    return pl.pallas_call(
        functools.partial(_mem_kv_kernel, layer=l),
        grid=(b,),
        in_specs=[pl.BlockSpec((1, m, D_MODEL), lambda bi: (bi, 0, 0)),
                  pl.BlockSpec((None, 1, D_MODEL), lambda bi: (l, 0, R_MEM_G // D_MODEL)),
                  pl.BlockSpec(memory_space=pl.ANY)],
        out_specs=pl.BlockSpec((1, m, 2 * XA_W), lambda bi: (bi, 0, 0)),
        out_shape=jax.ShapeDtypeStruct((b, m, 2 * XA_W), BF16),
        scratch_shapes=[pltpu.VMEM((D_MODEL, 2 * XA_W), BF16),
                        pltpu.VMEM((2, STAGE_ROWS, 2 * XA_W), F32),
                        pltpu.SemaphoreType.DMA((2,))],
        compiler_params=pltpu.CompilerParams(
            dimension_semantics=("arbitrary",), vmem_limit_bytes=_vmem_limit(32 << 20)),
        name="mem_kv",
    )(mem, g, wkv)


def _post_kernel(x_ref, yl_ref, ym_ref, yo_ref, wout_hbm, g_ref, wq_hbm, kv_ref, wo_hbm, fg_ref, o_ref,
                 x1_s, wout_ref, wq_ref, wo_ref, stage_s, stage_q_s, sem, sem_q, *, final_norm, layer):
    @pl.when((pl.program_id(0) == 0) & (pl.program_id(1) == 0))
    def _():
        _stage_weights(wout_hbm, layer, wout_ref, stage_s, sem)
        _stage_weights(wo_hbm, layer, wo_ref, stage_s, sem)
        _stage_weights(wq_hbm, layer, wq_ref, stage_q_s, sem_q)

    sub = x_ref.shape[1] // POST_SUB
    groups = [pl.ds(i * sub, sub) for i in range(POST_SUB)]

    for r in groups:
        acc = jnp.dot(yl_ref[0, r, :], wout_ref[0:LRU_W, :], preferred_element_type=F32)
        acc = acc + jnp.dot(ym_ref[0, r, :], wout_ref[LRU_W:LRU_W + ML_W, :], preferred_element_type=F32)
        acc = acc + jnp.dot(yo_ref[0, r, :], wout_ref[LRU_W + ML_W:, :], preferred_element_type=F32)
        x1_s[r, :] = x_ref[0, r, :] + acc
    qs = []
    for r in groups:
        hx = _rmsnorm_rows(x1_s[r, :], g_ref[...]).astype(BF16)
        qs.append(jnp.dot(hx, wq_ref[...], preferred_element_type=F32).astype(BF16))
    scores = []
    for q in qs:
        scores.append([
            lax.dot_general(q[:, h * XA_DH:(h + 1) * XA_DH], kv_ref[0, :, h * XA_DH:(h + 1) * XA_DH], NT_DIMS,
                            preferred_element_type=F32) * (XA_DH ** -0.5)
            for h in range(XA_HEADS)])
    outs = []
    for sc in scores:
        heads = []
        for h, s in enumerate(sc):
            e = jnp.exp(s - jnp.max(s, axis=1, keepdims=True))
            pr = e / jnp.sum(e, axis=1, keepdims=True)
            v_h = kv_ref[0, :, XA_W + h * XA_DH:XA_W + (h + 1) * XA_DH]
            heads.append(jnp.dot(pr.astype(BF16), v_h, preferred_element_type=F32).astype(BF16))
        outs.append(jnp.concatenate(heads, axis=1))
    for r, o in zip(groups, outs):
        y = x1_s[r, :] + jnp.dot(o, wo_ref[...], preferred_element_type=F32)
        if final_norm:
            y = _rmsnorm_rows(y, fg_ref[...])
        o_ref[0, r, :] = y


def _post_mix(x, yl, ym, yo, wout, g, wq, kv, wo, fg, l, final_norm):
    b, t, _ = x.shape
    m = kv.shape[1]
    rows = lambda width: pl.BlockSpec((1, POST_TM, width), lambda bi, i: (bi, i, 0))
    hbm = pl.BlockSpec(memory_space=pl.ANY)
    weights = (D_MODEL * D_MODEL + D_MODEL * XA_W + XA_W * D_MODEL) * 2
    stages = 2 * STAGE_ROWS * (D_MODEL + XA_W) * 4
    tiles = 2 * POST_TM * (2 * D_MODEL * 4 + D_MODEL * 2) + 2 * m * 2 * XA_W * 2
    vmem = weights + stages + tiles + 6 * POST_TM * D_MODEL * 4
    return pl.pallas_call(
        functools.partial(_post_kernel, final_norm=final_norm, layer=l),
        grid=(b, t // POST_TM),
        in_specs=[rows(D_MODEL), rows(LRU_W), rows(ML_W), rows(MO_W),
                  hbm,
                  pl.BlockSpec((None, 1, D_MODEL), lambda bi, i: (l, 0, R_XA_G // D_MODEL)),
                  hbm,
                  pl.BlockSpec((1, m, 2 * XA_W), lambda bi, i: (bi, 0, 0)),
                  hbm,
                  pl.BlockSpec((1, D_MODEL), lambda bi, i: (0, 0))],
        out_specs=rows(D_MODEL),
        out_shape=jax.ShapeDtypeStruct((b, t, D_MODEL), F32),
        scratch_shapes=[pltpu.VMEM((POST_TM, D_MODEL), F32),
                        pltpu.VMEM((D_MODEL, D_MODEL), BF16),
                        pltpu.VMEM((D_MODEL, XA_W), BF16),
                        pltpu.VMEM((XA_W, D_MODEL), BF16),
                        pltpu.VMEM((2, STAGE_ROWS, D_MODEL), F32),
                        pltpu.VMEM((2, STAGE_ROWS, XA_W), F32),
                        pltpu.SemaphoreType.DMA((2,)),
                        pltpu.SemaphoreType.DMA((2,))],
        compiler_params=pltpu.CompilerParams(
            dimension_semantics=("arbitrary", "arbitrary"), vmem_limit_bytes=_vmem_limit(vmem)),
        name="post_mix",
    )(x, yl, ym, yo, wout, g, wq, kv, wo, fg)


def _alibi_slopes(n):
    def pow2(m):
        start = 2.0 ** (-8.0 / m)
        return [start ** (i + 1) for i in range(m)]
    if math.log2(n).is_integer():
        s = pow2(n)
    else:
        c = 2 ** int(math.floor(math.log2(n)))
        s = pow2(c) + pow2(2 * c)[0::2][:n - c]
    return np.asarray(s, dtype=np.float32)


def _taps(w, perm):
    depth = w.shape[0]
    return jnp.transpose(w, perm).reshape(depth, ML_QKV_BLOCK, ML_W)


def kernel(x, mem, mix_norm_g, w_in, lru_conv_w, lru_conv_b, lru_wa, lru_ba, lru_wx, lru_bx, lru_lambda,
           ml_conv_w, ml_conv_b, ml_wq, ml_wk, ml_wv, ml_bi, ml_bf, ml_norm_g, w_out, xa_norm_g, mem_norm_g,
           xa_wq, xa_wkv, xa_wo, final_norm_g):
    b, t, d = x.shape
    depth = w_in.shape[0]
    nc = t // ML_CHUNK
    moba_srows = _moba_slope_rows(_alibi_slopes(MO_HEADS))
    moba_kext = _moba_key_ext(t // MOBA_BLOCK)
    w_in_t = jnp.swapaxes(w_in, 1, 2)
    lru_wax = jnp.concatenate([lru_wa, lru_wx], axis=-1).astype(BF16)
    ml_wq_t = _taps(ml_wq, (0, 2, 1, 3))
    ml_wkt_t = _taps(ml_wk, (0, 3, 1, 2))
    ml_wv_t = _taps(ml_wv, (0, 2, 1, 3))
    ml_gbias = jnp.broadcast_to(jnp.concatenate([ml_bi, ml_bf], axis=1)[:, :, None, None],
                                (depth, 2 * ML_HEADS, 1, ML_CHUNK))
    rows = jnp.concatenate([mix_norm_g, xa_norm_g, mem_norm_g, ml_conv_b, ml_norm_g,
                            lru_conv_b, lru_ba, lru_bx, lru_lambda], axis=1)[:, None, :]
    for l in range(depth):
        p, gates = _in_proj(x.reshape(b * t, d), rows, w_in_t, l)
        p, gates = p.reshape(b, t, IN_COLS_P), gates.reshape(2 * ML_HEADS, b, nc, ML_CHUNK)
        y_lru = _lru_branch(p, lru_conv_w, lru_wax, rows, l)
        y_ml = _mlstm_branch(p, gates, ml_gbias, ml_conv_w, ml_wq_t, ml_wkt_t, ml_wv_t, rows, l)
        y_mo = _moba_branch(p, moba_srows, moba_kext)
        kv = _mem_kv(mem, rows, xa_wkv, l)
        x = _post_mix(x, y_lru, y_ml, y_mo, w_out, rows, xa_wq, kv, xa_wo, final_norm_g[None, :], l,
                      final_norm=(l == depth - 1))
    return x
```

```python
import functools
import math

import jax
import jax.numpy as jnp
import numpy as np
from jax import lax
from jax.experimental import pallas as pl
from jax.experimental.pallas import tpu as pltpu

LANE = 128
SUBLANE = 8
V7X_VMEM_BYTES = 64 * 1024 * 1024

D_MODEL = 2048
LRU_W = 512
LRU_BLOCKS = 4
LRU_BW = LRU_W // LRU_BLOCKS
LRU_C = 8.0
ML_W = 768
ML_HEADS = 4
ML_DH = 192
ML_DHP = 256
ML_PAIRS = 2
ML_PW = 2 * ML_DH
ML_QKV_BLOCK = 4
ML_QKV_SHIFT = 2
ML_CHUNK = 128
MO_W = 768
MO_HEADS = 6
MO_DH = 128
MOBA_BLOCK = 256
MOBA_TOPK = 3
XA_HEADS = 4
XA_DH = 128
XA_W = XA_HEADS * XA_DH
RMS_EPS = 1e-6
LN_EPS = 1e-5
NEG_INF = -1e30
LOG2E = 1.4426950408889634

REF_LRU_END = 2 * LRU_W
REF_ML_END = REF_LRU_END + 3 * ML_W
REF_GATE_END = REF_ML_END + 2 * ML_HEADS

C_ML_U = 0
C_ML_O = C_ML_U + ML_W
C_ML_Z = C_ML_O + ML_W
C_LRU_X = C_ML_Z + ML_W
C_LRU_Z = C_LRU_X + LRU_W
C_MO_Q = C_LRU_Z + LRU_W
C_MO_K = C_MO_Q + MO_W
C_MO_V = C_MO_K + MO_W
C_MO_Z = C_MO_V + MO_W
C_GATE = C_MO_Z + MO_W
R_MIX_G = 0
R_XA_G = R_MIX_G + D_MODEL
R_MEM_G = R_XA_G + D_MODEL
R_ML_CB = R_MEM_G + D_MODEL
R_ML_NG = R_ML_CB + ML_W
R_LRU_CB = R_ML_NG + ML_W
R_LRU_BA = R_LRU_CB + LRU_W
R_LRU_BX = R_LRU_BA + LRU_W
R_LRU_LAM = R_LRU_BX + LRU_W
IN_TM = 512
IN_TN = 3328
IN_COLS_P = 6656
PREP_TR = 256
STAGE_ROWS = 256
POST_TM = 512
POST_SUB = 2

BF16 = jnp.bfloat16
F32 = jnp.float32
NT_DIMS = (((1,), (1,)), ((), ()))


def _vmem_limit(nbytes):
    return int(min(V7X_VMEM_BYTES - (4 << 20), max(32 << 20, nbytes)))


def _rmsnorm_rows(x, g):
    ms = jnp.mean(x * x, axis=-1, keepdims=True)
    return x * lax.rsqrt(ms + RMS_EPS) * g


def _sigmoid(x):
    return jax.nn.sigmoid(x)


def _silu(x):
    return x * jax.nn.sigmoid(x)


def _softplus(x):
    return jnp.maximum(x, 0.0) + jnp.log1p(jnp.exp(-jnp.abs(x)))


def _shift_rows(x, s):
    rolled = pltpu.roll(x, s, axis=0)
    row = lax.broadcasted_iota(jnp.int32, x.shape, 0)
    return jnp.where(row >= s, rolled, 0.0)


def _causal_conv(x, w_ref, b_ref):
    k = w_ref.shape[0]
    acc = x * w_ref[k - 1:k, :]
    head = x[:SUBLANE]
    acc_head = head * w_ref[k - 1:k, :]
    for j in range(k - 1):
        acc = acc + pltpu.roll(x, k - 1 - j, axis=0) * w_ref[j:j + 1, :]
        acc_head = acc_head + _shift_rows(head, k - 1 - j) * w_ref[j:j + 1, :]
    return jnp.concatenate([acc_head, acc[SUBLANE:]], axis=0) + b_ref[...]


def _w_in_src_rows():
    n_ml = (C_LRU_X - C_ML_U) // PREP_TR
    n_lru = (C_MO_Q - C_LRU_X) // PREP_TR
    n_mo = (C_GATE - C_MO_Q) // PREP_TR
    return ([REF_LRU_END + i * PREP_TR for i in range(n_ml)] + [i * PREP_TR for i in range(n_lru)]
            + [REF_GATE_END + i * PREP_TR for i in range(n_mo)] + [REF_ML_END])


def _in_proj_kernel(x_ref, g_ref, w_hbm, o_ref, gt_ref, w_s, stage_s, sem, *, layer):
    @pl.when(pl.program_id(0) == 0)
    def _():
        src = _w_in_src_rows()

        def copy(k):
            return pltpu.make_async_copy(w_hbm.at[layer, pl.ds(src[k], PREP_TR), :], stage_s.at[k % 2], sem.at[k % 2])

        copy(0).start()
        for k in range(len(src)):
            if k + 1 < len(src):
                copy(k + 1).start()
            copy(k).wait()
            tile = stage_s[k % 2]
            if k == len(src) - 1:
                row = lax.broadcasted_iota(jnp.int32, tile.shape, 0)
                tile = jnp.where(row < 2 * ML_HEADS, tile, 0.0)
            w_s[k * PREP_TR:(k + 1) * PREP_TR, :] = tile.astype(BF16)

    xn = _rmsnorm_rows(x_ref[...], g_ref[...]).astype(BF16)
    n_chunks = w_s.shape[0] // IN_TN
    for j in range(n_chunks):
        cols = pl.ds(j * IN_TN, IN_TN)
        res = lax.dot_general(xn, w_s[cols, :], NT_DIMS, preferred_element_type=F32)
        o_ref[:, cols] = res.astype(o_ref.dtype)
        if j == C_GATE // IN_TN:
            g0 = C_GATE - j * IN_TN
            gt_ref[...] = res[:, g0:g0 + LANE].T[:2 * ML_HEADS, :]


def _in_proj(x2d, g, w_in_t, l):
    m = x2d.shape[0]
    n = IN_COLS_P
    vmem = (2 * IN_TM * D_MODEL * 4 + 2 * IN_TM * D_MODEL * 2 + D_MODEL * n * 2 + 2 * PREP_TR * D_MODEL * 4
            + 2 * IN_TM * n * 2 + 2 * IN_TM * IN_TN * 4)
    return pl.pallas_call(
        functools.partial(_in_proj_kernel, layer=l),
        grid=(m // IN_TM,),
        in_specs=[
            pl.BlockSpec((IN_TM, D_MODEL), lambda i: (i, 0)),
            pl.BlockSpec((None, 1, D_MODEL), lambda i: (l, 0, R_MIX_G // D_MODEL)),
            pl.BlockSpec(memory_space=pl.ANY),
        ],
        out_specs=[pl.BlockSpec((IN_TM, n), lambda i: (i, 0)),
                   pl.BlockSpec((2 * ML_HEADS, IN_TM), lambda i: (0, i))],
        out_shape=[jax.ShapeDtypeStruct((m, n), BF16), jax.ShapeDtypeStruct((2 * ML_HEADS, m), F32)],
        scratch_shapes=[pltpu.VMEM((n, D_MODEL), BF16),
                        pltpu.VMEM((2, PREP_TR, D_MODEL), F32),
                        pltpu.SemaphoreType.DMA((2,))],
        compiler_params=pltpu.CompilerParams(
            dimension_semantics=("arbitrary",), vmem_limit_bytes=_vmem_limit(vmem)),
        name="in_proj",
    )(x2d, g, w_in_t)


def _lru_kernel(x_ref, z_ref, cw_ref, cb_ref, wax_ref, ba_ref, bx_ref, lam_ref, o_ref, a_s, u_s):
    t = x_ref.shape[1]
    x = x_ref[0].astype(F32)
    xc = _causal_conv(x, cw_ref, cb_ref)
    pre = jnp.dot(xc.astype(BF16), wax_ref[0], preferred_element_type=F32)
    r = _sigmoid(pre[:, :LRU_BW] + ba_ref[...])
    i = _sigmoid(pre[:, LRU_BW:] + bx_ref[...])
    log_a = (-LRU_C) * r * _softplus(-lam_ref[...])
    a = jnp.exp(log_a)
    a_s[...] = a
    u_s[...] = jnp.sqrt(-jnp.tanh(log_a) * (1.0 + a * a)) * (i * xc)

    row = lax.broadcasted_iota(jnp.int32, (SUBLANE, LRU_BW), 0)

    def block(blk, h_prev):
        r0 = pl.multiple_of(blk * SUBLANE, SUBLANE)
        a_b = a_s[pl.ds(r0, SUBLANE), :]
        u_b = u_s[pl.ds(r0, SUBLANE), :]
        for s in (1, 2, 4):
            a_sh = jnp.where(row >= s, pltpu.roll(a_b, s, axis=0), 1.0)
            u_sh = jnp.where(row >= s, pltpu.roll(u_b, s, axis=0), 0.0)
            u_b = a_b * u_sh + u_b
            a_b = a_b * a_sh
        h = a_b * h_prev + u_b
        u_s[pl.ds(r0, SUBLANE), :] = h
        return jnp.broadcast_to(h[SUBLANE - 1:SUBLANE, :], (SUBLANE, LRU_BW))

    lax.fori_loop(0, t // SUBLANE, block, jnp.zeros((SUBLANE, LRU_BW), F32), unroll=4)
    z = z_ref[0].astype(F32)
    o_ref[0] = (u_s[...] * _silu(z)).astype(o_ref.dtype)


def _lru_branch(p, cw, wax, rows, l):
    b, t, _ = p.shape
    xb, zb = C_LRU_X // LRU_BW, C_LRU_Z // LRU_BW
    vec = lambda off: pl.BlockSpec((None, 1, LRU_BW), lambda bi, g: (l, 0, off // LRU_BW + g))
    return pl.pallas_call(
        _lru_kernel,
        grid=(b, LRU_BLOCKS),
        in_specs=[
            pl.BlockSpec((1, t, LRU_BW), lambda bi, g: (bi, 0, xb + g)),
            pl.BlockSpec((1, t, LRU_BW), lambda bi, g: (bi, 0, zb + g)),
            pl.BlockSpec((None, cw.shape[1], LRU_BW), lambda bi, g: (l, 0, g)),
            vec(R_LRU_CB),
            pl.BlockSpec((None, 1, LRU_BW, 2 * LRU_BW), lambda bi, g: (l, g, 0, 0)),
            vec(R_LRU_BA), vec(R_LRU_BX), vec(R_LRU_LAM),
        ],
        out_specs=pl.BlockSpec((1, t, LRU_BW), lambda bi, g: (bi, 0, g)),
        out_shape=jax.ShapeDtypeStruct((b, t, LRU_W), BF16),
        scratch_shapes=[pltpu.VMEM((t, LRU_BW), F32), pltpu.VMEM((t, LRU_BW), F32)],
        compiler_params=pltpu.CompilerParams(
            dimension_semantics=("arbitrary", "arbitrary"), vmem_limit_bytes=_vmem_limit(24 * t * LRU_BW * 4)),
        name="rg_lru",
    )(p, p, cw, rows, wax, rows, rows, rows)


def _blockdiag_in_out(w_ref, hh):
    wh = w_ref[...][:, hh * ML_DH:(hh + 1) * ML_DH]
    wh = jnp.concatenate([wh, jnp.zeros((ML_QKV_BLOCK, ML_DHP - ML_DH), F32)], axis=1)
    r = lax.broadcasted_iota(jnp.int32, (ML_PW, ML_DHP), 0)
    c = lax.broadcasted_iota(jnp.int32, (ML_PW, ML_DHP), 1)
    d = jnp.zeros((ML_PW, ML_DHP), F32)
    for i in range(ML_QKV_BLOCK):
        d = jnp.where((r & (ML_QKV_BLOCK - 1)) == i, wh[i:i + 1, :], d)
    keep = (((r >> ML_QKV_SHIFT) - hh * (ML_DH // ML_QKV_BLOCK)) == (c >> ML_QKV_SHIFT)) & (c < ML_DH)
    return jnp.where(keep, d, 0.0)


def _blockdiag_in_pair(w_ref, hh):
    wh = w_ref[...]
    r = lax.broadcasted_iota(jnp.int32, (ML_PW, ML_PW), 0)
    c = lax.broadcasted_iota(jnp.int32, (ML_PW, ML_PW), 1)
    d = jnp.zeros((ML_PW, ML_PW), F32)
    for i in range(ML_QKV_BLOCK):
        d = jnp.where((r & (ML_QKV_BLOCK - 1)) == i, wh[i:i + 1, :], d)
    keep = ((r >> ML_QKV_SHIFT) == (c >> ML_QKV_SHIFT)) & (c >= hh * ML_DH) & (c < (hh + 1) * ML_DH)
    return jnp.where(keep, d, 0.0)


def _blockdiag_out_in(w_ref, hh):
    wk = w_ref[...]
    o = lax.broadcasted_iota(jnp.int32, (ML_DHP, ML_PW), 0)
    r = lax.broadcasted_iota(jnp.int32, (ML_DHP, ML_PW), 1)
    d = jnp.zeros((ML_DHP, ML_PW), F32)
    for j in range(ML_QKV_BLOCK):
        d = jnp.where((o & (ML_QKV_BLOCK - 1)) == j, wk[j:j + 1, :], d)
    keep = ((o >> ML_QKV_SHIFT) == ((r >> ML_QKV_SHIFT) - hh * (ML_DH // ML_QKV_BLOCK))) & (o < ML_DH)
    return jnp.where(keep, d, 0.0)


ML_DEN_LANE = (ML_DH, 0)


def _mlstm_kernel(u_ref, og_ref, z_ref, ig_ref, fg_ref, bi_ref, bf_ref, cw_ref, cb_ref,
                  wq_ref, wkt_ref, wv_ref, ng_ref, y_ref,
                  dq_s, dkt_s, dv_s, q_s, kt_s, v_s, cp_s, c_s, r_s, w_s, col_s, so_s, sn_s):
    t = u_ref.shape[1]
    nc = t // ML_CHUNK
    L = ML_CHUNK
    heads = range(2)

    @pl.when(pl.program_id(1) == 0)
    def _():
        for hh in heads:
            dq_s[hh] = _blockdiag_in_out(wq_ref, hh).astype(BF16)
            dkt_s[hh] = _blockdiag_out_in(wkt_ref, hh).astype(BF16)
            dv_s[hh] = _blockdiag_in_pair(wv_ref, hh).astype(BF16)

    u = u_ref[0]
    uc = _silu(_causal_conv(u.astype(F32), cw_ref, cb_ref)).astype(BF16)
    for hh in heads:
        w0 = hh * (ML_PW - ML_DHP)
        uc_w, u_w = uc[:, w0:w0 + ML_DHP], u[:, w0:w0 + ML_DHP]
        q_s[hh] = jnp.dot(uc_w, dq_s[hh, w0:w0 + ML_DHP, :], preferred_element_type=F32).astype(BF16)
        kt = lax.dot_general(dkt_s[hh, :, w0:w0 + ML_DHP], uc_w, NT_DIMS, preferred_element_type=F32)
        kt = kt * (ML_DH ** -0.5)
        for c in range(nc):
            kt_s[hh, c] = kt[:, c * L:(c + 1) * L].astype(BF16)
        v = jnp.dot(u_w, dv_s[hh, w0:w0 + ML_DHP, :], preferred_element_type=F32)
        vlane = lax.broadcasted_iota(jnp.int32, v.shape, 1)
        v_s[hh] = jnp.where(vlane == ML_DEN_LANE[hh], 1.0, v).astype(BF16)

    glane = lax.broadcasted_iota(jnp.int32, (nc, L), 1)
    grow = lax.broadcasted_iota(jnp.int32, (nc, L), 0)
    row8 = lax.broadcasted_iota(jnp.int32, (SUBLANE, L), 0)
    for hh in heads:
        ig = ig_ref[hh, 0] + bi_ref[hh]
        lf = -_softplus(-(fg_ref[hh, 0] + bf_ref[hh]))
        b = lf
        for k in range(int(math.log2(L))):
            sh = 1 << k
            b = b + jnp.where(glane >= sh, pltpu.roll(b, sh, axis=1), 0.0)
        g = jnp.broadcast_to(b[:, L - 1:L], (nc, L))
        a = g - b + ig
        mloc = jnp.broadcast_to(jnp.max(a, axis=1, keepdims=True), (nc, L))
        m = jnp.zeros((1, L), F32)
        m_prev = jnp.zeros((nc, L), F32)
        m_next = jnp.zeros((nc, L), F32)
        for c in range(nc):
            m_prev = jnp.where(grow == c, m, m_prev)
            m = jnp.maximum(g[c:c + 1, :] + m, mloc[c:c + 1, :])
            m_next = jnp.where(grow == c, m, m_next)
        r = ig - b
        cmx = r
        for k in range(int(math.log2(L))):
            sh = 1 << k
            cmx = jnp.maximum(cmx, jnp.where(glane >= sh, pltpu.roll(cmx, sh, axis=1), -jnp.inf))
        mm = jnp.maximum(m_prev, cmx)
        s_int = jnp.exp(m_prev - mm)
        clamp = jnp.exp(-(b + mm))
        r_s[hh] = r
        w_s[hh] = jnp.exp(a - mloc)
        so_s[hh] = jnp.exp(g + m_prev - m_next)
        sn_s[hh] = jnp.exp(mloc - m_next)
        for c in range(nc):
            col_s[hh, c] = jnp.where(row8 == 0, mm[c:c + 1, :],
                                     jnp.where(row8 == 1, s_int[c:c + 1, :],
                                               jnp.where(row8 == 2, clamp[c:c + 1, :], 0.0)))

    zero_rows = jnp.zeros((ML_DHP - ML_DH, ML_PW), BF16)
    c_s[...] = jnp.zeros(c_s.shape, F32)

    def state(c, carry):
        r0 = pl.multiple_of(c * L, L)
        for hh in heads:
            c_prev = c_s[hh]
            cp_s[hh, c] = jnp.concatenate([c_prev.astype(BF16), zero_rows], axis=0)
            ktw = (kt_s[hh, c, :ML_DH, :].astype(F32) * w_s[hh, pl.ds(c, 1), :]).astype(BF16)
            c_loc = jnp.dot(ktw, v_s[hh, pl.ds(r0, L), :], preferred_element_type=F32)
            c_s[hh] = so_s[hh, pl.ds(c, 1), :][:, :1] * c_prev + sn_s[hh, pl.ds(c, 1), :][:, :1] * c_loc
        return carry

    lax.fori_loop(0, nc, state, 0, unroll=2)

    tri = (lax.broadcasted_iota(jnp.int32, (L, L), 0) >= lax.broadcasted_iota(jnp.int32, (L, L), 1))
    first = lax.broadcasted_iota(jnp.int32, (L, ML_PW), 1) < ML_DH
    pad_rows = jnp.zeros((L - SUBLANE, L), F32)

    def chunk(c, carry):
        r0 = pl.multiple_of(c * L, L)
        nds, invs = [], []
        for hh in heads:
            q_c = q_s[hh, pl.ds(r0, L), :]
            cols = jnp.concatenate([col_s[hh, c], pad_rows], axis=0).T
            rb = jnp.broadcast_to(r_s[hh, pl.ds(c, 1), :], (L, L))
            decay = jnp.exp(jnp.where(tri, rb - cols[:, 0:1], -jnp.inf))
            s_mat = jnp.dot(q_c, kt_s[hh, c], preferred_element_type=F32) * decay
            nd = (jnp.dot(s_mat.astype(BF16), v_s[hh, pl.ds(r0, L), :], preferred_element_type=F32)
                  + cols[:, 1:2] * jnp.dot(q_c, cp_s[hh, c], preferred_element_type=F32))
            den = nd[:, ML_DEN_LANE[hh]:ML_DEN_LANE[hh] + 1]
            nds.append(nd)
            invs.append(1.0 / jnp.maximum(jnp.abs(den), cols[:, 2:3]))
        x = _sigmoid(og_ref[0, pl.ds(r0, L), :].astype(F32)) * jnp.where(first, nds[0], nds[1])
        mu0 = jnp.sum(jnp.where(first, x, 0.0), axis=1, keepdims=True) * (1.0 / ML_DH)
        mu1 = jnp.sum(jnp.where(first, 0.0, x), axis=1, keepdims=True) * (1.0 / ML_DH)
        dev = x - jnp.where(first, mu0, mu1)
        sq = dev * dev
        var0 = jnp.sum(jnp.where(first, sq, 0.0), axis=1, keepdims=True) * (1.0 / ML_DH)
        var1 = jnp.sum(jnp.where(first, 0.0, sq), axis=1, keepdims=True) * (1.0 / ML_DH)
        f0 = invs[0] * lax.rsqrt(invs[0] * invs[0] * var0 + LN_EPS)
        f1 = invs[1] * lax.rsqrt(invs[1] * invs[1] * var1 + LN_EPS)
        zs = _silu(z_ref[0, pl.ds(r0, L), :].astype(F32)) * ng_ref[...]
        y_ref[0, pl.ds(r0, L), :] = (dev * jnp.where(first, f0, f1) * zs).astype(y_ref.dtype)
        return carry

    lax.fori_loop(0, nc, chunk, 0, unroll=4)


def _mlstm_branch(p, gates, gbias, cw, wq, wkt, wv, rows, l):
    b, t, _ = p.shape
    nc = t // ML_CHUNK
    ub, ob, zb = C_ML_U // ML_PW, C_ML_O // ML_PW, C_ML_Z // ML_PW
    seq = lambda base: pl.BlockSpec((1, t, ML_PW), lambda pr, bi: (bi, 0, base + pr))
    vec = lambda off: pl.BlockSpec((None, 1, ML_PW), lambda pr, bi: (l, 0, off // ML_PW + pr))
    taps = pl.BlockSpec((None, ML_QKV_BLOCK, ML_PW), lambda pr, bi: (l, 0, pr))
    gate_rows = lambda: pltpu.VMEM((2, nc, ML_CHUNK), F32)
    return pl.pallas_call(
        _mlstm_kernel,
        grid=(ML_PAIRS, b),
        in_specs=[
            seq(ub), seq(ob), seq(zb),
            pl.BlockSpec((2, 1, nc, ML_CHUNK), lambda pr, bi: (pr, bi, 0, 0)),
            pl.BlockSpec((2, 1, nc, ML_CHUNK), lambda pr, bi: (ML_PAIRS + pr, bi, 0, 0)),
            pl.BlockSpec((None, 2, 1, ML_CHUNK), lambda pr, bi: (l, pr, 0, 0)),
            pl.BlockSpec((None, 2, 1, ML_CHUNK), lambda pr, bi: (l, ML_PAIRS + pr, 0, 0)),
            pl.BlockSpec((None, cw.shape[1], ML_PW), lambda pr, bi: (l, 0, pr)),
            vec(R_ML_CB), taps, taps, taps, vec(R_ML_NG),
        ],
        out_specs=pl.BlockSpec((1, t, ML_PW), lambda pr, bi: (bi, 0, pr)),
        out_shape=jax.ShapeDtypeStruct((b, t, ML_W), BF16),
        scratch_shapes=[
            pltpu.VMEM((2, ML_PW, ML_DHP), BF16),
            pltpu.VMEM((2, ML_DHP, ML_PW), BF16),
            pltpu.VMEM((2, ML_PW, ML_PW), BF16),
            pltpu.VMEM((2, t, ML_DHP), BF16),
            pltpu.VMEM((2, nc, ML_DHP, ML_CHUNK), BF16),
            pltpu.VMEM((2, t, ML_PW), BF16),
            pltpu.VMEM((2, nc, ML_DHP, ML_PW), BF16),
            pltpu.VMEM((2, ML_DH, ML_PW), F32),
            gate_rows(), gate_rows(),
            pltpu.VMEM((2, nc, SUBLANE, ML_CHUNK), F32),
            gate_rows(), gate_rows(),
        ],
        compiler_params=pltpu.CompilerParams(
            dimension_semantics=("arbitrary", "arbitrary"), vmem_limit_bytes=_vmem_limit(56 << 20)),
        name="mlstm",
    )(p, p, p, gates, gates, gbias, gbias, cw, rows, wq, wkt, wv, rows)


MOBA_SEL_LANE = 6
MOBA_AHEAD = 3


def _moba_key_ext(nb):
    ext = np.zeros((nb, MOBA_BLOCK, MO_DH), np.float32)
    for n in range(nb):
        ext[n, :, 0:3] = n * MOBA_BLOCK
        ext[n, :, 3:6] = np.arange(MOBA_BLOCK, dtype=np.float32)[:, None]
        ext[n, :, MOBA_SEL_LANE + n] = 1.0
    return jnp.asarray(ext, dtype=BF16)


def _moba_slope_rows(slopes):
    rows = np.zeros((len(slopes), 1, MO_DH), np.float32)
    for h, s in enumerate(slopes):
        rest = np.float32(np.float32(s) * np.float32(LOG2E))
        for i in range(3):
            piece = np.float32(rest).astype(BF16).astype(np.float32)
            rows[h, 0, i] = rows[h, 0, 3 + i] = piece
            rest = np.float32(rest - piece)
    return jnp.asarray(rows)


def _moba_kernel(q_ref, k_ref, v_ref, z_ref, srow_ref, kext_ref, o_ref, l_s, qa_s):
    t = q_ref.shape[1]
    nb = t // MOBA_BLOCK
    bs = MOBA_BLOCK
    qscale = (MO_DH ** -0.5) * LOG2E

    krow = lax.broadcasted_iota(jnp.int32, (LANE, MO_DH), 0)
    kmean = jnp.zeros((LANE, MO_DH), F32)
    for n in range(nb):
        mean_n = jnp.sum(k_ref[0, n * bs:(n + 1) * bs, :].astype(F32), axis=0, keepdims=True) * (1.0 / bs)
        kmean = jnp.where(krow == MOBA_SEL_LANE + n, mean_n, kmean)
    kmean = kmean.astype(BF16)

    lane = lax.broadcasted_iota(jnp.int32, (bs, MO_DH), 1)
    causal = (lax.broadcasted_iota(jnp.int32, (bs, bs), 0) >= lax.broadcasted_iota(jnp.int32, (bs, bs), 1))
    ones_col = jnp.where(lane == 0, 1.0, 0.0).astype(BF16)
    slope_cols = jnp.where(lane < MOBA_SEL_LANE, srow_ref[0], 0.0)

    def prepare(qb):
        q_b = q_ref[0, qb * bs:(qb + 1) * bs, :]
        q_ext = slope_cols
        if qb > MOBA_TOPK:
            gate = lax.dot_general(q_b, kmean, NT_DIMS, preferred_element_type=F32)
            beaten = jnp.zeros((bs, MO_DH), F32)
            for m in range(qb):
                gm = gate[:, MOBA_SEL_LANE + m:MOBA_SEL_LANE + m + 1]
                wins = (gm > gate) | ((gm == gate) & (lane > MOBA_SEL_LANE + m))
                beaten = beaten + jnp.where(wins, 1.0, 0.0)
            past = (lane >= MOBA_SEL_LANE) & (lane < MOBA_SEL_LANE + qb)
            q_ext = jnp.where(past & (beaten >= float(MOBA_TOPK)), NEG_INF, slope_cols)
        qa_s[qb] = jnp.concatenate([(q_b.astype(F32) * qscale).astype(BF16), q_ext.astype(BF16)], axis=1)

    def scores(qb):
        q_aug = qa_s[qb]
        l_q = l_s.at[qb % l_s.shape[0]]
        mx = None
        for n in range(qb + 1):
            k_aug = jnp.concatenate([k_ref[0, n * bs:(n + 1) * bs, :], kext_ref[n]], axis=1)
            logit = lax.dot_general(q_aug, k_aug, NT_DIMS, preferred_element_type=F32)
            if n == qb:
                logit = jnp.where(causal, logit, NEG_INF)
            l_q[:, n * bs:(n + 1) * bs] = logit
            half = jnp.maximum(logit[:, :LANE], logit[:, LANE:])
            mx = half if mx is None else jnp.maximum(mx, half)
        return jnp.max(mx, axis=1, keepdims=True)

    def outputs(qb, m_row):
        l_q = l_s.at[qb % l_s.shape[0]]
        acc = jnp.zeros((bs, 2 * MO_DH), F32)
        for n in range(qb + 1):
            pr = jnp.exp2(l_q[:, n * bs:(n + 1) * bs] - m_row).astype(BF16)
            v_aug = jnp.concatenate([v_ref[0, n * bs:(n + 1) * bs, :], ones_col], axis=1)
            acc = acc + jnp.dot(pr, v_aug, preferred_element_type=F32)
        z = z_ref[0, qb * bs:(qb + 1) * bs, :].astype(F32)
        inv = 1.0 / acc[:, MO_DH:MO_DH + 1]
        o_ref[0, qb * bs:(qb + 1) * bs, :] = (acc[:, :MO_DH] * (inv * _silu(z))).astype(o_ref.dtype)

    ahead = l_s.shape[0] - 1
    for qb in range(min(ahead + 1, nb)):
        prepare(qb)
    m_rows = {qb: scores(qb) for qb in range(min(ahead, nb))}
    for qb in range(nb):
        if qb + ahead + 1 < nb:
            prepare(qb + ahead + 1)
        if qb + ahead < nb:
            m_rows[qb + ahead] = scores(qb + ahead)
        outputs(qb, m_rows.pop(qb))


def _moba_branch(p, srows, kext):
    b, t, _ = p.shape
    nb = t // MOBA_BLOCK
    qb, kb, vb, zb = (c // MO_DH for c in (C_MO_Q, C_MO_K, C_MO_V, C_MO_Z))
    seq = lambda base: pl.BlockSpec((1, t, MO_DH), lambda bi, h: (bi, 0, base + h))
    return pl.pallas_call(
        _moba_kernel,
        grid=(b, MO_HEADS),
        in_specs=[seq(qb), seq(kb), seq(vb), seq(zb),
                  pl.BlockSpec((1, 1, MO_DH), lambda bi, h: (h, 0, 0)),
                  pl.BlockSpec((nb, MOBA_BLOCK, MO_DH), lambda bi, h: (0, 0, 0))],
        out_specs=pl.BlockSpec((1, t, MO_DH), lambda bi, h: (bi, 0, h)),
        out_shape=jax.ShapeDtypeStruct((b, t, MO_W), BF16),
        scratch_shapes=[pltpu.VMEM((MOBA_AHEAD + 1, MOBA_BLOCK, t), F32),
                        pltpu.VMEM((nb, MOBA_BLOCK, 2 * MO_DH), BF16)],
        compiler_params=pltpu.CompilerParams(
            dimension_semantics=("arbitrary", "arbitrary"), vmem_limit_bytes=_vmem_limit(32 << 20)),
        name="moba",
    )(p, p, p, p, srows, kext)


def _stage_weights(w_hbm, layer, dst_ref, stage_ref, sem):
    tile_rows = stage_ref.shape[1]
    n_tiles = dst_ref.shape[0] // tile_rows

    def copy(k):
        return pltpu.make_async_copy(w_hbm.at[layer, pl.ds(k * tile_rows, tile_rows), :], stage_ref.at[k % 2],
                                     sem.at[k % 2])

    copy(0).start()
    for k in range(n_tiles):
        if k + 1 < n_tiles:
            copy(k + 1).start()
        copy(k).wait()
        dst_ref[k * tile_rows:(k + 1) * tile_rows, :] = stage_ref[k % 2].astype(BF16)


def _mem_kv_kernel(mem_ref, g_ref, w_hbm, o_ref, w_s, stage_s, sem, *, layer):
    @pl.when(pl.program_id(0) == 0)
    def _():
        _stage_weights(w_hbm, layer, w_s, stage_s, sem)

    hm = _rmsnorm_rows(mem_ref[0], g_ref[...]).astype(BF16)
    o_ref[0] = jnp.dot(hm, w_s[...], preferred_element_type=F32).astype(o_ref.dtype)


def _mem_kv(mem, g, wkv, l):
    b, m, _ = mem.shape
    return pl.pallas_call(
        functools.partial(_mem_kv_kernel, layer=l),
        grid=(b,),
        in_specs=[pl.BlockSpec((1, m, D_MODEL), lambda bi: (bi, 0, 0)),
                  pl.BlockSpec((None, 1, D_MODEL), lambda bi: (l, 0, R_MEM_G // D_MODEL)),
                  pl.BlockSpec(memory_space=pl.ANY)],
        out_specs=pl.BlockSpec((1, m, 2 * XA_W), lambda bi: (bi, 0, 0)),
        out_shape=jax.ShapeDtypeStruct((b, m, 2 * XA_W), BF16),
        scratch_shapes=[pltpu.VMEM((D_MODEL, 2 * XA_W), BF16),
                        pltpu.VMEM((2, STAGE_ROWS, 2 * XA_W), F32),
                        pltpu.SemaphoreType.DMA((2,))],
        compiler_params=pltpu.CompilerParams(
            dimension_semantics=("arbitrary",), vmem_limit_bytes=_vmem_limit(32 << 20)),
        name="mem_kv",
    )(mem, g, wkv)


def _post_kernel(x_ref, yl_ref, ym_ref, yo_ref, wout_hbm, g_ref, wq_hbm, kv_ref, wo_hbm, fg_ref, o_ref,
                 x1_s, wout_ref, wq_ref, wo_ref, stage_s, stage_q_s, sem, sem_q, *, final_norm, layer):
    @pl.when((pl.program_id(0) == 0) & (pl.program_id(1) == 0))
    def _():
        _stage_weights(wout_hbm, layer, wout_ref, stage_s, sem)
        _stage_weights(wo_hbm, layer, wo_ref, stage_s, sem)
        _stage_weights(wq_hbm, layer, wq_ref, stage_q_s, sem_q)

    sub = x_ref.shape[1] // POST_SUB
    groups = [pl.ds(i * sub, sub) for i in range(POST_SUB)]

    for r in groups:
        acc = jnp.dot(yl_ref[0, r, :], wout_ref[0:LRU_W, :], preferred_element_type=F32)
        acc = acc + jnp.dot(ym_ref[0, r, :], wout_ref[LRU_W:LRU_W + ML_W, :], preferred_element_type=F32)
        acc = acc + jnp.dot(yo_ref[0, r, :], wout_ref[LRU_W + ML_W:, :], preferred_element_type=F32)
        x1_s[r, :] = x_ref[0, r, :] + acc
    qs = []
    for r in groups:
        hx = _rmsnorm_rows(x1_s[r, :], g_ref[...]).astype(BF16)
        qs.append(jnp.dot(hx, wq_ref[...], preferred_element_type=F32).astype(BF16))
    scores = []
    for q in qs:
        scores.append([
            lax.dot_general(q[:, h * XA_DH:(h + 1) * XA_DH], kv_ref[0, :, h * XA_DH:(h + 1) * XA_DH], NT_DIMS,
                            preferred_element_type=F32) * (XA_DH ** -0.5)
            for h in range(XA_HEADS)])
    outs = []
    for sc in scores:
        heads = []
        for h, s in enumerate(sc):
            e = jnp.exp(s - jnp.max(s, axis=1, keepdims=True))
            pr = e / jnp.sum(e, axis=1, keepdims=True)
            v_h = kv_ref[0, :, XA_W + h * XA_DH:XA_W + (h + 1) * XA_DH]
            heads.append(jnp.dot(pr.astype(BF16), v_h, preferred_element_type=F32).astype(BF16))
        outs.append(jnp.concatenate(heads, axis=1))
    for r, o in zip(groups, outs):
        y = x1_s[r, :] + jnp.dot(o, wo_ref[...], preferred_element_type=F32)
        if final_norm:
            y = _rmsnorm_rows(y, fg_ref[...])
        o_ref[0, r, :] = y


def _post_mix(x, yl, ym, yo, wout, g, wq, kv, wo, fg, l, final_norm):
    b, t, _ = x.shape
    m = kv.shape[1]
    rows = lambda width: pl.BlockSpec((1, POST_TM, width), lambda bi, i: (bi, i, 0))
    hbm = pl.BlockSpec(memory_space=pl.ANY)
    weights = (D_MODEL * D_MODEL + D_MODEL * XA_W + XA_W * D_MODEL) * 2
    stages = 2 * STAGE_ROWS * (D_MODEL + XA_W) * 4
    tiles = 2 * POST_TM * (2 * D_MODEL * 4 + D_MODEL * 2) + 2 * m * 2 * XA_W * 2
    vmem = weights + stages + tiles + 6 * POST_TM * D_MODEL * 4
    return pl.pallas_call(
        functools.partial(_post_kernel, final_norm=final_norm, layer=l),
        grid=(b, t // POST_TM),
        in_specs=[rows(D_MODEL), rows(LRU_W), rows(ML_W), rows(MO_W),
                  hbm,
                  pl.BlockSpec((None, 1, D_MODEL), lambda bi, i: (l, 0, R_XA_G // D_MODEL)),
                  hbm,
                  pl.BlockSpec((1, m, 2 * XA_W), lambda bi, i: (bi, 0, 0)),
                  hbm,
                  pl.BlockSpec((1, D_MODEL), lambda bi, i: (0, 0))],
        out_specs=rows(D_MODEL),
        out_shape=jax.ShapeDtypeStruct((b, t, D_MODEL), F32),
        scratch_shapes=[pltpu.VMEM((POST_TM, D_MODEL), F32),
                        pltpu.VMEM((D_MODEL, D_MODEL), BF16),
                        pltpu.VMEM((D_MODEL, XA_W), BF16),
                        pltpu.VMEM((XA_W, D_MODEL), BF16),
                        pltpu.VMEM((2, STAGE_ROWS, D_MODEL), F32),
                        pltpu.VMEM((2, STAGE_ROWS, XA_W), F32),
                        pltpu.SemaphoreType.DMA((2,)),
                        pltpu.SemaphoreType.DMA((2,))],
        compiler_params=pltpu.CompilerParams(
            dimension_semantics=("arbitrary", "arbitrary"), vmem_limit_bytes=_vmem_limit(vmem)),
        name="post_mix",
    )(x, yl, ym, yo, wout, g, wq, kv, wo, fg)


def _alibi_slopes(n):
    def pow2(m):
        start = 2.0 ** (-8.0 / m)
        return [start ** (i + 1) for i in range(m)]
    if math.log2(n).is_integer():
        s = pow2(n)
    else:
        c = 2 ** int(math.floor(math.log2(n)))
        s = pow2(c) + pow2(2 * c)[0::2][:n - c]
    return np.asarray(s, dtype=np.float32)


def _taps(w, perm):
    depth = w.shape[0]
    return jnp.transpose(w, perm).reshape(depth, ML_QKV_BLOCK, ML_W)


def kernel(x, mem, mix_norm_g, w_in, lru_conv_w, lru_conv_b, lru_wa, lru_ba, lru_wx, lru_bx, lru_lambda,
           ml_conv_w, ml_conv_b, ml_wq, ml_wk, ml_wv, ml_bi, ml_bf, ml_norm_g, w_out, xa_norm_g, mem_norm_g,
           xa_wq, xa_wkv, xa_wo, final_norm_g):
    b, t, d = x.shape
    depth = w_in.shape[0]
    nc = t // ML_CHUNK
    moba_srows = _moba_slope_rows(_alibi_slopes(MO_HEADS))
    moba_kext = _moba_key_ext(t // MOBA_BLOCK)
    w_in_t = jnp.swapaxes(w_in, 1, 2)
    lru_wax = jnp.concatenate([lru_wa, lru_wx], axis=-1).astype(BF16)
    ml_wq_t = _taps(ml_wq, (0, 2, 1, 3))
    ml_wkt_t = _taps(ml_wk, (0, 3, 1, 2))
    ml_wv_t = _taps(ml_wv, (0, 2, 1, 3))
    ml_gbias = jnp.broadcast_to(jnp.concatenate([ml_bi, ml_bf], axis=1)[:, :, None, None],
                                (depth, 2 * ML_HEADS, 1, ML_CHUNK))
    rows = jnp.concatenate([mix_norm_g, xa_norm_g, mem_norm_g, ml_conv_b, ml_norm_g,
                            lru_conv_b, lru_ba, lru_bx, lru_lambda], axis=1)[:, None, :]
    for l in range(depth):
        p, gates = _in_proj(x.reshape(b * t, d), rows, w_in_t, l)
        p, gates = p.reshape(b, t, IN_COLS_P), gates.reshape(2 * ML_HEADS, b, nc, ML_CHUNK)
        y_lru = _lru_branch(p, lru_conv_w, lru_wax, rows, l)
        y_ml = _mlstm_branch(p, gates, ml_gbias, ml_conv_w, ml_wq_t, ml_wkt_t, ml_wv_t, rows, l)
        y_mo = _moba_branch(p, moba_srows, moba_kext)
        kv = _mem_kv(mem, rows, xa_wkv, l)
        x = _post_mix(x, y_lru, y_ml, y_mo, w_out, rows, xa_wq, kv, xa_wo, final_norm_g[None, :], l,
                      final_norm=(l == depth - 1))
    return x
```

```python
import functools
import math

import jax
import jax.numpy as jnp
import numpy as np
from jax import lax
from jax.experimental import pallas as pl
from jax.experimental.pallas import tpu as pltpu

LANE = 128
SUBLANE = 8
V7X_VMEM_BYTES = 64 * 1024 * 1024

D_MODEL = 2048
LRU_W = 512
LRU_BLOCKS = 4
LRU_BW = LRU_W // LRU_BLOCKS
LRU_C = 8.0
ML_W = 768
ML_HEADS = 4
ML_DH = 192
ML_DHP = 256
ML_PAIRS = 2
ML_PW = 2 * ML_DH
ML_QKV_BLOCK = 4
ML_QKV_SHIFT = 2
ML_CHUNK = 128
ML_GROUP = 4
MO_W = 768
MO_HEADS = 6
MO_DH = 128
MOBA_BLOCK = 256
MOBA_TOPK = 3
XA_HEADS = 4
XA_DH = 128
XA_W = XA_HEADS * XA_DH
RMS_EPS = 1e-6
LN_EPS = 1e-5
NEG_INF = -1e30
LOG2E = 1.4426950408889634

REF_LRU_END = 2 * LRU_W
REF_ML_END = REF_LRU_END + 3 * ML_W
REF_GATE_END = REF_ML_END + 2 * ML_HEADS

C_ML_U = 0
C_ML_O = C_ML_U + ML_W
C_ML_Z = C_ML_O + ML_W
C_LRU_X = C_ML_Z + ML_W
C_LRU_Z = C_LRU_X + LRU_W
C_MO_Q = C_LRU_Z + LRU_W
C_MO_K = C_MO_Q + MO_W
C_MO_V = C_MO_K + MO_W
C_MO_Z = C_MO_V + MO_W
C_GATE = C_MO_Z + MO_W
R_MIX_G = 0
R_XA_G = R_MIX_G + D_MODEL
R_MEM_G = R_XA_G + D_MODEL
R_ML_CB = R_MEM_G + D_MODEL
R_ML_NG = R_ML_CB + ML_W
R_LRU_CB = R_ML_NG + ML_W
R_LRU_BA = R_LRU_CB + LRU_W
R_LRU_BX = R_LRU_BA + LRU_W
R_LRU_LAM = R_LRU_BX + LRU_W
IN_TM = 512
IN_TN = 3328
IN_COLS_P = 6656
PREP_TR = 256
STAGE_ROWS = 256
POST_TM = 512
POST_SUB = 2

BF16 = jnp.bfloat16
F32 = jnp.float32
NT_DIMS = (((1,), (1,)), ((), ()))


def _vmem_limit(nbytes):
    return int(min(V7X_VMEM_BYTES - (4 << 20), max(32 << 20, nbytes)))


def _rmsnorm_rows(x, g):
    ms = jnp.mean(x * x, axis=-1, keepdims=True)
    return x * lax.rsqrt(ms + RMS_EPS) * g


def _sigmoid(x):
    return jax.nn.sigmoid(x)


def _silu(x):
    return x * jax.nn.sigmoid(x)


def _softplus(x):
    return jnp.maximum(x, 0.0) + jnp.log1p(jnp.exp(-jnp.abs(x)))


def _shift_rows(x, s):
    rolled = pltpu.roll(x, s, axis=0)
    row = lax.broadcasted_iota(jnp.int32, x.shape, 0)
    return jnp.where(row >= s, rolled, 0.0)


def _causal_conv(x, w_ref, b_ref):
    k = w_ref.shape[0]
    acc = x * w_ref[k - 1:k, :]
    head = x[:SUBLANE]
    acc_head = head * w_ref[k - 1:k, :]
    for j in range(k - 1):
        acc = acc + pltpu.roll(x, k - 1 - j, axis=0) * w_ref[j:j + 1, :]
        acc_head = acc_head + _shift_rows(head, k - 1 - j) * w_ref[j:j + 1, :]
    return jnp.concatenate([acc_head, acc[SUBLANE:]], axis=0) + b_ref[...]


def _w_in_src_rows():
    n_ml = (C_LRU_X - C_ML_U) // PREP_TR
    n_lru = (C_MO_Q - C_LRU_X) // PREP_TR
    n_mo = (C_GATE - C_MO_Q) // PREP_TR
    return ([REF_LRU_END + i * PREP_TR for i in range(n_ml)] + [i * PREP_TR for i in range(n_lru)]
            + [REF_GATE_END + i * PREP_TR for i in range(n_mo)] + [REF_ML_END])


def _in_proj_kernel(x_ref, g_ref, w_hbm, o_ref, gt_ref, w_s, stage_s, sem, *, layer):
    @pl.when(pl.program_id(0) == 0)
    def _():
        src = _w_in_src_rows()

        def copy(k):
            return pltpu.make_async_copy(w_hbm.at[layer, pl.ds(src[k], PREP_TR), :], stage_s.at[k % 2], sem.at[k % 2])

        copy(0).start()
        for k in range(len(src)):
            if k + 1 < len(src):
                copy(k + 1).start()
            copy(k).wait()
            tile = stage_s[k % 2]
            if k == len(src) - 1:
                row = lax.broadcasted_iota(jnp.int32, tile.shape, 0)
                tile = jnp.where(row < 2 * ML_HEADS, tile, 0.0)
            w_s[k * PREP_TR:(k + 1) * PREP_TR, :] = tile.astype(BF16)

    xn = _rmsnorm_rows(x_ref[...], g_ref[...]).astype(BF16)
    n_chunks = w_s.shape[0] // IN_TN
    for j in range(n_chunks):
        cols = pl.ds(j * IN_TN, IN_TN)
        res = lax.dot_general(xn, w_s[cols, :], NT_DIMS, preferred_element_type=F32)
        o_ref[:, cols] = res.astype(o_ref.dtype)
        if j == C_GATE // IN_TN:
            g0 = C_GATE - j * IN_TN
            gt_ref[...] = res[:, g0:g0 + LANE].T[:2 * ML_HEADS, :]


def _in_proj(x2d, g, w_in_t, l):
    m = x2d.shape[0]
    n = IN_COLS_P
    vmem = (2 * IN_TM * D_MODEL * 4 + 2 * IN_TM * D_MODEL * 2 + D_MODEL * n * 2 + 2 * PREP_TR * D_MODEL * 4
            + 2 * IN_TM * n * 2 + 2 * IN_TM * IN_TN * 4)
    return pl.pallas_call(
        functools.partial(_in_proj_kernel, layer=l),
        grid=(m // IN_TM,),
        in_specs=[
            pl.BlockSpec((IN_TM, D_MODEL), lambda i: (i, 0)),
            pl.BlockSpec((None, 1, D_MODEL), lambda i: (l, 0, R_MIX_G // D_MODEL)),
            pl.BlockSpec(memory_space=pl.ANY),
        ],
        out_specs=[pl.BlockSpec((IN_TM, n), lambda i: (i, 0)),
                   pl.BlockSpec((2 * ML_HEADS, IN_TM), lambda i: (0, i))],
        out_shape=[jax.ShapeDtypeStruct((m, n), BF16), jax.ShapeDtypeStruct((2 * ML_HEADS, m), F32)],
        scratch_shapes=[pltpu.VMEM((n, D_MODEL), BF16),
                        pltpu.VMEM((2, PREP_TR, D_MODEL), F32),
                        pltpu.SemaphoreType.DMA((2,))],
        compiler_params=pltpu.CompilerParams(
            dimension_semantics=("arbitrary",), vmem_limit_bytes=_vmem_limit(vmem)),
        name="in_proj",
    )(x2d, g, w_in_t)


def _lru_kernel(x_ref, z_ref, cw_ref, cb_ref, wax_ref, ba_ref, bx_ref, lam_ref, o_ref, a_s, u_s):
    t = x_ref.shape[1]
    x = x_ref[0].astype(F32)
    xc = _causal_conv(x, cw_ref, cb_ref)
    pre = jnp.dot(xc.astype(BF16), wax_ref[0], preferred_element_type=F32)
    r = _sigmoid(pre[:, :LRU_BW] + ba_ref[...])
    i = _sigmoid(pre[:, LRU_BW:] + bx_ref[...])
    log_a = (-LRU_C) * r * _softplus(-lam_ref[...])
    a = jnp.exp(log_a)
    a_s[...] = a
    u_s[...] = jnp.sqrt(-jnp.tanh(log_a) * (1.0 + a * a)) * (i * xc)

    row = lax.broadcasted_iota(jnp.int32, (SUBLANE, LRU_BW), 0)

    def block(blk, h_prev):
        r0 = pl.multiple_of(blk * SUBLANE, SUBLANE)
        a_b = a_s[pl.ds(r0, SUBLANE), :]
        u_b = u_s[pl.ds(r0, SUBLANE), :]
        for s in (1, 2, 4):
            a_sh = jnp.where(row >= s, pltpu.roll(a_b, s, axis=0), 1.0)
            u_sh = jnp.where(row >= s, pltpu.roll(u_b, s, axis=0), 0.0)
            u_b = a_b * u_sh + u_b
            a_b = a_b * a_sh
        h = a_b * h_prev + u_b
        u_s[pl.ds(r0, SUBLANE), :] = h
        return jnp.broadcast_to(h[SUBLANE - 1:SUBLANE, :], (SUBLANE, LRU_BW))

    lax.fori_loop(0, t // SUBLANE, block, jnp.zeros((SUBLANE, LRU_BW), F32), unroll=4)
    z = z_ref[0].astype(F32)
    o_ref[0] = (u_s[...] * _silu(z)).astype(o_ref.dtype)


def _lru_branch(p, cw, wax, rows, l):
    b, t, _ = p.shape
    xb, zb = C_LRU_X // LRU_BW, C_LRU_Z // LRU_BW
    vec = lambda off: pl.BlockSpec((None, 1, LRU_BW), lambda bi, g: (l, 0, off // LRU_BW + g))
    return pl.pallas_call(
        _lru_kernel,
        grid=(b, LRU_BLOCKS),
        in_specs=[
            pl.BlockSpec((1, t, LRU_BW), lambda bi, g: (bi, 0, xb + g)),
            pl.BlockSpec((1, t, LRU_BW), lambda bi, g: (bi, 0, zb + g)),
            pl.BlockSpec((None, cw.shape[1], LRU_BW), lambda bi, g: (l, 0, g)),
            vec(R_LRU_CB),
            pl.BlockSpec((None, 1, LRU_BW, 2 * LRU_BW), lambda bi, g: (l, g, 0, 0)),
            vec(R_LRU_BA), vec(R_LRU_BX), vec(R_LRU_LAM),
        ],
        out_specs=pl.BlockSpec((1, t, LRU_BW), lambda bi, g: (bi, 0, g)),
        out_shape=jax.ShapeDtypeStruct((b, t, LRU_W), BF16),
        scratch_shapes=[pltpu.VMEM((t, LRU_BW), F32), pltpu.VMEM((t, LRU_BW), F32)],
        compiler_params=pltpu.CompilerParams(
            dimension_semantics=("arbitrary", "arbitrary"), vmem_limit_bytes=_vmem_limit(24 * t * LRU_BW * 4)),
        name="rg_lru",
    )(p, p, cw, rows, wax, rows, rows, rows)


def _blockdiag_in_out(w_ref, hh):
    wh = w_ref[...][:, hh * ML_DH:(hh + 1) * ML_DH]
    wh = jnp.concatenate([wh, jnp.zeros((ML_QKV_BLOCK, ML_DHP - ML_DH), F32)], axis=1)
    r = lax.broadcasted_iota(jnp.int32, (ML_PW, ML_DHP), 0)
    c = lax.broadcasted_iota(jnp.int32, (ML_PW, ML_DHP), 1)
    d = jnp.zeros((ML_PW, ML_DHP), F32)
    for i in range(ML_QKV_BLOCK):
        d = jnp.where((r & (ML_QKV_BLOCK - 1)) == i, wh[i:i + 1, :], d)
    keep = (((r >> ML_QKV_SHIFT) - hh * (ML_DH // ML_QKV_BLOCK)) == (c >> ML_QKV_SHIFT)) & (c < ML_DH)
    return jnp.where(keep, d, 0.0)


def _blockdiag_in_pair(w_ref, hh):
    wh = w_ref[...]
    r = lax.broadcasted_iota(jnp.int32, (ML_PW, ML_PW), 0)
    c = lax.broadcasted_iota(jnp.int32, (ML_PW, ML_PW), 1)
    d = jnp.zeros((ML_PW, ML_PW), F32)
    for i in range(ML_QKV_BLOCK):
        d = jnp.where((r & (ML_QKV_BLOCK - 1)) == i, wh[i:i + 1, :], d)
    keep = ((r >> ML_QKV_SHIFT) == (c >> ML_QKV_SHIFT)) & (c >= hh * ML_DH) & (c < (hh + 1) * ML_DH)
    return jnp.where(keep, d, 0.0)


def _blockdiag_out_in(w_ref, hh):
    wk = w_ref[...]
    o = lax.broadcasted_iota(jnp.int32, (ML_DHP, ML_PW), 0)
    r = lax.broadcasted_iota(jnp.int32, (ML_DHP, ML_PW), 1)
    d = jnp.zeros((ML_DHP, ML_PW), F32)
    for j in range(ML_QKV_BLOCK):
        d = jnp.where((o & (ML_QKV_BLOCK - 1)) == j, wk[j:j + 1, :], d)
    keep = ((o >> ML_QKV_SHIFT) == ((r >> ML_QKV_SHIFT) - hh * (ML_DH // ML_QKV_BLOCK))) & (o < ML_DH)
    return jnp.where(keep, d, 0.0)


ML_DEN_LANE = (ML_DH, 0)


def _mlstm_kernel(u_ref, og_ref, z_ref, ig_ref, fg_ref, bi_ref, bf_ref, cw_ref, cb_ref,
                  wq_ref, wkt_ref, wv_ref, ng_ref, y_ref,
                  dq_s, dkt_s, dv_s, q_s, kt_s, v_s, cp_s, c_s, r_s, w_s, col_s, so_s, sn_s):
    t = u_ref.shape[1]
    nc = t // ML_CHUNK
    L = ML_CHUNK
    heads = range(2)

    @pl.when(pl.program_id(1) == 0)
    def _():
        for hh in heads:
            dq_s[hh] = _blockdiag_in_out(wq_ref, hh).astype(BF16)
            dkt_s[hh] = _blockdiag_out_in(wkt_ref, hh).astype(BF16)
            dv_s[hh] = _blockdiag_in_pair(wv_ref, hh).astype(BF16)

    u = u_ref[0]
    uc = _silu(_causal_conv(u.astype(F32), cw_ref, cb_ref)).astype(BF16)
    for hh in heads:
        w0 = hh * (ML_PW - ML_DHP)
        uc_w, u_w = uc[:, w0:w0 + ML_DHP], u[:, w0:w0 + ML_DHP]
        q_s[hh] = jnp.dot(uc_w, dq_s[hh, w0:w0 + ML_DHP, :], preferred_element_type=F32).astype(BF16)
        kt = lax.dot_general(dkt_s[hh, :, w0:w0 + ML_DHP], uc_w, NT_DIMS, preferred_element_type=F32)
        kt = kt * (ML_DH ** -0.5)
        for c in range(nc):
            kt_s[hh, c] = kt[:, c * L:(c + 1) * L].astype(BF16)
        v = jnp.dot(u_w, dv_s[hh, w0:w0 + ML_DHP, :], preferred_element_type=F32)
        vlane = lax.broadcasted_iota(jnp.int32, v.shape, 1)
        v_s[hh] = jnp.where(vlane == ML_DEN_LANE[hh], 1.0, v).astype(BF16)

    glane = lax.broadcasted_iota(jnp.int32, (nc, L), 1)
    grow = lax.broadcasted_iota(jnp.int32, (nc, L), 0)
    row8 = lax.broadcasted_iota(jnp.int32, (SUBLANE, L), 0)
    for hh in heads:
        ig = ig_ref[hh, 0] + bi_ref[hh]
        lf = -_softplus(-(fg_ref[hh, 0] + bf_ref[hh]))
        b = lf
        for k in range(int(math.log2(L))):
            sh = 1 << k
            b = b + jnp.where(glane >= sh, pltpu.roll(b, sh, axis=1), 0.0)
        g = jnp.broadcast_to(b[:, L - 1:L], (nc, L))
        a = g - b + ig
        mloc = jnp.broadcast_to(jnp.max(a, axis=1, keepdims=True), (nc, L))
        m = jnp.zeros((1, L), F32)
        m_prev = jnp.zeros((nc, L), F32)
        m_next = jnp.zeros((nc, L), F32)
        for c in range(nc):
            m_prev = jnp.where(grow == c, m, m_prev)
            m = jnp.maximum(g[c:c + 1, :] + m, mloc[c:c + 1, :])
            m_next = jnp.where(grow == c, m, m_next)
        r = ig - b
        cmx = r
        for k in range(int(math.log2(L))):
            sh = 1 << k
            cmx = jnp.maximum(cmx, jnp.where(glane >= sh, pltpu.roll(cmx, sh, axis=1), -jnp.inf))
        mm = jnp.maximum(m_prev, cmx)
        s_int = jnp.exp(m_prev - mm)
        clamp = jnp.exp(-(b + mm))
        r_s[hh] = r
        w_s[hh] = jnp.exp(a - mloc)
        so_s[hh] = jnp.exp(g + m_prev - m_next)
        sn_s[hh] = jnp.exp(mloc - m_next)
        for c in range(nc):
            col_s[hh, c] = jnp.where(row8 == 0, mm[c:c + 1, :],
                                     jnp.where(row8 == 1, s_int[c:c + 1, :],
                                               jnp.where(row8 == 2, clamp[c:c + 1, :], 0.0)))

    zero_rows = jnp.zeros((ML_DHP - ML_DH, ML_PW), BF16)
    c_s[...] = jnp.zeros(c_s.shape, F32)

    def state(c, carry):
        r0 = pl.multiple_of(c * L, L)
        for hh in heads:
            c_prev = c_s[hh]
            cp_s[hh, c] = jnp.concatenate([c_prev.astype(BF16), zero_rows], axis=0)
            ktw = (kt_s[hh, c, :ML_DH, :].astype(F32) * w_s[hh, pl.ds(c, 1), :]).astype(BF16)
            c_loc = jnp.dot(ktw, v_s[hh, pl.ds(r0, L), :], preferred_element_type=F32)
            c_s[hh] = so_s[hh, pl.ds(c, 1), :][:, :1] * c_prev + sn_s[hh, pl.ds(c, 1), :][:, :1] * c_loc
        return carry

    lax.fori_loop(0, nc, state, 0, unroll=2)

    tri = (lax.broadcasted_iota(jnp.int32, (L, L), 0) >= lax.broadcasted_iota(jnp.int32, (L, L), 1))
    first = lax.broadcasted_iota(jnp.int32, (L, ML_PW), 1) < ML_DH
    pad_rows = jnp.zeros((L - SUBLANE, L), F32)

    def weights(c):
        rows = pl.ds(pl.multiple_of(c * L, L), L)
        out = []
        for hh in heads:
            cols = jnp.concatenate([col_s[hh, c], pad_rows], axis=0).T
            rb = jnp.broadcast_to(r_s[hh, pl.ds(c, 1), :], (L, L))
            decay = jnp.exp(jnp.where(tri, rb - cols[:, 0:1], -jnp.inf))
            s_mat = jnp.dot(q_s[hh, rows, :], kt_s[hh, c], preferred_element_type=F32) * decay
            out.append((s_mat.astype(BF16), cols))
        return out

    def numerators(c, wts):
        rows = pl.ds(pl.multiple_of(c * L, L), L)
        nds, invs = [], []
        for hh, (s_mat, cols) in zip(heads, wts):
            nd = (jnp.dot(s_mat, v_s[hh, rows, :], preferred_element_type=F32)
                  + cols[:, 1:2] * jnp.dot(q_s[hh, rows, :], cp_s[hh, c], preferred_element_type=F32))
            den = nd[:, ML_DEN_LANE[hh]:ML_DEN_LANE[hh] + 1]
            nds.append(nd)
            invs.append(1.0 / jnp.maximum(jnp.abs(den), cols[:, 2:3]))
        return nds, invs

    def finish(c, nds, invs):
        rows = pl.ds(pl.multiple_of(c * L, L), L)
        x = _sigmoid(og_ref[0, rows, :].astype(F32)) * jnp.where(first, nds[0], nds[1])
        mu0 = jnp.sum(jnp.where(first, x, 0.0), axis=1, keepdims=True) * (1.0 / ML_DH)
        mu1 = jnp.sum(jnp.where(first, 0.0, x), axis=1, keepdims=True) * (1.0 / ML_DH)
        dev = x - jnp.where(first, mu0, mu1)
        sq = dev * dev
        var0 = jnp.sum(jnp.where(first, sq, 0.0), axis=1, keepdims=True) * (1.0 / ML_DH)
        var1 = jnp.sum(jnp.where(first, 0.0, sq), axis=1, keepdims=True) * (1.0 / ML_DH)
        f0 = invs[0] * lax.rsqrt(invs[0] * invs[0] * var0 + LN_EPS)
        f1 = invs[1] * lax.rsqrt(invs[1] * invs[1] * var1 + LN_EPS)
        zs = _silu(z_ref[0, rows, :].astype(F32)) * ng_ref[...]
        y_ref[0, rows, :] = (dev * jnp.where(first, f0, f1) * zs).astype(y_ref.dtype)

    def group(gi, carry):
        cs = [gi * ML_GROUP + i for i in range(ML_GROUP)]
        wts = [weights(c) for c in cs]
        nums = [numerators(c, w) for c, w in zip(cs, wts)]
        for c, (nds, invs) in zip(cs, nums):
            finish(c, nds, invs)
        return carry

    lax.fori_loop(0, nc // ML_GROUP, group, 0)


def _mlstm_branch(p, gates, gbias, cw, wq, wkt, wv, rows, l):
    b, t, _ = p.shape
    nc = t // ML_CHUNK
    ub, ob, zb = C_ML_U // ML_PW, C_ML_O // ML_PW, C_ML_Z // ML_PW
    seq = lambda base: pl.BlockSpec((1, t, ML_PW), lambda pr, bi: (bi, 0, base + pr))
    vec = lambda off: pl.BlockSpec((None, 1, ML_PW), lambda pr, bi: (l, 0, off // ML_PW + pr))
    taps = pl.BlockSpec((None, ML_QKV_BLOCK, ML_PW), lambda pr, bi: (l, 0, pr))
    gate_rows = lambda: pltpu.VMEM((2, nc, ML_CHUNK), F32)
    return pl.pallas_call(
        _mlstm_kernel,
        grid=(ML_PAIRS, b),
        in_specs=[
            seq(ub), seq(ob), seq(zb),
            pl.BlockSpec((2, 1, nc, ML_CHUNK), lambda pr, bi: (pr, bi, 0, 0)),
            pl.BlockSpec((2, 1, nc, ML_CHUNK), lambda pr, bi: (ML_PAIRS + pr, bi, 0, 0)),
            pl.BlockSpec((None, 2, 1, ML_CHUNK), lambda pr, bi: (l, pr, 0, 0)),
            pl.BlockSpec((None, 2, 1, ML_CHUNK), lambda pr, bi: (l, ML_PAIRS + pr, 0, 0)),
            pl.BlockSpec((None, cw.shape[1], ML_PW), lambda pr, bi: (l, 0, pr)),
            vec(R_ML_CB), taps, taps, taps, vec(R_ML_NG),
        ],
        out_specs=pl.BlockSpec((1, t, ML_PW), lambda pr, bi: (bi, 0, pr)),
        out_shape=jax.ShapeDtypeStruct((b, t, ML_W), BF16),
        scratch_shapes=[
            pltpu.VMEM((2, ML_PW, ML_DHP), BF16),
            pltpu.VMEM((2, ML_DHP, ML_PW), BF16),
            pltpu.VMEM((2, ML_PW, ML_PW), BF16),
            pltpu.VMEM((2, t, ML_DHP), BF16),
            pltpu.VMEM((2, nc, ML_DHP, ML_CHUNK), BF16),
            pltpu.VMEM((2, t, ML_PW), BF16),
            pltpu.VMEM((2, nc, ML_DHP, ML_PW), BF16),
            pltpu.VMEM((2, ML_DH, ML_PW), F32),
            gate_rows(), gate_rows(),
            pltpu.VMEM((2, nc, SUBLANE, ML_CHUNK), F32),
            gate_rows(), gate_rows(),
        ],
        compiler_params=pltpu.CompilerParams(
            dimension_semantics=("arbitrary", "arbitrary"), vmem_limit_bytes=_vmem_limit(56 << 20)),
        name="mlstm",
    )(p, p, p, gates, gates, gbias, gbias, cw, rows, wq, wkt, wv, rows)


MOBA_SEL_LANE = 6
MOBA_AHEAD = 3


def _moba_key_ext(nb):
    ext = np.zeros((nb, MOBA_BLOCK, MO_DH), np.float32)
    for n in range(nb):
        ext[n, :, 0:3] = n * MOBA_BLOCK
        ext[n, :, 3:6] = np.arange(MOBA_BLOCK, dtype=np.float32)[:, None]
        ext[n, :, MOBA_SEL_LANE + n] = 1.0
    return jnp.asarray(ext, dtype=BF16)


def _moba_slope_rows(slopes):
    rows = np.zeros((len(slopes), 1, MO_DH), np.float32)
    for h, s in enumerate(slopes):
        rest = np.float32(np.float32(s) * np.float32(LOG2E))
        for i in range(3):
            piece = np.float32(rest).astype(BF16).astype(np.float32)
            rows[h, 0, i] = rows[h, 0, 3 + i] = piece
            rest = np.float32(rest - piece)
    return jnp.asarray(rows)


def _moba_kernel(q_ref, k_ref, v_ref, z_ref, srow_ref, kext_ref, o_ref, l_s, qa_s):
    t = q_ref.shape[1]
    nb = t // MOBA_BLOCK
    bs = MOBA_BLOCK
    qscale = (MO_DH ** -0.5) * LOG2E

    krow = lax.broadcasted_iota(jnp.int32, (LANE, MO_DH), 0)
    kmean = jnp.zeros((LANE, MO_DH), F32)
    for n in range(nb):
        mean_n = jnp.sum(k_ref[0, n * bs:(n + 1) * bs, :].astype(F32), axis=0, keepdims=True) * (1.0 / bs)
        kmean = jnp.where(krow == MOBA_SEL_LANE + n, mean_n, kmean)
    kmean = kmean.astype(BF16)

    lane = lax.broadcasted_iota(jnp.int32, (bs, MO_DH), 1)
    causal = (lax.broadcasted_iota(jnp.int32, (bs, bs), 0) >= lax.broadcasted_iota(jnp.int32, (bs, bs), 1))
    ones_col = jnp.where(lane == 0, 1.0, 0.0).astype(BF16)
    slope_cols = jnp.where(lane < MOBA_SEL_LANE, srow_ref[0], 0.0)

    def prepare(qb):
        q_b = q_ref[0, qb * bs:(qb + 1) * bs, :]
        q_ext = slope_cols
        if qb > MOBA_TOPK:
            gate = lax.dot_general(q_b, kmean, NT_DIMS, preferred_element_type=F32)
            beaten = jnp.zeros((bs, MO_DH), F32)
            for m in range(qb):
                gm = gate[:, MOBA_SEL_LANE + m:MOBA_SEL_LANE + m + 1]
                wins = (gm > gate) | ((gm == gate) & (lane > MOBA_SEL_LANE + m))
                beaten = beaten + jnp.where(wins, 1.0, 0.0)
            past = (lane >= MOBA_SEL_LANE) & (lane < MOBA_SEL_LANE + qb)
            q_ext = jnp.where(past & (beaten >= float(MOBA_TOPK)), NEG_INF, slope_cols)
        qa_s[qb] = jnp.concatenate([(q_b.astype(F32) * qscale).astype(BF16), q_ext.astype(BF16)], axis=1)

    def scores(qb):
        q_aug = qa_s[qb]
        l_q = l_s.at[qb % l_s.shape[0]]
        mx = None
        for n in range(qb + 1):
            k_aug = jnp.concatenate([k_ref[0, n * bs:(n + 1) * bs, :], kext_ref[n]], axis=1)
            logit = lax.dot_general(q_aug, k_aug, NT_DIMS, preferred_element_type=F32)
            if n == qb:
                logit = jnp.where(causal, logit, NEG_INF)
            l_q[:, n * bs:(n + 1) * bs] = logit
            half = jnp.maximum(logit[:, :LANE], logit[:, LANE:])
            mx = half if mx is None else jnp.maximum(mx, half)
        return jnp.max(mx, axis=1, keepdims=True)

    def outputs(qb, m_row):
        l_q = l_s.at[qb % l_s.shape[0]]
        acc = jnp.zeros((bs, 2 * MO_DH), F32)
        for n in range(qb + 1):
            pr = jnp.exp2(l_q[:, n * bs:(n + 1) * bs] - m_row).astype(BF16)
            v_aug = jnp.concatenate([v_ref[0, n * bs:(n + 1) * bs, :], ones_col], axis=1)
            acc = acc + jnp.dot(pr, v_aug, preferred_element_type=F32)
        z = z_ref[0, qb * bs:(qb + 1) * bs, :].astype(F32)
        inv = 1.0 / acc[:, MO_DH:MO_DH + 1]
        o_ref[0, qb * bs:(qb + 1) * bs, :] = (acc[:, :MO_DH] * (inv * _silu(z))).astype(o_ref.dtype)

    ahead = l_s.shape[0] - 1
    for qb in range(min(ahead + 1, nb)):
        prepare(qb)
    m_rows = {qb: scores(qb) for qb in range(min(ahead, nb))}
    for qb in range(nb):
        if qb + ahead + 1 < nb:
            prepare(qb + ahead + 1)
        if qb + ahead < nb:
            m_rows[qb + ahead] = scores(qb + ahead)
        outputs(qb, m_rows.pop(qb))


def _moba_branch(p, srows, kext):
    b, t, _ = p.shape
    nb = t // MOBA_BLOCK
    qb, kb, vb, zb = (c // MO_DH for c in (C_MO_Q, C_MO_K, C_MO_V, C_MO_Z))
    seq = lambda base: pl.BlockSpec((1, t, MO_DH), lambda bi, h: (bi, 0, base + h))
    return pl.pallas_call(
        _moba_kernel,
        grid=(b, MO_HEADS),
        in_specs=[seq(qb), seq(kb), seq(vb), seq(zb),
                  pl.BlockSpec((1, 1, MO_DH), lambda bi, h: (h, 0, 0)),
                  pl.BlockSpec((nb, MOBA_BLOCK, MO_DH), lambda bi, h: (0, 0, 0))],
        out_specs=pl.BlockSpec((1, t, MO_DH), lambda bi, h: (bi, 0, h)),
        out_shape=jax.ShapeDtypeStruct((b, t, MO_W), BF16),
        scratch_shapes=[pltpu.VMEM((MOBA_AHEAD + 1, MOBA_BLOCK, t), F32),
                        pltpu.VMEM((nb, MOBA_BLOCK, 2 * MO_DH), BF16)],
        compiler_params=pltpu.CompilerParams(
            dimension_semantics=("arbitrary", "arbitrary"), vmem_limit_bytes=_vmem_limit(32 << 20)),
        name="moba",
    )(p, p, p, p, srows, kext)


def _stage_weights(w_hbm, layer, dst_ref, stage_ref, sem):
    tile_rows = stage_ref.shape[1]
    n_tiles = dst_ref.shape[0] // tile_rows

    def copy(k):
        return pltpu.make_async_copy(w_hbm.at[layer, pl.ds(k * tile_rows, tile_rows), :], stage_ref.at[k % 2],
                                     sem.at[k % 2])

    copy(0).start()
    for k in range(n_tiles):
        if k + 1 < n_tiles:
            copy(k + 1).start()
        copy(k).wait()
        dst_ref[k * tile_rows:(k + 1) * tile_rows, :] = stage_ref[k % 2].astype(BF16)


def _mem_kv_kernel(mem_ref, g_ref, w_hbm, o_ref, w_s, stage_s, sem, *, layer):
    @pl.when(pl.program_id(0) == 0)
    def _():
        _stage_weights(w_hbm, layer, w_s, stage_s, sem)

    hm = _rmsnorm_rows(mem_ref[0], g_ref[...]).astype(BF16)
    o_ref[0] = jnp.dot(hm, w_s[...], preferred_element_type=F32).astype(o_ref.dtype)


def _mem_kv(mem, g, wkv, l):
    b, m, _ = mem.shape
    return pl.pallas_call(
        functools.partial(_mem_kv_kernel, layer=l),
        grid=(b,),
        in_specs=[pl.BlockSpec((1, m, D_MODEL), lambda bi: (bi, 0, 0)),
                  pl.BlockSpec((None, 1, D_MODEL), lambda bi: (l, 0, R_MEM_G // D_MODEL)),
                  pl.BlockSpec(memory_space=pl.ANY)],
        out_specs=pl.BlockSpec((1, m, 2 * XA_W), lambda bi: (bi, 0, 0)),
        out_shape=jax.ShapeDtypeStruct((b, m, 2 * XA_W), BF16),
        scratch_shapes=[pltpu.VMEM((D_MODEL, 2 * XA_W), BF16),
                        pltpu.VMEM((2, STAGE_ROWS, 2 * XA_W), F32),
                        pltpu.SemaphoreType.DMA((2,))],
        compiler_params=pltpu.CompilerParams(
            dimension_semantics=("arbitrary",), vmem_limit_bytes=_vmem_limit(32 << 20)),
        name="mem_kv",
    )(mem, g, wkv)


def _post_kernel(x_ref, yl_ref, ym_ref, yo_ref, wout_hbm, g_ref, wq_hbm, kv_ref, wo_hbm, fg_ref, o_ref,
                 x1_s, wout_ref, wq_ref, wo_ref, stage_s, stage_q_s, sem, sem_q, *, final_norm, layer):
    @pl.when((pl.program_id(0) == 0) & (pl.program_id(1) == 0))
    def _():
        _stage_weights(wout_hbm, layer, wout_ref, stage_s, sem)
        _stage_weights(wo_hbm, layer, wo_ref, stage_s, sem)
        _stage_weights(wq_hbm, layer, wq_ref, stage_q_s, sem_q)

    sub = x_ref.shape[1] // POST_SUB
    groups = [pl.ds(i * sub, sub) for i in range(POST_SUB)]

    for r in groups:
        acc = jnp.dot(yl_ref[0, r, :], wout_ref[0:LRU_W, :], preferred_element_type=F32)
        acc = acc + jnp.dot(ym_ref[0, r, :], wout_ref[LRU_W:LRU_W + ML_W, :], preferred_element_type=F32)
        acc = acc + jnp.dot(yo_ref[0, r, :], wout_ref[LRU_W + ML_W:, :], preferred_element_type=F32)
        x1_s[r, :] = x_ref[0, r, :] + acc
    qs = []
    for r in groups:
        hx = _rmsnorm_rows(x1_s[r, :], g_ref[...]).astype(BF16)
        qs.append(jnp.dot(hx, wq_ref[...], preferred_element_type=F32).astype(BF16))
    scores = []
    for q in qs:
        scores.append([
            lax.dot_general(q[:, h * XA_DH:(h + 1) * XA_DH], kv_ref[0, :, h * XA_DH:(h + 1) * XA_DH], NT_DIMS,
                            preferred_element_type=F32) * (XA_DH ** -0.5)
            for h in range(XA_HEADS)])
    outs = []
    for sc in scores:
        heads = []
        for h, s in enumerate(sc):
            e = jnp.exp(s - jnp.max(s, axis=1, keepdims=True))
            pr = e / jnp.sum(e, axis=1, keepdims=True)
            v_h = kv_ref[0, :, XA_W + h * XA_DH:XA_W + (h + 1) * XA_DH]
            heads.append(jnp.dot(pr.astype(BF16), v_h, preferred_element_type=F32).astype(BF16))
        outs.append(jnp.concatenate(heads, axis=1))
    for r, o in zip(groups, outs):
        y = x1_s[r, :] + jnp.dot(o, wo_ref[...], preferred_element_type=F32)
        if final_norm:
            y = _rmsnorm_rows(y, fg_ref[...])
        o_ref[0, r, :] = y


def _post_mix(x, yl, ym, yo, wout, g, wq, kv, wo, fg, l, final_norm):
    b, t, _ = x.shape
    m = kv.shape[1]
    rows = lambda width: pl.BlockSpec((1, POST_TM, width), lambda bi, i: (bi, i, 0))
    hbm = pl.BlockSpec(memory_space=pl.ANY)
    weights = (D_MODEL * D_MODEL + D_MODEL * XA_W + XA_W * D_MODEL) * 2
    stages = 2 * STAGE_ROWS * (D_MODEL + XA_W) * 4
    tiles = 2 * POST_TM * (2 * D_MODEL * 4 + D_MODEL * 2) + 2 * m * 2 * XA_W * 2
    vmem = weights + stages + tiles + 6 * POST_TM * D_MODEL * 4
    return pl.pallas_call(
        functools.partial(_post_kernel, final_norm=final_norm, layer=l),
        grid=(b, t // POST_TM),
        in_specs=[rows(D_MODEL), rows(LRU_W), rows(ML_W), rows(MO_W),
                  hbm,
                  pl.BlockSpec((None, 1, D_MODEL), lambda bi, i: (l, 0, R_XA_G // D_MODEL)),
                  hbm,
                  pl.BlockSpec((1, m, 2 * XA_W), lambda bi, i: (bi, 0, 0)),
                  hbm,
                  pl.BlockSpec((1, D_MODEL), lambda bi, i: (0, 0))],
        out_specs=rows(D_MODEL),
        out_shape=jax.ShapeDtypeStruct((b, t, D_MODEL), F32),
        scratch_shapes=[pltpu.VMEM((POST_TM, D_MODEL), F32),
                        pltpu.VMEM((D_MODEL, D_MODEL), BF16),
                        pltpu.VMEM((D_MODEL, XA_W), BF16),
                        pltpu.VMEM((XA_W, D_MODEL), BF16),
                        pltpu.VMEM((2, STAGE_ROWS, D_MODEL), F32),
                        pltpu.VMEM((2, STAGE_ROWS, XA_W), F32),
                        pltpu.SemaphoreType.DMA((2,)),
                        pltpu.SemaphoreType.DMA((2,))],
        compiler_params=pltpu.CompilerParams(
            dimension_semantics=("arbitrary", "arbitrary"), vmem_limit_bytes=_vmem_limit(vmem)),
        name="post_mix",
    )(x, yl, ym, yo, wout, g, wq, kv, wo, fg)


def _alibi_slopes(n):
    def pow2(m):
        start = 2.0 ** (-8.0 / m)
        return [start ** (i + 1) for i in range(m)]
    if math.log2(n).is_integer():
        s = pow2(n)
    else:
        c = 2 ** int(math.floor(math.log2(n)))
        s = pow2(c) + pow2(2 * c)[0::2][:n - c]
    return np.asarray(s, dtype=np.float32)


def _taps(w, perm):
    depth = w.shape[0]
    return jnp.transpose(w, perm).reshape(depth, ML_QKV_BLOCK, ML_W)


def kernel(x, mem, mix_norm_g, w_in, lru_conv_w, lru_conv_b, lru_wa, lru_ba, lru_wx, lru_bx, lru_lambda,
           ml_conv_w, ml_conv_b, ml_wq, ml_wk, ml_wv, ml_bi, ml_bf, ml_norm_g, w_out, xa_norm_g, mem_norm_g,
           xa_wq, xa_wkv, xa_wo, final_norm_g):
    b, t, d = x.shape
    depth = w_in.shape[0]
    nc = t // ML_CHUNK
    moba_srows = _moba_slope_rows(_alibi_slopes(MO_HEADS))
    moba_kext = _moba_key_ext(t // MOBA_BLOCK)
    w_in_t = jnp.swapaxes(w_in, 1, 2)
    lru_wax = jnp.concatenate([lru_wa, lru_wx], axis=-1).astype(BF16)
    ml_wq_t = _taps(ml_wq, (0, 2, 1, 3))
    ml_wkt_t = _taps(ml_wk, (0, 3, 1, 2))
    ml_wv_t = _taps(ml_wv, (0, 2, 1, 3))
    ml_gbias = jnp.broadcast_to(jnp.concatenate([ml_bi, ml_bf], axis=1)[:, :, None, None],
                                (depth, 2 * ML_HEADS, 1, ML_CHUNK))
    rows = jnp.concatenate([mix_norm_g, xa_norm_g, mem_norm_g, ml_conv_b, ml_norm_g,
                            lru_conv_b, lru_ba, lru_bx, lru_lambda], axis=1)[:, None, :]
    for l in range(depth):
        p, gates = _in_proj(x.reshape(b * t, d), rows, w_in_t, l)
        p, gates = p.reshape(b, t, IN_COLS_P), gates.reshape(2 * ML_HEADS, b, nc, ML_CHUNK)
        y_lru = _lru_branch(p, lru_conv_w, lru_wax, rows, l)
        y_ml = _mlstm_branch(p, gates, ml_gbias, ml_conv_w, ml_wq_t, ml_wkt_t, ml_wv_t, rows, l)
        y_mo = _moba_branch(p, moba_srows, moba_kext)
        kv = _mem_kv(mem, rows, xa_wkv, l)
        x = _post_mix(x, y_lru, y_ml, y_mo, w_out, rows, xa_wq, kv, xa_wo, final_norm_g[None, :], l,
                      final_norm=(l == depth - 1))
    return x
```

```python
import functools
import math

import jax
import jax.numpy as jnp
import numpy as np
from jax import lax
from jax.experimental import pallas as pl
from jax.experimental.pallas import tpu as pltpu

LANE = 128
SUBLANE = 8
V7X_VMEM_BYTES = 64 * 1024 * 1024

D_MODEL = 2048
LRU_W = 512
LRU_BLOCKS = 4
LRU_BW = LRU_W // LRU_BLOCKS
LRU_C = 8.0
ML_W = 768
ML_HEADS = 4
ML_DH = 192
ML_DHP = 256
ML_PAIRS = 2
ML_PW = 2 * ML_DH
ML_QKV_BLOCK = 4
ML_QKV_SHIFT = 2
ML_CHUNK = 128
ML_GROUP = 4
MO_W = 768
MO_HEADS = 6
MO_DH = 128
MOBA_BLOCK = 256
MOBA_TOPK = 3
XA_HEADS = 4
XA_DH = 128
XA_W = XA_HEADS * XA_DH
RMS_EPS = 1e-6
LN_EPS = 1e-5
NEG_INF = -1e30
LOG2E = 1.4426950408889634

REF_LRU_END = 2 * LRU_W
REF_ML_END = REF_LRU_END + 3 * ML_W
REF_GATE_END = REF_ML_END + 2 * ML_HEADS

C_ML_U = 0
C_ML_O = C_ML_U + ML_W
C_ML_Z = C_ML_O + ML_W
C_LRU_X = C_ML_Z + ML_W
C_LRU_Z = C_LRU_X + LRU_W
C_MO_Q = C_LRU_Z + LRU_W
C_MO_K = C_MO_Q + MO_W
C_MO_V = C_MO_K + MO_W
C_MO_Z = C_MO_V + MO_W
C_GATE = C_MO_Z + MO_W
R_MIX_G = 0
R_XA_G = R_MIX_G + D_MODEL
R_MEM_G = R_XA_G + D_MODEL
R_ML_CB = R_MEM_G + D_MODEL
R_ML_NG = R_ML_CB + ML_W
R_LRU_CB = R_ML_NG + ML_W
R_LRU_BA = R_LRU_CB + LRU_W
R_LRU_BX = R_LRU_BA + LRU_W
R_LRU_LAM = R_LRU_BX + LRU_W
IN_TM = 512
IN_TN = 3328
IN_COLS_P = 6656
PREP_TR = 256
STAGE_ROWS = 256
POST_TM = 512
POST_SUB = 2

BF16 = jnp.bfloat16
F32 = jnp.float32
NT_DIMS = (((1,), (1,)), ((), ()))


def _vmem_limit(nbytes):
    return int(min(V7X_VMEM_BYTES - (4 << 20), max(32 << 20, nbytes)))


def _rmsnorm_rows(x, g):
    ms = jnp.mean(x * x, axis=-1, keepdims=True)
    return x * lax.rsqrt(ms + RMS_EPS) * g


def _sigmoid(x):
    return jax.nn.sigmoid(x)


def _silu(x):
    return x * jax.nn.sigmoid(x)


def _softplus(x):
    return jnp.maximum(x, 0.0) + jnp.log1p(jnp.exp(-jnp.abs(x)))


def _shift_rows(x, s):
    rolled = pltpu.roll(x, s, axis=0)
    row = lax.broadcasted_iota(jnp.int32, x.shape, 0)
    return jnp.where(row >= s, rolled, 0.0)


def _causal_conv(x, w_ref, b_ref):
    k = w_ref.shape[0]
    acc = x * w_ref[k - 1:k, :]
    head = x[:SUBLANE]
    acc_head = head * w_ref[k - 1:k, :]
    for j in range(k - 1):
        acc = acc + pltpu.roll(x, k - 1 - j, axis=0) * w_ref[j:j + 1, :]
        acc_head = acc_head + _shift_rows(head, k - 1 - j) * w_ref[j:j + 1, :]
    return jnp.concatenate([acc_head, acc[SUBLANE:]], axis=0) + b_ref[...]


def _w_in_src_rows():
    n_ml = (C_LRU_X - C_ML_U) // PREP_TR
    n_lru = (C_MO_Q - C_LRU_X) // PREP_TR
    n_mo = (C_GATE - C_MO_Q) // PREP_TR
    return ([REF_LRU_END + i * PREP_TR for i in range(n_ml)] + [i * PREP_TR for i in range(n_lru)]
            + [REF_GATE_END + i * PREP_TR for i in range(n_mo)] + [REF_ML_END])


def _in_proj_kernel(x_ref, g_ref, w_hbm, o_ref, gt_ref, w_s, stage_s, sem, *, layer):
    @pl.when(pl.program_id(0) == 0)
    def _():
        src = _w_in_src_rows()

        def copy(k):
            return pltpu.make_async_copy(w_hbm.at[layer, pl.ds(src[k], PREP_TR), :], stage_s.at[k % 2], sem.at[k % 2])

        copy(0).start()
        for k in range(len(src)):
            if k + 1 < len(src):
                copy(k + 1).start()
            copy(k).wait()
            tile = stage_s[k % 2]
            if k == len(src) - 1:
                row = lax.broadcasted_iota(jnp.int32, tile.shape, 0)
                tile = jnp.where(row < 2 * ML_HEADS, tile, 0.0)
            w_s[k * PREP_TR:(k + 1) * PREP_TR, :] = tile.astype(BF16)

    xn = _rmsnorm_rows(x_ref[...], g_ref[...]).astype(BF16)
    n_chunks = w_s.shape[0] // IN_TN
    for j in range(n_chunks):
        cols = pl.ds(j * IN_TN, IN_TN)
        res = lax.dot_general(xn, w_s[cols, :], NT_DIMS, preferred_element_type=F32)
        o_ref[:, cols] = res.astype(o_ref.dtype)
        if j == C_GATE // IN_TN:
            g0 = C_GATE - j * IN_TN
            gt_ref[...] = res[:, g0:g0 + LANE].T[:2 * ML_HEADS, :]


def _in_proj(x2d, g, w_in_t, l):
    m = x2d.shape[0]
    n = IN_COLS_P
    vmem = (2 * IN_TM * D_MODEL * 4 + 2 * IN_TM * D_MODEL * 2 + D_MODEL * n * 2 + 2 * PREP_TR * D_MODEL * 4
            + 2 * IN_TM * n * 2 + 2 * IN_TM * IN_TN * 4)
    return pl.pallas_call(
        functools.partial(_in_proj_kernel, layer=l),
        grid=(m // IN_TM,),
        in_specs=[
            pl.BlockSpec((IN_TM, D_MODEL), lambda i: (i, 0)),
            pl.BlockSpec((None, 1, D_MODEL), lambda i: (l, 0, R_MIX_G // D_MODEL)),
            pl.BlockSpec(memory_space=pl.ANY),
        ],
        out_specs=[pl.BlockSpec((IN_TM, n), lambda i: (i, 0)),
                   pl.BlockSpec((2 * ML_HEADS, IN_TM), lambda i: (0, i))],
        out_shape=[jax.ShapeDtypeStruct((m, n), BF16), jax.ShapeDtypeStruct((2 * ML_HEADS, m), F32)],
        scratch_shapes=[pltpu.VMEM((n, D_MODEL), BF16),
                        pltpu.VMEM((2, PREP_TR, D_MODEL), F32),
                        pltpu.SemaphoreType.DMA((2,))],
        compiler_params=pltpu.CompilerParams(
            dimension_semantics=("arbitrary",), vmem_limit_bytes=_vmem_limit(vmem)),
        name="in_proj",
    )(x2d, g, w_in_t)


def _lru_kernel(x_ref, z_ref, cw_ref, cb_ref, wax_ref, ba_ref, bx_ref, lam_ref, o_ref, a_s, u_s):
    t = x_ref.shape[1]
    x = x_ref[0].astype(F32)
    xc = _causal_conv(x, cw_ref, cb_ref)
    pre = jnp.dot(xc.astype(BF16), wax_ref[0], preferred_element_type=F32)
    r = _sigmoid(pre[:, :LRU_BW] + ba_ref[...])
    i = _sigmoid(pre[:, LRU_BW:] + bx_ref[...])
    log_a = (-LRU_C) * r * _softplus(-lam_ref[...])
    a = jnp.exp(log_a)
    a_s[...] = a
    u_s[...] = jnp.sqrt(-jnp.tanh(log_a) * (1.0 + a * a)) * (i * xc)

    row = lax.broadcasted_iota(jnp.int32, (SUBLANE, LRU_BW), 0)

    def block(blk, h_prev):
        r0 = pl.multiple_of(blk * SUBLANE, SUBLANE)
        a_b = a_s[pl.ds(r0, SUBLANE), :]
        u_b = u_s[pl.ds(r0, SUBLANE), :]
        for s in (1, 2, 4):
            a_sh = jnp.where(row >= s, pltpu.roll(a_b, s, axis=0), 1.0)
            u_sh = jnp.where(row >= s, pltpu.roll(u_b, s, axis=0), 0.0)
            u_b = a_b * u_sh + u_b
            a_b = a_b * a_sh
        h = a_b * h_prev + u_b
        u_s[pl.ds(r0, SUBLANE), :] = h
        return jnp.broadcast_to(h[SUBLANE - 1:SUBLANE, :], (SUBLANE, LRU_BW))

    lax.fori_loop(0, t // SUBLANE, block, jnp.zeros((SUBLANE, LRU_BW), F32), unroll=4)
    z = z_ref[0].astype(F32)
    o_ref[0] = (u_s[...] * _silu(z)).astype(o_ref.dtype)


def _lru_branch(p, cw, wax, rows, l):
    b, t, _ = p.shape
    xb, zb = C_LRU_X // LRU_BW, C_LRU_Z // LRU_BW
    vec = lambda off: pl.BlockSpec((None, 1, LRU_BW), lambda bi, g: (l, 0, off // LRU_BW + g))
    return pl.pallas_call(
        _lru_kernel,
        grid=(b, LRU_BLOCKS),
        in_specs=[
            pl.BlockSpec((1, t, LRU_BW), lambda bi, g: (bi, 0, xb + g)),
            pl.BlockSpec((1, t, LRU_BW), lambda bi, g: (bi, 0, zb + g)),
            pl.BlockSpec((None, cw.shape[1], LRU_BW), lambda bi, g: (l, 0, g)),
            vec(R_LRU_CB),
            pl.BlockSpec((None, 1, LRU_BW, 2 * LRU_BW), lambda bi, g: (l, g, 0, 0)),
            vec(R_LRU_BA), vec(R_LRU_BX), vec(R_LRU_LAM),
        ],
        out_specs=pl.BlockSpec((1, t, LRU_BW), lambda bi, g: (bi, 0, g)),
        out_shape=jax.ShapeDtypeStruct((b, t, LRU_W), BF16),
        scratch_shapes=[pltpu.VMEM((t, LRU_BW), F32), pltpu.VMEM((t, LRU_BW), F32)],
        compiler_params=pltpu.CompilerParams(
            dimension_semantics=("arbitrary", "arbitrary"), vmem_limit_bytes=_vmem_limit(24 * t * LRU_BW * 4)),
        name="rg_lru",
    )(p, p, cw, rows, wax, rows, rows, rows)


def _blockdiag_in_out(w_ref, hh):
    wh = w_ref[...][:, hh * ML_DH:(hh + 1) * ML_DH]
    wh = jnp.concatenate([wh, jnp.zeros((ML_QKV_BLOCK, ML_DHP - ML_DH), F32)], axis=1)
    r = lax.broadcasted_iota(jnp.int32, (ML_PW, ML_DHP), 0)
    c = lax.broadcasted_iota(jnp.int32, (ML_PW, ML_DHP), 1)
    d = jnp.zeros((ML_PW, ML_DHP), F32)
    for i in range(ML_QKV_BLOCK):
        d = jnp.where((r & (ML_QKV_BLOCK - 1)) == i, wh[i:i + 1, :], d)
    keep = (((r >> ML_QKV_SHIFT) - hh * (ML_DH // ML_QKV_BLOCK)) == (c >> ML_QKV_SHIFT)) & (c < ML_DH)
    return jnp.where(keep, d, 0.0)


def _blockdiag_in_window(w_ref, hh):
    w0 = ML_WINDOW[hh]
    wh = w_ref[...][:, w0:w0 + ML_DHP]
    r = lax.broadcasted_iota(jnp.int32, (ML_DHP, ML_DHP), 0) + w0
    c = lax.broadcasted_iota(jnp.int32, (ML_DHP, ML_DHP), 1) + w0
    d = jnp.zeros((ML_DHP, ML_DHP), F32)
    for i in range(ML_QKV_BLOCK):
        d = jnp.where((r & (ML_QKV_BLOCK - 1)) == i, wh[i:i + 1, :], d)
    keep = ((r >> ML_QKV_SHIFT) == (c >> ML_QKV_SHIFT)) & (c >= hh * ML_DH) & (c < (hh + 1) * ML_DH)
    return jnp.where(keep, d, 0.0)


def _blockdiag_out_in(w_ref, hh):
    wk = w_ref[...]
    o = lax.broadcasted_iota(jnp.int32, (ML_DHP, ML_PW), 0)
    r = lax.broadcasted_iota(jnp.int32, (ML_DHP, ML_PW), 1)
    d = jnp.zeros((ML_DHP, ML_PW), F32)
    for j in range(ML_QKV_BLOCK):
        d = jnp.where((o & (ML_QKV_BLOCK - 1)) == j, wk[j:j + 1, :], d)
    keep = ((o >> ML_QKV_SHIFT) == ((r >> ML_QKV_SHIFT) - hh * (ML_DH // ML_QKV_BLOCK))) & (o < ML_DH)
    return jnp.where(keep, d, 0.0)


ML_WINDOW = (0, ML_PW - ML_DHP)
ML_DEN_LANE = (ML_DH + 2, 2)


def _mlstm_kernel(u_ref, og_ref, z_ref, ig_ref, fg_ref, bi_ref, bf_ref, cw_ref, cb_ref,
                  wq_ref, wkt_ref, wv_ref, ng_ref, y_ref,
                  dq_s, dkt_s, dv_s, q_s, kt_s, v_s, cp_s, c_s, r_s, w_s, col_s, so_s, sn_s):
    t = u_ref.shape[1]
    nc = t // ML_CHUNK
    L = ML_CHUNK
    heads = range(2)

    @pl.when(pl.program_id(1) == 0)
    def _():
        for hh in heads:
            dq_s[hh] = _blockdiag_in_out(wq_ref, hh).astype(BF16)
            dkt_s[hh] = _blockdiag_out_in(wkt_ref, hh).astype(BF16)
            dv_s[hh] = _blockdiag_in_window(wv_ref, hh).astype(BF16)

    u = u_ref[0]
    uc = _silu(_causal_conv(u.astype(F32), cw_ref, cb_ref)).astype(BF16)
    for hh in heads:
        w0 = ML_WINDOW[hh]
        uc_w, u_w = uc[:, w0:w0 + ML_DHP], u[:, w0:w0 + ML_DHP]
        q_s[hh] = jnp.dot(uc_w, dq_s[hh, w0:w0 + ML_DHP, :], preferred_element_type=F32).astype(BF16)
        kt = lax.dot_general(dkt_s[hh, :, w0:w0 + ML_DHP], uc_w, NT_DIMS, preferred_element_type=F32)
        kt = kt * (ML_DH ** -0.5)
        for c in range(nc):
            kt_s[hh, c] = kt[:, c * L:(c + 1) * L].astype(BF16)
        v = jnp.dot(u_w, dv_s[hh], preferred_element_type=F32)
        vlane = lax.broadcasted_iota(jnp.int32, v.shape, 1)
        v_s[hh] = jnp.where(vlane == ML_DEN_LANE[hh], 1.0, v).astype(BF16)

    glane = lax.broadcasted_iota(jnp.int32, (nc, L), 1)
    grow = lax.broadcasted_iota(jnp.int32, (nc, L), 0)
    row8 = lax.broadcasted_iota(jnp.int32, (SUBLANE, L), 0)
    for hh in heads:
        ig = ig_ref[hh, 0] + bi_ref[hh]
        lf = -_softplus(-(fg_ref[hh, 0] + bf_ref[hh]))
        b = lf
        for k in range(int(math.log2(L))):
            sh = 1 << k
            b = b + jnp.where(glane >= sh, pltpu.roll(b, sh, axis=1), 0.0)
        g = jnp.broadcast_to(b[:, L - 1:L], (nc, L))
        a = g - b + ig
        mloc = jnp.broadcast_to(jnp.max(a, axis=1, keepdims=True), (nc, L))
        m = jnp.zeros((1, L), F32)
        m_prev = jnp.zeros((nc, L), F32)
        m_next = jnp.zeros((nc, L), F32)
        for c in range(nc):
            m_prev = jnp.where(grow == c, m, m_prev)
            m = jnp.maximum(g[c:c + 1, :] + m, mloc[c:c + 1, :])
            m_next = jnp.where(grow == c, m, m_next)
        r = ig - b
        cmx = r
        for k in range(int(math.log2(L))):
            sh = 1 << k
            cmx = jnp.maximum(cmx, jnp.where(glane >= sh, pltpu.roll(cmx, sh, axis=1), -jnp.inf))
        mm = jnp.maximum(m_prev, cmx)
        s_int = jnp.exp(m_prev - mm)
        clamp = jnp.exp(-(b + mm))
        r_s[hh] = r
        w_s[hh] = jnp.exp(a - mloc)
        so_s[hh] = jnp.exp(g + m_prev - m_next)
        sn_s[hh] = jnp.exp(mloc - m_next)
        for c in range(nc):
            col_s[hh, c] = jnp.where(row8 == 0, mm[c:c + 1, :],
                                     jnp.where(row8 == 1, s_int[c:c + 1, :],
                                               jnp.where(row8 == 2, clamp[c:c + 1, :], 0.0)))

    zero_rows = jnp.zeros((ML_DHP - ML_DH, ML_DHP), BF16)
    c_s[...] = jnp.zeros(c_s.shape, F32)

    def state(c, carry):
        r0 = pl.multiple_of(c * L, L)
        for hh in heads:
            c_prev = c_s[hh]
            cp_s[hh, c] = jnp.concatenate([c_prev.astype(BF16), zero_rows], axis=0)
            ktw = (kt_s[hh, c, :ML_DH, :].astype(F32) * w_s[hh, pl.ds(c, 1), :]).astype(BF16)
            c_loc = jnp.dot(ktw, v_s[hh, pl.ds(r0, L), :], preferred_element_type=F32)
            c_s[hh] = so_s[hh, pl.ds(c, 1), :][:, :1] * c_prev + sn_s[hh, pl.ds(c, 1), :][:, :1] * c_loc
        return carry

    lax.fori_loop(0, nc, state, 0, unroll=2)

    tri = (lax.broadcasted_iota(jnp.int32, (L, L), 0) >= lax.broadcasted_iota(jnp.int32, (L, L), 1))
    first = lax.broadcasted_iota(jnp.int32, (L, ML_PW), 1) < ML_DH

    def chunk_rows(c):
        return pl.ds(pl.multiple_of(c * L, L), L)

    def raw_scores(c):
        rows = chunk_rows(c)
        return [jnp.dot(q_s[hh, rows, :], kt_s[hh, c], preferred_element_type=F32) for hh in heads]

    def weights(c, raw):
        out = []
        for hh in heads:
            c0 = ML_DEN_LANE[hh] % LANE - 2
            pads = [jnp.zeros((n, L), F32) for n in (c0, L - SUBLANE - c0)]
            cols = jnp.concatenate([a for a in (pads[0], col_s[hh, c], pads[1]) if a.shape[0]], axis=0).T
            rb = jnp.broadcast_to(r_s[hh, pl.ds(c, 1), :], (L, L))
            decay = jnp.exp(jnp.where(tri, rb - cols[:, c0:c0 + 1], -jnp.inf))
            out.append(((raw[hh] * decay).astype(BF16), cols[:, c0 + 1:c0 + 2], cols[:, c0 + 2:c0 + 3]))
        return out

    def numerators(c, wts):
        rows = chunk_rows(c)
        nds, invs = [], []
        for hh, (s_mat, s_int, clamp) in zip(heads, wts):
            nd = (jnp.dot(s_mat, v_s[hh, rows, :], preferred_element_type=F32)
                  + s_int * jnp.dot(q_s[hh, rows, :], cp_s[hh, c], preferred_element_type=F32))
            den = nd[:, ML_DEN_LANE[hh]:ML_DEN_LANE[hh] + 1]
            nds.append(nd)
            invs.append(1.0 / jnp.maximum(jnp.abs(den), clamp))
        w1 = ML_WINDOW[1]
        shared_first = lax.broadcasted_iota(jnp.int32, (L, ML_DHP - w1), 1) < ML_DH - w1
        shared = jnp.where(shared_first, nds[0][:, w1:], nds[1][:, :ML_DHP - w1])
        return jnp.concatenate([nds[0][:, :w1], shared, nds[1][:, ML_DHP - w1:]], axis=1), invs

    def gates(c):
        rows = chunk_rows(c)
        return (_sigmoid(og_ref[0, rows, :].astype(F32)),
                _silu(z_ref[0, rows, :].astype(F32)) * ng_ref[...])

    def means(nd, og):
        x = og * nd
        mu0 = jnp.sum(jnp.where(first, x, 0.0), axis=1, keepdims=True) * (1.0 / ML_DH)
        mu1 = jnp.sum(jnp.where(first, 0.0, x), axis=1, keepdims=True) * (1.0 / ML_DH)
        return x, mu0, mu1

    def variances(x, mu0, mu1):
        dev = x - jnp.where(first, mu0, mu1)
        sq = dev * dev
        var0 = jnp.sum(jnp.where(first, sq, 0.0), axis=1, keepdims=True) * (1.0 / ML_DH)
        var1 = jnp.sum(jnp.where(first, 0.0, sq), axis=1, keepdims=True) * (1.0 / ML_DH)
        return dev, var0, var1

    def finish(c, dev, var0, var1, invs, zs):
        f0 = invs[0] * lax.rsqrt(invs[0] * invs[0] * var0 + LN_EPS)
        f1 = invs[1] * lax.rsqrt(invs[1] * invs[1] * var1 + LN_EPS)
        y_ref[0, chunk_rows(c), :] = (dev * jnp.where(first, f0, f1) * zs).astype(y_ref.dtype)

    def group(gi, carry):
        cs = [gi * ML_GROUP + i for i in range(ML_GROUP)]
        n = range(ML_GROUP)
        raws = [raw_scores(c) for c in cs]
        gts = [gates(c) for c in cs]
        wts = [weights(c, raws[i]) for i, c in enumerate(cs)]
        nums = [numerators(c, wts[i]) for i, c in enumerate(cs)]
        mus = [means(nums[i][0], gts[i][0]) for i in n]
        vrs = [variances(*mus[i]) for i in n]
        for i, c in enumerate(cs):
            finish(c, *vrs[i], nums[i][1], gts[i][1])
        return carry

    lax.fori_loop(0, nc // ML_GROUP, group, 0)


def _mlstm_branch(p, gates, gbias, cw, wq, wkt, wv, rows, l):
    b, t, _ = p.shape
    nc = t // ML_CHUNK
    ub, ob, zb = C_ML_U // ML_PW, C_ML_O // ML_PW, C_ML_Z // ML_PW
    seq = lambda base: pl.BlockSpec((1, t, ML_PW), lambda pr, bi: (bi, 0, base + pr))
    vec = lambda off: pl.BlockSpec((None, 1, ML_PW), lambda pr, bi: (l, 0, off // ML_PW + pr))
    taps = pl.BlockSpec((None, ML_QKV_BLOCK, ML_PW), lambda pr, bi: (l, 0, pr))
    gate_rows = lambda: pltpu.VMEM((2, nc, ML_CHUNK), F32)
    return pl.pallas_call(
        _mlstm_kernel,
        grid=(ML_PAIRS, b),
        in_specs=[
            seq(ub), seq(ob), seq(zb),
            pl.BlockSpec((2, 1, nc, ML_CHUNK), lambda pr, bi: (pr, bi, 0, 0)),
            pl.BlockSpec((2, 1, nc, ML_CHUNK), lambda pr, bi: (ML_PAIRS + pr, bi, 0, 0)),
            pl.BlockSpec((None, 2, 1, ML_CHUNK), lambda pr, bi: (l, pr, 0, 0)),
            pl.BlockSpec((None, 2, 1, ML_CHUNK), lambda pr, bi: (l, ML_PAIRS + pr, 0, 0)),
            pl.BlockSpec((None, cw.shape[1], ML_PW), lambda pr, bi: (l, 0, pr)),
            vec(R_ML_CB), taps, taps, taps, vec(R_ML_NG),
        ],
        out_specs=pl.BlockSpec((1, t, ML_PW), lambda pr, bi: (bi, 0, pr)),
        out_shape=jax.ShapeDtypeStruct((b, t, ML_W), BF16),
        scratch_shapes=[
            pltpu.VMEM((2, ML_PW, ML_DHP), BF16),
            pltpu.VMEM((2, ML_DHP, ML_PW), BF16),
            pltpu.VMEM((2, ML_DHP, ML_DHP), BF16),
            pltpu.VMEM((2, t, ML_DHP), BF16),
            pltpu.VMEM((2, nc, ML_DHP, ML_CHUNK), BF16),
            pltpu.VMEM((2, t, ML_DHP), BF16),
            pltpu.VMEM((2, nc, ML_DHP, ML_DHP), BF16),
            pltpu.VMEM((2, ML_DH, ML_DHP), F32),
            gate_rows(), gate_rows(),
            pltpu.VMEM((2, nc, SUBLANE, ML_CHUNK), F32),
            gate_rows(), gate_rows(),
        ],
        compiler_params=pltpu.CompilerParams(
            dimension_semantics=("arbitrary", "arbitrary"), vmem_limit_bytes=_vmem_limit(56 << 20)),
        name="mlstm",
    )(p, p, p, gates, gates, gbias, gbias, cw, rows, wq, wkt, wv, rows)


MOBA_SEL_LANE = 6
MOBA_AHEAD = 3


def _moba_key_ext(nb):
    ext = np.zeros((nb, MOBA_BLOCK, MO_DH), np.float32)
    for n in range(nb):
        ext[n, :, 0:3] = n * MOBA_BLOCK
        ext[n, :, 3:6] = np.arange(MOBA_BLOCK, dtype=np.float32)[:, None]
        ext[n, :, MOBA_SEL_LANE + n] = 1.0
    return jnp.asarray(ext, dtype=BF16)


def _moba_slope_rows(slopes):
    rows = np.zeros((len(slopes), 1, MO_DH), np.float32)
    for h, s in enumerate(slopes):
        rest = np.float32(np.float32(s) * np.float32(LOG2E))
        for i in range(3):
            piece = np.float32(rest).astype(BF16).astype(np.float32)
            rows[h, 0, i] = rows[h, 0, 3 + i] = piece
            rest = np.float32(rest - piece)
    return jnp.asarray(rows)


def _moba_kernel(q_ref, k_ref, v_ref, z_ref, srow_ref, kext_ref, o_ref, l_s, qa_s):
    t = q_ref.shape[1]
    nb = t // MOBA_BLOCK
    bs = MOBA_BLOCK
    qscale = (MO_DH ** -0.5) * LOG2E

    krow = lax.broadcasted_iota(jnp.int32, (LANE, MO_DH), 0)
    kmean = jnp.zeros((LANE, MO_DH), F32)
    for n in range(nb):
        mean_n = jnp.sum(k_ref[0, n * bs:(n + 1) * bs, :].astype(F32), axis=0, keepdims=True) * (1.0 / bs)
        kmean = jnp.where(krow == MOBA_SEL_LANE + n, mean_n, kmean)
    kmean = kmean.astype(BF16)

    lane = lax.broadcasted_iota(jnp.int32, (bs, MO_DH), 1)
    causal = (lax.broadcasted_iota(jnp.int32, (bs, bs), 0) >= lax.broadcasted_iota(jnp.int32, (bs, bs), 1))
    ones_col = jnp.where(lane == 0, 1.0, 0.0).astype(BF16)
    slope_cols = jnp.where(lane < MOBA_SEL_LANE, srow_ref[0], 0.0)

    def prepare(qb):
        q_b = q_ref[0, qb * bs:(qb + 1) * bs, :]
        q_ext = slope_cols
        if qb > MOBA_TOPK:
            gate = lax.dot_general(q_b, kmean, NT_DIMS, preferred_element_type=F32)
            beaten = jnp.zeros((bs, MO_DH), F32)
            for m in range(qb):
                gm = gate[:, MOBA_SEL_LANE + m:MOBA_SEL_LANE + m + 1]
                wins = (gm > gate) | ((gm == gate) & (lane > MOBA_SEL_LANE + m))
                beaten = beaten + jnp.where(wins, 1.0, 0.0)
            past = (lane >= MOBA_SEL_LANE) & (lane < MOBA_SEL_LANE + qb)
            q_ext = jnp.where(past & (beaten >= float(MOBA_TOPK)), NEG_INF, slope_cols)
        qa_s[qb] = jnp.concatenate([(q_b.astype(F32) * qscale).astype(BF16), q_ext.astype(BF16)], axis=1)

    def scores(qb):
        q_aug = qa_s[qb]
        l_q = l_s.at[qb % l_s.shape[0]]
        mx = None
        for n in range(qb + 1):
            k_aug = jnp.concatenate([k_ref[0, n * bs:(n + 1) * bs, :], kext_ref[n]], axis=1)
            logit = lax.dot_general(q_aug, k_aug, NT_DIMS, preferred_element_type=F32)
            if n == qb:
                logit = jnp.where(causal, logit, NEG_INF)
            l_q[:, n * bs:(n + 1) * bs] = logit
            half = jnp.maximum(logit[:, :LANE], logit[:, LANE:])
            mx = half if mx is None else jnp.maximum(mx, half)
        return jnp.max(mx, axis=1, keepdims=True)

    def outputs(qb, m_row):
        l_q = l_s.at[qb % l_s.shape[0]]
        acc = jnp.zeros((bs, 2 * MO_DH), F32)
        for n in range(qb + 1):
            pr = jnp.exp2(l_q[:, n * bs:(n + 1) * bs] - m_row).astype(BF16)
            v_aug = jnp.concatenate([v_ref[0, n * bs:(n + 1) * bs, :], ones_col], axis=1)
            acc = acc + jnp.dot(pr, v_aug, preferred_element_type=F32)
        z = z_ref[0, qb * bs:(qb + 1) * bs, :].astype(F32)
        inv = 1.0 / acc[:, MO_DH:MO_DH + 1]
        o_ref[0, qb * bs:(qb + 1) * bs, :] = (acc[:, :MO_DH] * (inv * _silu(z))).astype(o_ref.dtype)

    ahead = l_s.shape[0] - 1
    for qb in range(min(ahead + 1, nb)):
        prepare(qb)
    m_rows = {qb: scores(qb) for qb in range(min(ahead, nb))}
    for qb in range(nb):
        if qb + ahead + 1 < nb:
            prepare(qb + ahead + 1)
        if qb + ahead < nb:
            m_rows[qb + ahead] = scores(qb + ahead)
        outputs(qb, m_rows.pop(qb))


def _moba_branch(p, srows, kext):
    b, t, _ = p.shape
    nb = t // MOBA_BLOCK
    qb, kb, vb, zb = (c // MO_DH for c in (C_MO_Q, C_MO_K, C_MO_V, C_MO_Z))
    seq = lambda base: pl.BlockSpec((1, t, MO_DH), lambda bi, h: (bi, 0, base + h))
    return pl.pallas_call(
        _moba_kernel,
        grid=(b, MO_HEADS),
        in_specs=[seq(qb), seq(kb), seq(vb), seq(zb),
                  pl.BlockSpec((1, 1, MO_DH), lambda bi, h: (h, 0, 0)),
                  pl.BlockSpec((nb, MOBA_BLOCK, MO_DH), lambda bi, h: (0, 0, 0))],
        out_specs=pl.BlockSpec((1, t, MO_DH), lambda bi, h: (bi, 0, h)),
        out_shape=jax.ShapeDtypeStruct((b, t, MO_W), BF16),
        scratch_shapes=[pltpu.VMEM((MOBA_AHEAD + 1, MOBA_BLOCK, t), F32),
                        pltpu.VMEM((nb, MOBA_BLOCK, 2 * MO_DH), BF16)],
        compiler_params=pltpu.CompilerParams(
            dimension_semantics=("arbitrary", "arbitrary"), vmem_limit_bytes=_vmem_limit(32 << 20)),
        name="moba",
    )(p, p, p, p, srows, kext)


def _stage_weights(w_hbm, layer, dst_ref, stage_ref, sem):
    tile_rows = stage_ref.shape[1]
    n_tiles = dst_ref.shape[0] // tile_rows

    def copy(k):
        return pltpu.make_async_copy(w_hbm.at[layer, pl.ds(k * tile_rows, tile_rows), :], stage_ref.at[k % 2],
                                     sem.at[k % 2])

    copy(0).start()
    for k in range(n_tiles):
        if k + 1 < n_tiles:
            copy(k + 1).start()
        copy(k).wait()
        dst_ref[k * tile_rows:(k + 1) * tile_rows, :] = stage_ref[k % 2].astype(BF16)


def _mem_kv_kernel(mem_ref, g_ref, w_hbm, o_ref, w_s, stage_s, sem, *, layer):
    @pl.when(pl.program_id(0) == 0)
    def _():
        _stage_weights(w_hbm, layer, w_s, stage_s, sem)

    hm = _rmsnorm_rows(mem_ref[0], g_ref[...]).astype(BF16)
    o_ref[0] = jnp.dot(hm, w_s[...], preferred_element_type=F32).astype(o_ref.dtype)


def _mem_kv(mem, g, wkv, l):
    b, m, _ = mem.shape
    return pl.pallas_call(
        functools.partial(_mem_kv_kernel, layer=l),
        grid=(b,),
        in_specs=[pl.BlockSpec((1, m, D_MODEL), lambda bi: (bi, 0, 0)),
                  pl.BlockSpec((None, 1, D_MODEL), lambda bi: (l, 0, R_MEM_G // D_MODEL)),
                  pl.BlockSpec(memory_space=pl.ANY)],
        out_specs=pl.BlockSpec((1, m, 2 * XA_W), lambda bi: (bi, 0, 0)),
        out_shape=jax.ShapeDtypeStruct((b, m, 2 * XA_W), BF16),
        scratch_shapes=[pltpu.VMEM((D_MODEL, 2 * XA_W), BF16),
                        pltpu.VMEM((2, STAGE_ROWS, 2 * XA_W), F32),
                        pltpu.SemaphoreType.DMA((2,))],
        compiler_params=pltpu.CompilerParams(
            dimension_semantics=("arbitrary",), vmem_limit_bytes=_vmem_limit(32 << 20)),
        name="mem_kv",
    )(mem, g, wkv)


def _post_kernel(x_ref, yl_ref, ym_ref, yo_ref, wout_hbm, g_ref, wq_hbm, kv_ref, wo_hbm, fg_ref, o_ref,
                 x1_s, wout_ref, wq_ref, wo_ref, stage_s, stage_q_s, sem, sem_q, *, final_norm, layer):
    @pl.when((pl.program_id(0) == 0) & (pl.program_id(1) == 0))
    def _():
        _stage_weights(wout_hbm, layer, wout_ref, stage_s, sem)
        _stage_weights(wo_hbm, layer, wo_ref, stage_s, sem)
        _stage_weights(wq_hbm, layer, wq_ref, stage_q_s, sem_q)

    sub = x_ref.shape[1] // POST_SUB
    groups = [pl.ds(i * sub, sub) for i in range(POST_SUB)]

    for r in groups:
        acc = jnp.dot(yl_ref[0, r, :], wout_ref[0:LRU_W, :], preferred_element_type=F32)
        acc = acc + jnp.dot(ym_ref[0, r, :], wout_ref[LRU_W:LRU_W + ML_W, :], preferred_element_type=F32)
        acc = acc + jnp.dot(yo_ref[0, r, :], wout_ref[LRU_W + ML_W:, :], preferred_element_type=F32)
        x1_s[r, :] = x_ref[0, r, :] + acc
    qs = []
    for r in groups:
        hx = _rmsnorm_rows(x1_s[r, :], g_ref[...]).astype(BF16)
        qs.append(jnp.dot(hx, wq_ref[...], preferred_element_type=F32).astype(BF16))
    scores = []
    for q in qs:
        scores.append([
            lax.dot_general(q[:, h * XA_DH:(h + 1) * XA_DH], kv_ref[0, :, h * XA_DH:(h + 1) * XA_DH], NT_DIMS,
                            preferred_element_type=F32) * (XA_DH ** -0.5)
            for h in range(XA_HEADS)])
    outs = []
    for sc in scores:
        heads = []
        for h, s in enumerate(sc):
            e = jnp.exp(s - jnp.max(s, axis=1, keepdims=True))
            pr = e / jnp.sum(e, axis=1, keepdims=True)
            v_h = kv_ref[0, :, XA_W + h * XA_DH:XA_W + (h + 1) * XA_DH]
            heads.append(jnp.dot(pr.astype(BF16), v_h, preferred_element_type=F32).astype(BF16))
        outs.append(jnp.concatenate(heads, axis=1))
    for r, o in zip(groups, outs):
        y = x1_s[r, :] + jnp.dot(o, wo_ref[...], preferred_element_type=F32)
        if final_norm:
            y = _rmsnorm_rows(y, fg_ref[...])
        o_ref[0, r, :] = y


def _post_mix(x, yl, ym, yo, wout, g, wq, kv, wo, fg, l, final_norm):
    b, t, _ = x.shape
    m = kv.shape[1]
    rows = lambda width: pl.BlockSpec((1, POST_TM, width), lambda bi, i: (bi, i, 0))
    hbm = pl.BlockSpec(memory_space=pl.ANY)
    weights = (D_MODEL * D_MODEL + D_MODEL * XA_W + XA_W * D_MODEL) * 2
    stages = 2 * STAGE_ROWS * (D_MODEL + XA_W) * 4
    tiles = 2 * POST_TM * (2 * D_MODEL * 4 + D_MODEL * 2) + 2 * m * 2 * XA_W * 2
    vmem = weights + stages + tiles + 6 * POST_TM * D_MODEL * 4
    return pl.pallas_call(
        functools.partial(_post_kernel, final_norm=final_norm, layer=l),
        grid=(b, t // POST_TM),
        in_specs=[rows(D_MODEL), rows(LRU_W), rows(ML_W), rows(MO_W),
                  hbm,
                  pl.BlockSpec((None, 1, D_MODEL), lambda bi, i: (l, 0, R_XA_G // D_MODEL)),
                  hbm,
                  pl.BlockSpec((1, m, 2 * XA_W), lambda bi, i: (bi, 0, 0)),
                  hbm,
                  pl.BlockSpec((1, D_MODEL), lambda bi, i: (0, 0))],
        out_specs=rows(D_MODEL),
        out_shape=jax.ShapeDtypeStruct((b, t, D_MODEL), F32),
        scratch_shapes=[pltpu.VMEM((POST_TM, D_MODEL), F32),
                        pltpu.VMEM((D_MODEL, D_MODEL), BF16),
                        pltpu.VMEM((D_MODEL, XA_W), BF16),
                        pltpu.VMEM((XA_W, D_MODEL), BF16),
                        pltpu.VMEM((2, STAGE_ROWS, D_MODEL), F32),
                        pltpu.VMEM((2, STAGE_ROWS, XA_W), F32),
                        pltpu.SemaphoreType.DMA((2,)),
                        pltpu.SemaphoreType.DMA((2,))],
        compiler_params=pltpu.CompilerParams(
            dimension_semantics=("arbitrary", "arbitrary"), vmem_limit_bytes=_vmem_limit(vmem)),
        name="post_mix",
    )(x, yl, ym, yo, wout, g, wq, kv, wo, fg)


def _alibi_slopes(n):
    def pow2(m):
        start = 2.0 ** (-8.0 / m)
        return [start ** (i + 1) for i in range(m)]
    if math.log2(n).is_integer():
        s = pow2(n)
    else:
        c = 2 ** int(math.floor(math.log2(n)))
        s = pow2(c) + pow2(2 * c)[0::2][:n - c]
    return np.asarray(s, dtype=np.float32)


def _taps(w, perm):
    depth = w.shape[0]
    return jnp.transpose(w, perm).reshape(depth, ML_QKV_BLOCK, ML_W)


def kernel(x, mem, mix_norm_g, w_in, lru_conv_w, lru_conv_b, lru_wa, lru_ba, lru_wx, lru_bx, lru_lambda,
           ml_conv_w, ml_conv_b, ml_wq, ml_wk, ml_wv, ml_bi, ml_bf, ml_norm_g, w_out, xa_norm_g, mem_norm_g,
           xa_wq, xa_wkv, xa_wo, final_norm_g):
    b, t, d = x.shape
    depth = w_in.shape[0]
    nc = t // ML_CHUNK
    moba_srows = _moba_slope_rows(_alibi_slopes(MO_HEADS))
    moba_kext = _moba_key_ext(t // MOBA_BLOCK)
    w_in_t = jnp.swapaxes(w_in, 1, 2)
    lru_wax = jnp.concatenate([lru_wa, lru_wx], axis=-1).astype(BF16)
    ml_wq_t = _taps(ml_wq, (0, 2, 1, 3))
    ml_wkt_t = _taps(ml_wk, (0, 3, 1, 2))
    ml_wv_t = _taps(ml_wv, (0, 2, 1, 3))
    ml_gbias = jnp.broadcast_to(jnp.concatenate([ml_bi, ml_bf], axis=1)[:, :, None, None],
                                (depth, 2 * ML_HEADS, 1, ML_CHUNK))
    rows = jnp.concatenate([mix_norm_g, xa_norm_g, mem_norm_g, ml_conv_b, ml_norm_g,
                            lru_conv_b, lru_ba, lru_bx, lru_lambda], axis=1)[:, None, :]
    for l in range(depth):
        p, gates = _in_proj(x.reshape(b * t, d), rows, w_in_t, l)
        p, gates = p.reshape(b, t, IN_COLS_P), gates.reshape(2 * ML_HEADS, b, nc, ML_CHUNK)
        y_lru = _lru_branch(p, lru_conv_w, lru_wax, rows, l)
        y_ml = _mlstm_branch(p, gates, ml_gbias, ml_conv_w, ml_wq_t, ml_wkt_t, ml_wv_t, rows, l)
        y_mo = _moba_branch(p, moba_srows, moba_kext)
        kv = _mem_kv(mem, rows, xa_wkv, l)
        x = _post_mix(x, y_lru, y_ml, y_mo, w_out, rows, xa_wq, kv, xa_wo, final_norm_g[None, :], l,
                      final_norm=(l == depth - 1))
    return x
```

```python
import functools
import math

import jax
import jax.numpy as jnp
import numpy as np
from jax import lax
from jax.experimental import pallas as pl
from jax.experimental.pallas import tpu as pltpu

LANE = 128
SUBLANE = 8
V7X_VMEM_BYTES = 64 * 1024 * 1024

D_MODEL = 2048
LRU_W = 512
LRU_BLOCKS = 4
LRU_BW = LRU_W // LRU_BLOCKS
LRU_C = 8.0
ML_W = 768
ML_HEADS = 4
ML_DH = 192
ML_DHP = 256
ML_PAIRS = 2
ML_PW = 2 * ML_DH
ML_QKV_BLOCK = 4
ML_QKV_SHIFT = 2
ML_CHUNK = 128
ML_GROUP = 4
MO_W = 768
MO_HEADS = 6
MO_DH = 128
MOBA_BLOCK = 256
MOBA_TOPK = 3
XA_HEADS = 4
XA_DH = 128
XA_W = XA_HEADS * XA_DH
RMS_EPS = 1e-6
LN_EPS = 1e-5
NEG_INF = -1e30
LOG2E = 1.4426950408889634

REF_LRU_END = 2 * LRU_W
REF_ML_END = REF_LRU_END + 3 * ML_W
REF_GATE_END = REF_ML_END + 2 * ML_HEADS

C_ML_U = 0
C_ML_O = C_ML_U + ML_W
C_ML_Z = C_ML_O + ML_W
C_LRU_X = C_ML_Z + ML_W
C_LRU_Z = C_LRU_X + LRU_W
C_MO_Q = C_LRU_Z + LRU_W
C_MO_K = C_MO_Q + MO_W
C_MO_V = C_MO_K + MO_W
C_MO_Z = C_MO_V + MO_W
C_GATE = C_MO_Z + MO_W
R_MIX_G = 0
R_XA_G = R_MIX_G + D_MODEL
R_MEM_G = R_XA_G + D_MODEL
R_ML_CB = R_MEM_G + D_MODEL
R_ML_NG = R_ML_CB + ML_W
R_LRU_CB = R_ML_NG + ML_W
R_LRU_BA = R_LRU_CB + LRU_W
R_LRU_BX = R_LRU_BA + LRU_W
R_LRU_LAM = R_LRU_BX + LRU_W
IN_TM = 512
IN_TN = 3328
IN_COLS_P = 6656
PREP_TR = 256
STAGE_ROWS = 256
POST_TM = 512
POST_SUB = 2

BF16 = jnp.bfloat16
F32 = jnp.float32
NT_DIMS = (((1,), (1,)), ((), ()))


def _vmem_limit(nbytes):
    return int(min(V7X_VMEM_BYTES - (4 << 20), max(32 << 20, nbytes)))


def _rmsnorm_rows(x, g):
    ms = jnp.mean(x * x, axis=-1, keepdims=True)
    return x * lax.rsqrt(ms + RMS_EPS) * g


def _sigmoid(x):
    return jax.nn.sigmoid(x)


def _silu(x):
    return x * jax.nn.sigmoid(x)


def _softplus(x):
    return jnp.maximum(x, 0.0) + jnp.log1p(jnp.exp(-jnp.abs(x)))


def _shift_rows(x, s):
    rolled = pltpu.roll(x, s, axis=0)
    row = lax.broadcasted_iota(jnp.int32, x.shape, 0)
    return jnp.where(row >= s, rolled, 0.0)


def _causal_conv(x, w_ref, b_ref):
    k = w_ref.shape[0]
    acc = x * w_ref[k - 1:k, :]
    head = x[:SUBLANE]
    acc_head = head * w_ref[k - 1:k, :]
    for j in range(k - 1):
        acc = acc + pltpu.roll(x, k - 1 - j, axis=0) * w_ref[j:j + 1, :]
        acc_head = acc_head + _shift_rows(head, k - 1 - j) * w_ref[j:j + 1, :]
    return jnp.concatenate([acc_head, acc[SUBLANE:]], axis=0) + b_ref[...]


def _w_in_src_rows():
    n_ml = (C_LRU_X - C_ML_U) // PREP_TR
    n_lru = (C_MO_Q - C_LRU_X) // PREP_TR
    n_mo = (C_GATE - C_MO_Q) // PREP_TR
    return ([REF_LRU_END + i * PREP_TR for i in range(n_ml)] + [i * PREP_TR for i in range(n_lru)]
            + [REF_GATE_END + i * PREP_TR for i in range(n_mo)] + [REF_ML_END])


def _in_proj_kernel(x_ref, g_ref, w_hbm, o_ref, gt_ref, w_s, stage_s, sem, *, layer):
    @pl.when(pl.program_id(0) == 0)
    def _():
        src = _w_in_src_rows()

        def copy(k):
            return pltpu.make_async_copy(w_hbm.at[layer, pl.ds(src[k], PREP_TR), :], stage_s.at[k % 2], sem.at[k % 2])

        copy(0).start()
        for k in range(len(src)):
            if k + 1 < len(src):
                copy(k + 1).start()
            copy(k).wait()
            tile = stage_s[k % 2]
            if k == len(src) - 1:
                row = lax.broadcasted_iota(jnp.int32, tile.shape, 0)
                tile = jnp.where(row < 2 * ML_HEADS, tile, 0.0)
            w_s[k * PREP_TR:(k + 1) * PREP_TR, :] = tile.astype(BF16)

    xn = _rmsnorm_rows(x_ref[...], g_ref[...]).astype(BF16)
    n_chunks = w_s.shape[0] // IN_TN
    for j in range(n_chunks):
        cols = pl.ds(j * IN_TN, IN_TN)
        res = lax.dot_general(xn, w_s[cols, :], NT_DIMS, preferred_element_type=F32)
        o_ref[:, cols] = res.astype(o_ref.dtype)
        if j == C_GATE // IN_TN:
            g0 = C_GATE - j * IN_TN
            gt_ref[...] = res[:, g0:g0 + LANE].T[:2 * ML_HEADS, :]


def _in_proj(x2d, g, w_in_t, l):
    m = x2d.shape[0]
    n = IN_COLS_P
    vmem = (2 * IN_TM * D_MODEL * 4 + 2 * IN_TM * D_MODEL * 2 + D_MODEL * n * 2 + 2 * PREP_TR * D_MODEL * 4
            + 2 * IN_TM * n * 2 + 2 * IN_TM * IN_TN * 4)
    return pl.pallas_call(
        functools.partial(_in_proj_kernel, layer=l),
        grid=(m // IN_TM,),
        in_specs=[
            pl.BlockSpec((IN_TM, D_MODEL), lambda i: (i, 0)),
            pl.BlockSpec((None, 1, D_MODEL), lambda i: (l, 0, R_MIX_G // D_MODEL)),
            pl.BlockSpec(memory_space=pl.ANY),
        ],
        out_specs=[pl.BlockSpec((IN_TM, n), lambda i: (i, 0)),
                   pl.BlockSpec((2 * ML_HEADS, IN_TM), lambda i: (0, i))],
        out_shape=[jax.ShapeDtypeStruct((m, n), BF16), jax.ShapeDtypeStruct((2 * ML_HEADS, m), F32)],
        scratch_shapes=[pltpu.VMEM((n, D_MODEL), BF16),
                        pltpu.VMEM((2, PREP_TR, D_MODEL), F32),
                        pltpu.SemaphoreType.DMA((2,))],
        compiler_params=pltpu.CompilerParams(
            dimension_semantics=("arbitrary",), vmem_limit_bytes=_vmem_limit(vmem)),
        name="in_proj",
    )(x2d, g, w_in_t)


def _lru_kernel(x_ref, z_ref, cw_ref, cb_ref, wax_ref, ba_ref, bx_ref, lam_ref, o_ref, a_s, u_s):
    t = x_ref.shape[1]
    x = x_ref[0].astype(F32)
    xc = _causal_conv(x, cw_ref, cb_ref)
    pre = jnp.dot(xc.astype(BF16), wax_ref[0], preferred_element_type=F32)
    r = _sigmoid(pre[:, :LRU_BW] + ba_ref[...])
    i = _sigmoid(pre[:, LRU_BW:] + bx_ref[...])
    log_a = (-LRU_C) * r * _softplus(-lam_ref[...])
    a = jnp.exp(log_a)
    a_s[...] = a
    u_s[...] = jnp.sqrt(-jnp.tanh(log_a) * (1.0 + a * a)) * (i * xc)

    row = lax.broadcasted_iota(jnp.int32, (SUBLANE, LRU_BW), 0)

    def block(blk, h_prev):
        r0 = pl.multiple_of(blk * SUBLANE, SUBLANE)
        a_b = a_s[pl.ds(r0, SUBLANE), :]
        u_b = u_s[pl.ds(r0, SUBLANE), :]
        for s in (1, 2, 4):
            a_sh = jnp.where(row >= s, pltpu.roll(a_b, s, axis=0), 1.0)
            u_sh = jnp.where(row >= s, pltpu.roll(u_b, s, axis=0), 0.0)
            u_b = a_b * u_sh + u_b
            a_b = a_b * a_sh
        u_s[pl.ds(r0, SUBLANE), :] = a_b * h_prev + u_b
        last = lambda v: jnp.broadcast_to(v[SUBLANE - 1:SUBLANE, :], (SUBLANE, LRU_BW))
        return last(a_b) * h_prev + last(u_b)

    lax.fori_loop(0, t // SUBLANE, block, jnp.zeros((SUBLANE, LRU_BW), F32), unroll=4)
    z = z_ref[0].astype(F32)
    o_ref[0] = (u_s[...] * _silu(z)).astype(o_ref.dtype)


def _lru_branch(p, cw, wax, rows, l):
    b, t, _ = p.shape
    xb, zb = C_LRU_X // LRU_BW, C_LRU_Z // LRU_BW
    vec = lambda off: pl.BlockSpec((None, 1, LRU_BW), lambda bi, g: (l, 0, off // LRU_BW + g))
    return pl.pallas_call(
        _lru_kernel,
        grid=(b, LRU_BLOCKS),
        in_specs=[
            pl.BlockSpec((1, t, LRU_BW), lambda bi, g: (bi, 0, xb + g)),
            pl.BlockSpec((1, t, LRU_BW), lambda bi, g: (bi, 0, zb + g)),
            pl.BlockSpec((None, cw.shape[1], LRU_BW), lambda bi, g: (l, 0, g)),
            vec(R_LRU_CB),
            pl.BlockSpec((None, 1, LRU_BW, 2 * LRU_BW), lambda bi, g: (l, g, 0, 0)),
            vec(R_LRU_BA), vec(R_LRU_BX), vec(R_LRU_LAM),
        ],
        out_specs=pl.BlockSpec((1, t, LRU_BW), lambda bi, g: (bi, 0, g)),
        out_shape=jax.ShapeDtypeStruct((b, t, LRU_W), BF16),
        scratch_shapes=[pltpu.VMEM((t, LRU_BW), F32), pltpu.VMEM((t, LRU_BW), F32)],
        compiler_params=pltpu.CompilerParams(
            dimension_semantics=("arbitrary", "arbitrary"), vmem_limit_bytes=_vmem_limit(24 * t * LRU_BW * 4)),
        name="rg_lru",
    )(p, p, cw, rows, wax, rows, rows, rows)


def _blockdiag_in_out(w_ref, hh):
    wh = w_ref[...][:, hh * ML_DH:(hh + 1) * ML_DH]
    wh = jnp.concatenate([wh, jnp.zeros((ML_QKV_BLOCK, ML_DHP - ML_DH), F32)], axis=1)
    r = lax.broadcasted_iota(jnp.int32, (ML_PW, ML_DHP), 0)
    c = lax.broadcasted_iota(jnp.int32, (ML_PW, ML_DHP), 1)
    d = jnp.zeros((ML_PW, ML_DHP), F32)
    for i in range(ML_QKV_BLOCK):
        d = jnp.where((r & (ML_QKV_BLOCK - 1)) == i, wh[i:i + 1, :], d)
    keep = (((r >> ML_QKV_SHIFT) - hh * (ML_DH // ML_QKV_BLOCK)) == (c >> ML_QKV_SHIFT)) & (c < ML_DH)
    return jnp.where(keep, d, 0.0)


def _blockdiag_in_window(w_ref, hh):
    w0 = ML_WINDOW[hh]
    wh = w_ref[...][:, w0:w0 + ML_DHP]
    r = lax.broadcasted_iota(jnp.int32, (ML_DHP, ML_DHP), 0) + w0
    c = lax.broadcasted_iota(jnp.int32, (ML_DHP, ML_DHP), 1) + w0
    d = jnp.zeros((ML_DHP, ML_DHP), F32)
    for i in range(ML_QKV_BLOCK):
        d = jnp.where((r & (ML_QKV_BLOCK - 1)) == i, wh[i:i + 1, :], d)
    keep = ((r >> ML_QKV_SHIFT) == (c >> ML_QKV_SHIFT)) & (c >= hh * ML_DH) & (c < (hh + 1) * ML_DH)
    return jnp.where(keep, d, 0.0)


def _blockdiag_out_in(w_ref, hh):
    wk = w_ref[...]
    o = lax.broadcasted_iota(jnp.int32, (ML_DHP, ML_PW), 0)
    r = lax.broadcasted_iota(jnp.int32, (ML_DHP, ML_PW), 1)
    d = jnp.zeros((ML_DHP, ML_PW), F32)
    for j in range(ML_QKV_BLOCK):
        d = jnp.where((o & (ML_QKV_BLOCK - 1)) == j, wk[j:j + 1, :], d)
    keep = ((o >> ML_QKV_SHIFT) == ((r >> ML_QKV_SHIFT) - hh * (ML_DH // ML_QKV_BLOCK))) & (o < ML_DH)
    return jnp.where(keep, d, 0.0)


ML_WINDOW = (0, ML_PW - ML_DHP)
ML_DEN_LANE = (ML_DH + 2, 2)


def _mlstm_kernel(u_ref, og_ref, z_ref, ig_ref, fg_ref, bi_ref, bf_ref, cw_ref, cb_ref,
                  wq_ref, wkt_ref, wv_ref, ng_ref, y_ref,
                  dq_s, dkt_s, dv_s, q_s, kt_s, v_s, cp_s, c_s, r_s, w_s, col_s, so_s, sn_s, loca_s, locb_s):
    t = u_ref.shape[1]
    nc = t // ML_CHUNK
    L = ML_CHUNK
    heads = range(2)

    @pl.when(pl.program_id(1) == 0)
    def _():
        for hh in heads:
            dq_s[hh] = _blockdiag_in_out(wq_ref, hh).astype(BF16)
            dkt_s[hh] = _blockdiag_out_in(wkt_ref, hh).astype(BF16)
            dv_s[hh] = _blockdiag_in_window(wv_ref, hh).astype(BF16)

    u = u_ref[0]
    uc = _silu(_causal_conv(u.astype(F32), cw_ref, cb_ref)).astype(BF16)
    for hh in heads:
        w0 = ML_WINDOW[hh]
        uc_w, u_w = uc[:, w0:w0 + ML_DHP], u[:, w0:w0 + ML_DHP]
        q_s[hh] = jnp.dot(uc_w, dq_s[hh, w0:w0 + ML_DHP, :], preferred_element_type=F32).astype(BF16)
        kt = lax.dot_general(dkt_s[hh, :, w0:w0 + ML_DHP], uc_w, NT_DIMS, preferred_element_type=F32)
        kt = kt * (ML_DH ** -0.5)
        for c in range(nc):
            kt_s[hh, c] = kt[:, c * L:(c + 1) * L].astype(BF16)
        v = jnp.dot(u_w, dv_s[hh], preferred_element_type=F32)
        vlane = lax.broadcasted_iota(jnp.int32, v.shape, 1)
        v_s[hh] = jnp.where(vlane == ML_DEN_LANE[hh], 1.0, v).astype(BF16)

    glane = lax.broadcasted_iota(jnp.int32, (nc, L), 1)
    grow = lax.broadcasted_iota(jnp.int32, (nc, L), 0)
    row8 = lax.broadcasted_iota(jnp.int32, (SUBLANE, L), 0)
    for hh in heads:
        ig = ig_ref[hh, 0] + bi_ref[hh]
        lf = -_softplus(-(fg_ref[hh, 0] + bf_ref[hh]))
        b = lf
        for k in range(int(math.log2(L))):
            sh = 1 << k
            b = b + jnp.where(glane >= sh, pltpu.roll(b, sh, axis=1), 0.0)
        g = jnp.broadcast_to(b[:, L - 1:L], (nc, L))
        a = g - b + ig
        mloc = jnp.broadcast_to(jnp.max(a, axis=1, keepdims=True), (nc, L))
        m = jnp.zeros((1, L), F32)
        m_prev = jnp.zeros((nc, L), F32)
        m_next = jnp.zeros((nc, L), F32)
        for c in range(nc):
            m_prev = jnp.where(grow == c, m, m_prev)
            m = jnp.maximum(g[c:c + 1, :] + m, mloc[c:c + 1, :])
            m_next = jnp.where(grow == c, m, m_next)
        r = ig - b
        cmx = r
        for k in range(int(math.log2(L))):
            sh = 1 << k
            cmx = jnp.maximum(cmx, jnp.where(glane >= sh, pltpu.roll(cmx, sh, axis=1), -jnp.inf))
        mm = jnp.maximum(m_prev, cmx)
        s_int = jnp.exp(m_prev - mm)
        clamp = jnp.exp(-(b + mm))
        r_s[hh] = r
        w_s[hh] = jnp.exp(a - mloc)
        so_s[hh] = jnp.exp(g + m_prev - m_next)
        sn_s[hh] = jnp.exp(mloc - m_next)
        for c in range(nc):
            col_s[hh, c] = jnp.where(row8 == 0, mm[c:c + 1, :],
                                     jnp.where(row8 == 1, s_int[c:c + 1, :],
                                               jnp.where(row8 == 2, clamp[c:c + 1, :], 0.0)))

    zero_rows = jnp.zeros((ML_DHP - ML_DH, ML_DHP), BF16)
    c_s[...] = jnp.zeros(c_s.shape, F32)

    n_pairs = nc // 2

    def local_states(pair, loc_ref):
        for i in range(2):
            c = 2 * pair + i
            r0 = pl.multiple_of(c * L, L)
            for hh in heads:
                ktw = (kt_s[hh, c, :ML_DH, :].astype(F32) * w_s[hh, pl.ds(c, 1), :]).astype(BF16)
                loc_ref[i, hh] = jnp.dot(ktw, v_s[hh, pl.ds(r0, L), :], preferred_element_type=F32)

    def advance(pair, loc_ref):
        for i in range(2):
            c = 2 * pair + i
            for hh in heads:
                c_prev = c_s[hh]
                cp_s[hh, c] = jnp.concatenate([c_prev.astype(BF16), zero_rows], axis=0)
                c_s[hh] = (so_s[hh, pl.ds(c, 1), :][:, :1] * c_prev
                           + sn_s[hh, pl.ds(c, 1), :][:, :1] * loc_ref[i, hh])

    def state(k, carry):
        local_states(2 * k + 1, locb_s)
        advance(2 * k, loca_s)
        local_states(jnp.minimum(2 * k + 2, n_pairs - 1), loca_s)
        advance(2 * k + 1, locb_s)
        return carry

    local_states(0, loca_s)
    lax.fori_loop(0, n_pairs // 2, state, 0)

    tri = (lax.broadcasted_iota(jnp.int32, (L, L), 0) >= lax.broadcasted_iota(jnp.int32, (L, L), 1))
    first = lax.broadcasted_iota(jnp.int32, (L, ML_PW), 1) < ML_DH

    def chunk_rows(c):
        return pl.ds(pl.multiple_of(c * L, L), L)

    def raw_scores(c):
        rows = chunk_rows(c)
        return [jnp.dot(q_s[hh, rows, :], kt_s[hh, c], preferred_element_type=F32) for hh in heads]

    def weights(c, raw):
        out = []
        for hh in heads:
            c0 = ML_DEN_LANE[hh] % LANE - 2
            pads = [jnp.zeros((n, L), F32) for n in (c0, L - SUBLANE - c0)]
            cols = jnp.concatenate([a for a in (pads[0], col_s[hh, c], pads[1]) if a.shape[0]], axis=0).T
            rb = jnp.broadcast_to(r_s[hh, pl.ds(c, 1), :], (L, L))
            decay = jnp.exp(jnp.where(tri, rb - cols[:, c0:c0 + 1], -jnp.inf))
            out.append(((raw[hh] * decay).astype(BF16), cols[:, c0 + 1:c0 + 2], cols[:, c0 + 2:c0 + 3]))
        return out

    def numerators(c, wts):
        rows = chunk_rows(c)
        nds, invs = [], []
        for hh, (s_mat, s_int, clamp) in zip(heads, wts):
            nd = (jnp.dot(s_mat, v_s[hh, rows, :], preferred_element_type=F32)
                  + s_int * jnp.dot(q_s[hh, rows, :], cp_s[hh, c], preferred_element_type=F32))
            den = nd[:, ML_DEN_LANE[hh]:ML_DEN_LANE[hh] + 1]
            nds.append(nd)
            invs.append(1.0 / jnp.maximum(jnp.abs(den), clamp))
        w1 = ML_WINDOW[1]
        shared_first = lax.broadcasted_iota(jnp.int32, (L, ML_DHP - w1), 1) < ML_DH - w1
        shared = jnp.where(shared_first, nds[0][:, w1:], nds[1][:, :ML_DHP - w1])
        return jnp.concatenate([nds[0][:, :w1], shared, nds[1][:, ML_DHP - w1:]], axis=1), invs

    def gates(c):
        rows = chunk_rows(c)
        return (_sigmoid(og_ref[0, rows, :].astype(F32)),
                _silu(z_ref[0, rows, :].astype(F32)) * ng_ref[...])

    def means(nd, og):
        x = og * nd
        mu0 = jnp.sum(jnp.where(first, x, 0.0), axis=1, keepdims=True) * (1.0 / ML_DH)
        mu1 = jnp.sum(jnp.where(first, 0.0, x), axis=1, keepdims=True) * (1.0 / ML_DH)
        return x, mu0, mu1

    def variances(x, mu0, mu1):
        dev = x - jnp.where(first, mu0, mu1)
        sq = dev * dev
        var0 = jnp.sum(jnp.where(first, sq, 0.0), axis=1, keepdims=True) * (1.0 / ML_DH)
        var1 = jnp.sum(jnp.where(first, 0.0, sq), axis=1, keepdims=True) * (1.0 / ML_DH)
        return dev, var0, var1

    def finish(c, dev, var0, var1, invs, zs):
        f0 = invs[0] * lax.rsqrt(invs[0] * invs[0] * var0 + LN_EPS)
        f1 = invs[1] * lax.rsqrt(invs[1] * invs[1] * var1 + LN_EPS)
        y_ref[0, chunk_rows(c), :] = (dev * jnp.where(first, f0, f1) * zs).astype(y_ref.dtype)

    def group(gi, carry):
        cs = [gi * ML_GROUP + i for i in range(ML_GROUP)]
        n = range(ML_GROUP)
        raws = [raw_scores(c) for c in cs]
        gts = [gates(c) for c in cs]
        wts = [weights(c, raws[i]) for i, c in enumerate(cs)]
        nums = [numerators(c, wts[i]) for i, c in enumerate(cs)]
        mus = [means(nums[i][0], gts[i][0]) for i in n]
        vrs = [variances(*mus[i]) for i in n]
        for i, c in enumerate(cs):
            finish(c, *vrs[i], nums[i][1], gts[i][1])
        return carry

    lax.fori_loop(0, nc // ML_GROUP, group, 0)


def _mlstm_branch(p, gates, gbias, cw, wq, wkt, wv, rows, l):
    b, t, _ = p.shape
    nc = t // ML_CHUNK
    ub, ob, zb = C_ML_U // ML_PW, C_ML_O // ML_PW, C_ML_Z // ML_PW
    seq = lambda base: pl.BlockSpec((1, t, ML_PW), lambda pr, bi: (bi, 0, base + pr))
    vec = lambda off: pl.BlockSpec((None, 1, ML_PW), lambda pr, bi: (l, 0, off // ML_PW + pr))
    taps = pl.BlockSpec((None, ML_QKV_BLOCK, ML_PW), lambda pr, bi: (l, 0, pr))
    gate_rows = lambda: pltpu.VMEM((2, nc, ML_CHUNK), F32)
    pair_states = lambda: pltpu.VMEM((2, 2, ML_DH, ML_DHP), F32)
    return pl.pallas_call(
        _mlstm_kernel,
        grid=(ML_PAIRS, b),
        in_specs=[
            seq(ub), seq(ob), seq(zb),
            pl.BlockSpec((2, 1, nc, ML_CHUNK), lambda pr, bi: (pr, bi, 0, 0)),
            pl.BlockSpec((2, 1, nc, ML_CHUNK), lambda pr, bi: (ML_PAIRS + pr, bi, 0, 0)),
            pl.BlockSpec((None, 2, 1, ML_CHUNK), lambda pr, bi: (l, pr, 0, 0)),
            pl.BlockSpec((None, 2, 1, ML_CHUNK), lambda pr, bi: (l, ML_PAIRS + pr, 0, 0)),
            pl.BlockSpec((None, cw.shape[1], ML_PW), lambda pr, bi: (l, 0, pr)),
            vec(R_ML_CB), taps, taps, taps, vec(R_ML_NG),
        ],
        out_specs=pl.BlockSpec((1, t, ML_PW), lambda pr, bi: (bi, 0, pr)),
        out_shape=jax.ShapeDtypeStruct((b, t, ML_W), BF16),
        scratch_shapes=[
            pltpu.VMEM((2, ML_PW, ML_DHP), BF16),
            pltpu.VMEM((2, ML_DHP, ML_PW), BF16),
            pltpu.VMEM((2, ML_DHP, ML_DHP), BF16),
            pltpu.VMEM((2, t, ML_DHP), BF16),
            pltpu.VMEM((2, nc, ML_DHP, ML_CHUNK), BF16),
            pltpu.VMEM((2, t, ML_DHP), BF16),
            pltpu.VMEM((2, nc, ML_DHP, ML_DHP), BF16),
            pltpu.VMEM((2, ML_DH, ML_DHP), F32),
            gate_rows(), gate_rows(),
            pltpu.VMEM((2, nc, SUBLANE, ML_CHUNK), F32),
            gate_rows(), gate_rows(),
            pair_states(), pair_states(),
        ],
        compiler_params=pltpu.CompilerParams(
            dimension_semantics=("arbitrary", "arbitrary"), vmem_limit_bytes=_vmem_limit(56 << 20)),
        name="mlstm",
    )(p, p, p, gates, gates, gbias, gbias, cw, rows, wq, wkt, wv, rows)


MOBA_SEL_LANE = 6
MOBA_AHEAD = 3


def _moba_key_ext(nb):
    ext = np.zeros((nb, MOBA_BLOCK, MO_DH), np.float32)
    for n in range(nb):
        ext[n, :, 0:3] = n * MOBA_BLOCK
        ext[n, :, 3:6] = np.arange(MOBA_BLOCK, dtype=np.float32)[:, None]
        ext[n, :, MOBA_SEL_LANE + n] = 1.0
    return jnp.asarray(ext, dtype=BF16)


def _moba_slope_rows(slopes):
    rows = np.zeros((len(slopes), 1, MO_DH), np.float32)
    for h, s in enumerate(slopes):
        rest = np.float32(np.float32(s) * np.float32(LOG2E))
        for i in range(3):
            piece = np.float32(rest).astype(BF16).astype(np.float32)
            rows[h, 0, i] = rows[h, 0, 3 + i] = piece
            rest = np.float32(rest - piece)
    return jnp.asarray(rows)


def _moba_kernel(q_ref, k_ref, v_ref, z_ref, srow_ref, kext_ref, o_ref, l_s, qa_s):
    t = q_ref.shape[1]
    nb = t // MOBA_BLOCK
    bs = MOBA_BLOCK
    qscale = (MO_DH ** -0.5) * LOG2E

    krow = lax.broadcasted_iota(jnp.int32, (LANE, MO_DH), 0)
    kmean = jnp.zeros((LANE, MO_DH), F32)
    for n in range(nb):
        mean_n = jnp.sum(k_ref[0, n * bs:(n + 1) * bs, :].astype(F32), axis=0, keepdims=True) * (1.0 / bs)
        kmean = jnp.where(krow == MOBA_SEL_LANE + n, mean_n, kmean)
    kmean = kmean.astype(BF16)

    lane = lax.broadcasted_iota(jnp.int32, (bs, MO_DH), 1)
    causal = (lax.broadcasted_iota(jnp.int32, (bs, bs), 0) >= lax.broadcasted_iota(jnp.int32, (bs, bs), 1))
    ones_col = jnp.where(lane == 0, 1.0, 0.0).astype(BF16)
    slope_cols = jnp.where(lane < MOBA_SEL_LANE, srow_ref[0], 0.0)

    def prepare(qb):
        q_b = q_ref[0, qb * bs:(qb + 1) * bs, :]
        q_ext = slope_cols
        if qb > MOBA_TOPK:
            gate = lax.dot_general(q_b, kmean, NT_DIMS, preferred_element_type=F32)
            beaten = jnp.zeros((bs, MO_DH), F32)
            for m in range(qb):
                gm = gate[:, MOBA_SEL_LANE + m:MOBA_SEL_LANE + m + 1]
                wins = (gm > gate) | ((gm == gate) & (lane > MOBA_SEL_LANE + m))
                beaten = beaten + jnp.where(wins, 1.0, 0.0)
            past = (lane >= MOBA_SEL_LANE) & (lane < MOBA_SEL_LANE + qb)
            q_ext = jnp.where(past & (beaten >= float(MOBA_TOPK)), NEG_INF, slope_cols)
        qa_s[qb] = jnp.concatenate([(q_b.astype(F32) * qscale).astype(BF16), q_ext.astype(BF16)], axis=1)

    def scores(qb):
        q_aug = qa_s[qb]
        l_q = l_s.at[qb % l_s.shape[0]]
        mx = None
        for n in range(qb + 1):
            k_aug = jnp.concatenate([k_ref[0, n * bs:(n + 1) * bs, :], kext_ref[n]], axis=1)
            logit = lax.dot_general(q_aug, k_aug, NT_DIMS, preferred_element_type=F32)
            if n == qb:
                logit = jnp.where(causal, logit, NEG_INF)
            l_q[:, n * bs:(n + 1) * bs] = logit
            half = jnp.maximum(logit[:, :LANE], logit[:, LANE:])
            mx = half if mx is None else jnp.maximum(mx, half)
        return jnp.max(mx, axis=1, keepdims=True)

    def outputs(qb, m_row):
        l_q = l_s.at[qb % l_s.shape[0]]
        acc = jnp.zeros((bs, 2 * MO_DH), F32)
        for n in range(qb + 1):
            pr = jnp.exp2(l_q[:, n * bs:(n + 1) * bs] - m_row).astype(BF16)
            v_aug = jnp.concatenate([v_ref[0, n * bs:(n + 1) * bs, :], ones_col], axis=1)
            acc = acc + jnp.dot(pr, v_aug, preferred_element_type=F32)
        z = z_ref[0, qb * bs:(qb + 1) * bs, :].astype(F32)
        inv = 1.0 / acc[:, MO_DH:MO_DH + 1]
        o_ref[0, qb * bs:(qb + 1) * bs, :] = (acc[:, :MO_DH] * (inv * _silu(z))).astype(o_ref.dtype)

    ahead = l_s.shape[0] - 1
    for qb in range(min(ahead + 1, nb)):
        prepare(qb)
    m_rows = {qb: scores(qb) for qb in range(min(ahead, nb))}
    for qb in range(nb):
        if qb + ahead + 1 < nb:
            prepare(qb + ahead + 1)
        if qb + ahead < nb:
            m_rows[qb + ahead] = scores(qb + ahead)
        outputs(qb, m_rows.pop(qb))


def _moba_branch(p, srows, kext):
    b, t, _ = p.shape
    nb = t // MOBA_BLOCK
    qb, kb, vb, zb = (c // MO_DH for c in (C_MO_Q, C_MO_K, C_MO_V, C_MO_Z))
    seq = lambda base: pl.BlockSpec((1, t, MO_DH), lambda bi, h: (bi, 0, base + h))
    return pl.pallas_call(
        _moba_kernel,
        grid=(b, MO_HEADS),
        in_specs=[seq(qb), seq(kb), seq(vb), seq(zb),
                  pl.BlockSpec((1, 1, MO_DH), lambda bi, h: (h, 0, 0)),
                  pl.BlockSpec((nb, MOBA_BLOCK, MO_DH), lambda bi, h: (0, 0, 0))],
        out_specs=pl.BlockSpec((1, t, MO_DH), lambda bi, h: (bi, 0, h)),
        out_shape=jax.ShapeDtypeStruct((b, t, MO_W), BF16),
        scratch_shapes=[pltpu.VMEM((MOBA_AHEAD + 1, MOBA_BLOCK, t), F32),
                        pltpu.VMEM((nb, MOBA_BLOCK, 2 * MO_DH), BF16)],
        compiler_params=pltpu.CompilerParams(
            dimension_semantics=("arbitrary", "arbitrary"), vmem_limit_bytes=_vmem_limit(32 << 20)),
        name="moba",
    )(p, p, p, p, srows, kext)


def _stage_weights(w_hbm, layer, dst_ref, stage_ref, sem):
    tile_rows = stage_ref.shape[1]
    n_tiles = dst_ref.shape[0] // tile_rows

    def copy(k):
        return pltpu.make_async_copy(w_hbm.at[layer, pl.ds(k * tile_rows, tile_rows), :], stage_ref.at[k % 2],
                                     sem.at[k % 2])

    copy(0).start()
    for k in range(n_tiles):
        if k + 1 < n_tiles:
            copy(k + 1).start()
        copy(k).wait()
        dst_ref[k * tile_rows:(k + 1) * tile_rows, :] = stage_ref[k % 2].astype(BF16)


def _mem_kv_kernel(mem_ref, g_ref, w_hbm, o_ref, w_s, stage_s, sem, *, layer):
    @pl.when(pl.program_id(0) == 0)
    def _():
        _stage_weights(w_hbm, layer, w_s, stage_s, sem)

    hm = _rmsnorm_rows(mem_ref[0], g_ref[...]).astype(BF16)
    o_ref[0] = jnp.dot(hm, w_s[...], preferred_element_type=F32).astype(o_ref.dtype)


def _mem_kv(mem, g, wkv, l):
    b, m, _ = mem.shape
    return pl.pallas_call(
        functools.partial(_mem_kv_kernel, layer=l),
        grid=(b,),
        in_specs=[pl.BlockSpec((1, m, D_MODEL), lambda bi: (bi, 0, 0)),
                  pl.BlockSpec((None, 1, D_MODEL), lambda bi: (l, 0, R_MEM_G // D_MODEL)),
                  pl.BlockSpec(memory_space=pl.ANY)],
        out_specs=pl.BlockSpec((1, m, 2 * XA_W), lambda bi: (bi, 0, 0)),
        out_shape=jax.ShapeDtypeStruct((b, m, 2 * XA_W), BF16),
        scratch_shapes=[pltpu.VMEM((D_MODEL, 2 * XA_W), BF16),
                        pltpu.VMEM((2, STAGE_ROWS, 2 * XA_W), F32),
                        pltpu.SemaphoreType.DMA((2,))],
        compiler_params=pltpu.CompilerParams(
            dimension_semantics=("arbitrary",), vmem_limit_bytes=_vmem_limit(32 << 20)),
        name="mem_kv",
    )(mem, g, wkv)


def _post_kernel(x_ref, yl_ref, ym_ref, yo_ref, wout_hbm, g_ref, wq_hbm, kv_ref, wo_hbm, fg_ref, o_ref,
                 x1_s, wout_ref, wq_ref, wo_ref, stage_s, stage_q_s, sem, sem_q, *, final_norm, layer):
    @pl.when((pl.program_id(0) == 0) & (pl.program_id(1) == 0))
    def _():
        _stage_weights(wout_hbm, layer, wout_ref, stage_s, sem)
        _stage_weights(wo_hbm, layer, wo_ref, stage_s, sem)
        _stage_weights(wq_hbm, layer, wq_ref, stage_q_s, sem_q)

    sub = x_ref.shape[1] // POST_SUB
    groups = [pl.ds(i * sub, sub) for i in range(POST_SUB)]

    for r in groups:
        acc = jnp.dot(yl_ref[0, r, :], wout_ref[0:LRU_W, :], preferred_element_type=F32)
        acc = acc + jnp.dot(ym_ref[0, r, :], wout_ref[LRU_W:LRU_W + ML_W, :], preferred_element_type=F32)
        acc = acc + jnp.dot(yo_ref[0, r, :], wout_ref[LRU_W + ML_W:, :], preferred_element_type=F32)
        x1_s[r, :] = x_ref[0, r, :] + acc
    qs = []
    for r in groups:
        hx = _rmsnorm_rows(x1_s[r, :], g_ref[...]).astype(BF16)
        qs.append(jnp.dot(hx, wq_ref[...], preferred_element_type=F32).astype(BF16))
    scores = []
    for q in qs:
        scores.append([
            lax.dot_general(q[:, h * XA_DH:(h + 1) * XA_DH], kv_ref[0, :, h * XA_DH:(h + 1) * XA_DH], NT_DIMS,
                            preferred_element_type=F32) * (XA_DH ** -0.5)
            for h in range(XA_HEADS)])
    outs = []
    for sc in scores:
        heads = []
        for h, s in enumerate(sc):
            e = jnp.exp(s - jnp.max(s, axis=1, keepdims=True))
            pr = e / jnp.sum(e, axis=1, keepdims=True)
            v_h = kv_ref[0, :, XA_W + h * XA_DH:XA_W + (h + 1) * XA_DH]
            heads.append(jnp.dot(pr.astype(BF16), v_h, preferred_element_type=F32).astype(BF16))
        outs.append(jnp.concatenate(heads, axis=1))
    for r, o in zip(groups, outs):
        y = x1_s[r, :] + jnp.dot(o, wo_ref[...], preferred_element_type=F32)
        if final_norm:
            y = _rmsnorm_rows(y, fg_ref[...])
        o_ref[0, r, :] = y


def _post_mix(x, yl, ym, yo, wout, g, wq, kv, wo, fg, l, final_norm):
    b, t, _ = x.shape
    m = kv.shape[1]
    rows = lambda width: pl.BlockSpec((1, POST_TM, width), lambda bi, i: (bi, i, 0))
    hbm = pl.BlockSpec(memory_space=pl.ANY)
    weights = (D_MODEL * D_MODEL + D_MODEL * XA_W + XA_W * D_MODEL) * 2
    stages = 2 * STAGE_ROWS * (D_MODEL + XA_W) * 4
    tiles = 2 * POST_TM * (2 * D_MODEL * 4 + D_MODEL * 2) + 2 * m * 2 * XA_W * 2
    vmem = weights + stages + tiles + 6 * POST_TM * D_MODEL * 4
    return pl.pallas_call(
        functools.partial(_post_kernel, final_norm=final_norm, layer=l),
        grid=(b, t // POST_TM),
        in_specs=[rows(D_MODEL), rows(LRU_W), rows(ML_W), rows(MO_W),
                  hbm,
                  pl.BlockSpec((None, 1, D_MODEL), lambda bi, i: (l, 0, R_XA_G // D_MODEL)),
                  hbm,
                  pl.BlockSpec((1, m, 2 * XA_W), lambda bi, i: (bi, 0, 0)),
                  hbm,
                  pl.BlockSpec((1, D_MODEL), lambda bi, i: (0, 0))],
        out_specs=rows(D_MODEL),
        out_shape=jax.ShapeDtypeStruct((b, t, D_MODEL), F32),
        scratch_shapes=[pltpu.VMEM((POST_TM, D_MODEL), F32),
                        pltpu.VMEM((D_MODEL, D_MODEL), BF16),
                        pltpu.VMEM((D_MODEL, XA_W), BF16),
                        pltpu.VMEM((XA_W, D_MODEL), BF16),
                        pltpu.VMEM((2, STAGE_ROWS, D_MODEL), F32),
                        pltpu.VMEM((2, STAGE_ROWS, XA_W), F32),
                        pltpu.SemaphoreType.DMA((2,)),
                        pltpu.SemaphoreType.DMA((2,))],
        compiler_params=pltpu.CompilerParams(
            dimension_semantics=("arbitrary", "arbitrary"), vmem_limit_bytes=_vmem_limit(vmem)),
        name="post_mix",
    )(x, yl, ym, yo, wout, g, wq, kv, wo, fg)


def _alibi_slopes(n):
    def pow2(m):
        start = 2.0 ** (-8.0 / m)
        return [start ** (i + 1) for i in range(m)]
    if math.log2(n).is_integer():
        s = pow2(n)
    else:
        c = 2 ** int(math.floor(math.log2(n)))
        s = pow2(c) + pow2(2 * c)[0::2][:n - c]
    return np.asarray(s, dtype=np.float32)


def _taps(w, perm):
    depth = w.shape[0]
    return jnp.transpose(w, perm).reshape(depth, ML_QKV_BLOCK, ML_W)


def kernel(x, mem, mix_norm_g, w_in, lru_conv_w, lru_conv_b, lru_wa, lru_ba, lru_wx, lru_bx, lru_lambda,
           ml_conv_w, ml_conv_b, ml_wq, ml_wk, ml_wv, ml_bi, ml_bf, ml_norm_g, w_out, xa_norm_g, mem_norm_g,
           xa_wq, xa_wkv, xa_wo, final_norm_g):
    b, t, d = x.shape
    depth = w_in.shape[0]
    nc = t // ML_CHUNK
    moba_srows = _moba_slope_rows(_alibi_slopes(MO_HEADS))
    moba_kext = _moba_key_ext(t // MOBA_BLOCK)
    w_in_t = jnp.swapaxes(w_in, 1, 2)
    lru_wax = jnp.concatenate([lru_wa, lru_wx], axis=-1).astype(BF16)
    ml_wq_t = _taps(ml_wq, (0, 2, 1, 3))
    ml_wkt_t = _taps(ml_wk, (0, 3, 1, 2))
    ml_wv_t = _taps(ml_wv, (0, 2, 1, 3))
    ml_gbias = jnp.broadcast_to(jnp.concatenate([ml_bi, ml_bf], axis=1)[:, :, None, None],
                                (depth, 2 * ML_HEADS, 1, ML_CHUNK))
    rows = jnp.concatenate([mix_norm_g, xa_norm_g, mem_norm_g, ml_conv_b, ml_norm_g,
                            lru_conv_b, lru_ba, lru_bx, lru_lambda], axis=1)[:, None, :]
    for l in range(depth):
        p, gates = _in_proj(x.reshape(b * t, d), rows, w_in_t, l)
        p, gates = p.reshape(b, t, IN_COLS_P), gates.reshape(2 * ML_HEADS, b, nc, ML_CHUNK)
        y_lru = _lru_branch(p, lru_conv_w, lru_wax, rows, l)
        y_ml = _mlstm_branch(p, gates, ml_gbias, ml_conv_w, ml_wq_t, ml_wkt_t, ml_wv_t, rows, l)
        y_mo = _moba_branch(p, moba_srows, moba_kext)
        kv = _mem_kv(mem, rows, xa_wkv, l)
        x = _post_mix(x, y_lru, y_ml, y_mo, w_out, rows, xa_wq, kv, xa_wo, final_norm_g[None, :], l,
                      final_norm=(l == depth - 1))
    return x
```

```python
import functools
import math

import jax
import jax.numpy as jnp
import numpy as np
from jax import lax
from jax.experimental import pallas as pl
from jax.experimental.pallas import tpu as pltpu

LANE = 128
SUBLANE = 8
V7X_VMEM_BYTES = 64 * 1024 * 1024

D_MODEL = 2048
LRU_W = 512
LRU_BLOCKS = 4
LRU_BW = LRU_W // LRU_BLOCKS
LRU_C = 8.0
ML_W = 768
ML_HEADS = 4
ML_DH = 192
ML_DHP = 256
ML_PAIRS = 2
ML_PW = 2 * ML_DH
ML_QKV_BLOCK = 4
ML_QKV_SHIFT = 2
ML_CHUNK = 128
ML_GROUP = 4
MO_W = 768
MO_HEADS = 6
MO_DH = 128
MOBA_BLOCK = 256
MOBA_TOPK = 3
XA_HEADS = 4
XA_DH = 128
XA_W = XA_HEADS * XA_DH
RMS_EPS = 1e-6
LN_EPS = 1e-5
NEG_INF = -1e30
LOG2E = 1.4426950408889634

REF_LRU_END = 2 * LRU_W
REF_ML_END = REF_LRU_END + 3 * ML_W
REF_GATE_END = REF_ML_END + 2 * ML_HEADS

C_ML_U = 0
C_ML_O = C_ML_U + ML_W
C_ML_Z = C_ML_O + ML_W
C_LRU_X = C_ML_Z + ML_W
C_LRU_Z = C_LRU_X + LRU_W
C_MO_Q = C_LRU_Z + LRU_W
C_MO_K = C_MO_Q + MO_W
C_MO_V = C_MO_K + MO_W
C_MO_Z = C_MO_V + MO_W
C_GATE = C_MO_Z + MO_W
IN_TM = 512
IN_TN = 3328
IN_COLS_P = 6656
PREP_TR = 256
STAGE_ROWS = 256
POST_TM = 512
POST_SUB = 2

BF16 = jnp.bfloat16
F32 = jnp.float32
NT_DIMS = (((1,), (1,)), ((), ()))


def _vmem_limit(nbytes):
    return int(min(V7X_VMEM_BYTES - (4 << 20), max(32 << 20, nbytes)))


def _rmsnorm_rows(x, g):
    ms = jnp.mean(x * x, axis=-1, keepdims=True)
    return x * lax.rsqrt(ms + RMS_EPS) * g


def _sigmoid(x):
    return jax.nn.sigmoid(x)


def _silu(x):
    return x * jax.nn.sigmoid(x)


def _softplus(x):
    return jnp.maximum(x, 0.0) + jnp.log1p(jnp.exp(-jnp.abs(x)))


def _shift_rows(x, s):
    rolled = pltpu.roll(x, s, axis=0)
    row = lax.broadcasted_iota(jnp.int32, x.shape, 0)
    return jnp.where(row >= s, rolled, 0.0)


def _layer_row(ref, layer):
    return ref[layer:layer + 1, :]


def _causal_conv(x, w_ref, b):
    k = w_ref.shape[0]
    acc = x * w_ref[k - 1:k, :]
    head = x[:SUBLANE]
    acc_head = head * w_ref[k - 1:k, :]
    for j in range(k - 1):
        acc = acc + pltpu.roll(x, k - 1 - j, axis=0) * w_ref[j:j + 1, :]
        acc_head = acc_head + _shift_rows(head, k - 1 - j) * w_ref[j:j + 1, :]
    return jnp.concatenate([acc_head, acc[SUBLANE:]], axis=0) + b


def _w_in_src_rows():
    n_ml = (C_LRU_X - C_ML_U) // PREP_TR
    n_lru = (C_MO_Q - C_LRU_X) // PREP_TR
    n_mo = (C_GATE - C_MO_Q) // PREP_TR
    return ([REF_LRU_END + i * PREP_TR for i in range(n_ml)] + [i * PREP_TR for i in range(n_lru)]
            + [REF_GATE_END + i * PREP_TR for i in range(n_mo)] + [REF_ML_END])


def _in_proj_kernel(x_ref, g_ref, w_hbm, o_ref, gt_ref, w_s, stage_s, sem, *, layer):
    @pl.when(pl.program_id(0) == 0)
    def _():
        src = _w_in_src_rows()

        def copy(k):
            return pltpu.make_async_copy(w_hbm.at[layer, pl.ds(src[k], PREP_TR), :], stage_s.at[k % 2], sem.at[k % 2])

        copy(0).start()
        for k in range(len(src)):
            if k + 1 < len(src):
                copy(k + 1).start()
            copy(k).wait()
            tile = stage_s[k % 2]
            if k == len(src) - 1:
                row = lax.broadcasted_iota(jnp.int32, tile.shape, 0)
                tile = jnp.where(row < 2 * ML_HEADS, tile, 0.0)
            w_s[k * PREP_TR:(k + 1) * PREP_TR, :] = tile.astype(BF16)

    xn = _rmsnorm_rows(x_ref[...], _layer_row(g_ref, layer)).astype(BF16)
    n_chunks = w_s.shape[0] // IN_TN
    for j in range(n_chunks):
        cols = pl.ds(j * IN_TN, IN_TN)
        res = lax.dot_general(xn, w_s[cols, :], NT_DIMS, preferred_element_type=F32)
        o_ref[:, cols] = res.astype(o_ref.dtype)
        if j == C_GATE // IN_TN:
            g0 = C_GATE - j * IN_TN
            gt_ref[...] = res[:, g0:g0 + LANE].T[:2 * ML_HEADS, :]


def _in_proj(x2d, g, w_in_t, l):
    m = x2d.shape[0]
    n = IN_COLS_P
    vmem = (2 * IN_TM * D_MODEL * 4 + 2 * IN_TM * D_MODEL * 2 + D_MODEL * n * 2 + 2 * PREP_TR * D_MODEL * 4
            + 2 * IN_TM * n * 2 + 2 * IN_TM * IN_TN * 4)
    return pl.pallas_call(
        functools.partial(_in_proj_kernel, layer=l),
        grid=(m // IN_TM,),
        in_specs=[
            pl.BlockSpec((IN_TM, D_MODEL), lambda i: (i, 0)),
            pl.BlockSpec(g.shape, lambda i: (0, 0)),
            pl.BlockSpec(memory_space=pl.ANY),
        ],
        out_specs=[pl.BlockSpec((IN_TM, n), lambda i: (i, 0)),
                   pl.BlockSpec((2 * ML_HEADS, IN_TM), lambda i: (0, i))],
        out_shape=[jax.ShapeDtypeStruct((m, n), BF16), jax.ShapeDtypeStruct((2 * ML_HEADS, m), F32)],
        scratch_shapes=[pltpu.VMEM((n, D_MODEL), BF16),
                        pltpu.VMEM((2, PREP_TR, D_MODEL), F32),
                        pltpu.SemaphoreType.DMA((2,))],
        compiler_params=pltpu.CompilerParams(
            dimension_semantics=("arbitrary",), vmem_limit_bytes=_vmem_limit(vmem)),
        name="in_proj",
    )(x2d, g, w_in_t)


def _lru_kernel(x_ref, z_ref, cw_ref, cb_ref, wax_ref, ba_ref, bx_ref, lam_ref, o_ref, a_s, u_s, *, layer):
    t = x_ref.shape[1]
    x = x_ref[0].astype(F32)
    xc = _causal_conv(x, cw_ref, _layer_row(cb_ref, layer))
    pre = jnp.dot(xc.astype(BF16), wax_ref[0], preferred_element_type=F32)
    r = _sigmoid(pre[:, :LRU_BW] + _layer_row(ba_ref, layer))
    i = _sigmoid(pre[:, LRU_BW:] + _layer_row(bx_ref, layer))
    log_a = (-LRU_C) * r * _softplus(-_layer_row(lam_ref, layer))
    a = jnp.exp(log_a)
    a_s[...] = a
    u_s[...] = jnp.sqrt(-jnp.tanh(log_a) * (1.0 + a * a)) * (i * xc)

    row = lax.broadcasted_iota(jnp.int32, (SUBLANE, LRU_BW), 0)

    def block(blk, h_prev):
        r0 = pl.multiple_of(blk * SUBLANE, SUBLANE)
        a_b = a_s[pl.ds(r0, SUBLANE), :]
        u_b = u_s[pl.ds(r0, SUBLANE), :]
        for s in (1, 2, 4):
            a_sh = jnp.where(row >= s, pltpu.roll(a_b, s, axis=0), 1.0)
            u_sh = jnp.where(row >= s, pltpu.roll(u_b, s, axis=0), 0.0)
            u_b = a_b * u_sh + u_b
            a_b = a_b * a_sh
        u_s[pl.ds(r0, SUBLANE), :] = a_b * h_prev + u_b
        last = lambda v: jnp.broadcast_to(v[SUBLANE - 1:SUBLANE, :], (SUBLANE, LRU_BW))
        return last(a_b) * h_prev + last(u_b)

    lax.fori_loop(0, t // SUBLANE, block, jnp.zeros((SUBLANE, LRU_BW), F32), unroll=4)
    z = z_ref[0].astype(F32)
    o_ref[0] = (u_s[...] * _silu(z)).astype(o_ref.dtype)


def _lru_branch(p, cw, wax, cb, ba, bx, lam, l):
    b, t, _ = p.shape
    xb, zb = C_LRU_X // LRU_BW, C_LRU_Z // LRU_BW
    vec = pl.BlockSpec((cb.shape[0], LRU_BW), lambda bi, g: (0, g))
    return pl.pallas_call(
        functools.partial(_lru_kernel, layer=l),
        grid=(b, LRU_BLOCKS),
        in_specs=[
            pl.BlockSpec((1, t, LRU_BW), lambda bi, g: (bi, 0, xb + g)),
            pl.BlockSpec((1, t, LRU_BW), lambda bi, g: (bi, 0, zb + g)),
            pl.BlockSpec((None, cw.shape[1], LRU_BW), lambda bi, g: (l, 0, g)),
            vec,
            pl.BlockSpec((None, 1, LRU_BW, 2 * LRU_BW), lambda bi, g: (l, g, 0, 0)),
            vec, vec, vec,
        ],
        out_specs=pl.BlockSpec((1, t, LRU_BW), lambda bi, g: (bi, 0, g)),
        out_shape=jax.ShapeDtypeStruct((b, t, LRU_W), BF16),
        scratch_shapes=[pltpu.VMEM((t, LRU_BW), F32), pltpu.VMEM((t, LRU_BW), F32)],
        compiler_params=pltpu.CompilerParams(
            dimension_semantics=("arbitrary", "arbitrary"), vmem_limit_bytes=_vmem_limit(V7X_VMEM_BYTES)),
        name="rg_lru",
    )(p, p, cw, cb, wax, ba, bx, lam)


def _blockdiag_in_out(w_ref, hh):
    wh = w_ref[...][:, hh * ML_DH:(hh + 1) * ML_DH]
    wh = jnp.concatenate([wh, jnp.zeros((ML_QKV_BLOCK, ML_DHP - ML_DH), F32)], axis=1)
    r = lax.broadcasted_iota(jnp.int32, (ML_PW, ML_DHP), 0)
    c = lax.broadcasted_iota(jnp.int32, (ML_PW, ML_DHP), 1)
    d = jnp.zeros((ML_PW, ML_DHP), F32)
    for i in range(ML_QKV_BLOCK):
        d = jnp.where((r & (ML_QKV_BLOCK - 1)) == i, wh[i:i + 1, :], d)
    keep = (((r >> ML_QKV_SHIFT) - hh * (ML_DH // ML_QKV_BLOCK)) == (c >> ML_QKV_SHIFT)) & (c < ML_DH)
    return jnp.where(keep, d, 0.0)


def _blockdiag_in_window(w_ref, hh):
    w0 = ML_WINDOW[hh]
    wh = w_ref[...][:, w0:w0 + ML_DHP]
    r = lax.broadcasted_iota(jnp.int32, (ML_DHP, ML_DHP), 0) + w0
    c = lax.broadcasted_iota(jnp.int32, (ML_DHP, ML_DHP), 1) + w0
    d = jnp.zeros((ML_DHP, ML_DHP), F32)
    for i in range(ML_QKV_BLOCK):
        d = jnp.where((r & (ML_QKV_BLOCK - 1)) == i, wh[i:i + 1, :], d)
    keep = ((r >> ML_QKV_SHIFT) == (c >> ML_QKV_SHIFT)) & (c >= hh * ML_DH) & (c < (hh + 1) * ML_DH)
    return jnp.where(keep, d, 0.0)


def _blockdiag_out_in(w_ref, hh):
    wk = w_ref[...]
    o = lax.broadcasted_iota(jnp.int32, (ML_DHP, ML_PW), 0)
    r = lax.broadcasted_iota(jnp.int32, (ML_DHP, ML_PW), 1)
    d = jnp.zeros((ML_DHP, ML_PW), F32)
    for j in range(ML_QKV_BLOCK):
        d = jnp.where((o & (ML_QKV_BLOCK - 1)) == j, wk[j:j + 1, :], d)
    keep = ((o >> ML_QKV_SHIFT) == ((r >> ML_QKV_SHIFT) - hh * (ML_DH // ML_QKV_BLOCK))) & (o < ML_DH)
    return jnp.where(keep, d, 0.0)


ML_WINDOW = (0, ML_PW - ML_DHP)
ML_DEN_LANE = (ML_DH + 2, 2)


def _mlstm_kernel(u_ref, og_ref, z_ref, ig_ref, fg_ref, bi_ref, bf_ref, cw_ref, cb_ref,
                  wq_ref, wkt_ref, wv_ref, ng_ref, y_ref,
                  dq_s, dkt_s, dv_s, q_s, kt_s, v_s, cp_s, c_s, r_s, w_s, col_s, so_s, sn_s, loca_s, locb_s, *, layer):
    t = u_ref.shape[1]
    nc = t // ML_CHUNK
    L = ML_CHUNK
    heads = range(2)

    @pl.when(pl.program_id(1) == 0)
    def _():
        for hh in heads:
            dq_s[hh] = _blockdiag_in_out(wq_ref, hh).astype(BF16)
            dkt_s[hh] = _blockdiag_out_in(wkt_ref, hh).astype(BF16)
            dv_s[hh] = _blockdiag_in_window(wv_ref, hh).astype(BF16)

    u = u_ref[0]
    uc = _silu(_causal_conv(u.astype(F32), cw_ref, _layer_row(cb_ref, layer))).astype(BF16)
    for hh in heads:
        w0 = ML_WINDOW[hh]
        uc_w, u_w = uc[:, w0:w0 + ML_DHP], u[:, w0:w0 + ML_DHP]
        q_s[hh] = jnp.dot(uc_w, dq_s[hh, w0:w0 + ML_DHP, :], preferred_element_type=F32).astype(BF16)
        kt = lax.dot_general(dkt_s[hh, :, w0:w0 + ML_DHP], uc_w, NT_DIMS, preferred_element_type=F32)
        kt = kt * (ML_DH ** -0.5)
        for c in range(nc):
            kt_s[hh, c] = kt[:, c * L:(c + 1) * L].astype(BF16)
        v = jnp.dot(u_w, dv_s[hh], preferred_element_type=F32)
        vlane = lax.broadcasted_iota(jnp.int32, v.shape, 1)
        v_s[hh] = jnp.where(vlane == ML_DEN_LANE[hh], 1.0, v).astype(BF16)

    glane = lax.broadcasted_iota(jnp.int32, (nc, L), 1)
    grow = lax.broadcasted_iota(jnp.int32, (nc, L), 0)
    row8 = lax.broadcasted_iota(jnp.int32, (SUBLANE, L), 0)
    for hh in heads:
        ig = ig_ref[hh, 0] + bi_ref[hh]
        lf = -_softplus(-(fg_ref[hh, 0] + bf_ref[hh]))
        b = lf
        for k in range(int(math.log2(L))):
            sh = 1 << k
            b = b + jnp.where(glane >= sh, pltpu.roll(b, sh, axis=1), 0.0)
        g = jnp.broadcast_to(b[:, L - 1:L], (nc, L))
        a = g - b + ig
        mloc = jnp.broadcast_to(jnp.max(a, axis=1, keepdims=True), (nc, L))
        m = jnp.zeros((1, L), F32)
        m_prev = jnp.zeros((nc, L), F32)
        m_next = jnp.zeros((nc, L), F32)
        for c in range(nc):
            m_prev = jnp.where(grow == c, m, m_prev)
            m = jnp.maximum(g[c:c + 1, :] + m, mloc[c:c + 1, :])
            m_next = jnp.where(grow == c, m, m_next)
        r = ig - b
        cmx = r
        for k in range(int(math.log2(L))):
            sh = 1 << k
            cmx = jnp.maximum(cmx, jnp.where(glane >= sh, pltpu.roll(cmx, sh, axis=1), -jnp.inf))
        mm = jnp.maximum(m_prev, cmx)
        s_int = jnp.exp(m_prev - mm)
        clamp = jnp.exp(-(b + mm))
        r_s[hh] = r
        w_s[hh] = jnp.exp(a - mloc)
        so_s[hh] = jnp.exp(g + m_prev - m_next)
        sn_s[hh] = jnp.exp(mloc - m_next)
        for c in range(nc):
            col_s[hh, c] = jnp.where(row8 == 0, mm[c:c + 1, :],
                                     jnp.where(row8 == 1, s_int[c:c + 1, :],
                                               jnp.where(row8 == 2, clamp[c:c + 1, :], 0.0)))

    zero_rows = jnp.zeros((ML_DHP - ML_DH, ML_DHP), BF16)
    c_s[...] = jnp.zeros(c_s.shape, F32)

    n_pairs = nc // 2

    def local_states(pair, loc_ref):
        for i in range(2):
            c = 2 * pair + i
            r0 = pl.multiple_of(c * L, L)
            for hh in heads:
                ktw = (kt_s[hh, c, :ML_DH, :].astype(F32) * w_s[hh, pl.ds(c, 1), :]).astype(BF16)
                loc_ref[i, hh] = jnp.dot(ktw, v_s[hh, pl.ds(r0, L), :], preferred_element_type=F32)

    def advance(pair, loc_ref):
        for i in range(2):
            c = 2 * pair + i
            for hh in heads:
                c_prev = c_s[hh]
                cp_s[hh, c] = jnp.concatenate([c_prev.astype(BF16), zero_rows], axis=0)
                c_s[hh] = (so_s[hh, pl.ds(c, 1), :][:, :1] * c_prev
                           + sn_s[hh, pl.ds(c, 1), :][:, :1] * loc_ref[i, hh])

    def state(k, carry):
        local_states(2 * k + 1, locb_s)
        advance(2 * k, loca_s)
        local_states(jnp.minimum(2 * k + 2, n_pairs - 1), loca_s)
        advance(2 * k + 1, locb_s)
        return carry

    local_states(0, loca_s)
    lax.fori_loop(0, n_pairs // 2, state, 0)

    tri = (lax.broadcasted_iota(jnp.int32, (L, L), 0) >= lax.broadcasted_iota(jnp.int32, (L, L), 1))
    first = lax.broadcasted_iota(jnp.int32, (L, ML_PW), 1) < ML_DH

    def chunk_rows(c):
        return pl.ds(pl.multiple_of(c * L, L), L)

    def raw_scores(c):
        rows = chunk_rows(c)
        return [jnp.dot(q_s[hh, rows, :], kt_s[hh, c], preferred_element_type=F32) for hh in heads]

    def weights(c, raw):
        out = []
        for hh in heads:
            c0 = ML_DEN_LANE[hh] % LANE - 2
            pads = [jnp.zeros((n, L), F32) for n in (c0, L - SUBLANE - c0)]
            cols = jnp.concatenate([a for a in (pads[0], col_s[hh, c], pads[1]) if a.shape[0]], axis=0).T
            rb = jnp.broadcast_to(r_s[hh, pl.ds(c, 1), :], (L, L))
            decay = jnp.exp(jnp.where(tri, rb - cols[:, c0:c0 + 1], -jnp.inf))
            out.append(((raw[hh] * decay).astype(BF16), cols[:, c0 + 1:c0 + 2], cols[:, c0 + 2:c0 + 3]))
        return out

    def numerators(c, wts):
        rows = chunk_rows(c)
        nds, invs = [], []
        for hh, (s_mat, s_int, clamp) in zip(heads, wts):
            nd = (jnp.dot(s_mat, v_s[hh, rows, :], preferred_element_type=F32)
                  + s_int * jnp.dot(q_s[hh, rows, :], cp_s[hh, c], preferred_element_type=F32))
            den = nd[:, ML_DEN_LANE[hh]:ML_DEN_LANE[hh] + 1]
            nds.append(nd)
            invs.append(1.0 / jnp.maximum(jnp.abs(den), clamp))
        w1 = ML_WINDOW[1]
        shared_first = lax.broadcasted_iota(jnp.int32, (L, ML_DHP - w1), 1) < ML_DH - w1
        shared = jnp.where(shared_first, nds[0][:, w1:], nds[1][:, :ML_DHP - w1])
        return jnp.concatenate([nds[0][:, :w1], shared, nds[1][:, ML_DHP - w1:]], axis=1), invs

    def gates(c):
        rows = chunk_rows(c)
        return (_sigmoid(og_ref[0, rows, :].astype(F32)),
                _silu(z_ref[0, rows, :].astype(F32)) * _layer_row(ng_ref, layer))

    def means(nd, og):
        x = og * nd
        mu0 = jnp.sum(jnp.where(first, x, 0.0), axis=1, keepdims=True) * (1.0 / ML_DH)
        mu1 = jnp.sum(jnp.where(first, 0.0, x), axis=1, keepdims=True) * (1.0 / ML_DH)
        return x, mu0, mu1

    def variances(x, mu0, mu1):
        dev = x - jnp.where(first, mu0, mu1)
        sq = dev * dev
        var0 = jnp.sum(jnp.where(first, sq, 0.0), axis=1, keepdims=True) * (1.0 / ML_DH)
        var1 = jnp.sum(jnp.where(first, 0.0, sq), axis=1, keepdims=True) * (1.0 / ML_DH)
        return dev, var0, var1

    def finish(c, dev, var0, var1, invs, zs):
        f0 = invs[0] * lax.rsqrt(invs[0] * invs[0] * var0 + LN_EPS)
        f1 = invs[1] * lax.rsqrt(invs[1] * invs[1] * var1 + LN_EPS)
        y_ref[0, chunk_rows(c), :] = (dev * jnp.where(first, f0, f1) * zs).astype(y_ref.dtype)

    def group(gi, carry):
        cs = [gi * ML_GROUP + i for i in range(ML_GROUP)]
        n = range(ML_GROUP)
        raws = [raw_scores(c) for c in cs]
        gts = [gates(c) for c in cs]
        wts = [weights(c, raws[i]) for i, c in enumerate(cs)]
        nums = [numerators(c, wts[i]) for i, c in enumerate(cs)]
        mus = [means(nums[i][0], gts[i][0]) for i in n]
        vrs = [variances(*mus[i]) for i in n]
        for i, c in enumerate(cs):
            finish(c, *vrs[i], nums[i][1], gts[i][1])
        return carry

    lax.fori_loop(0, nc // ML_GROUP, group, 0)


def _mlstm_branch(p, gates, gbias, cw, cb, wq, wkt, wv, ng, l):
    b, t, _ = p.shape
    nc = t // ML_CHUNK
    ub, ob, zb = C_ML_U // ML_PW, C_ML_O // ML_PW, C_ML_Z // ML_PW
    seq = lambda base: pl.BlockSpec((1, t, ML_PW), lambda pr, bi: (bi, 0, base + pr))
    vec = pl.BlockSpec((cb.shape[0], ML_PW), lambda pr, bi: (0, pr))
    taps = pl.BlockSpec((None, ML_QKV_BLOCK, ML_PW), lambda pr, bi: (l, 0, pr))
    gate_rows = lambda: pltpu.VMEM((2, nc, ML_CHUNK), F32)
    pair_states = lambda: pltpu.VMEM((2, 2, ML_DH, ML_DHP), F32)
    return pl.pallas_call(
        functools.partial(_mlstm_kernel, layer=l),
        grid=(ML_PAIRS, b),
        in_specs=[
            seq(ub), seq(ob), seq(zb),
            pl.BlockSpec((2, 1, nc, ML_CHUNK), lambda pr, bi: (pr, bi, 0, 0)),
            pl.BlockSpec((2, 1, nc, ML_CHUNK), lambda pr, bi: (ML_PAIRS + pr, bi, 0, 0)),
            pl.BlockSpec((None, 2, 1, ML_CHUNK), lambda pr, bi: (l, pr, 0, 0)),
            pl.BlockSpec((None, 2, 1, ML_CHUNK), lambda pr, bi: (l, ML_PAIRS + pr, 0, 0)),
            pl.BlockSpec((None, cw.shape[1], ML_PW), lambda pr, bi: (l, 0, pr)),
            vec, taps, taps, taps, vec,
        ],
        out_specs=pl.BlockSpec((1, t, ML_PW), lambda pr, bi: (bi, 0, pr)),
        out_shape=jax.ShapeDtypeStruct((b, t, ML_W), BF16),
        scratch_shapes=[
            pltpu.VMEM((2, ML_PW, ML_DHP), BF16),
            pltpu.VMEM((2, ML_DHP, ML_PW), BF16),
            pltpu.VMEM((2, ML_DHP, ML_DHP), BF16),
            pltpu.VMEM((2, t, ML_DHP), BF16),
            pltpu.VMEM((2, nc, ML_DHP, ML_CHUNK), BF16),
            pltpu.VMEM((2, t, ML_DHP), BF16),
            pltpu.VMEM((2, nc, ML_DHP, ML_DHP), BF16),
            pltpu.VMEM((2, ML_DH, ML_DHP), F32),
            gate_rows(), gate_rows(),
            pltpu.VMEM((2, nc, SUBLANE, ML_CHUNK), F32),
            gate_rows(), gate_rows(),
            pair_states(), pair_states(),
        ],
        compiler_params=pltpu.CompilerParams(
            dimension_semantics=("arbitrary", "arbitrary"), vmem_limit_bytes=_vmem_limit(56 << 20)),
        name="mlstm",
    )(p, p, p, gates, gates, gbias, gbias, cw, cb, wq, wkt, wv, ng)


MOBA_SEL_LANE = 6
MOBA_AHEAD = 3


def _moba_key_ext(nb):
    ext = np.zeros((nb, MOBA_BLOCK, MO_DH), np.float32)
    for n in range(nb):
        ext[n, :, 0:3] = n * MOBA_BLOCK
        ext[n, :, 3:6] = np.arange(MOBA_BLOCK, dtype=np.float32)[:, None]
        ext[n, :, MOBA_SEL_LANE + n] = 1.0
    return jnp.asarray(ext, dtype=BF16)


def _moba_slope_rows(slopes):
    rows = np.zeros((len(slopes), 1, MO_DH), np.float32)
    for h, s in enumerate(slopes):
        rest = np.float32(np.float32(s) * np.float32(LOG2E))
        for i in range(3):
            piece = np.float32(rest).astype(BF16).astype(np.float32)
            rows[h, 0, i] = rows[h, 0, 3 + i] = piece
            rest = np.float32(rest - piece)
    return jnp.asarray(rows)


def _moba_kernel(q_ref, k_ref, v_ref, z_ref, srow_ref, kext_ref, o_ref, l_s, qa_s):
    t = q_ref.shape[1]
    nb = t // MOBA_BLOCK
    bs = MOBA_BLOCK
    qscale = (MO_DH ** -0.5) * LOG2E

    krow = lax.broadcasted_iota(jnp.int32, (LANE, MO_DH), 0)
    kmean = jnp.zeros((LANE, MO_DH), F32)
    for n in range(nb):
        mean_n = jnp.sum(k_ref[0, n * bs:(n + 1) * bs, :].astype(F32), axis=0, keepdims=True) * (1.0 / bs)
        kmean = jnp.where(krow == MOBA_SEL_LANE + n, mean_n, kmean)
    kmean = kmean.astype(BF16)

    lane = lax.broadcasted_iota(jnp.int32, (bs, MO_DH), 1)
    causal = (lax.broadcasted_iota(jnp.int32, (bs, bs), 0) >= lax.broadcasted_iota(jnp.int32, (bs, bs), 1))
    ones_col = jnp.where(lane == 0, 1.0, 0.0).astype(BF16)
    slope_cols = jnp.where(lane < MOBA_SEL_LANE, srow_ref[0], 0.0)

    def prepare(qb):
        q_b = q_ref[0, qb * bs:(qb + 1) * bs, :]
        q_ext = slope_cols
        if qb > MOBA_TOPK:
            gate = lax.dot_general(q_b, kmean, NT_DIMS, preferred_element_type=F32)
            beaten = jnp.zeros((bs, MO_DH), F32)
            for m in range(qb):
                gm = gate[:, MOBA_SEL_LANE + m:MOBA_SEL_LANE + m + 1]
                wins = (gm > gate) | ((gm == gate) & (lane > MOBA_SEL_LANE + m))
                beaten = beaten + jnp.where(wins, 1.0, 0.0)
            past = (lane >= MOBA_SEL_LANE) & (lane < MOBA_SEL_LANE + qb)
            q_ext = jnp.where(past & (beaten >= float(MOBA_TOPK)), NEG_INF, slope_cols)
        qa_s[qb] = jnp.concatenate([(q_b.astype(F32) * qscale).astype(BF16), q_ext.astype(BF16)], axis=1)

    def scores(qb):
        q_aug = qa_s[qb]
        l_q = l_s.at[qb % l_s.shape[0]]
        mx = None
        for n in range(qb + 1):
            k_aug = jnp.concatenate([k_ref[0, n * bs:(n + 1) * bs, :], kext_ref[n]], axis=1)
            logit = lax.dot_general(q_aug, k_aug, NT_DIMS, preferred_element_type=F32)
            if n == qb:
                logit = jnp.where(causal, logit, NEG_INF)
            l_q[:, n * bs:(n + 1) * bs] = logit
            half = jnp.maximum(logit[:, :LANE], logit[:, LANE:])
            mx = half if mx is None else jnp.maximum(mx, half)
        return jnp.max(mx, axis=1, keepdims=True)

    def outputs(qb, m_row):
        l_q = l_s.at[qb % l_s.shape[0]]
        acc = jnp.zeros((bs, 2 * MO_DH), F32)
        for n in range(qb + 1):
            pr = jnp.exp2(l_q[:, n * bs:(n + 1) * bs] - m_row).astype(BF16)
            v_aug = jnp.concatenate([v_ref[0, n * bs:(n + 1) * bs, :], ones_col], axis=1)
            acc = acc + jnp.dot(pr, v_aug, preferred_element_type=F32)
        z = z_ref[0, qb * bs:(qb + 1) * bs, :].astype(F32)
        inv = 1.0 / acc[:, MO_DH:MO_DH + 1]
        o_ref[0, qb * bs:(qb + 1) * bs, :] = (acc[:, :MO_DH] * (inv * _silu(z))).astype(o_ref.dtype)

    ahead = l_s.shape[0] - 1
    for qb in range(min(ahead + 1, nb)):
        prepare(qb)
    m_rows = {qb: scores(qb) for qb in range(min(ahead, nb))}
    for qb in range(nb):
        if qb + ahead + 1 < nb:
            prepare(qb + ahead + 1)
        if qb + ahead < nb:
            m_rows[qb + ahead] = scores(qb + ahead)
        outputs(qb, m_rows.pop(qb))


def _moba_branch(p, srows, kext):
    b, t, _ = p.shape
    nb = t // MOBA_BLOCK
    qb, kb, vb, zb = (c // MO_DH for c in (C_MO_Q, C_MO_K, C_MO_V, C_MO_Z))
    seq = lambda base: pl.BlockSpec((1, t, MO_DH), lambda bi, h: (bi, 0, base + h))
    return pl.pallas_call(
        _moba_kernel,
        grid=(b, MO_HEADS),
        in_specs=[seq(qb), seq(kb), seq(vb), seq(zb),
                  pl.BlockSpec((1, 1, MO_DH), lambda bi, h: (h, 0, 0)),
                  pl.BlockSpec((nb, MOBA_BLOCK, MO_DH), lambda bi, h: (0, 0, 0))],
        out_specs=pl.BlockSpec((1, t, MO_DH), lambda bi, h: (bi, 0, h)),
        out_shape=jax.ShapeDtypeStruct((b, t, MO_W), BF16),
        scratch_shapes=[pltpu.VMEM((MOBA_AHEAD + 1, MOBA_BLOCK, t), F32),
                        pltpu.VMEM((nb, MOBA_BLOCK, 2 * MO_DH), BF16)],
        compiler_params=pltpu.CompilerParams(
            dimension_semantics=("arbitrary", "arbitrary"), vmem_limit_bytes=_vmem_limit(V7X_VMEM_BYTES)),
        name="moba",
    )(p, p, p, p, srows, kext)


def _stage_weights(w_hbm, layer, dst_ref, stage_ref, sem):
    tile_rows = stage_ref.shape[1]
    n_tiles = dst_ref.shape[0] // tile_rows

    def copy(k):
        return pltpu.make_async_copy(w_hbm.at[layer, pl.ds(k * tile_rows, tile_rows), :], stage_ref.at[k % 2],
                                     sem.at[k % 2])

    copy(0).start()
    for k in range(n_tiles):
        if k + 1 < n_tiles:
            copy(k + 1).start()
        copy(k).wait()
        dst_ref[k * tile_rows:(k + 1) * tile_rows, :] = stage_ref[k % 2].astype(BF16)


def _mem_kv_kernel(mem_ref, g_ref, w_hbm, o_ref, w_s, stage_s, sem, *, layer):
    @pl.when(pl.program_id(0) == 0)
    def _():
        _stage_weights(w_hbm, layer, w_s, stage_s, sem)

    hm = _rmsnorm_rows(mem_ref[0], _layer_row(g_ref, layer)).astype(BF16)
    o_ref[0] = jnp.dot(hm, w_s[...], preferred_element_type=F32).astype(o_ref.dtype)


def _mem_kv(mem, g, wkv, l):
    b, m, _ = mem.shape
    return pl.pallas_call(
        functools.partial(_mem_kv_kernel, layer=l),
        grid=(b,),
        in_specs=[pl.BlockSpec((1, m, D_MODEL), lambda bi: (bi, 0, 0)),
                  pl.BlockSpec(g.shape, lambda bi: (0, 0)),
                  pl.BlockSpec(memory_space=pl.ANY)],
        out_specs=pl.BlockSpec((1, m, 2 * XA_W), lambda bi: (bi, 0, 0)),
        out_shape=jax.ShapeDtypeStruct((b, m, 2 * XA_W), BF16),
        scratch_shapes=[pltpu.VMEM((D_MODEL, 2 * XA_W), BF16),
                        pltpu.VMEM((2, STAGE_ROWS, 2 * XA_W), F32),
                        pltpu.SemaphoreType.DMA((2,))],
        compiler_params=pltpu.CompilerParams(
            dimension_semantics=("arbitrary",), vmem_limit_bytes=_vmem_limit(V7X_VMEM_BYTES)),
        name="mem_kv",
    )(mem, g, wkv)


def _post_kernel(x_ref, yl_ref, ym_ref, yo_ref, wout_hbm, g_ref, wq_hbm, kv_ref, wo_hbm, fg_ref, o_ref,
                 x1_s, wout_ref, wq_ref, wo_ref, stage_s, stage_q_s, sem, sem_q, *, final_norm, layer):
    @pl.when((pl.program_id(0) == 0) & (pl.program_id(1) == 0))
    def _():
        _stage_weights(wout_hbm, layer, wout_ref, stage_s, sem)
        _stage_weights(wo_hbm, layer, wo_ref, stage_s, sem)
        _stage_weights(wq_hbm, layer, wq_ref, stage_q_s, sem_q)

    sub = x_ref.shape[1] // POST_SUB
    groups = [pl.ds(i * sub, sub) for i in range(POST_SUB)]

    for r in groups:
        acc = jnp.dot(yl_ref[0, r, :], wout_ref[0:LRU_W, :], preferred_element_type=F32)
        acc = acc + jnp.dot(ym_ref[0, r, :], wout_ref[LRU_W:LRU_W + ML_W, :], preferred_element_type=F32)
        acc = acc + jnp.dot(yo_ref[0, r, :], wout_ref[LRU_W + ML_W:, :], preferred_element_type=F32)
        x1_s[r, :] = x_ref[0, r, :] + acc
    qs = []
    for r in groups:
        hx = _rmsnorm_rows(x1_s[r, :], _layer_row(g_ref, layer)).astype(BF16)
        qs.append(jnp.dot(hx, wq_ref[...], preferred_element_type=F32).astype(BF16))
    scores = []
    for q in qs:
        scores.append([
            lax.dot_general(q[:, h * XA_DH:(h + 1) * XA_DH], kv_ref[0, :, h * XA_DH:(h + 1) * XA_DH], NT_DIMS,
                            preferred_element_type=F32) * (XA_DH ** -0.5)
            for h in range(XA_HEADS)])
    outs = []
    for sc in scores:
        heads = []
        for h, s in enumerate(sc):
            e = jnp.exp(s - jnp.max(s, axis=1, keepdims=True))
            pr = e / jnp.sum(e, axis=1, keepdims=True)
            v_h = kv_ref[0, :, XA_W + h * XA_DH:XA_W + (h + 1) * XA_DH]
            heads.append(jnp.dot(pr.astype(BF16), v_h, preferred_element_type=F32).astype(BF16))
        outs.append(jnp.concatenate(heads, axis=1))
    for r, o in zip(groups, outs):
        y = x1_s[r, :] + jnp.dot(o, wo_ref[...], preferred_element_type=F32)
        if final_norm:
            y = _rmsnorm_rows(y, fg_ref[...])
        o_ref[0, r, :] = y


def _post_mix(x, yl, ym, yo, wout, g, wq, kv, wo, fg, l, final_norm):
    b, t, _ = x.shape
    m = kv.shape[1]
    rows = lambda width: pl.BlockSpec((1, POST_TM, width), lambda bi, i: (bi, i, 0))
    hbm = pl.BlockSpec(memory_space=pl.ANY)
    weights = (D_MODEL * D_MODEL + D_MODEL * XA_W + XA_W * D_MODEL) * 2
    stages = 2 * STAGE_ROWS * (D_MODEL + XA_W) * 4
    tiles = 2 * POST_TM * (2 * D_MODEL * 4 + D_MODEL * 2) + 2 * m * 2 * XA_W * 2
    vmem = weights + stages + tiles + 6 * POST_TM * D_MODEL * 4
    return pl.pallas_call(
        functools.partial(_post_kernel, final_norm=final_norm, layer=l),
        grid=(b, t // POST_TM),
        in_specs=[rows(D_MODEL), rows(LRU_W), rows(ML_W), rows(MO_W),
                  hbm,
                  pl.BlockSpec(g.shape, lambda bi, i: (0, 0)),
                  hbm,
                  pl.BlockSpec((1, m, 2 * XA_W), lambda bi, i: (bi, 0, 0)),
                  hbm,
                  pl.BlockSpec((1, D_MODEL), lambda bi, i: (0, 0))],
        out_specs=rows(D_MODEL),
        out_shape=jax.ShapeDtypeStruct((b, t, D_MODEL), F32),
        scratch_shapes=[pltpu.VMEM((POST_TM, D_MODEL), F32),
                        pltpu.VMEM((D_MODEL, D_MODEL), BF16),
                        pltpu.VMEM((D_MODEL, XA_W), BF16),
                        pltpu.VMEM((XA_W, D_MODEL), BF16),
                        pltpu.VMEM((2, STAGE_ROWS, D_MODEL), F32),
                        pltpu.VMEM((2, STAGE_ROWS, XA_W), F32),
                        pltpu.SemaphoreType.DMA((2,)),
                        pltpu.SemaphoreType.DMA((2,))],
        compiler_params=pltpu.CompilerParams(
            dimension_semantics=("arbitrary", "arbitrary"), vmem_limit_bytes=_vmem_limit(vmem)),
        name="post_mix",
    )(x, yl, ym, yo, wout, g, wq, kv, wo, fg)


def _alibi_slopes(n):
    def pow2(m):
        start = 2.0 ** (-8.0 / m)
        return [start ** (i + 1) for i in range(m)]
    if math.log2(n).is_integer():
        s = pow2(n)
    else:
        c = 2 ** int(math.floor(math.log2(n)))
        s = pow2(c) + pow2(2 * c)[0::2][:n - c]
    return np.asarray(s, dtype=np.float32)


def _taps(w, perm):
    depth = w.shape[0]
    return jnp.transpose(w, perm).reshape(depth, ML_QKV_BLOCK, ML_W)


def kernel(x, mem, mix_norm_g, w_in, lru_conv_w, lru_conv_b, lru_wa, lru_ba, lru_wx, lru_bx, lru_lambda,
           ml_conv_w, ml_conv_b, ml_wq, ml_wk, ml_wv, ml_bi, ml_bf, ml_norm_g, w_out, xa_norm_g, mem_norm_g,
           xa_wq, xa_wkv, xa_wo, final_norm_g):
    b, t, d = x.shape
    depth = w_in.shape[0]
    nc = t // ML_CHUNK
    moba_srows = _moba_slope_rows(_alibi_slopes(MO_HEADS))
    moba_kext = _moba_key_ext(t // MOBA_BLOCK)
    w_in_t = jnp.swapaxes(w_in, 1, 2)
    lru_wax = jnp.concatenate([lru_wa, lru_wx], axis=-1).astype(BF16)
    ml_wq_t = _taps(ml_wq, (0, 2, 1, 3))
    ml_wkt_t = _taps(ml_wk, (0, 3, 1, 2))
    ml_wv_t = _taps(ml_wv, (0, 2, 1, 3))
    ml_gbias = jnp.broadcast_to(jnp.concatenate([ml_bi, ml_bf], axis=1)[:, :, None, None],
                                (depth, 2 * ML_HEADS, 1, ML_CHUNK))
    for l in range(depth):
        p, gates = _in_proj(x.reshape(b * t, d), mix_norm_g, w_in_t, l)
        p, gates = p.reshape(b, t, IN_COLS_P), gates.reshape(2 * ML_HEADS, b, nc, ML_CHUNK)
        y_lru = _lru_branch(p, lru_conv_w, lru_wax, lru_conv_b, lru_ba, lru_bx, lru_lambda, l)
        y_ml = _mlstm_branch(p, gates, ml_gbias, ml_conv_w, ml_conv_b, ml_wq_t, ml_wkt_t, ml_wv_t, ml_norm_g, l)
        y_mo = _moba_branch(p, moba_srows, moba_kext)
        kv = _mem_kv(mem, mem_norm_g, xa_wkv, l)
        x = _post_mix(x, y_lru, y_ml, y_mo, w_out, xa_norm_g, xa_wq, kv, xa_wo, final_norm_g[None, :], l,
                      final_norm=(l == depth - 1))
    return x
```

```python
import functools
import math

import jax
import jax.numpy as jnp
import numpy as np
from jax import lax
from jax.experimental import pallas as pl
from jax.experimental.pallas import tpu as pltpu

LANE = 128
SUBLANE = 8
V7X_VMEM_BYTES = 64 * 1024 * 1024

D_MODEL = 2048
LRU_W = 512
LRU_BLOCKS = 4
LRU_BW = LRU_W // LRU_BLOCKS
LRU_C = 8.0
ML_W = 768
ML_HEADS = 4
ML_DH = 192
ML_DHP = 256
ML_PAIRS = 2
ML_PW = 2 * ML_DH
ML_QKV_BLOCK = 4
ML_QKV_SHIFT = 2
ML_CHUNK = 128
ML_GROUP = 4
MO_W = 768
MO_HEADS = 6
MO_DH = 128
MOBA_BLOCK = 256
MOBA_TOPK = 3
XA_HEADS = 4
XA_DH = 128
XA_W = XA_HEADS * XA_DH
RMS_EPS = 1e-6
LN_EPS = 1e-5
NEG_INF = -1e30
LOG2E = 1.4426950408889634

REF_LRU_END = 2 * LRU_W
REF_ML_END = REF_LRU_END + 3 * ML_W
REF_GATE_END = REF_ML_END + 2 * ML_HEADS

C_ML_U = 0
C_ML_O = C_ML_U + ML_W
C_ML_Z = C_ML_O + ML_W
C_LRU_X = C_ML_Z + ML_W
C_LRU_Z = C_LRU_X + LRU_W
C_MO_Q = C_LRU_Z + LRU_W
C_MO_K = C_MO_Q + MO_W
C_MO_V = C_MO_K + MO_W
C_MO_Z = C_MO_V + MO_W
C_GATE = C_MO_Z + MO_W
IN_TM = 512
IN_TN = 3328
IN_COLS_P = 6656
PREP_TR = 256
STAGE_ROWS = 256
POST_TM = 512
POST_SUB = 2

BF16 = jnp.bfloat16
F32 = jnp.float32
NT_DIMS = (((1,), (1,)), ((), ()))


def _vmem_limit(nbytes):
    return int(min(V7X_VMEM_BYTES - (4 << 20), max(32 << 20, nbytes)))


def _rmsnorm_rows(x, g):
    ms = jnp.mean(x * x, axis=-1, keepdims=True)
    return x * lax.rsqrt(ms + RMS_EPS) * g


def _sigmoid(x):
    return jax.nn.sigmoid(x)


def _silu(x):
    return x * jax.nn.sigmoid(x)


def _softplus(x):
    return jnp.maximum(x, 0.0) + jnp.log1p(jnp.exp(-jnp.abs(x)))


def _shift_rows(x, s):
    rolled = pltpu.roll(x, s, axis=0)
    row = lax.broadcasted_iota(jnp.int32, x.shape, 0)
    return jnp.where(row >= s, rolled, 0.0)


def _layer_row(ref, layer):
    return ref[layer:layer + 1, :]


def _causal_conv(x, w_ref, b):
    k = w_ref.shape[0]
    acc = x * w_ref[k - 1:k, :]
    head = x[:SUBLANE]
    acc_head = head * w_ref[k - 1:k, :]
    for j in range(k - 1):
        acc = acc + pltpu.roll(x, k - 1 - j, axis=0) * w_ref[j:j + 1, :]
        acc_head = acc_head + _shift_rows(head, k - 1 - j) * w_ref[j:j + 1, :]
    return jnp.concatenate([acc_head, acc[SUBLANE:]], axis=0) + b


def _w_in_src_rows():
    n_ml = (C_LRU_X - C_ML_U) // PREP_TR
    n_lru = (C_MO_Q - C_LRU_X) // PREP_TR
    n_mo = (C_GATE - C_MO_Q) // PREP_TR
    return ([REF_LRU_END + i * PREP_TR for i in range(n_ml)] + [i * PREP_TR for i in range(n_lru)]
            + [REF_GATE_END + i * PREP_TR for i in range(n_mo)] + [REF_ML_END])


def _in_proj_kernel(x_ref, g_ref, w_hbm, o_ref, gt_ref, w_s, stage_s, sem, *, layer):
    @pl.when(pl.program_id(0) == 0)
    def _():
        src = _w_in_src_rows()

        def copy(k):
            return pltpu.make_async_copy(w_hbm.at[layer, pl.ds(src[k], PREP_TR), :], stage_s.at[k % 2], sem.at[k % 2])

        copy(0).start()
        for k in range(len(src)):
            if k + 1 < len(src):
                copy(k + 1).start()
            copy(k).wait()
            tile = stage_s[k % 2]
            if k == len(src) - 1:
                row = lax.broadcasted_iota(jnp.int32, tile.shape, 0)
                tile = jnp.where(row < 2 * ML_HEADS, tile, 0.0)
            w_s[k * PREP_TR:(k + 1) * PREP_TR, :] = tile.astype(BF16)

    xn = _rmsnorm_rows(x_ref[...], _layer_row(g_ref, layer)).astype(BF16)
    n_chunks = w_s.shape[0] // IN_TN
    for j in range(n_chunks):
        cols = pl.ds(j * IN_TN, IN_TN)
        res = lax.dot_general(xn, w_s[cols, :], NT_DIMS, preferred_element_type=F32)
        o_ref[:, cols] = res.astype(o_ref.dtype)
        if j == C_GATE // IN_TN:
            g0 = C_GATE - j * IN_TN
            gt_ref[...] = res[:, g0:g0 + LANE].T[:2 * ML_HEADS, :]


def _in_proj(x2d, g, w_in_t, l):
    m = x2d.shape[0]
    n = IN_COLS_P
    vmem = (2 * IN_TM * D_MODEL * 4 + 2 * IN_TM * D_MODEL * 2 + D_MODEL * n * 2 + 2 * PREP_TR * D_MODEL * 4
            + 2 * IN_TM * n * 2 + 2 * IN_TM * IN_TN * 4)
    return pl.pallas_call(
        functools.partial(_in_proj_kernel, layer=l),
        grid=(m // IN_TM,),
        in_specs=[
            pl.BlockSpec((IN_TM, D_MODEL), lambda i: (i, 0)),
            pl.BlockSpec(g.shape, lambda i: (0, 0)),
            pl.BlockSpec(memory_space=pl.ANY),
        ],
        out_specs=[pl.BlockSpec((IN_TM, n), lambda i: (i, 0)),
                   pl.BlockSpec((2 * ML_HEADS, IN_TM), lambda i: (0, i))],
        out_shape=[jax.ShapeDtypeStruct((m, n), BF16), jax.ShapeDtypeStruct((2 * ML_HEADS, m), F32)],
        scratch_shapes=[pltpu.VMEM((n, D_MODEL), BF16),
                        pltpu.VMEM((2, PREP_TR, D_MODEL), F32),
                        pltpu.SemaphoreType.DMA((2,))],
        compiler_params=pltpu.CompilerParams(
            dimension_semantics=("arbitrary",), vmem_limit_bytes=_vmem_limit(vmem)),
        name="in_proj",
    )(x2d, g, w_in_t)


def _lru_kernel(x_ref, z_ref, cw_ref, cb_ref, wax_ref, ba_ref, bx_ref, lam_ref, o_ref, a_s, u_s, *, layer):
    t = x_ref.shape[1]
    x = x_ref[0].astype(F32)
    xc = _causal_conv(x, cw_ref, _layer_row(cb_ref, layer))
    pre = jnp.dot(xc.astype(BF16), wax_ref[0], preferred_element_type=F32)
    r = _sigmoid(pre[:, :LRU_BW] + _layer_row(ba_ref, layer))
    i = _sigmoid(pre[:, LRU_BW:] + _layer_row(bx_ref, layer))
    log_a = (-LRU_C) * r * _softplus(-_layer_row(lam_ref, layer))
    a = jnp.exp(log_a)
    a_s[...] = a
    u_s[...] = jnp.sqrt(-jnp.tanh(log_a) * (1.0 + a * a)) * (i * xc)

    row = lax.broadcasted_iota(jnp.int32, (SUBLANE, LRU_BW), 0)

    def block(blk, h_prev):
        r0 = pl.multiple_of(blk * SUBLANE, SUBLANE)
        a_b = a_s[pl.ds(r0, SUBLANE), :]
        u_b = u_s[pl.ds(r0, SUBLANE), :]
        for s in (1, 2, 4):
            a_sh = jnp.where(row >= s, pltpu.roll(a_b, s, axis=0), 1.0)
            u_sh = jnp.where(row >= s, pltpu.roll(u_b, s, axis=0), 0.0)
            u_b = a_b * u_sh + u_b
            a_b = a_b * a_sh
        u_s[pl.ds(r0, SUBLANE), :] = a_b * h_prev + u_b
        last = lambda v: jnp.broadcast_to(v[SUBLANE - 1:SUBLANE, :], (SUBLANE, LRU_BW))
        return last(a_b) * h_prev + last(u_b)

    lax.fori_loop(0, t // SUBLANE, block, jnp.zeros((SUBLANE, LRU_BW), F32), unroll=4)
    z = z_ref[0].astype(F32)
    o_ref[0] = (u_s[...] * _silu(z)).astype(o_ref.dtype)


def _lru_branch(p, cw, wax, cb, ba, bx, lam, l):
    b, t, _ = p.shape
    xb, zb = C_LRU_X // LRU_BW, C_LRU_Z // LRU_BW
    vec = pl.BlockSpec((cb.shape[0], LRU_BW), lambda bi, g: (0, g))
    return pl.pallas_call(
        functools.partial(_lru_kernel, layer=l),
        grid=(b, LRU_BLOCKS),
        in_specs=[
            pl.BlockSpec((1, t, LRU_BW), lambda bi, g: (bi, 0, xb + g)),
            pl.BlockSpec((1, t, LRU_BW), lambda bi, g: (bi, 0, zb + g)),
            pl.BlockSpec((None, cw.shape[1], LRU_BW), lambda bi, g: (l, 0, g)),
            vec,
            pl.BlockSpec((None, 1, LRU_BW, 2 * LRU_BW), lambda bi, g: (l, g, 0, 0)),
            vec, vec, vec,
        ],
        out_specs=pl.BlockSpec((1, t, LRU_BW), lambda bi, g: (bi, 0, g)),
        out_shape=jax.ShapeDtypeStruct((b, t, LRU_W), BF16),
        scratch_shapes=[pltpu.VMEM((t, LRU_BW), F32), pltpu.VMEM((t, LRU_BW), F32)],
        compiler_params=pltpu.CompilerParams(
            dimension_semantics=("arbitrary", "arbitrary"), vmem_limit_bytes=_vmem_limit(V7X_VMEM_BYTES)),
        name="rg_lru",
    )(p, p, cw, cb, wax, ba, bx, lam)


def _blockdiag_in_out(w_ref, hh):
    wh = w_ref[...][:, hh * ML_DH:(hh + 1) * ML_DH]
    wh = jnp.concatenate([wh, jnp.zeros((ML_QKV_BLOCK, ML_DHP - ML_DH), F32)], axis=1)
    r = lax.broadcasted_iota(jnp.int32, (ML_PW, ML_DHP), 0)
    c = lax.broadcasted_iota(jnp.int32, (ML_PW, ML_DHP), 1)
    d = jnp.zeros((ML_PW, ML_DHP), F32)
    for i in range(ML_QKV_BLOCK):
        d = jnp.where((r & (ML_QKV_BLOCK - 1)) == i, wh[i:i + 1, :], d)
    keep = (((r >> ML_QKV_SHIFT) - hh * (ML_DH // ML_QKV_BLOCK)) == (c >> ML_QKV_SHIFT)) & (c < ML_DH)
    return jnp.where(keep, d, 0.0)


def _blockdiag_in_window(w_ref, hh):
    w0 = ML_WINDOW[hh]
    wh = w_ref[...][:, w0:w0 + ML_DHP]
    r = lax.broadcasted_iota(jnp.int32, (ML_DHP, ML_DHP), 0) + w0
    c = lax.broadcasted_iota(jnp.int32, (ML_DHP, ML_DHP), 1) + w0
    d = jnp.zeros((ML_DHP, ML_DHP), F32)
    for i in range(ML_QKV_BLOCK):
        d = jnp.where((r & (ML_QKV_BLOCK - 1)) == i, wh[i:i + 1, :], d)
    keep = ((r >> ML_QKV_SHIFT) == (c >> ML_QKV_SHIFT)) & (c >= hh * ML_DH) & (c < (hh + 1) * ML_DH)
    return jnp.where(keep, d, 0.0)


def _blockdiag_out_in(w_ref, hh):
    wk = w_ref[...]
    o = lax.broadcasted_iota(jnp.int32, (ML_DHP, ML_PW), 0)
    r = lax.broadcasted_iota(jnp.int32, (ML_DHP, ML_PW), 1)
    d = jnp.zeros((ML_DHP, ML_PW), F32)
    for j in range(ML_QKV_BLOCK):
        d = jnp.where((o & (ML_QKV_BLOCK - 1)) == j, wk[j:j + 1, :], d)
    keep = ((o >> ML_QKV_SHIFT) == ((r >> ML_QKV_SHIFT) - hh * (ML_DH // ML_QKV_BLOCK))) & (o < ML_DH)
    return jnp.where(keep, d, 0.0)


ML_WINDOW = (0, ML_PW - ML_DHP)
ML_DEN_LANE = (ML_DH + 2, 2)


def _mlstm_kernel(u_ref, og_ref, z_ref, ig_ref, fg_ref, bi_ref, bf_ref, cw_ref, cb_ref,
                  wq_ref, wkt_ref, wv_ref, ng_ref, y_ref,
                  dq_s, dkt_s, dv_s, q_s, kt_s, v_s, cp_s, c_s, r_s, w_s, col_s, so_s, sn_s, loca_s, locb_s, *, layer):
    t = u_ref.shape[1]
    nc = t // ML_CHUNK
    L = ML_CHUNK
    heads = range(2)

    @pl.when(pl.program_id(1) == 0)
    def _():
        for hh in heads:
            dq_s[hh] = _blockdiag_in_out(wq_ref, hh).astype(BF16)
            dkt_s[hh] = _blockdiag_out_in(wkt_ref, hh).astype(BF16)
            dv_s[hh] = _blockdiag_in_window(wv_ref, hh).astype(BF16)

    u = u_ref[0]
    uc = _silu(_causal_conv(u.astype(F32), cw_ref, _layer_row(cb_ref, layer))).astype(BF16)
    for hh in heads:
        w0 = ML_WINDOW[hh]
        uc_w, u_w = uc[:, w0:w0 + ML_DHP], u[:, w0:w0 + ML_DHP]
        q_s[hh] = jnp.dot(uc_w, dq_s[hh, w0:w0 + ML_DHP, :], preferred_element_type=F32).astype(BF16)
        kt = lax.dot_general(dkt_s[hh, :, w0:w0 + ML_DHP], uc_w, NT_DIMS, preferred_element_type=F32)
        kt = kt * (ML_DH ** -0.5)
        for c in range(nc):
            kt_s[hh, c] = kt[:, c * L:(c + 1) * L].astype(BF16)
        v = jnp.dot(u_w, dv_s[hh], preferred_element_type=F32)
        vlane = lax.broadcasted_iota(jnp.int32, v.shape, 1)
        v_s[hh] = jnp.where(vlane == ML_DEN_LANE[hh], 1.0, v).astype(BF16)

    glane = lax.broadcasted_iota(jnp.int32, (nc, L), 1)
    grow = lax.broadcasted_iota(jnp.int32, (nc, L), 0)
    row8 = lax.broadcasted_iota(jnp.int32, (SUBLANE, L), 0)
    for hh in heads:
        ig = ig_ref[hh, 0] + bi_ref[hh]
        lf = -_softplus(-(fg_ref[hh, 0] + bf_ref[hh]))
        b = lf
        for k in range(int(math.log2(L))):
            sh = 1 << k
            b = b + jnp.where(glane >= sh, pltpu.roll(b, sh, axis=1), 0.0)
        g = jnp.broadcast_to(b[:, L - 1:L], (nc, L))
        a = g - b + ig
        mloc = jnp.broadcast_to(jnp.max(a, axis=1, keepdims=True), (nc, L))
        m = jnp.zeros((1, L), F32)
        m_prev = jnp.zeros((nc, L), F32)
        m_next = jnp.zeros((nc, L), F32)
        for c in range(nc):
            m_prev = jnp.where(grow == c, m, m_prev)
            m = jnp.maximum(g[c:c + 1, :] + m, mloc[c:c + 1, :])
            m_next = jnp.where(grow == c, m, m_next)
        r = ig - b
        cmx = r
        for k in range(int(math.log2(L))):
            sh = 1 << k
            cmx = jnp.maximum(cmx, jnp.where(glane >= sh, pltpu.roll(cmx, sh, axis=1), -jnp.inf))
        mm = jnp.maximum(m_prev, cmx)
        s_int = jnp.exp(m_prev - mm)
        clamp = jnp.exp(-(b + mm))
        r_s[hh] = r
        w_s[hh] = jnp.exp(a - mloc)
        so_s[hh] = jnp.exp(g + m_prev - m_next)
        sn_s[hh] = jnp.exp(mloc - m_next)
        for c in range(nc):
            col_s[hh, c] = jnp.where(row8 == 0, mm[c:c + 1, :],
                                     jnp.where(row8 == 1, s_int[c:c + 1, :],
                                               jnp.where(row8 == 2, clamp[c:c + 1, :], 0.0)))

    zero_rows = jnp.zeros((ML_DHP - ML_DH, ML_DHP), BF16)
    c_s[...] = jnp.zeros(c_s.shape, F32)

    n_pairs = nc // 2

    def local_states(pair, loc_ref):
        for i in range(2):
            c = 2 * pair + i
            r0 = pl.multiple_of(c * L, L)
            for hh in heads:
                ktw = (kt_s[hh, c, :ML_DH, :].astype(F32) * w_s[hh, pl.ds(c, 1), :]).astype(BF16)
                loc_ref[i, hh] = jnp.dot(ktw, v_s[hh, pl.ds(r0, L), :], preferred_element_type=F32)

    def advance(pair, loc_ref):
        for i in range(2):
            c = 2 * pair + i
            for hh in heads:
                c_prev = c_s[hh]
                cp_s[hh, c] = jnp.concatenate([c_prev.astype(BF16), zero_rows], axis=0)
                c_s[hh] = (so_s[hh, pl.ds(c, 1), :][:, :1] * c_prev
                           + sn_s[hh, pl.ds(c, 1), :][:, :1] * loc_ref[i, hh])

    def state(k, carry):
        local_states(2 * k + 1, locb_s)
        advance(2 * k, loca_s)
        local_states(jnp.minimum(2 * k + 2, n_pairs - 1), loca_s)
        advance(2 * k + 1, locb_s)
        return carry

    local_states(0, loca_s)
    lax.fori_loop(0, n_pairs // 2, state, 0)

    tri = (lax.broadcasted_iota(jnp.int32, (L, L), 0) >= lax.broadcasted_iota(jnp.int32, (L, L), 1))
    first = lax.broadcasted_iota(jnp.int32, (L, ML_PW), 1) < ML_DH

    def chunk_rows(c):
        return pl.ds(pl.multiple_of(c * L, L), L)

    def raw_scores(c):
        rows = chunk_rows(c)
        return [jnp.dot(q_s[hh, rows, :], kt_s[hh, c], preferred_element_type=F32) for hh in heads]

    def weights(c, raw):
        out = []
        for hh in heads:
            c0 = ML_DEN_LANE[hh] % LANE - 2
            pads = [jnp.zeros((n, L), F32) for n in (c0, L - SUBLANE - c0)]
            cols = jnp.concatenate([a for a in (pads[0], col_s[hh, c], pads[1]) if a.shape[0]], axis=0).T
            rb = jnp.broadcast_to(r_s[hh, pl.ds(c, 1), :], (L, L))
            decay = jnp.exp(jnp.where(tri, rb - cols[:, c0:c0 + 1], -jnp.inf))
            out.append(((raw[hh] * decay).astype(BF16), cols[:, c0 + 1:c0 + 2], cols[:, c0 + 2:c0 + 3]))
        return out

    def numerators(c, wts):
        rows = chunk_rows(c)
        nds, invs = [], []
        for hh, (s_mat, s_int, clamp) in zip(heads, wts):
            nd = (jnp.dot(s_mat, v_s[hh, rows, :], preferred_element_type=F32)
                  + s_int * jnp.dot(q_s[hh, rows, :], cp_s[hh, c], preferred_element_type=F32))
            den = nd[:, ML_DEN_LANE[hh]:ML_DEN_LANE[hh] + 1]
            nds.append(nd)
            invs.append(1.0 / jnp.maximum(jnp.abs(den), clamp))
        w1 = ML_WINDOW[1]
        shared_first = lax.broadcasted_iota(jnp.int32, (L, ML_DHP - w1), 1) < ML_DH - w1
        shared = jnp.where(shared_first, nds[0][:, w1:], nds[1][:, :ML_DHP - w1])
        return jnp.concatenate([nds[0][:, :w1], shared, nds[1][:, ML_DHP - w1:]], axis=1), invs

    def gates(c):
        rows = chunk_rows(c)
        return (_sigmoid(og_ref[0, rows, :].astype(F32)),
                _silu(z_ref[0, rows, :].astype(F32)) * _layer_row(ng_ref, layer))

    def means(nd, og):
        x = og * nd
        mu0 = jnp.sum(jnp.where(first, x, 0.0), axis=1, keepdims=True) * (1.0 / ML_DH)
        mu1 = jnp.sum(jnp.where(first, 0.0, x), axis=1, keepdims=True) * (1.0 / ML_DH)
        return x, mu0, mu1

    def variances(x, mu0, mu1):
        dev = x - jnp.where(first, mu0, mu1)
        sq = dev * dev
        var0 = jnp.sum(jnp.where(first, sq, 0.0), axis=1, keepdims=True) * (1.0 / ML_DH)
        var1 = jnp.sum(jnp.where(first, 0.0, sq), axis=1, keepdims=True) * (1.0 / ML_DH)
        return dev, var0, var1

    def finish(c, dev, var0, var1, invs, zs):
        f0 = invs[0] * lax.rsqrt(invs[0] * invs[0] * var0 + LN_EPS)
        f1 = invs[1] * lax.rsqrt(invs[1] * invs[1] * var1 + LN_EPS)
        y_ref[0, chunk_rows(c), :] = (dev * jnp.where(first, f0, f1) * zs).astype(y_ref.dtype)

    def group(gi, carry):
        cs = [gi * ML_GROUP + i for i in range(ML_GROUP)]
        n = range(ML_GROUP)
        raws = [raw_scores(c) for c in cs]
        gts = [gates(c) for c in cs]
        wts = [weights(c, raws[i]) for i, c in enumerate(cs)]
        nums = [numerators(c, wts[i]) for i, c in enumerate(cs)]
        mus = [means(nums[i][0], gts[i][0]) for i in n]
        vrs = [variances(*mus[i]) for i in n]
        for i, c in enumerate(cs):
            finish(c, *vrs[i], nums[i][1], gts[i][1])
        return carry

    lax.fori_loop(0, nc // ML_GROUP, group, 0)


def _mlstm_branch(p, gates, gbias, cw, cb, wq, wkt, wv, ng, l):
    b, t, _ = p.shape
    nc = t // ML_CHUNK
    ub, ob, zb = C_ML_U // ML_PW, C_ML_O // ML_PW, C_ML_Z // ML_PW
    seq = lambda base: pl.BlockSpec((1, t, ML_PW), lambda pr, bi: (bi, 0, base + pr))
    vec = pl.BlockSpec((cb.shape[0], ML_PW), lambda pr, bi: (0, pr))
    taps = pl.BlockSpec((None, ML_QKV_BLOCK, ML_PW), lambda pr, bi: (l, 0, pr))
    gate_rows = lambda: pltpu.VMEM((2, nc, ML_CHUNK), F32)
    pair_states = lambda: pltpu.VMEM((2, 2, ML_DH, ML_DHP), F32)
    return pl.pallas_call(
        functools.partial(_mlstm_kernel, layer=l),
        grid=(ML_PAIRS, b),
        in_specs=[
            seq(ub), seq(ob), seq(zb),
            pl.BlockSpec((2, 1, nc, ML_CHUNK), lambda pr, bi: (pr, bi, 0, 0)),
            pl.BlockSpec((2, 1, nc, ML_CHUNK), lambda pr, bi: (ML_PAIRS + pr, bi, 0, 0)),
            pl.BlockSpec((None, 2, 1, ML_CHUNK), lambda pr, bi: (l, pr, 0, 0)),
            pl.BlockSpec((None, 2, 1, ML_CHUNK), lambda pr, bi: (l, ML_PAIRS + pr, 0, 0)),
            pl.BlockSpec((None, cw.shape[1], ML_PW), lambda pr, bi: (l, 0, pr)),
            vec, taps, taps, taps, vec,
        ],
        out_specs=pl.BlockSpec((1, t, ML_PW), lambda pr, bi: (bi, 0, pr)),
        out_shape=jax.ShapeDtypeStruct((b, t, ML_W), BF16),
        scratch_shapes=[
            pltpu.VMEM((2, ML_PW, ML_DHP), BF16),
            pltpu.VMEM((2, ML_DHP, ML_PW), BF16),
            pltpu.VMEM((2, ML_DHP, ML_DHP), BF16),
            pltpu.VMEM((2, t, ML_DHP), BF16),
            pltpu.VMEM((2, nc, ML_DHP, ML_CHUNK), BF16),
            pltpu.VMEM((2, t, ML_DHP), BF16),
            pltpu.VMEM((2, nc, ML_DHP, ML_DHP), BF16),
            pltpu.VMEM((2, ML_DH, ML_DHP), F32),
            gate_rows(), gate_rows(),
            pltpu.VMEM((2, nc, SUBLANE, ML_CHUNK), F32),
            gate_rows(), gate_rows(),
            pair_states(), pair_states(),
        ],
        compiler_params=pltpu.CompilerParams(
            dimension_semantics=("arbitrary", "arbitrary"), vmem_limit_bytes=_vmem_limit(56 << 20)),
        name="mlstm",
    )(p, p, p, gates, gates, gbias, gbias, cw, cb, wq, wkt, wv, ng)


MOBA_SEL_LANE = 6
MOBA_AHEAD = 3


def _moba_key_ext(nb):
    ext = np.zeros((nb, MOBA_BLOCK, MO_DH), np.float32)
    for n in range(nb):
        ext[n, :, 0:3] = n * MOBA_BLOCK
        ext[n, :, 3:6] = np.arange(MOBA_BLOCK, dtype=np.float32)[:, None]
        ext[n, :, MOBA_SEL_LANE + n] = 1.0
    return jnp.asarray(ext, dtype=BF16)


def _moba_slope_rows(slopes):
    rows = np.zeros((len(slopes), 1, MO_DH), np.float32)
    for h, s in enumerate(slopes):
        rest = np.float32(np.float32(s) * np.float32(LOG2E))
        for i in range(3):
            piece = np.float32(rest).astype(BF16).astype(np.float32)
            rows[h, 0, i] = rows[h, 0, 3 + i] = piece
            rest = np.float32(rest - piece)
    return jnp.asarray(rows)


def _moba_kernel(q_ref, k_ref, v_ref, z_ref, srow_ref, kext_ref, o_ref, l_s, qa_s):
    t = q_ref.shape[1]
    nb = t // MOBA_BLOCK
    bs = MOBA_BLOCK
    qscale = (MO_DH ** -0.5) * LOG2E

    krow = lax.broadcasted_iota(jnp.int32, (LANE, MO_DH), 0)
    kmean = jnp.zeros((LANE, MO_DH), F32)
    for n in range(nb):
        mean_n = jnp.sum(k_ref[0, n * bs:(n + 1) * bs, :].astype(F32), axis=0, keepdims=True) * (1.0 / bs)
        kmean = jnp.where(krow == MOBA_SEL_LANE + n, mean_n, kmean)
    kmean = kmean.astype(BF16)

    lane = lax.broadcasted_iota(jnp.int32, (bs, MO_DH), 1)
    causal = (lax.broadcasted_iota(jnp.int32, (bs, bs), 0) >= lax.broadcasted_iota(jnp.int32, (bs, bs), 1))
    ones_col = jnp.where(lane == 0, 1.0, 0.0).astype(BF16)
    slope_cols = jnp.where(lane < MOBA_SEL_LANE, srow_ref[0], 0.0)

    def prepare(qb):
        q_b = q_ref[0, qb * bs:(qb + 1) * bs, :]
        q_ext = slope_cols
        if qb > MOBA_TOPK:
            gate = lax.dot_general(q_b, kmean, NT_DIMS, preferred_element_type=F32)
            beaten = jnp.zeros((bs, MO_DH), F32)
            for m in range(qb):
                gm = gate[:, MOBA_SEL_LANE + m:MOBA_SEL_LANE + m + 1]
                wins = (gm > gate) | ((gm == gate) & (lane > MOBA_SEL_LANE + m))
                beaten = beaten + jnp.where(wins, 1.0, 0.0)
            past = (lane >= MOBA_SEL_LANE) & (lane < MOBA_SEL_LANE + qb)
            q_ext = jnp.where(past & (beaten >= float(MOBA_TOPK)), NEG_INF, slope_cols)
        qa_s[qb] = jnp.concatenate([(q_b.astype(F32) * qscale).astype(BF16), q_ext.astype(BF16)], axis=1)

    def scores(qb):
        q_aug = qa_s[qb]
        l_q = l_s.at[qb % l_s.shape[0]]
        mx = None
        for n in range(qb + 1):
            k_aug = jnp.concatenate([k_ref[0, n * bs:(n + 1) * bs, :], kext_ref[n]], axis=1)
            logit = lax.dot_general(q_aug, k_aug, NT_DIMS, preferred_element_type=F32)
            if n == qb:
                logit = jnp.where(causal, logit, NEG_INF)
            l_q[:, n * bs:(n + 1) * bs] = logit
            half = jnp.maximum(logit[:, :LANE], logit[:, LANE:])
            mx = half if mx is None else jnp.maximum(mx, half)
        return jnp.max(mx, axis=1, keepdims=True)

    def outputs(qb, m_row):
        l_q = l_s.at[qb % l_s.shape[0]]
        acc = jnp.zeros((bs, 2 * MO_DH), F32)
        for n in range(qb + 1):
            pr = jnp.exp2(l_q[:, n * bs:(n + 1) * bs] - m_row).astype(BF16)
            v_aug = jnp.concatenate([v_ref[0, n * bs:(n + 1) * bs, :], ones_col], axis=1)
            acc = acc + jnp.dot(pr, v_aug, preferred_element_type=F32)
        z = z_ref[0, qb * bs:(qb + 1) * bs, :].astype(F32)
        inv = 1.0 / acc[:, MO_DH:MO_DH + 1]
        o_ref[0, qb * bs:(qb + 1) * bs, :] = (acc[:, :MO_DH] * (inv * _silu(z))).astype(o_ref.dtype)

    ahead = l_s.shape[0] - 1
    for qb in range(min(ahead + 1, nb)):
        prepare(qb)
    m_rows = {qb: scores(qb) for qb in range(min(ahead, nb))}
    for qb in range(nb):
        if qb + ahead + 1 < nb:
            prepare(qb + ahead + 1)
        if qb + ahead < nb:
            m_rows[qb + ahead] = scores(qb + ahead)
        outputs(qb, m_rows.pop(qb))


def _moba_branch(p, srows, kext):
    b, t, _ = p.shape
    nb = t // MOBA_BLOCK
    qb, kb, vb, zb = (c // MO_DH for c in (C_MO_Q, C_MO_K, C_MO_V, C_MO_Z))
    seq = lambda base: pl.BlockSpec((1, t, MO_DH), lambda bi, h: (bi, 0, base + h))
    return pl.pallas_call(
        _moba_kernel,
        grid=(b, MO_HEADS),
        in_specs=[seq(qb), seq(kb), seq(vb), seq(zb),
                  pl.BlockSpec((1, 1, MO_DH), lambda bi, h: (h, 0, 0)),
                  pl.BlockSpec((nb, MOBA_BLOCK, MO_DH), lambda bi, h: (0, 0, 0))],
        out_specs=pl.BlockSpec((1, t, MO_DH), lambda bi, h: (bi, 0, h)),
        out_shape=jax.ShapeDtypeStruct((b, t, MO_W), BF16),
        scratch_shapes=[pltpu.VMEM((MOBA_AHEAD + 1, MOBA_BLOCK, t), F32),
                        pltpu.VMEM((nb, MOBA_BLOCK, 2 * MO_DH), BF16)],
        compiler_params=pltpu.CompilerParams(
            dimension_semantics=("arbitrary", "arbitrary"), vmem_limit_bytes=_vmem_limit(V7X_VMEM_BYTES)),
        name="moba",
    )(p, p, p, p, srows, kext)


def _stage_weights(w_hbm, layer, dst_ref, stage_ref, sem):
    tile_rows = stage_ref.shape[1]
    n_tiles = dst_ref.shape[0] // tile_rows

    def copy(k):
        return pltpu.make_async_copy(w_hbm.at[layer, pl.ds(k * tile_rows, tile_rows), :], stage_ref.at[k % 2],
                                     sem.at[k % 2])

    copy(0).start()
    for k in range(n_tiles):
        if k + 1 < n_tiles:
            copy(k + 1).start()
        copy(k).wait()
        dst_ref[k * tile_rows:(k + 1) * tile_rows, :] = stage_ref[k % 2].astype(BF16)


def _mem_kv_kernel(mem_ref, g_ref, w_hbm, o_ref, w_s, stage_s, sem):
    layer = pl.program_id(0)
    _stage_weights(w_hbm, layer, w_s, stage_s, sem)
    g = g_ref[pl.ds(layer, 1), :]
    for bi in range(mem_ref.shape[0]):
        hm = _rmsnorm_rows(mem_ref[bi], g).astype(BF16)
        o_ref[bi] = jnp.dot(hm, w_s[...], preferred_element_type=F32).astype(o_ref.dtype)


def _mem_kv(mem, g, wkv):
    b, m, _ = mem.shape
    depth = wkv.shape[0]
    return pl.pallas_call(
        _mem_kv_kernel,
        grid=(depth,),
        in_specs=[pl.BlockSpec((b, m, D_MODEL), lambda l: (0, 0, 0)),
                  pl.BlockSpec(g.shape, lambda l: (0, 0)),
                  pl.BlockSpec(memory_space=pl.ANY)],
        out_specs=pl.BlockSpec((None, b, m, 2 * XA_W), lambda l: (l, 0, 0, 0)),
        out_shape=jax.ShapeDtypeStruct((depth, b, m, 2 * XA_W), BF16),
        scratch_shapes=[pltpu.VMEM((D_MODEL, 2 * XA_W), BF16),
                        pltpu.VMEM((2, STAGE_ROWS, 2 * XA_W), F32),
                        pltpu.SemaphoreType.DMA((2,))],
        compiler_params=pltpu.CompilerParams(
            dimension_semantics=("arbitrary",), vmem_limit_bytes=_vmem_limit(V7X_VMEM_BYTES)),
        name="mem_kv",
    )(mem, g, wkv)


def _post_kernel(x_ref, yl_ref, ym_ref, yo_ref, wout_hbm, g_ref, wq_hbm, kv_ref, wo_hbm, fg_ref, o_ref,
                 x1_s, wout_ref, wq_ref, wo_ref, stage_s, stage_q_s, sem, sem_q, *, final_norm, layer):
    @pl.when((pl.program_id(0) == 0) & (pl.program_id(1) == 0))
    def _():
        _stage_weights(wout_hbm, layer, wout_ref, stage_s, sem)
        _stage_weights(wo_hbm, layer, wo_ref, stage_s, sem)
        _stage_weights(wq_hbm, layer, wq_ref, stage_q_s, sem_q)

    sub = x_ref.shape[1] // POST_SUB
    groups = [pl.ds(i * sub, sub) for i in range(POST_SUB)]

    for r in groups:
        acc = jnp.dot(yl_ref[0, r, :], wout_ref[0:LRU_W, :], preferred_element_type=F32)
        acc = acc + jnp.dot(ym_ref[0, r, :], wout_ref[LRU_W:LRU_W + ML_W, :], preferred_element_type=F32)
        acc = acc + jnp.dot(yo_ref[0, r, :], wout_ref[LRU_W + ML_W:, :], preferred_element_type=F32)
        x1_s[r, :] = x_ref[0, r, :] + acc
    qs = []
    for r in groups:
        hx = _rmsnorm_rows(x1_s[r, :], _layer_row(g_ref, layer)).astype(BF16)
        qs.append(jnp.dot(hx, wq_ref[...], preferred_element_type=F32).astype(BF16))
    scores = []
    for q in qs:
        scores.append([
            lax.dot_general(q[:, h * XA_DH:(h + 1) * XA_DH], kv_ref[0, :, h * XA_DH:(h + 1) * XA_DH], NT_DIMS,
                            preferred_element_type=F32) * (XA_DH ** -0.5)
            for h in range(XA_HEADS)])
    outs = []
    for sc in scores:
        heads = []
        for h, s in enumerate(sc):
            e = jnp.exp(s - jnp.max(s, axis=1, keepdims=True))
            pr = e / jnp.sum(e, axis=1, keepdims=True)
            v_h = kv_ref[0, :, XA_W + h * XA_DH:XA_W + (h + 1) * XA_DH]
            heads.append(jnp.dot(pr.astype(BF16), v_h, preferred_element_type=F32).astype(BF16))
        outs.append(jnp.concatenate(heads, axis=1))
    for r, o in zip(groups, outs):
        y = x1_s[r, :] + jnp.dot(o, wo_ref[...], preferred_element_type=F32)
        if final_norm:
            y = _rmsnorm_rows(y, fg_ref[...])
        o_ref[0, r, :] = y


def _post_mix(x, yl, ym, yo, wout, g, wq, kv, wo, fg, l, final_norm):
    b, t, _ = x.shape
    m = kv.shape[2]
    rows = lambda width: pl.BlockSpec((1, POST_TM, width), lambda bi, i: (bi, i, 0))
    hbm = pl.BlockSpec(memory_space=pl.ANY)
    weights = (D_MODEL * D_MODEL + D_MODEL * XA_W + XA_W * D_MODEL) * 2
    stages = 2 * STAGE_ROWS * (D_MODEL + XA_W) * 4
    tiles = 2 * POST_TM * (2 * D_MODEL * 4 + D_MODEL * 2) + 2 * m * 2 * XA_W * 2
    vmem = weights + stages + tiles + 6 * POST_TM * D_MODEL * 4
    return pl.pallas_call(
        functools.partial(_post_kernel, final_norm=final_norm, layer=l),
        grid=(b, t // POST_TM),
        in_specs=[rows(D_MODEL), rows(LRU_W), rows(ML_W), rows(MO_W),
                  hbm,
                  pl.BlockSpec(g.shape, lambda bi, i: (0, 0)),
                  hbm,
                  pl.BlockSpec((None, 1, m, 2 * XA_W), lambda bi, i: (l, bi, 0, 0)),
                  hbm,
                  pl.BlockSpec((1, D_MODEL), lambda bi, i: (0, 0))],
        out_specs=rows(D_MODEL),
        out_shape=jax.ShapeDtypeStruct((b, t, D_MODEL), F32),
        scratch_shapes=[pltpu.VMEM((POST_TM, D_MODEL), F32),
                        pltpu.VMEM((D_MODEL, D_MODEL), BF16),
                        pltpu.VMEM((D_MODEL, XA_W), BF16),
                        pltpu.VMEM((XA_W, D_MODEL), BF16),
                        pltpu.VMEM((2, STAGE_ROWS, D_MODEL), F32),
                        pltpu.VMEM((2, STAGE_ROWS, XA_W), F32),
                        pltpu.SemaphoreType.DMA((2,)),
                        pltpu.SemaphoreType.DMA((2,))],
        compiler_params=pltpu.CompilerParams(
            dimension_semantics=("arbitrary", "arbitrary"), vmem_limit_bytes=_vmem_limit(vmem)),
        name="post_mix",
    )(x, yl, ym, yo, wout, g, wq, kv, wo, fg)


def _alibi_slopes(n):
    def pow2(m):
        start = 2.0 ** (-8.0 / m)
        return [start ** (i + 1) for i in range(m)]
    if math.log2(n).is_integer():
        s = pow2(n)
    else:
        c = 2 ** int(math.floor(math.log2(n)))
        s = pow2(c) + pow2(2 * c)[0::2][:n - c]
    return np.asarray(s, dtype=np.float32)


def _taps(w, perm):
    depth = w.shape[0]
    return jnp.transpose(w, perm).reshape(depth, ML_QKV_BLOCK, ML_W)


def kernel(x, mem, mix_norm_g, w_in, lru_conv_w, lru_conv_b, lru_wa, lru_ba, lru_wx, lru_bx, lru_lambda,
           ml_conv_w, ml_conv_b, ml_wq, ml_wk, ml_wv, ml_bi, ml_bf, ml_norm_g, w_out, xa_norm_g, mem_norm_g,
           xa_wq, xa_wkv, xa_wo, final_norm_g):
    b, t, d = x.shape
    depth = w_in.shape[0]
    nc = t // ML_CHUNK
    moba_srows = _moba_slope_rows(_alibi_slopes(MO_HEADS))
    moba_kext = _moba_key_ext(t // MOBA_BLOCK)
    w_in_t = jnp.swapaxes(w_in, 1, 2)
    lru_wax = jnp.concatenate([lru_wa, lru_wx], axis=-1).astype(BF16)
    ml_wq_t = _taps(ml_wq, (0, 2, 1, 3))
    ml_wkt_t = _taps(ml_wk, (0, 3, 1, 2))
    ml_wv_t = _taps(ml_wv, (0, 2, 1, 3))
    ml_gbias = jnp.broadcast_to(jnp.concatenate([ml_bi, ml_bf], axis=1)[:, :, None, None],
                                (depth, 2 * ML_HEADS, 1, ML_CHUNK))
    kv = _mem_kv(mem, mem_norm_g, xa_wkv)
    for l in range(depth):
        p, gates = _in_proj(x.reshape(b * t, d), mix_norm_g, w_in_t, l)
        p, gates = p.reshape(b, t, IN_COLS_P), gates.reshape(2 * ML_HEADS, b, nc, ML_CHUNK)
        y_lru = _lru_branch(p, lru_conv_w, lru_wax, lru_conv_b, lru_ba, lru_bx, lru_lambda, l)
        y_ml = _mlstm_branch(p, gates, ml_gbias, ml_conv_w, ml_conv_b, ml_wq_t, ml_wkt_t, ml_wv_t, ml_norm_g, l)
        y_mo = _moba_branch(p, moba_srows, moba_kext)
        x = _post_mix(x, y_lru, y_ml, y_mo, w_out, xa_norm_g, xa_wq, kv, xa_wo, final_norm_g[None, :], l,
                      final_norm=(l == depth - 1))
    return x
```

```python
import functools
import math

import jax
import jax.numpy as jnp
import numpy as np
from jax import lax
from jax.experimental import pallas as pl
from jax.experimental.pallas import tpu as pltpu

LANE = 128
SUBLANE = 8
V7X_VMEM_BYTES = 64 * 1024 * 1024

D_MODEL = 2048
LRU_W = 512
LRU_BLOCKS = 4
LRU_BW = LRU_W // LRU_BLOCKS
LRU_C = 8.0
ML_W = 768
ML_HEADS = 4
ML_DH = 192
ML_DHP = 256
ML_PAIRS = 2
ML_PW = 2 * ML_DH
ML_QKV_BLOCK = 4
ML_QKV_SHIFT = 2
ML_CHUNK = 128
ML_GROUP = 4
MO_W = 768
MO_HEADS = 6
MO_DH = 128
MOBA_BLOCK = 256
MOBA_TOPK = 3
XA_HEADS = 4
XA_DH = 128
XA_W = XA_HEADS * XA_DH
RMS_EPS = 1e-6
LN_EPS = 1e-5
NEG_INF = -1e30
LOG2E = 1.4426950408889634

REF_LRU_END = 2 * LRU_W
REF_ML_END = REF_LRU_END + 3 * ML_W
REF_GATE_END = REF_ML_END + 2 * ML_HEADS

C_ML_U = 0
C_ML_O = C_ML_U + ML_W
C_ML_Z = C_ML_O + ML_W
C_LRU_X = C_ML_Z + ML_W
C_LRU_Z = C_LRU_X + LRU_W
C_MO_Q = C_LRU_Z + LRU_W
C_MO_K = C_MO_Q + MO_W
C_MO_V = C_MO_K + MO_W
C_MO_Z = C_MO_V + MO_W
C_GATE = C_MO_Z + MO_W
IN_TM = 512
IN_TN = 3328
IN_COLS_P = 6656
PREP_TR = 256
STAGE_ROWS = 256
MEM_KV_SLOTS = 4
POST_TM = 512
POST_SUB = 2

BF16 = jnp.bfloat16
F32 = jnp.float32
NT_DIMS = (((1,), (1,)), ((), ()))


def _vmem_limit(nbytes):
    return int(min(V7X_VMEM_BYTES - (4 << 20), max(32 << 20, nbytes)))


def _rmsnorm_rows(x, g):
    ms = jnp.mean(x * x, axis=-1, keepdims=True)
    return x * lax.rsqrt(ms + RMS_EPS) * g


def _sigmoid(x):
    return jax.nn.sigmoid(x)


def _silu(x):
    return x * jax.nn.sigmoid(x)


def _softplus(x):
    return jnp.maximum(x, 0.0) + jnp.log1p(jnp.exp(-jnp.abs(x)))


def _shift_rows(x, s):
    rolled = pltpu.roll(x, s, axis=0)
    row = lax.broadcasted_iota(jnp.int32, x.shape, 0)
    return jnp.where(row >= s, rolled, 0.0)


def _layer_row(ref, layer):
    return ref[layer:layer + 1, :]


def _causal_conv(x, w_ref, b):
    k = w_ref.shape[0]
    acc = x * w_ref[k - 1:k, :]
    head = x[:SUBLANE]
    acc_head = head * w_ref[k - 1:k, :]
    for j in range(k - 1):
        acc = acc + pltpu.roll(x, k - 1 - j, axis=0) * w_ref[j:j + 1, :]
        acc_head = acc_head + _shift_rows(head, k - 1 - j) * w_ref[j:j + 1, :]
    return jnp.concatenate([acc_head, acc[SUBLANE:]], axis=0) + b


def _w_in_src_rows():
    n_ml = (C_LRU_X - C_ML_U) // PREP_TR
    n_lru = (C_MO_Q - C_LRU_X) // PREP_TR
    n_mo = (C_GATE - C_MO_Q) // PREP_TR
    return ([REF_LRU_END + i * PREP_TR for i in range(n_ml)] + [i * PREP_TR for i in range(n_lru)]
            + [REF_GATE_END + i * PREP_TR for i in range(n_mo)] + [REF_ML_END])


def _in_proj_kernel(x_ref, g_ref, w_hbm, o_ref, gt_ref, w_s, stage_s, sem, *, layer):
    @pl.when(pl.program_id(0) == 0)
    def _():
        src = _w_in_src_rows()

        def copy(k):
            return pltpu.make_async_copy(w_hbm.at[layer, pl.ds(src[k], PREP_TR), :], stage_s.at[k % 2], sem.at[k % 2])

        copy(0).start()
        for k in range(len(src)):
            if k + 1 < len(src):
                copy(k + 1).start()
            copy(k).wait()
            tile = stage_s[k % 2]
            if k == len(src) - 1:
                row = lax.broadcasted_iota(jnp.int32, tile.shape, 0)
                tile = jnp.where(row < 2 * ML_HEADS, tile, 0.0)
            w_s[k * PREP_TR:(k + 1) * PREP_TR, :] = tile.astype(BF16)

    xn = _rmsnorm_rows(x_ref[...], _layer_row(g_ref, layer)).astype(BF16)
    n_chunks = w_s.shape[0] // IN_TN
    for j in range(n_chunks):
        cols = pl.ds(j * IN_TN, IN_TN)
        res = lax.dot_general(xn, w_s[cols, :], NT_DIMS, preferred_element_type=F32)
        o_ref[:, cols] = res.astype(o_ref.dtype)
        if j == C_GATE // IN_TN:
            g0 = C_GATE - j * IN_TN
            gt_ref[...] = res[:, g0:g0 + LANE].T[:2 * ML_HEADS, :]


def _in_proj(x2d, g, w_in_t, l):
    m = x2d.shape[0]
    n = IN_COLS_P
    vmem = (2 * IN_TM * D_MODEL * 4 + 2 * IN_TM * D_MODEL * 2 + D_MODEL * n * 2 + 2 * PREP_TR * D_MODEL * 4
            + 2 * IN_TM * n * 2 + 2 * IN_TM * IN_TN * 4)
    return pl.pallas_call(
        functools.partial(_in_proj_kernel, layer=l),
        grid=(m // IN_TM,),
        in_specs=[
            pl.BlockSpec((IN_TM, D_MODEL), lambda i: (i, 0)),
            pl.BlockSpec(g.shape, lambda i: (0, 0)),
            pl.BlockSpec(memory_space=pl.ANY),
        ],
        out_specs=[pl.BlockSpec((IN_TM, n), lambda i: (i, 0)),
                   pl.BlockSpec((2 * ML_HEADS, IN_TM), lambda i: (0, i))],
        out_shape=[jax.ShapeDtypeStruct((m, n), BF16), jax.ShapeDtypeStruct((2 * ML_HEADS, m), F32)],
        scratch_shapes=[pltpu.VMEM((n, D_MODEL), BF16),
                        pltpu.VMEM((2, PREP_TR, D_MODEL), F32),
                        pltpu.SemaphoreType.DMA((2,))],
        compiler_params=pltpu.CompilerParams(
            dimension_semantics=("arbitrary",), vmem_limit_bytes=_vmem_limit(vmem)),
        name="in_proj",
    )(x2d, g, w_in_t)


def _lru_kernel(x_ref, z_ref, cw_ref, cb_ref, wax_ref, ba_ref, bx_ref, lam_ref, o_ref, a_s, u_s, *, layer):
    t = x_ref.shape[1]
    x = x_ref[0].astype(F32)
    xc = _causal_conv(x, cw_ref, _layer_row(cb_ref, layer))
    pre = jnp.dot(xc.astype(BF16), wax_ref[0], preferred_element_type=F32)
    r = _sigmoid(pre[:, :LRU_BW] + _layer_row(ba_ref, layer))
    i = _sigmoid(pre[:, LRU_BW:] + _layer_row(bx_ref, layer))
    log_a = (-LRU_C) * r * _softplus(-_layer_row(lam_ref, layer))
    a = jnp.exp(log_a)
    a_s[...] = a
    u_s[...] = jnp.sqrt(-jnp.tanh(log_a) * (1.0 + a * a)) * (i * xc)

    row = lax.broadcasted_iota(jnp.int32, (SUBLANE, LRU_BW), 0)

    def block(blk, h_prev):
        r0 = pl.multiple_of(blk * SUBLANE, SUBLANE)
        a_b = a_s[pl.ds(r0, SUBLANE), :]
        u_b = u_s[pl.ds(r0, SUBLANE), :]
        for s in (1, 2, 4):
            a_sh = jnp.where(row >= s, pltpu.roll(a_b, s, axis=0), 1.0)
            u_sh = jnp.where(row >= s, pltpu.roll(u_b, s, axis=0), 0.0)
            u_b = a_b * u_sh + u_b
            a_b = a_b * a_sh
        u_s[pl.ds(r0, SUBLANE), :] = a_b * h_prev + u_b
        last = lambda v: jnp.broadcast_to(v[SUBLANE - 1:SUBLANE, :], (SUBLANE, LRU_BW))
        return last(a_b) * h_prev + last(u_b)

    lax.fori_loop(0, t // SUBLANE, block, jnp.zeros((SUBLANE, LRU_BW), F32), unroll=4)
    z = z_ref[0].astype(F32)
    o_ref[0] = (u_s[...] * _silu(z)).astype(o_ref.dtype)


def _lru_branch(p, cw, wax, cb, ba, bx, lam, l):
    b, t, _ = p.shape
    xb, zb = C_LRU_X // LRU_BW, C_LRU_Z // LRU_BW
    vec = pl.BlockSpec((cb.shape[0], LRU_BW), lambda bi, g: (0, g))
    return pl.pallas_call(
        functools.partial(_lru_kernel, layer=l),
        grid=(b, LRU_BLOCKS),
        in_specs=[
            pl.BlockSpec((1, t, LRU_BW), lambda bi, g: (bi, 0, xb + g)),
            pl.BlockSpec((1, t, LRU_BW), lambda bi, g: (bi, 0, zb + g)),
            pl.BlockSpec((None, cw.shape[1], LRU_BW), lambda bi, g: (l, 0, g)),
            vec,
            pl.BlockSpec((None, 1, LRU_BW, 2 * LRU_BW), lambda bi, g: (l, g, 0, 0)),
            vec, vec, vec,
        ],
        out_specs=pl.BlockSpec((1, t, LRU_BW), lambda bi, g: (bi, 0, g)),
        out_shape=jax.ShapeDtypeStruct((b, t, LRU_W), BF16),
        scratch_shapes=[pltpu.VMEM((t, LRU_BW), F32), pltpu.VMEM((t, LRU_BW), F32)],
        compiler_params=pltpu.CompilerParams(
            dimension_semantics=("arbitrary", "arbitrary"), vmem_limit_bytes=_vmem_limit(V7X_VMEM_BYTES)),
        name="rg_lru",
    )(p, p, cw, cb, wax, ba, bx, lam)


def _blockdiag_in_out(w_ref, hh):
    wh = w_ref[...][:, hh * ML_DH:(hh + 1) * ML_DH]
    wh = jnp.concatenate([wh, jnp.zeros((ML_QKV_BLOCK, ML_DHP - ML_DH), F32)], axis=1)
    r = lax.broadcasted_iota(jnp.int32, (ML_PW, ML_DHP), 0)
    c = lax.broadcasted_iota(jnp.int32, (ML_PW, ML_DHP), 1)
    d = jnp.zeros((ML_PW, ML_DHP), F32)
    for i in range(ML_QKV_BLOCK):
        d = jnp.where((r & (ML_QKV_BLOCK - 1)) == i, wh[i:i + 1, :], d)
    keep = (((r >> ML_QKV_SHIFT) - hh * (ML_DH // ML_QKV_BLOCK)) == (c >> ML_QKV_SHIFT)) & (c < ML_DH)
    return jnp.where(keep, d, 0.0)


def _blockdiag_in_window(w_ref, hh):
    w0 = ML_WINDOW[hh]
    wh = w_ref[...][:, w0:w0 + ML_DHP]
    r = lax.broadcasted_iota(jnp.int32, (ML_DHP, ML_DHP), 0) + w0
    c = lax.broadcasted_iota(jnp.int32, (ML_DHP, ML_DHP), 1) + w0
    d = jnp.zeros((ML_DHP, ML_DHP), F32)
    for i in range(ML_QKV_BLOCK):
        d = jnp.where((r & (ML_QKV_BLOCK - 1)) == i, wh[i:i + 1, :], d)
    keep = ((r >> ML_QKV_SHIFT) == (c >> ML_QKV_SHIFT)) & (c >= hh * ML_DH) & (c < (hh + 1) * ML_DH)
    return jnp.where(keep, d, 0.0)


def _blockdiag_out_in(w_ref, hh):
    wk = w_ref[...]
    o = lax.broadcasted_iota(jnp.int32, (ML_DHP, ML_PW), 0)
    r = lax.broadcasted_iota(jnp.int32, (ML_DHP, ML_PW), 1)
    d = jnp.zeros((ML_DHP, ML_PW), F32)
    for j in range(ML_QKV_BLOCK):
        d = jnp.where((o & (ML_QKV_BLOCK - 1)) == j, wk[j:j + 1, :], d)
    keep = ((o >> ML_QKV_SHIFT) == ((r >> ML_QKV_SHIFT) - hh * (ML_DH // ML_QKV_BLOCK))) & (o < ML_DH)
    return jnp.where(keep, d, 0.0)


ML_WINDOW = (0, ML_PW - ML_DHP)
ML_DEN_LANE = (ML_DH + 2, 2)


def _mlstm_kernel(u_ref, og_ref, z_ref, ig_ref, fg_ref, bi_ref, bf_ref, cw_ref, cb_ref,
                  wq_ref, wkt_ref, wv_ref, ng_ref, y_ref,
                  dq_s, dkt_s, dv_s, q_s, kt_s, v_s, cp_s, c_s, r_s, w_s, col_s, so_s, sn_s, loca_s, locb_s, *, layer):
    t = u_ref.shape[1]
    nc = t // ML_CHUNK
    L = ML_CHUNK
    heads = range(2)

    @pl.when(pl.program_id(1) == 0)
    def _():
        for hh in heads:
            dq_s[hh] = _blockdiag_in_out(wq_ref, hh).astype(BF16)
            dkt_s[hh] = _blockdiag_out_in(wkt_ref, hh).astype(BF16)
            dv_s[hh] = _blockdiag_in_window(wv_ref, hh).astype(BF16)

    u = u_ref[0]
    uc = _silu(_causal_conv(u.astype(F32), cw_ref, _layer_row(cb_ref, layer))).astype(BF16)
    for hh in heads:
        w0 = ML_WINDOW[hh]
        uc_w, u_w = uc[:, w0:w0 + ML_DHP], u[:, w0:w0 + ML_DHP]
        q_s[hh] = jnp.dot(uc_w, dq_s[hh, w0:w0 + ML_DHP, :], preferred_element_type=F32).astype(BF16)
        kt = lax.dot_general(dkt_s[hh, :, w0:w0 + ML_DHP], uc_w, NT_DIMS, preferred_element_type=F32)
        kt = kt * (ML_DH ** -0.5)
        for c in range(nc):
            kt_s[hh, c] = kt[:, c * L:(c + 1) * L].astype(BF16)
        v = jnp.dot(u_w, dv_s[hh], preferred_element_type=F32)
        vlane = lax.broadcasted_iota(jnp.int32, v.shape, 1)
        v_s[hh] = jnp.where(vlane == ML_DEN_LANE[hh], 1.0, v).astype(BF16)

    glane = lax.broadcasted_iota(jnp.int32, (nc, L), 1)
    grow = lax.broadcasted_iota(jnp.int32, (nc, L), 0)
    row8 = lax.broadcasted_iota(jnp.int32, (SUBLANE, L), 0)
    for hh in heads:
        ig = ig_ref[hh, 0] + bi_ref[hh]
        lf = -_softplus(-(fg_ref[hh, 0] + bf_ref[hh]))
        b = lf
        for k in range(int(math.log2(L))):
            sh = 1 << k
            b = b + jnp.where(glane >= sh, pltpu.roll(b, sh, axis=1), 0.0)
        g = jnp.broadcast_to(b[:, L - 1:L], (nc, L))
        a = g - b + ig
        mloc = jnp.broadcast_to(jnp.max(a, axis=1, keepdims=True), (nc, L))
        m = jnp.zeros((1, L), F32)
        m_prev = jnp.zeros((nc, L), F32)
        m_next = jnp.zeros((nc, L), F32)
        for c in range(nc):
            m_prev = jnp.where(grow == c, m, m_prev)
            m = jnp.maximum(g[c:c + 1, :] + m, mloc[c:c + 1, :])
            m_next = jnp.where(grow == c, m, m_next)
        r = ig - b
        cmx = r
        for k in range(int(math.log2(L))):
            sh = 1 << k
            cmx = jnp.maximum(cmx, jnp.where(glane >= sh, pltpu.roll(cmx, sh, axis=1), -jnp.inf))
        mm = jnp.maximum(m_prev, cmx)
        s_int = jnp.exp(m_prev - mm)
        clamp = jnp.exp(-(b + mm))
        r_s[hh] = r
        w_s[hh] = jnp.exp(a - mloc)
        so_s[hh] = jnp.exp(g + m_prev - m_next)
        sn_s[hh] = jnp.exp(mloc - m_next)
        for c in range(nc):
            col_s[hh, c] = jnp.where(row8 == 0, mm[c:c + 1, :],
                                     jnp.where(row8 == 1, s_int[c:c + 1, :],
                                               jnp.where(row8 == 2, clamp[c:c + 1, :], 0.0)))

    zero_rows = jnp.zeros((ML_DHP - ML_DH, ML_DHP), BF16)
    c_s[...] = jnp.zeros(c_s.shape, F32)

    n_pairs = nc // 2

    def local_states(pair, loc_ref):
        for i in range(2):
            c = 2 * pair + i
            r0 = pl.multiple_of(c * L, L)
            for hh in heads:
                ktw = (kt_s[hh, c, :ML_DH, :].astype(F32) * w_s[hh, pl.ds(c, 1), :]).astype(BF16)
                loc_ref[i, hh] = jnp.dot(ktw, v_s[hh, pl.ds(r0, L), :], preferred_element_type=F32)

    def advance(pair, loc_ref):
        for i in range(2):
            c = 2 * pair + i
            for hh in heads:
                c_prev = c_s[hh]
                cp_s[hh, c] = jnp.concatenate([c_prev.astype(BF16), zero_rows], axis=0)
                c_s[hh] = (so_s[hh, pl.ds(c, 1), :][:, :1] * c_prev
                           + sn_s[hh, pl.ds(c, 1), :][:, :1] * loc_ref[i, hh])

    def state(k, carry):
        local_states(2 * k + 1, locb_s)
        advance(2 * k, loca_s)
        local_states(jnp.minimum(2 * k + 2, n_pairs - 1), loca_s)
        advance(2 * k + 1, locb_s)
        return carry

    local_states(0, loca_s)
    lax.fori_loop(0, n_pairs // 2, state, 0)

    tri = (lax.broadcasted_iota(jnp.int32, (L, L), 0) >= lax.broadcasted_iota(jnp.int32, (L, L), 1))
    first = lax.broadcasted_iota(jnp.int32, (L, ML_PW), 1) < ML_DH

    def chunk_rows(c):
        return pl.ds(pl.multiple_of(c * L, L), L)

    def raw_scores(c):
        rows = chunk_rows(c)
        return [jnp.dot(q_s[hh, rows, :], kt_s[hh, c], preferred_element_type=F32) for hh in heads]

    def weights(c, raw):
        out = []
        for hh in heads:
            c0 = ML_DEN_LANE[hh] % LANE - 2
            pads = [jnp.zeros((n, L), F32) for n in (c0, L - SUBLANE - c0)]
            cols = jnp.concatenate([a for a in (pads[0], col_s[hh, c], pads[1]) if a.shape[0]], axis=0).T
            rb = jnp.broadcast_to(r_s[hh, pl.ds(c, 1), :], (L, L))
            decay = jnp.exp(jnp.where(tri, rb - cols[:, c0:c0 + 1], -jnp.inf))
            out.append(((raw[hh] * decay).astype(BF16), cols[:, c0 + 1:c0 + 2], cols[:, c0 + 2:c0 + 3]))
        return out

    def numerators(c, wts):
        rows = chunk_rows(c)
        nds, invs = [], []
        for hh, (s_mat, s_int, clamp) in zip(heads, wts):
            nd = (jnp.dot(s_mat, v_s[hh, rows, :], preferred_element_type=F32)
                  + s_int * jnp.dot(q_s[hh, rows, :], cp_s[hh, c], preferred_element_type=F32))
            den = nd[:, ML_DEN_LANE[hh]:ML_DEN_LANE[hh] + 1]
            nds.append(nd)
            invs.append(1.0 / jnp.maximum(jnp.abs(den), clamp))
        w1 = ML_WINDOW[1]
        shared_first = lax.broadcasted_iota(jnp.int32, (L, ML_DHP - w1), 1) < ML_DH - w1
        shared = jnp.where(shared_first, nds[0][:, w1:], nds[1][:, :ML_DHP - w1])
        return jnp.concatenate([nds[0][:, :w1], shared, nds[1][:, ML_DHP - w1:]], axis=1), invs

    def gates(c):
        rows = chunk_rows(c)
        return (_sigmoid(og_ref[0, rows, :].astype(F32)),
                _silu(z_ref[0, rows, :].astype(F32)) * _layer_row(ng_ref, layer))

    def means(nd, og):
        x = og * nd
        mu0 = jnp.sum(jnp.where(first, x, 0.0), axis=1, keepdims=True) * (1.0 / ML_DH)
        mu1 = jnp.sum(jnp.where(first, 0.0, x), axis=1, keepdims=True) * (1.0 / ML_DH)
        return x, mu0, mu1

    def variances(x, mu0, mu1):
        dev = x - jnp.where(first, mu0, mu1)
        sq = dev * dev
        var0 = jnp.sum(jnp.where(first, sq, 0.0), axis=1, keepdims=True) * (1.0 / ML_DH)
        var1 = jnp.sum(jnp.where(first, 0.0, sq), axis=1, keepdims=True) * (1.0 / ML_DH)
        return dev, var0, var1

    def finish(c, dev, var0, var1, invs, zs):
        f0 = invs[0] * lax.rsqrt(invs[0] * invs[0] * var0 + LN_EPS)
        f1 = invs[1] * lax.rsqrt(invs[1] * invs[1] * var1 + LN_EPS)
        y_ref[0, chunk_rows(c), :] = (dev * jnp.where(first, f0, f1) * zs).astype(y_ref.dtype)

    def group(gi, carry):
        cs = [gi * ML_GROUP + i for i in range(ML_GROUP)]
        n = range(ML_GROUP)
        raws = [raw_scores(c) for c in cs]
        gts = [gates(c) for c in cs]
        wts = [weights(c, raws[i]) for i, c in enumerate(cs)]
        nums = [numerators(c, wts[i]) for i, c in enumerate(cs)]
        mus = [means(nums[i][0], gts[i][0]) for i in n]
        vrs = [variances(*mus[i]) for i in n]
        for i, c in enumerate(cs):
            finish(c, *vrs[i], nums[i][1], gts[i][1])
        return carry

    lax.fori_loop(0, nc // ML_GROUP, group, 0)


def _mlstm_branch(p, gates, gbias, cw, cb, wq, wkt, wv, ng, l):
    b, t, _ = p.shape
    nc = t // ML_CHUNK
    ub, ob, zb = C_ML_U // ML_PW, C_ML_O // ML_PW, C_ML_Z // ML_PW
    seq = lambda base: pl.BlockSpec((1, t, ML_PW), lambda pr, bi: (bi, 0, base + pr))
    vec = pl.BlockSpec((cb.shape[0], ML_PW), lambda pr, bi: (0, pr))
    taps = pl.BlockSpec((None, ML_QKV_BLOCK, ML_PW), lambda pr, bi: (l, 0, pr))
    gate_rows = lambda: pltpu.VMEM((2, nc, ML_CHUNK), F32)
    pair_states = lambda: pltpu.VMEM((2, 2, ML_DH, ML_DHP), F32)
    return pl.pallas_call(
        functools.partial(_mlstm_kernel, layer=l),
        grid=(ML_PAIRS, b),
        in_specs=[
            seq(ub), seq(ob), seq(zb),
            pl.BlockSpec((2, 1, nc, ML_CHUNK), lambda pr, bi: (pr, bi, 0, 0)),
            pl.BlockSpec((2, 1, nc, ML_CHUNK), lambda pr, bi: (ML_PAIRS + pr, bi, 0, 0)),
            pl.BlockSpec((None, 2, 1, ML_CHUNK), lambda pr, bi: (l, pr, 0, 0)),
            pl.BlockSpec((None, 2, 1, ML_CHUNK), lambda pr, bi: (l, ML_PAIRS + pr, 0, 0)),
            pl.BlockSpec((None, cw.shape[1], ML_PW), lambda pr, bi: (l, 0, pr)),
            vec, taps, taps, taps, vec,
        ],
        out_specs=pl.BlockSpec((1, t, ML_PW), lambda pr, bi: (bi, 0, pr)),
        out_shape=jax.ShapeDtypeStruct((b, t, ML_W), BF16),
        scratch_shapes=[
            pltpu.VMEM((2, ML_PW, ML_DHP), BF16),
            pltpu.VMEM((2, ML_DHP, ML_PW), BF16),
            pltpu.VMEM((2, ML_DHP, ML_DHP), BF16),
            pltpu.VMEM((2, t, ML_DHP), BF16),
            pltpu.VMEM((2, nc, ML_DHP, ML_CHUNK), BF16),
            pltpu.VMEM((2, t, ML_DHP), BF16),
            pltpu.VMEM((2, nc, ML_DHP, ML_DHP), BF16),
            pltpu.VMEM((2, ML_DH, ML_DHP), F32),
            gate_rows(), gate_rows(),
            pltpu.VMEM((2, nc, SUBLANE, ML_CHUNK), F32),
            gate_rows(), gate_rows(),
            pair_states(), pair_states(),
        ],
        compiler_params=pltpu.CompilerParams(
            dimension_semantics=("arbitrary", "arbitrary"), vmem_limit_bytes=_vmem_limit(56 << 20)),
        name="mlstm",
    )(p, p, p, gates, gates, gbias, gbias, cw, cb, wq, wkt, wv, ng)


MOBA_SEL_LANE = 6
MOBA_AHEAD = 3


def _moba_key_ext(nb):
    ext = np.zeros((nb, MOBA_BLOCK, MO_DH), np.float32)
    for n in range(nb):
        ext[n, :, 0:3] = n * MOBA_BLOCK
        ext[n, :, 3:6] = np.arange(MOBA_BLOCK, dtype=np.float32)[:, None]
        ext[n, :, MOBA_SEL_LANE + n] = 1.0
    return jnp.asarray(ext, dtype=BF16)


def _moba_slope_rows(slopes):
    rows = np.zeros((len(slopes), 1, MO_DH), np.float32)
    for h, s in enumerate(slopes):
        rest = np.float32(np.float32(s) * np.float32(LOG2E))
        for i in range(3):
            piece = np.float32(rest).astype(BF16).astype(np.float32)
            rows[h, 0, i] = rows[h, 0, 3 + i] = piece
            rest = np.float32(rest - piece)
    return jnp.asarray(rows)


def _moba_kernel(q_ref, k_ref, v_ref, z_ref, srow_ref, kext_ref, o_ref, l_s, qa_s):
    t = q_ref.shape[1]
    nb = t // MOBA_BLOCK
    bs = MOBA_BLOCK
    qscale = (MO_DH ** -0.5) * LOG2E

    krow = lax.broadcasted_iota(jnp.int32, (LANE, MO_DH), 0)
    kmean = jnp.zeros((LANE, MO_DH), F32)
    for n in range(nb):
        mean_n = jnp.sum(k_ref[0, n * bs:(n + 1) * bs, :].astype(F32), axis=0, keepdims=True) * (1.0 / bs)
        kmean = jnp.where(krow == MOBA_SEL_LANE + n, mean_n, kmean)
    kmean = kmean.astype(BF16)

    lane = lax.broadcasted_iota(jnp.int32, (bs, MO_DH), 1)
    causal = (lax.broadcasted_iota(jnp.int32, (bs, bs), 0) >= lax.broadcasted_iota(jnp.int32, (bs, bs), 1))
    ones_col = jnp.where(lane == 0, 1.0, 0.0).astype(BF16)
    slope_cols = jnp.where(lane < MOBA_SEL_LANE, srow_ref[0], 0.0)

    def prepare(qb):
        q_b = q_ref[0, qb * bs:(qb + 1) * bs, :]
        q_ext = slope_cols
        if qb > MOBA_TOPK:
            gate = lax.dot_general(q_b, kmean, NT_DIMS, preferred_element_type=F32)
            beaten = jnp.zeros((bs, MO_DH), F32)
            for m in range(qb):
                gm = gate[:, MOBA_SEL_LANE + m:MOBA_SEL_LANE + m + 1]
                wins = (gm > gate) | ((gm == gate) & (lane > MOBA_SEL_LANE + m))
                beaten = beaten + jnp.where(wins, 1.0, 0.0)
            past = (lane >= MOBA_SEL_LANE) & (lane < MOBA_SEL_LANE + qb)
            q_ext = jnp.where(past & (beaten >= float(MOBA_TOPK)), NEG_INF, slope_cols)
        qa_s[qb] = jnp.concatenate([(q_b.astype(F32) * qscale).astype(BF16), q_ext.astype(BF16)], axis=1)

    def scores(qb):
        q_aug = qa_s[qb]
        l_q = l_s.at[qb % l_s.shape[0]]
        mx = None
        for n in range(qb + 1):
            k_aug = jnp.concatenate([k_ref[0, n * bs:(n + 1) * bs, :], kext_ref[n]], axis=1)
            logit = lax.dot_general(q_aug, k_aug, NT_DIMS, preferred_element_type=F32)
            if n == qb:
                logit = jnp.where(causal, logit, NEG_INF)
            l_q[:, n * bs:(n + 1) * bs] = logit
            half = jnp.maximum(logit[:, :LANE], logit[:, LANE:])
            mx = half if mx is None else jnp.maximum(mx, half)
        return jnp.max(mx, axis=1, keepdims=True)

    def outputs(qb, m_row):
        l_q = l_s.at[qb % l_s.shape[0]]
        acc = jnp.zeros((bs, 2 * MO_DH), F32)
        for n in range(qb + 1):
            pr = jnp.exp2(l_q[:, n * bs:(n + 1) * bs] - m_row).astype(BF16)
            v_aug = jnp.concatenate([v_ref[0, n * bs:(n + 1) * bs, :], ones_col], axis=1)
            acc = acc + jnp.dot(pr, v_aug, preferred_element_type=F32)
        z = z_ref[0, qb * bs:(qb + 1) * bs, :].astype(F32)
        inv = 1.0 / acc[:, MO_DH:MO_DH + 1]
        o_ref[0, qb * bs:(qb + 1) * bs, :] = (acc[:, :MO_DH] * (inv * _silu(z))).astype(o_ref.dtype)

    ahead = l_s.shape[0] - 1
    for qb in range(min(ahead + 1, nb)):
        prepare(qb)
    m_rows = {qb: scores(qb) for qb in range(min(ahead, nb))}
    for qb in range(nb):
        if qb + ahead + 1 < nb:
            prepare(qb + ahead + 1)
        if qb + ahead < nb:
            m_rows[qb + ahead] = scores(qb + ahead)
        outputs(qb, m_rows.pop(qb))


def _moba_branch(p, srows, kext):
    b, t, _ = p.shape
    nb = t // MOBA_BLOCK
    qb, kb, vb, zb = (c // MO_DH for c in (C_MO_Q, C_MO_K, C_MO_V, C_MO_Z))
    seq = lambda base: pl.BlockSpec((1, t, MO_DH), lambda bi, h: (bi, 0, base + h))
    return pl.pallas_call(
        _moba_kernel,
        grid=(b, MO_HEADS),
        in_specs=[seq(qb), seq(kb), seq(vb), seq(zb),
                  pl.BlockSpec((1, 1, MO_DH), lambda bi, h: (h, 0, 0)),
                  pl.BlockSpec((nb, MOBA_BLOCK, MO_DH), lambda bi, h: (0, 0, 0))],
        out_specs=pl.BlockSpec((1, t, MO_DH), lambda bi, h: (bi, 0, h)),
        out_shape=jax.ShapeDtypeStruct((b, t, MO_W), BF16),
        scratch_shapes=[pltpu.VMEM((MOBA_AHEAD + 1, MOBA_BLOCK, t), F32),
                        pltpu.VMEM((nb, MOBA_BLOCK, 2 * MO_DH), BF16)],
        compiler_params=pltpu.CompilerParams(
            dimension_semantics=("arbitrary", "arbitrary"), vmem_limit_bytes=_vmem_limit(V7X_VMEM_BYTES)),
        name="moba",
    )(p, p, p, p, srows, kext)


def _stage_weights(w_hbm, layer, dst_ref, stage_ref, sem, overlap=None):
    slots, tile_rows = stage_ref.shape[0], stage_ref.shape[1]
    n_tiles = dst_ref.shape[0] // tile_rows

    def copy(k):
        return pltpu.make_async_copy(w_hbm.at[layer, pl.ds(k * tile_rows, tile_rows), :], stage_ref.at[k % slots],
                                     sem.at[k % slots])

    for k in range(min(slots - 1, n_tiles)):
        copy(k).start()
    if overlap is not None:
        overlap()
    for k in range(n_tiles):
        if k + slots - 1 < n_tiles:
            copy(k + slots - 1).start()
        copy(k).wait()
        dst_ref[k * tile_rows:(k + 1) * tile_rows, :] = stage_ref[k % slots].astype(BF16)


def _mem_kv_kernel(mem_ref, g_ref, w_hbm, o_ref, w_s, stage_s, sem, hm_s):
    layer = pl.program_id(0)
    g = g_ref[pl.ds(layer, 1), :]

    def normalise():
        for bi in range(mem_ref.shape[0]):
            hm_s[bi] = _rmsnorm_rows(mem_ref[bi], g).astype(BF16)

    _stage_weights(w_hbm, layer, w_s, stage_s, sem, overlap=normalise)
    for bi in range(mem_ref.shape[0]):
        o_ref[bi] = jnp.dot(hm_s[bi], w_s[...], preferred_element_type=F32).astype(o_ref.dtype)


def _mem_kv(mem, g, wkv):
    b, m, _ = mem.shape
    depth = wkv.shape[0]
    return pl.pallas_call(
        _mem_kv_kernel,
        grid=(depth,),
        in_specs=[pl.BlockSpec((b, m, D_MODEL), lambda l: (0, 0, 0)),
                  pl.BlockSpec(g.shape, lambda l: (0, 0)),
                  pl.BlockSpec(memory_space=pl.ANY)],
        out_specs=pl.BlockSpec((None, b, m, 2 * XA_W), lambda l: (l, 0, 0, 0)),
        out_shape=jax.ShapeDtypeStruct((depth, b, m, 2 * XA_W), BF16),
        scratch_shapes=[pltpu.VMEM((D_MODEL, 2 * XA_W), BF16),
                        pltpu.VMEM((MEM_KV_SLOTS, STAGE_ROWS, 2 * XA_W), F32),
                        pltpu.SemaphoreType.DMA((MEM_KV_SLOTS,)),
                        pltpu.VMEM((b, m, D_MODEL), BF16)],
        compiler_params=pltpu.CompilerParams(
            dimension_semantics=("arbitrary",), vmem_limit_bytes=_vmem_limit(V7X_VMEM_BYTES)),
        name="mem_kv",
    )(mem, g, wkv)


def _post_kernel(x_ref, yl_ref, ym_ref, yo_ref, wout_hbm, g_ref, wq_hbm, kv_ref, wo_hbm, fg_ref, o_ref,
                 x1_s, wout_ref, wq_ref, wo_ref, stage_s, stage_q_s, sem, sem_q, *, final_norm, layer):
    @pl.when((pl.program_id(0) == 0) & (pl.program_id(1) == 0))
    def _():
        _stage_weights(wout_hbm, layer, wout_ref, stage_s, sem)
        _stage_weights(wo_hbm, layer, wo_ref, stage_s, sem)
        _stage_weights(wq_hbm, layer, wq_ref, stage_q_s, sem_q)

    sub = x_ref.shape[1] // POST_SUB
    groups = [pl.ds(i * sub, sub) for i in range(POST_SUB)]

    for r in groups:
        acc = jnp.dot(yl_ref[0, r, :], wout_ref[0:LRU_W, :], preferred_element_type=F32)
        acc = acc + jnp.dot(ym_ref[0, r, :], wout_ref[LRU_W:LRU_W + ML_W, :], preferred_element_type=F32)
        acc = acc + jnp.dot(yo_ref[0, r, :], wout_ref[LRU_W + ML_W:, :], preferred_element_type=F32)
        x1_s[r, :] = x_ref[0, r, :] + acc
    qs = []
    for r in groups:
        hx = _rmsnorm_rows(x1_s[r, :], _layer_row(g_ref, layer)).astype(BF16)
        qs.append(jnp.dot(hx, wq_ref[...], preferred_element_type=F32).astype(BF16))
    scores = []
    for q in qs:
        scores.append([
            lax.dot_general(q[:, h * XA_DH:(h + 1) * XA_DH], kv_ref[0, :, h * XA_DH:(h + 1) * XA_DH], NT_DIMS,
                            preferred_element_type=F32) * (XA_DH ** -0.5)
            for h in range(XA_HEADS)])
    outs = []
    for sc in scores:
        heads = []
        for h, s in enumerate(sc):
            e = jnp.exp(s - jnp.max(s, axis=1, keepdims=True))
            pr = e / jnp.sum(e, axis=1, keepdims=True)
            v_h = kv_ref[0, :, XA_W + h * XA_DH:XA_W + (h + 1) * XA_DH]
            heads.append(jnp.dot(pr.astype(BF16), v_h, preferred_element_type=F32).astype(BF16))
        outs.append(jnp.concatenate(heads, axis=1))
    for r, o in zip(groups, outs):
        y = x1_s[r, :] + jnp.dot(o, wo_ref[...], preferred_element_type=F32)
        if final_norm:
            y = _rmsnorm_rows(y, fg_ref[...])
        o_ref[0, r, :] = y


def _post_mix(x, yl, ym, yo, wout, g, wq, kv, wo, fg, l, final_norm):
    b, t, _ = x.shape
    m = kv.shape[2]
    rows = lambda width: pl.BlockSpec((1, POST_TM, width), lambda bi, i: (bi, i, 0))
    hbm = pl.BlockSpec(memory_space=pl.ANY)
    weights = (D_MODEL * D_MODEL + D_MODEL * XA_W + XA_W * D_MODEL) * 2
    stages = 2 * STAGE_ROWS * (D_MODEL + XA_W) * 4
    tiles = 2 * POST_TM * (2 * D_MODEL * 4 + D_MODEL * 2) + 2 * m * 2 * XA_W * 2
    vmem = weights + stages + tiles + 6 * POST_TM * D_MODEL * 4
    return pl.pallas_call(
        functools.partial(_post_kernel, final_norm=final_norm, layer=l),
        grid=(b, t // POST_TM),
        in_specs=[rows(D_MODEL), rows(LRU_W), rows(ML_W), rows(MO_W),
                  hbm,
                  pl.BlockSpec(g.shape, lambda bi, i: (0, 0)),
                  hbm,
                  pl.BlockSpec((None, 1, m, 2 * XA_W), lambda bi, i: (l, bi, 0, 0)),
                  hbm,
                  pl.BlockSpec((1, D_MODEL), lambda bi, i: (0, 0))],
        out_specs=rows(D_MODEL),
        out_shape=jax.ShapeDtypeStruct((b, t, D_MODEL), F32),
        scratch_shapes=[pltpu.VMEM((POST_TM, D_MODEL), F32),
                        pltpu.VMEM((D_MODEL, D_MODEL), BF16),
                        pltpu.VMEM((D_MODEL, XA_W), BF16),
                        pltpu.VMEM((XA_W, D_MODEL), BF16),
                        pltpu.VMEM((2, STAGE_ROWS, D_MODEL), F32),
                        pltpu.VMEM((2, STAGE_ROWS, XA_W), F32),
                        pltpu.SemaphoreType.DMA((2,)),
                        pltpu.SemaphoreType.DMA((2,))],
        compiler_params=pltpu.CompilerParams(
            dimension_semantics=("arbitrary", "arbitrary"), vmem_limit_bytes=_vmem_limit(vmem)),
        name="post_mix",
    )(x, yl, ym, yo, wout, g, wq, kv, wo, fg)


def _alibi_slopes(n):
    def pow2(m):
        start = 2.0 ** (-8.0 / m)
        return [start ** (i + 1) for i in range(m)]
    if math.log2(n).is_integer():
        s = pow2(n)
    else:
        c = 2 ** int(math.floor(math.log2(n)))
        s = pow2(c) + pow2(2 * c)[0::2][:n - c]
    return np.asarray(s, dtype=np.float32)


def _taps(w, perm):
    depth = w.shape[0]
    return jnp.transpose(w, perm).reshape(depth, ML_QKV_BLOCK, ML_W)


def kernel(x, mem, mix_norm_g, w_in, lru_conv_w, lru_conv_b, lru_wa, lru_ba, lru_wx, lru_bx, lru_lambda,
           ml_conv_w, ml_conv_b, ml_wq, ml_wk, ml_wv, ml_bi, ml_bf, ml_norm_g, w_out, xa_norm_g, mem_norm_g,
           xa_wq, xa_wkv, xa_wo, final_norm_g):
    b, t, d = x.shape
    depth = w_in.shape[0]
    nc = t // ML_CHUNK
    moba_srows = _moba_slope_rows(_alibi_slopes(MO_HEADS))
    moba_kext = _moba_key_ext(t // MOBA_BLOCK)
    w_in_t = jnp.swapaxes(w_in, 1, 2)
    lru_wax = jnp.concatenate([lru_wa, lru_wx], axis=-1).astype(BF16)
    ml_wq_t = _taps(ml_wq, (0, 2, 1, 3))
    ml_wkt_t = _taps(ml_wk, (0, 3, 1, 2))
    ml_wv_t = _taps(ml_wv, (0, 2, 1, 3))
    ml_gbias = jnp.broadcast_to(jnp.concatenate([ml_bi, ml_bf], axis=1)[:, :, None, None],
                                (depth, 2 * ML_HEADS, 1, ML_CHUNK))
    kv = _mem_kv(mem, mem_norm_g, xa_wkv)
    for l in range(depth):
        p, gates = _in_proj(x.reshape(b * t, d), mix_norm_g, w_in_t, l)
        p, gates = p.reshape(b, t, IN_COLS_P), gates.reshape(2 * ML_HEADS, b, nc, ML_CHUNK)
        y_lru = _lru_branch(p, lru_conv_w, lru_wax, lru_conv_b, lru_ba, lru_bx, lru_lambda, l)
        y_ml = _mlstm_branch(p, gates, ml_gbias, ml_conv_w, ml_conv_b, ml_wq_t, ml_wkt_t, ml_wv_t, ml_norm_g, l)
        y_mo = _moba_branch(p, moba_srows, moba_kext)
        x = _post_mix(x, y_lru, y_ml, y_mo, w_out, xa_norm_g, xa_wq, kv, xa_wo, final_norm_g[None, :], l,
                      final_norm=(l == depth - 1))
    return x
```

```python
import functools
import math

import jax
import jax.numpy as jnp
import numpy as np
from jax import lax
from jax.experimental import pallas as pl
from jax.experimental.pallas import tpu as pltpu

LANE = 128
SUBLANE = 8
V7X_VMEM_BYTES = 64 * 1024 * 1024

D_MODEL = 2048
LRU_W = 512
LRU_BLOCKS = 4
LRU_BW = LRU_W // LRU_BLOCKS
LRU_C = 8.0
ML_W = 768
ML_HEADS = 4
ML_DH = 192
ML_DHP = 256
ML_PAIRS = 2
ML_PW = 2 * ML_DH
ML_QKV_BLOCK = 4
ML_QKV_SHIFT = 2
ML_CHUNK = 128
ML_GROUP = 4
MO_W = 768
MO_HEADS = 6
MO_DH = 128
MOBA_BLOCK = 256
MOBA_TOPK = 3
XA_HEADS = 4
XA_DH = 128
XA_W = XA_HEADS * XA_DH
RMS_EPS = 1e-6
LN_EPS = 1e-5
NEG_INF = -1e30
LOG2E = 1.4426950408889634

REF_LRU_END = 2 * LRU_W
REF_ML_END = REF_LRU_END + 3 * ML_W
REF_GATE_END = REF_ML_END + 2 * ML_HEADS

C_ML_U = 0
C_ML_O = C_ML_U + ML_W
C_ML_Z = C_ML_O + ML_W
C_LRU_X = C_ML_Z + ML_W
C_LRU_Z = C_LRU_X + LRU_W
C_MO_Q = C_LRU_Z + LRU_W
C_MO_K = C_MO_Q + MO_W
C_MO_V = C_MO_K + MO_W
C_MO_Z = C_MO_V + MO_W
C_GATE = C_MO_Z + MO_W
IN_TM = 512
IN_TN = 3328
IN_COLS_P = 6656
PREP_TR = 256
STAGE_ROWS = 256
MEM_KV_SLOTS = 4
POST_SLOTS = 3
POST_TM = 512
POST_SUB = 2

BF16 = jnp.bfloat16
F32 = jnp.float32
NT_DIMS = (((1,), (1,)), ((), ()))


def _vmem_limit(nbytes):
    return int(min(V7X_VMEM_BYTES - (4 << 20), max(32 << 20, nbytes)))


def _rmsnorm_rows(x, g):
    ms = jnp.mean(x * x, axis=-1, keepdims=True)
    return x * lax.rsqrt(ms + RMS_EPS) * g


def _sigmoid(x):
    return jax.nn.sigmoid(x)


def _silu(x):
    return x * jax.nn.sigmoid(x)


def _softplus(x):
    return jnp.maximum(x, 0.0) + jnp.log1p(jnp.exp(-jnp.abs(x)))


def _shift_rows(x, s):
    rolled = pltpu.roll(x, s, axis=0)
    row = lax.broadcasted_iota(jnp.int32, x.shape, 0)
    return jnp.where(row >= s, rolled, 0.0)


def _layer_row(ref, layer):
    return ref[layer:layer + 1, :]


def _causal_conv(x, w_ref, b):
    k = w_ref.shape[0]
    acc = x * w_ref[k - 1:k, :]
    head = x[:SUBLANE]
    acc_head = head * w_ref[k - 1:k, :]
    for j in range(k - 1):
        acc = acc + pltpu.roll(x, k - 1 - j, axis=0) * w_ref[j:j + 1, :]
        acc_head = acc_head + _shift_rows(head, k - 1 - j) * w_ref[j:j + 1, :]
    return jnp.concatenate([acc_head, acc[SUBLANE:]], axis=0) + b


def _w_in_src_rows():
    n_ml = (C_LRU_X - C_ML_U) // PREP_TR
    n_lru = (C_MO_Q - C_LRU_X) // PREP_TR
    n_mo = (C_GATE - C_MO_Q) // PREP_TR
    return ([REF_LRU_END + i * PREP_TR for i in range(n_ml)] + [i * PREP_TR for i in range(n_lru)]
            + [REF_GATE_END + i * PREP_TR for i in range(n_mo)] + [REF_ML_END])


def _in_proj_kernel(x_ref, g_ref, w_hbm, o_ref, gt_ref, w_s, stage_s, sem, *, layer):
    @pl.when(pl.program_id(0) == 0)
    def _():
        src = _w_in_src_rows()

        def copy(k):
            return pltpu.make_async_copy(w_hbm.at[layer, pl.ds(src[k], PREP_TR), :], stage_s.at[k % 2], sem.at[k % 2])

        copy(0).start()
        for k in range(len(src)):
            if k + 1 < len(src):
                copy(k + 1).start()
            copy(k).wait()
            tile = stage_s[k % 2]
            if k == len(src) - 1:
                row = lax.broadcasted_iota(jnp.int32, tile.shape, 0)
                tile = jnp.where(row < 2 * ML_HEADS, tile, 0.0)
            w_s[k * PREP_TR:(k + 1) * PREP_TR, :] = tile.astype(BF16)

    xn = _rmsnorm_rows(x_ref[...], _layer_row(g_ref, layer)).astype(BF16)
    n_chunks = w_s.shape[0] // IN_TN
    for j in range(n_chunks):
        cols = pl.ds(j * IN_TN, IN_TN)
        res = lax.dot_general(xn, w_s[cols, :], NT_DIMS, preferred_element_type=F32)
        o_ref[:, cols] = res.astype(o_ref.dtype)
        if j == C_GATE // IN_TN:
            g0 = C_GATE - j * IN_TN
            gt_ref[...] = res[:, g0:g0 + LANE].T[:2 * ML_HEADS, :]


def _in_proj(x2d, g, w_in_t, l):
    m = x2d.shape[0]
    n = IN_COLS_P
    vmem = (2 * IN_TM * D_MODEL * 4 + 2 * IN_TM * D_MODEL * 2 + D_MODEL * n * 2 + 2 * PREP_TR * D_MODEL * 4
            + 2 * IN_TM * n * 2 + 2 * IN_TM * IN_TN * 4)
    return pl.pallas_call(
        functools.partial(_in_proj_kernel, layer=l),
        grid=(m // IN_TM,),
        in_specs=[
            pl.BlockSpec((IN_TM, D_MODEL), lambda i: (i, 0)),
            pl.BlockSpec(g.shape, lambda i: (0, 0)),
            pl.BlockSpec(memory_space=pl.ANY),
        ],
        out_specs=[pl.BlockSpec((IN_TM, n), lambda i: (i, 0)),
                   pl.BlockSpec((2 * ML_HEADS, IN_TM), lambda i: (0, i))],
        out_shape=[jax.ShapeDtypeStruct((m, n), BF16), jax.ShapeDtypeStruct((2 * ML_HEADS, m), F32)],
        scratch_shapes=[pltpu.VMEM((n, D_MODEL), BF16),
                        pltpu.VMEM((2, PREP_TR, D_MODEL), F32),
                        pltpu.SemaphoreType.DMA((2,))],
        compiler_params=pltpu.CompilerParams(
            dimension_semantics=("arbitrary",), vmem_limit_bytes=_vmem_limit(vmem)),
        name="in_proj",
    )(x2d, g, w_in_t)


def _lru_kernel(x_ref, z_ref, cw_ref, cb_ref, wax_ref, ba_ref, bx_ref, lam_ref, o_ref, a_s, u_s, *, layer):
    t = x_ref.shape[1]
    x = x_ref[0].astype(F32)
    xc = _causal_conv(x, cw_ref, _layer_row(cb_ref, layer))
    pre = jnp.dot(xc.astype(BF16), wax_ref[0], preferred_element_type=F32)
    r = _sigmoid(pre[:, :LRU_BW] + _layer_row(ba_ref, layer))
    i = _sigmoid(pre[:, LRU_BW:] + _layer_row(bx_ref, layer))
    log_a = (-LRU_C) * r * _softplus(-_layer_row(lam_ref, layer))
    a = jnp.exp(log_a)
    a_s[...] = a
    u_s[...] = jnp.sqrt(-jnp.tanh(log_a) * (1.0 + a * a)) * (i * xc)

    row = lax.broadcasted_iota(jnp.int32, (SUBLANE, LRU_BW), 0)

    def block(blk, h_prev):
        r0 = pl.multiple_of(blk * SUBLANE, SUBLANE)
        a_b = a_s[pl.ds(r0, SUBLANE), :]
        u_b = u_s[pl.ds(r0, SUBLANE), :]
        for s in (1, 2, 4):
            a_sh = jnp.where(row >= s, pltpu.roll(a_b, s, axis=0), 1.0)
            u_sh = jnp.where(row >= s, pltpu.roll(u_b, s, axis=0), 0.0)
            u_b = a_b * u_sh + u_b
            a_b = a_b * a_sh
        u_s[pl.ds(r0, SUBLANE), :] = a_b * h_prev + u_b
        last = lambda v: jnp.broadcast_to(v[SUBLANE - 1:SUBLANE, :], (SUBLANE, LRU_BW))
        return last(a_b) * h_prev + last(u_b)

    lax.fori_loop(0, t // SUBLANE, block, jnp.zeros((SUBLANE, LRU_BW), F32), unroll=4)
    z = z_ref[0].astype(F32)
    o_ref[0] = (u_s[...] * _silu(z)).astype(o_ref.dtype)


def _lru_branch(p, cw, wax, cb, ba, bx, lam, l):
    b, t, _ = p.shape
    xb, zb = C_LRU_X // LRU_BW, C_LRU_Z // LRU_BW
    vec = pl.BlockSpec((cb.shape[0], LRU_BW), lambda bi, g: (0, g))
    return pl.pallas_call(
        functools.partial(_lru_kernel, layer=l),
        grid=(b, LRU_BLOCKS),
        in_specs=[
            pl.BlockSpec((1, t, LRU_BW), lambda bi, g: (bi, 0, xb + g)),
            pl.BlockSpec((1, t, LRU_BW), lambda bi, g: (bi, 0, zb + g)),
            pl.BlockSpec((None, cw.shape[1], LRU_BW), lambda bi, g: (l, 0, g)),
            vec,
            pl.BlockSpec((None, 1, LRU_BW, 2 * LRU_BW), lambda bi, g: (l, g, 0, 0)),
            vec, vec, vec,
        ],
        out_specs=pl.BlockSpec((1, t, LRU_BW), lambda bi, g: (bi, 0, g)),
        out_shape=jax.ShapeDtypeStruct((b, t, LRU_W), BF16),
        scratch_shapes=[pltpu.VMEM((t, LRU_BW), F32), pltpu.VMEM((t, LRU_BW), F32)],
        compiler_params=pltpu.CompilerParams(
            dimension_semantics=("arbitrary", "arbitrary"), vmem_limit_bytes=_vmem_limit(V7X_VMEM_BYTES)),
        name="rg_lru",
    )(p, p, cw, cb, wax, ba, bx, lam)


def _blockdiag_in_out(w_ref, hh):
    wh = w_ref[...][:, hh * ML_DH:(hh + 1) * ML_DH]
    wh = jnp.concatenate([wh, jnp.zeros((ML_QKV_BLOCK, ML_DHP - ML_DH), F32)], axis=1)
    r = lax.broadcasted_iota(jnp.int32, (ML_PW, ML_DHP), 0)
    c = lax.broadcasted_iota(jnp.int32, (ML_PW, ML_DHP), 1)
    d = jnp.zeros((ML_PW, ML_DHP), F32)
    for i in range(ML_QKV_BLOCK):
        d = jnp.where((r & (ML_QKV_BLOCK - 1)) == i, wh[i:i + 1, :], d)
    keep = (((r >> ML_QKV_SHIFT) - hh * (ML_DH // ML_QKV_BLOCK)) == (c >> ML_QKV_SHIFT)) & (c < ML_DH)
    return jnp.where(keep, d, 0.0)


def _blockdiag_in_window(w_ref, hh):
    w0 = ML_WINDOW[hh]
    wh = w_ref[...][:, w0:w0 + ML_DHP]
    r = lax.broadcasted_iota(jnp.int32, (ML_DHP, ML_DHP), 0) + w0
    c = lax.broadcasted_iota(jnp.int32, (ML_DHP, ML_DHP), 1) + w0
    d = jnp.zeros((ML_DHP, ML_DHP), F32)
    for i in range(ML_QKV_BLOCK):
        d = jnp.where((r & (ML_QKV_BLOCK - 1)) == i, wh[i:i + 1, :], d)
    keep = ((r >> ML_QKV_SHIFT) == (c >> ML_QKV_SHIFT)) & (c >= hh * ML_DH) & (c < (hh + 1) * ML_DH)
    return jnp.where(keep, d, 0.0)


def _blockdiag_out_in(w_ref, hh):
    wk = w_ref[...]
    o = lax.broadcasted_iota(jnp.int32, (ML_DHP, ML_PW), 0)
    r = lax.broadcasted_iota(jnp.int32, (ML_DHP, ML_PW), 1)
    d = jnp.zeros((ML_DHP, ML_PW), F32)
    for j in range(ML_QKV_BLOCK):
        d = jnp.where((o & (ML_QKV_BLOCK - 1)) == j, wk[j:j + 1, :], d)
    keep = ((o >> ML_QKV_SHIFT) == ((r >> ML_QKV_SHIFT) - hh * (ML_DH // ML_QKV_BLOCK))) & (o < ML_DH)
    return jnp.where(keep, d, 0.0)


ML_WINDOW = (0, ML_PW - ML_DHP)
ML_DEN_LANE = (ML_DH + 2, 2)


def _mlstm_kernel(u_ref, og_ref, z_ref, ig_ref, fg_ref, bi_ref, bf_ref, cw_ref, cb_ref,
                  wq_ref, wkt_ref, wv_ref, ng_ref, y_ref,
                  dq_s, dkt_s, dv_s, q_s, kt_s, v_s, cp_s, c_s, r_s, w_s, col_s, so_s, sn_s, loca_s, locb_s, *, layer):
    t = u_ref.shape[1]
    nc = t // ML_CHUNK
    L = ML_CHUNK
    heads = range(2)

    @pl.when(pl.program_id(1) == 0)
    def _():
        for hh in heads:
            dq_s[hh] = _blockdiag_in_out(wq_ref, hh).astype(BF16)
            dkt_s[hh] = _blockdiag_out_in(wkt_ref, hh).astype(BF16)
            dv_s[hh] = _blockdiag_in_window(wv_ref, hh).astype(BF16)

    u = u_ref[0]
    uc = _silu(_causal_conv(u.astype(F32), cw_ref, _layer_row(cb_ref, layer))).astype(BF16)
    for hh in heads:
        w0 = ML_WINDOW[hh]
        uc_w, u_w = uc[:, w0:w0 + ML_DHP], u[:, w0:w0 + ML_DHP]
        q_s[hh] = jnp.dot(uc_w, dq_s[hh, w0:w0 + ML_DHP, :], preferred_element_type=F32).astype(BF16)
        kt = lax.dot_general(dkt_s[hh, :, w0:w0 + ML_DHP], uc_w, NT_DIMS, preferred_element_type=F32)
        kt = kt * (ML_DH ** -0.5)
        for c in range(nc):
            kt_s[hh, c] = kt[:, c * L:(c + 1) * L].astype(BF16)
        v = jnp.dot(u_w, dv_s[hh], preferred_element_type=F32)
        vlane = lax.broadcasted_iota(jnp.int32, v.shape, 1)
        v_s[hh] = jnp.where(vlane == ML_DEN_LANE[hh], 1.0, v).astype(BF16)

    glane = lax.broadcasted_iota(jnp.int32, (nc, L), 1)
    grow = lax.broadcasted_iota(jnp.int32, (nc, L), 0)
    row8 = lax.broadcasted_iota(jnp.int32, (SUBLANE, L), 0)
    for hh in heads:
        ig = ig_ref[hh, 0] + bi_ref[hh]
        lf = -_softplus(-(fg_ref[hh, 0] + bf_ref[hh]))
        b = lf
        for k in range(int(math.log2(L))):
            sh = 1 << k
            b = b + jnp.where(glane >= sh, pltpu.roll(b, sh, axis=1), 0.0)
        g = jnp.broadcast_to(b[:, L - 1:L], (nc, L))
        a = g - b + ig
        mloc = jnp.broadcast_to(jnp.max(a, axis=1, keepdims=True), (nc, L))
        m = jnp.zeros((1, L), F32)
        m_prev = jnp.zeros((nc, L), F32)
        m_next = jnp.zeros((nc, L), F32)
        for c in range(nc):
            m_prev = jnp.where(grow == c, m, m_prev)
            m = jnp.maximum(g[c:c + 1, :] + m, mloc[c:c + 1, :])
            m_next = jnp.where(grow == c, m, m_next)
        r = ig - b
        cmx = r
        for k in range(int(math.log2(L))):
            sh = 1 << k
            cmx = jnp.maximum(cmx, jnp.where(glane >= sh, pltpu.roll(cmx, sh, axis=1), -jnp.inf))
        mm = jnp.maximum(m_prev, cmx)
        s_int = jnp.exp(m_prev - mm)
        clamp = jnp.exp(-(b + mm))
        r_s[hh] = r
        w_s[hh] = jnp.exp(a - mloc)
        so_s[hh] = jnp.exp(g + m_prev - m_next)
        sn_s[hh] = jnp.exp(mloc - m_next)
        for c in range(nc):
            col_s[hh, c] = jnp.where(row8 == 0, mm[c:c + 1, :],
                                     jnp.where(row8 == 1, s_int[c:c + 1, :],
                                               jnp.where(row8 == 2, clamp[c:c + 1, :], 0.0)))

    zero_rows = jnp.zeros((ML_DHP - ML_DH, ML_DHP), BF16)
    c_s[...] = jnp.zeros(c_s.shape, F32)

    n_pairs = nc // 2

    def local_states(pair, loc_ref):
        for i in range(2):
            c = 2 * pair + i
            r0 = pl.multiple_of(c * L, L)
            for hh in heads:
                ktw = (kt_s[hh, c, :ML_DH, :].astype(F32) * w_s[hh, pl.ds(c, 1), :]).astype(BF16)
                loc_ref[i, hh] = jnp.dot(ktw, v_s[hh, pl.ds(r0, L), :], preferred_element_type=F32)

    def advance(pair, loc_ref):
        for i in range(2):
            c = 2 * pair + i
            for hh in heads:
                c_prev = c_s[hh]
                cp_s[hh, c] = jnp.concatenate([c_prev.astype(BF16), zero_rows], axis=0)
                c_s[hh] = (so_s[hh, pl.ds(c, 1), :][:, :1] * c_prev
                           + sn_s[hh, pl.ds(c, 1), :][:, :1] * loc_ref[i, hh])

    def state(k, carry):
        local_states(2 * k + 1, locb_s)
        advance(2 * k, loca_s)
        local_states(jnp.minimum(2 * k + 2, n_pairs - 1), loca_s)
        advance(2 * k + 1, locb_s)
        return carry

    local_states(0, loca_s)
    lax.fori_loop(0, n_pairs // 2, state, 0)

    tri = (lax.broadcasted_iota(jnp.int32, (L, L), 0) >= lax.broadcasted_iota(jnp.int32, (L, L), 1))
    first = lax.broadcasted_iota(jnp.int32, (L, ML_PW), 1) < ML_DH

    def chunk_rows(c):
        return pl.ds(pl.multiple_of(c * L, L), L)

    def raw_scores(c):
        rows = chunk_rows(c)
        return [jnp.dot(q_s[hh, rows, :], kt_s[hh, c], preferred_element_type=F32) for hh in heads]

    def weights(c, raw):
        out = []
        for hh in heads:
            c0 = ML_DEN_LANE[hh] % LANE - 2
            pads = [jnp.zeros((n, L), F32) for n in (c0, L - SUBLANE - c0)]
            cols = jnp.concatenate([a for a in (pads[0], col_s[hh, c], pads[1]) if a.shape[0]], axis=0).T
            rb = jnp.broadcast_to(r_s[hh, pl.ds(c, 1), :], (L, L))
            decay = jnp.exp(jnp.where(tri, rb - cols[:, c0:c0 + 1], -jnp.inf))
            out.append(((raw[hh] * decay).astype(BF16), cols[:, c0 + 1:c0 + 2], cols[:, c0 + 2:c0 + 3]))
        return out

    def numerators(c, wts):
        rows = chunk_rows(c)
        nds, invs = [], []
        for hh, (s_mat, s_int, clamp) in zip(heads, wts):
            nd = (jnp.dot(s_mat, v_s[hh, rows, :], preferred_element_type=F32)
                  + s_int * jnp.dot(q_s[hh, rows, :], cp_s[hh, c], preferred_element_type=F32))
            den = nd[:, ML_DEN_LANE[hh]:ML_DEN_LANE[hh] + 1]
            nds.append(nd)
            invs.append(1.0 / jnp.maximum(jnp.abs(den), clamp))
        w1 = ML_WINDOW[1]
        shared_first = lax.broadcasted_iota(jnp.int32, (L, ML_DHP - w1), 1) < ML_DH - w1
        shared = jnp.where(shared_first, nds[0][:, w1:], nds[1][:, :ML_DHP - w1])
        return jnp.concatenate([nds[0][:, :w1], shared, nds[1][:, ML_DHP - w1:]], axis=1), invs

    def gates(c):
        rows = chunk_rows(c)
        return (_sigmoid(og_ref[0, rows, :].astype(F32)),
                _silu(z_ref[0, rows, :].astype(F32)) * _layer_row(ng_ref, layer))

    def means(nd, og):
        x = og * nd
        mu0 = jnp.sum(jnp.where(first, x, 0.0), axis=1, keepdims=True) * (1.0 / ML_DH)
        mu1 = jnp.sum(jnp.where(first, 0.0, x), axis=1, keepdims=True) * (1.0 / ML_DH)
        return x, mu0, mu1

    def variances(x, mu0, mu1):
        dev = x - jnp.where(first, mu0, mu1)
        sq = dev * dev
        var0 = jnp.sum(jnp.where(first, sq, 0.0), axis=1, keepdims=True) * (1.0 / ML_DH)
        var1 = jnp.sum(jnp.where(first, 0.0, sq), axis=1, keepdims=True) * (1.0 / ML_DH)
        return dev, var0, var1

    def finish(c, dev, var0, var1, invs, zs):
        f0 = invs[0] * lax.rsqrt(invs[0] * invs[0] * var0 + LN_EPS)
        f1 = invs[1] * lax.rsqrt(invs[1] * invs[1] * var1 + LN_EPS)
        y_ref[0, chunk_rows(c), :] = (dev * jnp.where(first, f0, f1) * zs).astype(y_ref.dtype)

    def group(gi, carry):
        cs = [gi * ML_GROUP + i for i in range(ML_GROUP)]
        n = range(ML_GROUP)
        raws = [raw_scores(c) for c in cs]
        gts = [gates(c) for c in cs]
        wts = [weights(c, raws[i]) for i, c in enumerate(cs)]
        nums = [numerators(c, wts[i]) for i, c in enumerate(cs)]
        mus = [means(nums[i][0], gts[i][0]) for i in n]
        vrs = [variances(*mus[i]) for i in n]
        for i, c in enumerate(cs):
            finish(c, *vrs[i], nums[i][1], gts[i][1])
        return carry

    lax.fori_loop(0, nc // ML_GROUP, group, 0)


def _mlstm_branch(p, gates, gbias, cw, cb, wq, wkt, wv, ng, l):
    b, t, _ = p.shape
    nc = t // ML_CHUNK
    ub, ob, zb = C_ML_U // ML_PW, C_ML_O // ML_PW, C_ML_Z // ML_PW
    seq = lambda base: pl.BlockSpec((1, t, ML_PW), lambda pr, bi: (bi, 0, base + pr))
    vec = pl.BlockSpec((cb.shape[0], ML_PW), lambda pr, bi: (0, pr))
    taps = pl.BlockSpec((None, ML_QKV_BLOCK, ML_PW), lambda pr, bi: (l, 0, pr))
    gate_rows = lambda: pltpu.VMEM((2, nc, ML_CHUNK), F32)
    pair_states = lambda: pltpu.VMEM((2, 2, ML_DH, ML_DHP), F32)
    return pl.pallas_call(
        functools.partial(_mlstm_kernel, layer=l),
        grid=(ML_PAIRS, b),
        in_specs=[
            seq(ub), seq(ob), seq(zb),
            pl.BlockSpec((2, 1, nc, ML_CHUNK), lambda pr, bi: (pr, bi, 0, 0)),
            pl.BlockSpec((2, 1, nc, ML_CHUNK), lambda pr, bi: (ML_PAIRS + pr, bi, 0, 0)),
            pl.BlockSpec((None, 2, 1, ML_CHUNK), lambda pr, bi: (l, pr, 0, 0)),
            pl.BlockSpec((None, 2, 1, ML_CHUNK), lambda pr, bi: (l, ML_PAIRS + pr, 0, 0)),
            pl.BlockSpec((None, cw.shape[1], ML_PW), lambda pr, bi: (l, 0, pr)),
            vec, taps, taps, taps, vec,
        ],
        out_specs=pl.BlockSpec((1, t, ML_PW), lambda pr, bi: (bi, 0, pr)),
        out_shape=jax.ShapeDtypeStruct((b, t, ML_W), BF16),
        scratch_shapes=[
            pltpu.VMEM((2, ML_PW, ML_DHP), BF16),
            pltpu.VMEM((2, ML_DHP, ML_PW), BF16),
            pltpu.VMEM((2, ML_DHP, ML_DHP), BF16),
            pltpu.VMEM((2, t, ML_DHP), BF16),
            pltpu.VMEM((2, nc, ML_DHP, ML_CHUNK), BF16),
            pltpu.VMEM((2, t, ML_DHP), BF16),
            pltpu.VMEM((2, nc, ML_DHP, ML_DHP), BF16),
            pltpu.VMEM((2, ML_DH, ML_DHP), F32),
            gate_rows(), gate_rows(),
            pltpu.VMEM((2, nc, SUBLANE, ML_CHUNK), F32),
            gate_rows(), gate_rows(),
            pair_states(), pair_states(),
        ],
        compiler_params=pltpu.CompilerParams(
            dimension_semantics=("arbitrary", "arbitrary"), vmem_limit_bytes=_vmem_limit(56 << 20)),
        name="mlstm",
    )(p, p, p, gates, gates, gbias, gbias, cw, cb, wq, wkt, wv, ng)


MOBA_SEL_LANE = 6
MOBA_AHEAD = 3


def _moba_key_ext(nb):
    ext = np.zeros((nb, MOBA_BLOCK, MO_DH), np.float32)
    for n in range(nb):
        ext[n, :, 0:3] = n * MOBA_BLOCK
        ext[n, :, 3:6] = np.arange(MOBA_BLOCK, dtype=np.float32)[:, None]
        ext[n, :, MOBA_SEL_LANE + n] = 1.0
    return jnp.asarray(ext, dtype=BF16)


def _moba_slope_rows(slopes):
    rows = np.zeros((len(slopes), 1, MO_DH), np.float32)
    for h, s in enumerate(slopes):
        rest = np.float32(np.float32(s) * np.float32(LOG2E))
        for i in range(3):
            piece = np.float32(rest).astype(BF16).astype(np.float32)
            rows[h, 0, i] = rows[h, 0, 3 + i] = piece
            rest = np.float32(rest - piece)
    return jnp.asarray(rows)


def _moba_kernel(q_ref, k_ref, v_ref, z_ref, srow_ref, kext_ref, o_ref, l_s, qa_s):
    t = q_ref.shape[1]
    nb = t // MOBA_BLOCK
    bs = MOBA_BLOCK
    qscale = (MO_DH ** -0.5) * LOG2E

    krow = lax.broadcasted_iota(jnp.int32, (LANE, MO_DH), 0)
    kmean = jnp.zeros((LANE, MO_DH), F32)
    for n in range(nb):
        mean_n = jnp.sum(k_ref[0, n * bs:(n + 1) * bs, :].astype(F32), axis=0, keepdims=True) * (1.0 / bs)
        kmean = jnp.where(krow == MOBA_SEL_LANE + n, mean_n, kmean)
    kmean = kmean.astype(BF16)

    lane = lax.broadcasted_iota(jnp.int32, (bs, MO_DH), 1)
    causal = (lax.broadcasted_iota(jnp.int32, (bs, bs), 0) >= lax.broadcasted_iota(jnp.int32, (bs, bs), 1))
    ones_col = jnp.where(lane == 0, 1.0, 0.0).astype(BF16)
    slope_cols = jnp.where(lane < MOBA_SEL_LANE, srow_ref[0], 0.0)

    def prepare(qb):
        q_b = q_ref[0, qb * bs:(qb + 1) * bs, :]
        q_ext = slope_cols
        if qb > MOBA_TOPK:
            gate = lax.dot_general(q_b, kmean, NT_DIMS, preferred_element_type=F32)
            beaten = jnp.zeros((bs, MO_DH), F32)
            for m in range(qb):
                gm = gate[:, MOBA_SEL_LANE + m:MOBA_SEL_LANE + m + 1]
                wins = (gm > gate) | ((gm == gate) & (lane > MOBA_SEL_LANE + m))
                beaten = beaten + jnp.where(wins, 1.0, 0.0)
            past = (lane >= MOBA_SEL_LANE) & (lane < MOBA_SEL_LANE + qb)
            q_ext = jnp.where(past & (beaten >= float(MOBA_TOPK)), NEG_INF, slope_cols)
        qa_s[qb] = jnp.concatenate([(q_b.astype(F32) * qscale).astype(BF16), q_ext.astype(BF16)], axis=1)

    def scores(qb):
        q_aug = qa_s[qb]
        l_q = l_s.at[qb % l_s.shape[0]]
        mx = None
        for n in range(qb + 1):
            k_aug = jnp.concatenate([k_ref[0, n * bs:(n + 1) * bs, :], kext_ref[n]], axis=1)
            logit = lax.dot_general(q_aug, k_aug, NT_DIMS, preferred_element_type=F32)
            if n == qb:
                logit = jnp.where(causal, logit, NEG_INF)
            l_q[:, n * bs:(n + 1) * bs] = logit
            half = jnp.maximum(logit[:, :LANE], logit[:, LANE:])
            mx = half if mx is None else jnp.maximum(mx, half)
        return jnp.max(mx, axis=1, keepdims=True)

    def outputs(qb, m_row):
        l_q = l_s.at[qb % l_s.shape[0]]
        acc = jnp.zeros((bs, 2 * MO_DH), F32)
        for n in range(qb + 1):
            pr = jnp.exp2(l_q[:, n * bs:(n + 1) * bs] - m_row).astype(BF16)
            v_aug = jnp.concatenate([v_ref[0, n * bs:(n + 1) * bs, :], ones_col], axis=1)
            acc = acc + jnp.dot(pr, v_aug, preferred_element_type=F32)
        z = z_ref[0, qb * bs:(qb + 1) * bs, :].astype(F32)
        inv = 1.0 / acc[:, MO_DH:MO_DH + 1]
        o_ref[0, qb * bs:(qb + 1) * bs, :] = (acc[:, :MO_DH] * (inv * _silu(z))).astype(o_ref.dtype)

    ahead = l_s.shape[0] - 1
    for qb in range(min(ahead + 1, nb)):
        prepare(qb)
    m_rows = {qb: scores(qb) for qb in range(min(ahead, nb))}
    for qb in range(nb):
        if qb + ahead + 1 < nb:
            prepare(qb + ahead + 1)
        if qb + ahead < nb:
            m_rows[qb + ahead] = scores(qb + ahead)
        outputs(qb, m_rows.pop(qb))


def _moba_branch(p, srows, kext):
    b, t, _ = p.shape
    nb = t // MOBA_BLOCK
    qb, kb, vb, zb = (c // MO_DH for c in (C_MO_Q, C_MO_K, C_MO_V, C_MO_Z))
    seq = lambda base: pl.BlockSpec((1, t, MO_DH), lambda bi, h: (bi, 0, base + h))
    return pl.pallas_call(
        _moba_kernel,
        grid=(b, MO_HEADS),
        in_specs=[seq(qb), seq(kb), seq(vb), seq(zb),
                  pl.BlockSpec((1, 1, MO_DH), lambda bi, h: (h, 0, 0)),
                  pl.BlockSpec((nb, MOBA_BLOCK, MO_DH), lambda bi, h: (0, 0, 0))],
        out_specs=pl.BlockSpec((1, t, MO_DH), lambda bi, h: (bi, 0, h)),
        out_shape=jax.ShapeDtypeStruct((b, t, MO_W), BF16),
        scratch_shapes=[pltpu.VMEM((MOBA_AHEAD + 1, MOBA_BLOCK, t), F32),
                        pltpu.VMEM((nb, MOBA_BLOCK, 2 * MO_DH), BF16)],
        compiler_params=pltpu.CompilerParams(
            dimension_semantics=("arbitrary", "arbitrary"), vmem_limit_bytes=_vmem_limit(V7X_VMEM_BYTES)),
        name="moba",
    )(p, p, p, p, srows, kext)


def _stage_weights(w_hbm, layer, dst_ref, stage_ref, sem, overlap=None):
    slots, tile_rows = stage_ref.shape[0], stage_ref.shape[1]
    n_tiles = dst_ref.shape[0] // tile_rows

    def copy(k):
        return pltpu.make_async_copy(w_hbm.at[layer, pl.ds(k * tile_rows, tile_rows), :], stage_ref.at[k % slots],
                                     sem.at[k % slots])

    for k in range(min(slots - 1, n_tiles)):
        copy(k).start()
    if overlap is not None:
        overlap()
    for k in range(n_tiles):
        if k + slots - 1 < n_tiles:
            copy(k + slots - 1).start()
        copy(k).wait()
        dst_ref[k * tile_rows:(k + 1) * tile_rows, :] = stage_ref[k % slots].astype(BF16)


def _mem_kv_kernel(mem_ref, g_ref, w_hbm, o_ref, w_s, stage_s, sem, hm_s):
    layer = pl.program_id(0)
    g = g_ref[pl.ds(layer, 1), :]

    def normalise():
        for bi in range(mem_ref.shape[0]):
            hm_s[bi] = _rmsnorm_rows(mem_ref[bi], g).astype(BF16)

    _stage_weights(w_hbm, layer, w_s, stage_s, sem, overlap=normalise)
    for bi in range(mem_ref.shape[0]):
        o_ref[bi] = jnp.dot(hm_s[bi], w_s[...], preferred_element_type=F32).astype(o_ref.dtype)


def _mem_kv(mem, g, wkv):
    b, m, _ = mem.shape
    depth = wkv.shape[0]
    return pl.pallas_call(
        _mem_kv_kernel,
        grid=(depth,),
        in_specs=[pl.BlockSpec((b, m, D_MODEL), lambda l: (0, 0, 0)),
                  pl.BlockSpec(g.shape, lambda l: (0, 0)),
                  pl.BlockSpec(memory_space=pl.ANY)],
        out_specs=pl.BlockSpec((None, b, m, 2 * XA_W), lambda l: (l, 0, 0, 0)),
        out_shape=jax.ShapeDtypeStruct((depth, b, m, 2 * XA_W), BF16),
        scratch_shapes=[pltpu.VMEM((D_MODEL, 2 * XA_W), BF16),
                        pltpu.VMEM((MEM_KV_SLOTS, STAGE_ROWS, 2 * XA_W), F32),
                        pltpu.SemaphoreType.DMA((MEM_KV_SLOTS,)),
                        pltpu.VMEM((b, m, D_MODEL), BF16)],
        compiler_params=pltpu.CompilerParams(
            dimension_semantics=("arbitrary",), vmem_limit_bytes=_vmem_limit(V7X_VMEM_BYTES)),
        name="mem_kv",
    )(mem, g, wkv)


def _post_kernel(x_ref, yl_ref, ym_ref, yo_ref, wout_hbm, g_ref, wq_hbm, kv_ref, wo_hbm, fg_ref, o_ref,
                 x1_s, wout_ref, wq_ref, wo_ref, stage_s, stage_q_s, sem, sem_q, *, final_norm, layer):
    @pl.when((pl.program_id(0) == 0) & (pl.program_id(1) == 0))
    def _():
        _stage_weights(wout_hbm, layer, wout_ref, stage_s, sem)
        _stage_weights(wo_hbm, layer, wo_ref, stage_s, sem)
        _stage_weights(wq_hbm, layer, wq_ref, stage_q_s, sem_q)

    sub = x_ref.shape[1] // POST_SUB
    groups = [pl.ds(i * sub, sub) for i in range(POST_SUB)]

    for r in groups:
        acc = jnp.dot(yl_ref[0, r, :], wout_ref[0:LRU_W, :], preferred_element_type=F32)
        acc = acc + jnp.dot(ym_ref[0, r, :], wout_ref[LRU_W:LRU_W + ML_W, :], preferred_element_type=F32)
        acc = acc + jnp.dot(yo_ref[0, r, :], wout_ref[LRU_W + ML_W:, :], preferred_element_type=F32)
        x1_s[r, :] = x_ref[0, r, :] + acc
    qs = []
    for r in groups:
        hx = _rmsnorm_rows(x1_s[r, :], _layer_row(g_ref, layer)).astype(BF16)
        qs.append(jnp.dot(hx, wq_ref[...], preferred_element_type=F32).astype(BF16))
    scores = []
    for q in qs:
        scores.append([
            lax.dot_general(q[:, h * XA_DH:(h + 1) * XA_DH], kv_ref[0, :, h * XA_DH:(h + 1) * XA_DH], NT_DIMS,
                            preferred_element_type=F32) * (XA_DH ** -0.5)
            for h in range(XA_HEADS)])
    outs = []
    for sc in scores:
        heads = []
        for h, s in enumerate(sc):
            e = jnp.exp(s - jnp.max(s, axis=1, keepdims=True))
            pr = e / jnp.sum(e, axis=1, keepdims=True)
            v_h = kv_ref[0, :, XA_W + h * XA_DH:XA_W + (h + 1) * XA_DH]
            heads.append(jnp.dot(pr.astype(BF16), v_h, preferred_element_type=F32).astype(BF16))
        outs.append(jnp.concatenate(heads, axis=1))
    for r, o in zip(groups, outs):
        y = x1_s[r, :] + jnp.dot(o, wo_ref[...], preferred_element_type=F32)
        if final_norm:
            y = _rmsnorm_rows(y, fg_ref[...])
        o_ref[0, r, :] = y


def _post_mix(x, yl, ym, yo, wout, g, wq, kv, wo, fg, l, final_norm):
    b, t, _ = x.shape
    m = kv.shape[2]
    rows = lambda width: pl.BlockSpec((1, POST_TM, width), lambda bi, i: (bi, i, 0))
    hbm = pl.BlockSpec(memory_space=pl.ANY)
    weights = (D_MODEL * D_MODEL + D_MODEL * XA_W + XA_W * D_MODEL) * 2
    stages = POST_SLOTS * STAGE_ROWS * (D_MODEL + XA_W) * 4
    tiles = 2 * POST_TM * (2 * D_MODEL * 4 + D_MODEL * 2) + 2 * m * 2 * XA_W * 2
    vmem = weights + stages + tiles + 6 * POST_TM * D_MODEL * 4
    return pl.pallas_call(
        functools.partial(_post_kernel, final_norm=final_norm, layer=l),
        grid=(b, t // POST_TM),
        in_specs=[rows(D_MODEL), rows(LRU_W), rows(ML_W), rows(MO_W),
                  hbm,
                  pl.BlockSpec(g.shape, lambda bi, i: (0, 0)),
                  hbm,
                  pl.BlockSpec((None, 1, m, 2 * XA_W), lambda bi, i: (l, bi, 0, 0)),
                  hbm,
                  pl.BlockSpec((1, D_MODEL), lambda bi, i: (0, 0))],
        out_specs=rows(D_MODEL),
        out_shape=jax.ShapeDtypeStruct((b, t, D_MODEL), F32),
        scratch_shapes=[pltpu.VMEM((POST_TM, D_MODEL), F32),
                        pltpu.VMEM((D_MODEL, D_MODEL), BF16),
                        pltpu.VMEM((D_MODEL, XA_W), BF16),
                        pltpu.VMEM((XA_W, D_MODEL), BF16),
                        pltpu.VMEM((POST_SLOTS, STAGE_ROWS, D_MODEL), F32),
                        pltpu.VMEM((POST_SLOTS, STAGE_ROWS, XA_W), F32),
                        pltpu.SemaphoreType.DMA((POST_SLOTS,)),
                        pltpu.SemaphoreType.DMA((POST_SLOTS,))],
        compiler_params=pltpu.CompilerParams(
            dimension_semantics=("arbitrary", "arbitrary"), vmem_limit_bytes=_vmem_limit(vmem)),
        name="post_mix",
    )(x, yl, ym, yo, wout, g, wq, kv, wo, fg)


def _alibi_slopes(n):
    def pow2(m):
        start = 2.0 ** (-8.0 / m)
        return [start ** (i + 1) for i in range(m)]
    if math.log2(n).is_integer():
        s = pow2(n)
    else:
        c = 2 ** int(math.floor(math.log2(n)))
        s = pow2(c) + pow2(2 * c)[0::2][:n - c]
    return np.asarray(s, dtype=np.float32)


def _taps(w, perm):
    depth = w.shape[0]
    return jnp.transpose(w, perm).reshape(depth, ML_QKV_BLOCK, ML_W)


def kernel(x, mem, mix_norm_g, w_in, lru_conv_w, lru_conv_b, lru_wa, lru_ba, lru_wx, lru_bx, lru_lambda,
           ml_conv_w, ml_conv_b, ml_wq, ml_wk, ml_wv, ml_bi, ml_bf, ml_norm_g, w_out, xa_norm_g, mem_norm_g,
           xa_wq, xa_wkv, xa_wo, final_norm_g):
    b, t, d = x.shape
    depth = w_in.shape[0]
    nc = t // ML_CHUNK
    moba_srows = _moba_slope_rows(_alibi_slopes(MO_HEADS))
    moba_kext = _moba_key_ext(t // MOBA_BLOCK)
    w_in_t = jnp.swapaxes(w_in, 1, 2)
    lru_wax = jnp.concatenate([lru_wa, lru_wx], axis=-1).astype(BF16)
    ml_wq_t = _taps(ml_wq, (0, 2, 1, 3))
    ml_wkt_t = _taps(ml_wk, (0, 3, 1, 2))
    ml_wv_t = _taps(ml_wv, (0, 2, 1, 3))
    ml_gbias = jnp.broadcast_to(jnp.concatenate([ml_bi, ml_bf], axis=1)[:, :, None, None],
                                (depth, 2 * ML_HEADS, 1, ML_CHUNK))
    kv = _mem_kv(mem, mem_norm_g, xa_wkv)
    for l in range(depth):
        p, gates = _in_proj(x.reshape(b * t, d), mix_norm_g, w_in_t, l)
        p, gates = p.reshape(b, t, IN_COLS_P), gates.reshape(2 * ML_HEADS, b, nc, ML_CHUNK)
        y_lru = _lru_branch(p, lru_conv_w, lru_wax, lru_conv_b, lru_ba, lru_bx, lru_lambda, l)
        y_ml = _mlstm_branch(p, gates, ml_gbias, ml_conv_w, ml_conv_b, ml_wq_t, ml_wkt_t, ml_wv_t, ml_norm_g, l)
        y_mo = _moba_branch(p, moba_srows, moba_kext)
        x = _post_mix(x, y_lru, y_ml, y_mo, w_out, xa_norm_g, xa_wq, kv, xa_wo, final_norm_g[None, :], l,
                      final_norm=(l == depth - 1))
    return x
```

```python
import functools
import math

import jax
import jax.numpy as jnp
import numpy as np
from jax import lax
from jax.experimental import pallas as pl
from jax.experimental.pallas import tpu as pltpu

LANE = 128
SUBLANE = 8
V7X_VMEM_BYTES = 64 * 1024 * 1024

D_MODEL = 2048
LRU_W = 512
LRU_BLOCKS = 4
LRU_BW = LRU_W // LRU_BLOCKS
LRU_C = 8.0
ML_W = 768
ML_HEADS = 4
ML_DH = 192
ML_DHP = 256
ML_PAIRS = 2
ML_PW = 2 * ML_DH
ML_QKV_BLOCK = 4
ML_QKV_SHIFT = 2
ML_CHUNK = 128
ML_GROUP = 4
MO_W = 768
MO_HEADS = 6
MO_DH = 128
MOBA_BLOCK = 256
MOBA_TOPK = 3
XA_HEADS = 4
XA_DH = 128
XA_W = XA_HEADS * XA_DH
RMS_EPS = 1e-6
LN_EPS = 1e-5
NEG_INF = -1e30
LOG2E = 1.4426950408889634

REF_LRU_END = 2 * LRU_W
REF_ML_END = REF_LRU_END + 3 * ML_W
REF_GATE_END = REF_ML_END + 2 * ML_HEADS

C_ML_U = 0
C_ML_O = C_ML_U + ML_W
C_ML_Z = C_ML_O + ML_W
C_LRU_X = C_ML_Z + ML_W
C_LRU_Z = C_LRU_X + LRU_W
C_MO_Q = C_LRU_Z + LRU_W
C_MO_K = C_MO_Q + MO_W
C_MO_V = C_MO_K + MO_W
C_MO_Z = C_MO_V + MO_W
C_GATE = C_MO_Z + MO_W
IN_TM = 512
IN_TN = 3328
IN_COLS_P = 6656
PREP_TR = 256
STAGE_ROWS = 256
MEM_KV_SLOTS = 4
POST_SLOTS = 4
POST_TM = 512
POST_SUB = 2

BF16 = jnp.bfloat16
F32 = jnp.float32
NT_DIMS = (((1,), (1,)), ((), ()))


def _vmem_limit(nbytes):
    return int(min(V7X_VMEM_BYTES - (4 << 20), max(32 << 20, nbytes)))


def _rmsnorm_rows(x, g):
    ms = jnp.mean(x * x, axis=-1, keepdims=True)
    return x * lax.rsqrt(ms + RMS_EPS) * g


def _sigmoid(x):
    return jax.nn.sigmoid(x)


def _silu(x):
    return x * jax.nn.sigmoid(x)


def _softplus(x):
    return jnp.maximum(x, 0.0) + jnp.log1p(jnp.exp(-jnp.abs(x)))


def _shift_rows(x, s):
    rolled = pltpu.roll(x, s, axis=0)
    row = lax.broadcasted_iota(jnp.int32, x.shape, 0)
    return jnp.where(row >= s, rolled, 0.0)


def _layer_row(ref, layer):
    return ref[layer:layer + 1, :]


def _causal_conv(x, w_ref, b):
    k = w_ref.shape[0]
    acc = x * w_ref[k - 1:k, :]
    head = x[:SUBLANE]
    acc_head = head * w_ref[k - 1:k, :]
    for j in range(k - 1):
        acc = acc + pltpu.roll(x, k - 1 - j, axis=0) * w_ref[j:j + 1, :]
        acc_head = acc_head + _shift_rows(head, k - 1 - j) * w_ref[j:j + 1, :]
    return jnp.concatenate([acc_head, acc[SUBLANE:]], axis=0) + b


def _w_in_src_rows():
    n_ml = (C_LRU_X - C_ML_U) // PREP_TR
    n_lru = (C_MO_Q - C_LRU_X) // PREP_TR
    n_mo = (C_GATE - C_MO_Q) // PREP_TR
    return ([REF_LRU_END + i * PREP_TR for i in range(n_ml)] + [i * PREP_TR for i in range(n_lru)]
            + [REF_GATE_END + i * PREP_TR for i in range(n_mo)] + [REF_ML_END])


def _in_proj_kernel(x_ref, g_ref, w_hbm, o_ref, gt_ref, w_s, stage_s, sem, *, layer):
    @pl.when(pl.program_id(0) == 0)
    def _():
        src = _w_in_src_rows()

        def copy(k):
            return pltpu.make_async_copy(w_hbm.at[layer, pl.ds(src[k], PREP_TR), :], stage_s.at[k % 2], sem.at[k % 2])

        copy(0).start()
        for k in range(len(src)):
            if k + 1 < len(src):
                copy(k + 1).start()
            copy(k).wait()
            tile = stage_s[k % 2]
            if k == len(src) - 1:
                row = lax.broadcasted_iota(jnp.int32, tile.shape, 0)
                tile = jnp.where(row < 2 * ML_HEADS, tile, 0.0)
            w_s[k * PREP_TR:(k + 1) * PREP_TR, :] = tile.astype(BF16)

    xn = _rmsnorm_rows(x_ref[...], _layer_row(g_ref, layer)).astype(BF16)
    n_chunks = w_s.shape[0] // IN_TN
    for j in range(n_chunks):
        cols = pl.ds(j * IN_TN, IN_TN)
        res = lax.dot_general(xn, w_s[cols, :], NT_DIMS, preferred_element_type=F32)
        o_ref[:, cols] = res.astype(o_ref.dtype)
        if j == C_GATE // IN_TN:
            g0 = C_GATE - j * IN_TN
            gt_ref[...] = res[:, g0:g0 + LANE].T[:2 * ML_HEADS, :]


def _in_proj(x2d, g, w_in_t, l):
    m = x2d.shape[0]
    n = IN_COLS_P
    vmem = (2 * IN_TM * D_MODEL * 4 + 2 * IN_TM * D_MODEL * 2 + D_MODEL * n * 2 + 2 * PREP_TR * D_MODEL * 4
            + 2 * IN_TM * n * 2 + 2 * IN_TM * IN_TN * 4)
    return pl.pallas_call(
        functools.partial(_in_proj_kernel, layer=l),
        grid=(m // IN_TM,),
        in_specs=[
            pl.BlockSpec((IN_TM, D_MODEL), lambda i: (i, 0)),
            pl.BlockSpec(g.shape, lambda i: (0, 0)),
            pl.BlockSpec(memory_space=pl.ANY),
        ],
        out_specs=[pl.BlockSpec((IN_TM, n), lambda i: (i, 0)),
                   pl.BlockSpec((2 * ML_HEADS, IN_TM), lambda i: (0, i))],
        out_shape=[jax.ShapeDtypeStruct((m, n), BF16), jax.ShapeDtypeStruct((2 * ML_HEADS, m), F32)],
        scratch_shapes=[pltpu.VMEM((n, D_MODEL), BF16),
                        pltpu.VMEM((2, PREP_TR, D_MODEL), F32),
                        pltpu.SemaphoreType.DMA((2,))],
        compiler_params=pltpu.CompilerParams(
            dimension_semantics=("arbitrary",), vmem_limit_bytes=_vmem_limit(vmem)),
        name="in_proj",
    )(x2d, g, w_in_t)


def _lru_kernel(x_ref, z_ref, cw_ref, cb_ref, wax_ref, ba_ref, bx_ref, lam_ref, o_ref, a_s, u_s, *, layer):
    t = x_ref.shape[1]
    x = x_ref[0].astype(F32)
    xc = _causal_conv(x, cw_ref, _layer_row(cb_ref, layer))
    pre = jnp.dot(xc.astype(BF16), wax_ref[0], preferred_element_type=F32)
    r = _sigmoid(pre[:, :LRU_BW] + _layer_row(ba_ref, layer))
    i = _sigmoid(pre[:, LRU_BW:] + _layer_row(bx_ref, layer))
    log_a = (-LRU_C) * r * _softplus(-_layer_row(lam_ref, layer))
    a = jnp.exp(log_a)
    a_s[...] = a
    u_s[...] = jnp.sqrt(-jnp.tanh(log_a) * (1.0 + a * a)) * (i * xc)

    row = lax.broadcasted_iota(jnp.int32, (SUBLANE, LRU_BW), 0)

    def block(blk, h_prev):
        r0 = pl.multiple_of(blk * SUBLANE, SUBLANE)
        a_b = a_s[pl.ds(r0, SUBLANE), :]
        u_b = u_s[pl.ds(r0, SUBLANE), :]
        for s in (1, 2, 4):
            a_sh = jnp.where(row >= s, pltpu.roll(a_b, s, axis=0), 1.0)
            u_sh = jnp.where(row >= s, pltpu.roll(u_b, s, axis=0), 0.0)
            u_b = a_b * u_sh + u_b
            a_b = a_b * a_sh
        u_s[pl.ds(r0, SUBLANE), :] = a_b * h_prev + u_b
        last = lambda v: jnp.broadcast_to(v[SUBLANE - 1:SUBLANE, :], (SUBLANE, LRU_BW))
        return last(a_b) * h_prev + last(u_b)

    lax.fori_loop(0, t // SUBLANE, block, jnp.zeros((SUBLANE, LRU_BW), F32), unroll=4)
    z = z_ref[0].astype(F32)
    o_ref[0] = (u_s[...] * _silu(z)).astype(o_ref.dtype)


def _lru_branch(p, cw, wax, cb, ba, bx, lam, l):
    b, t, _ = p.shape
    xb, zb = C_LRU_X // LRU_BW, C_LRU_Z // LRU_BW
    vec = pl.BlockSpec((cb.shape[0], LRU_BW), lambda bi, g: (0, g))
    return pl.pallas_call(
        functools.partial(_lru_kernel, layer=l),
        grid=(b, LRU_BLOCKS),
        in_specs=[
            pl.BlockSpec((1, t, LRU_BW), lambda bi, g: (bi, 0, xb + g)),
            pl.BlockSpec((1, t, LRU_BW), lambda bi, g: (bi, 0, zb + g)),
            pl.BlockSpec((None, cw.shape[1], LRU_BW), lambda bi, g: (l, 0, g)),
            vec,
            pl.BlockSpec((None, 1, LRU_BW, 2 * LRU_BW), lambda bi, g: (l, g, 0, 0)),
            vec, vec, vec,
        ],
        out_specs=pl.BlockSpec((1, t, LRU_BW), lambda bi, g: (bi, 0, g)),
        out_shape=jax.ShapeDtypeStruct((b, t, LRU_W), BF16),
        scratch_shapes=[pltpu.VMEM((t, LRU_BW), F32), pltpu.VMEM((t, LRU_BW), F32)],
        compiler_params=pltpu.CompilerParams(
            dimension_semantics=("arbitrary", "arbitrary"), vmem_limit_bytes=_vmem_limit(V7X_VMEM_BYTES)),
        name="rg_lru",
    )(p, p, cw, cb, wax, ba, bx, lam)


def _blockdiag_in_out(w_ref, hh):
    wh = w_ref[...][:, hh * ML_DH:(hh + 1) * ML_DH]
    wh = jnp.concatenate([wh, jnp.zeros((ML_QKV_BLOCK, ML_DHP - ML_DH), F32)], axis=1)
    r = lax.broadcasted_iota(jnp.int32, (ML_PW, ML_DHP), 0)
    c = lax.broadcasted_iota(jnp.int32, (ML_PW, ML_DHP), 1)
    d = jnp.zeros((ML_PW, ML_DHP), F32)
    for i in range(ML_QKV_BLOCK):
        d = jnp.where((r & (ML_QKV_BLOCK - 1)) == i, wh[i:i + 1, :], d)
    keep = (((r >> ML_QKV_SHIFT) - hh * (ML_DH // ML_QKV_BLOCK)) == (c >> ML_QKV_SHIFT)) & (c < ML_DH)
    return jnp.where(keep, d, 0.0)


def _blockdiag_in_window(w_ref, hh):
    w0 = ML_WINDOW[hh]
    wh = w_ref[...][:, w0:w0 + ML_DHP]
    r = lax.broadcasted_iota(jnp.int32, (ML_DHP, ML_DHP), 0) + w0
    c = lax.broadcasted_iota(jnp.int32, (ML_DHP, ML_DHP), 1) + w0
    d = jnp.zeros((ML_DHP, ML_DHP), F32)
    for i in range(ML_QKV_BLOCK):
        d = jnp.where((r & (ML_QKV_BLOCK - 1)) == i, wh[i:i + 1, :], d)
    keep = ((r >> ML_QKV_SHIFT) == (c >> ML_QKV_SHIFT)) & (c >= hh * ML_DH) & (c < (hh + 1) * ML_DH)
    return jnp.where(keep, d, 0.0)


def _blockdiag_out_in(w_ref, hh):
    wk = w_ref[...]
    o = lax.broadcasted_iota(jnp.int32, (ML_DHP, ML_PW), 0)
    r = lax.broadcasted_iota(jnp.int32, (ML_DHP, ML_PW), 1)
    d = jnp.zeros((ML_DHP, ML_PW), F32)
    for j in range(ML_QKV_BLOCK):
        d = jnp.where((o & (ML_QKV_BLOCK - 1)) == j, wk[j:j + 1, :], d)
    keep = ((o >> ML_QKV_SHIFT) == ((r >> ML_QKV_SHIFT) - hh * (ML_DH // ML_QKV_BLOCK))) & (o < ML_DH)
    return jnp.where(keep, d, 0.0)


ML_WINDOW = (0, ML_PW - ML_DHP)
ML_DEN_LANE = (ML_DH + 2, 2)


def _mlstm_kernel(u_ref, og_ref, z_ref, ig_ref, fg_ref, bi_ref, bf_ref, cw_ref, cb_ref,
                  wq_ref, wkt_ref, wv_ref, ng_ref, y_ref,
                  dq_s, dkt_s, dv_s, q_s, kt_s, v_s, cp_s, c_s, r_s, w_s, col_s, so_s, sn_s, loca_s, locb_s, *, layer):
    t = u_ref.shape[1]
    nc = t // ML_CHUNK
    L = ML_CHUNK
    heads = range(2)

    @pl.when(pl.program_id(1) == 0)
    def _():
        for hh in heads:
            dq_s[hh] = _blockdiag_in_out(wq_ref, hh).astype(BF16)
            dkt_s[hh] = _blockdiag_out_in(wkt_ref, hh).astype(BF16)
            dv_s[hh] = _blockdiag_in_window(wv_ref, hh).astype(BF16)

    u = u_ref[0]
    uc = _silu(_causal_conv(u.astype(F32), cw_ref, _layer_row(cb_ref, layer))).astype(BF16)
    for hh in heads:
        w0 = ML_WINDOW[hh]
        uc_w, u_w = uc[:, w0:w0 + ML_DHP], u[:, w0:w0 + ML_DHP]
        q_s[hh] = jnp.dot(uc_w, dq_s[hh, w0:w0 + ML_DHP, :], preferred_element_type=F32).astype(BF16)
        kt = lax.dot_general(dkt_s[hh, :, w0:w0 + ML_DHP], uc_w, NT_DIMS, preferred_element_type=F32)
        kt = kt * (ML_DH ** -0.5)
        for c in range(nc):
            kt_s[hh, c] = kt[:, c * L:(c + 1) * L].astype(BF16)
        v = jnp.dot(u_w, dv_s[hh], preferred_element_type=F32)
        vlane = lax.broadcasted_iota(jnp.int32, v.shape, 1)
        v_s[hh] = jnp.where(vlane == ML_DEN_LANE[hh], 1.0, v).astype(BF16)

    glane = lax.broadcasted_iota(jnp.int32, (nc, L), 1)
    grow = lax.broadcasted_iota(jnp.int32, (nc, L), 0)
    row8 = lax.broadcasted_iota(jnp.int32, (SUBLANE, L), 0)
    for hh in heads:
        ig = ig_ref[hh, 0] + bi_ref[hh]
        lf = -_softplus(-(fg_ref[hh, 0] + bf_ref[hh]))
        b = lf
        for k in range(int(math.log2(L))):
            sh = 1 << k
            b = b + jnp.where(glane >= sh, pltpu.roll(b, sh, axis=1), 0.0)
        g = jnp.broadcast_to(b[:, L - 1:L], (nc, L))
        a = g - b + ig
        mloc = jnp.broadcast_to(jnp.max(a, axis=1, keepdims=True), (nc, L))
        m = jnp.zeros((1, L), F32)
        m_prev = jnp.zeros((nc, L), F32)
        m_next = jnp.zeros((nc, L), F32)
        for c in range(nc):
            m_prev = jnp.where(grow == c, m, m_prev)
            m = jnp.maximum(g[c:c + 1, :] + m, mloc[c:c + 1, :])
            m_next = jnp.where(grow == c, m, m_next)
        r = ig - b
        cmx = r
        for k in range(int(math.log2(L))):
            sh = 1 << k
            cmx = jnp.maximum(cmx, jnp.where(glane >= sh, pltpu.roll(cmx, sh, axis=1), -jnp.inf))
        mm = jnp.maximum(m_prev, cmx)
        s_int = jnp.exp(m_prev - mm)
        clamp = jnp.exp(-(b + mm))
        r_s[hh] = r
        w_s[hh] = jnp.exp(a - mloc)
        so_s[hh] = jnp.exp(g + m_prev - m_next)
        sn_s[hh] = jnp.exp(mloc - m_next)
        for c in range(nc):
            col_s[hh, c] = jnp.where(row8 == 0, mm[c:c + 1, :],
                                     jnp.where(row8 == 1, s_int[c:c + 1, :],
                                               jnp.where(row8 == 2, clamp[c:c + 1, :], 0.0)))

    zero_rows = jnp.zeros((ML_DHP - ML_DH, ML_DHP), BF16)
    c_s[...] = jnp.zeros(c_s.shape, F32)

    n_pairs = nc // 2

    def local_states(pair, loc_ref):
        for i in range(2):
            c = 2 * pair + i
            r0 = pl.multiple_of(c * L, L)
            for hh in heads:
                ktw = (kt_s[hh, c, :ML_DH, :].astype(F32) * w_s[hh, pl.ds(c, 1), :]).astype(BF16)
                loc_ref[i, hh] = jnp.dot(ktw, v_s[hh, pl.ds(r0, L), :], preferred_element_type=F32)

    def advance(pair, loc_ref):
        for i in range(2):
            c = 2 * pair + i
            for hh in heads:
                c_prev = c_s[hh]
                cp_s[hh, c] = jnp.concatenate([c_prev.astype(BF16), zero_rows], axis=0)
                c_s[hh] = (so_s[hh, pl.ds(c, 1), :][:, :1] * c_prev
                           + sn_s[hh, pl.ds(c, 1), :][:, :1] * loc_ref[i, hh])

    def state(k, carry):
        local_states(2 * k + 1, locb_s)
        advance(2 * k, loca_s)
        local_states(jnp.minimum(2 * k + 2, n_pairs - 1), loca_s)
        advance(2 * k + 1, locb_s)
        return carry

    local_states(0, loca_s)
    lax.fori_loop(0, n_pairs // 2, state, 0)

    tri = (lax.broadcasted_iota(jnp.int32, (L, L), 0) >= lax.broadcasted_iota(jnp.int32, (L, L), 1))
    first = lax.broadcasted_iota(jnp.int32, (L, ML_PW), 1) < ML_DH

    def chunk_rows(c):
        return pl.ds(pl.multiple_of(c * L, L), L)

    def raw_scores(c):
        rows = chunk_rows(c)
        return [jnp.dot(q_s[hh, rows, :], kt_s[hh, c], preferred_element_type=F32) for hh in heads]

    def weights(c, raw):
        out = []
        for hh in heads:
            c0 = ML_DEN_LANE[hh] % LANE - 2
            pads = [jnp.zeros((n, L), F32) for n in (c0, L - SUBLANE - c0)]
            cols = jnp.concatenate([a for a in (pads[0], col_s[hh, c], pads[1]) if a.shape[0]], axis=0).T
            rb = jnp.broadcast_to(r_s[hh, pl.ds(c, 1), :], (L, L))
            decay = jnp.exp(jnp.where(tri, rb - cols[:, c0:c0 + 1], -jnp.inf))
            out.append(((raw[hh] * decay).astype(BF16), cols[:, c0 + 1:c0 + 2], cols[:, c0 + 2:c0 + 3]))
        return out

    def numerators(c, wts):
        rows = chunk_rows(c)
        nds, invs = [], []
        for hh, (s_mat, s_int, clamp) in zip(heads, wts):
            nd = (jnp.dot(s_mat, v_s[hh, rows, :], preferred_element_type=F32)
                  + s_int * jnp.dot(q_s[hh, rows, :], cp_s[hh, c], preferred_element_type=F32))
            den = nd[:, ML_DEN_LANE[hh]:ML_DEN_LANE[hh] + 1]
            nds.append(nd)
            invs.append(1.0 / jnp.maximum(jnp.abs(den), clamp))
        w1 = ML_WINDOW[1]
        shared_first = lax.broadcasted_iota(jnp.int32, (L, ML_DHP - w1), 1) < ML_DH - w1
        shared = jnp.where(shared_first, nds[0][:, w1:], nds[1][:, :ML_DHP - w1])
        return jnp.concatenate([nds[0][:, :w1], shared, nds[1][:, ML_DHP - w1:]], axis=1), invs

    def gates(c):
        rows = chunk_rows(c)
        return (_sigmoid(og_ref[0, rows, :].astype(F32)),
                _silu(z_ref[0, rows, :].astype(F32)) * _layer_row(ng_ref, layer))

    def means(nd, og):
        x = og * nd
        mu0 = jnp.sum(jnp.where(first, x, 0.0), axis=1, keepdims=True) * (1.0 / ML_DH)
        mu1 = jnp.sum(jnp.where(first, 0.0, x), axis=1, keepdims=True) * (1.0 / ML_DH)
        return x, mu0, mu1

    def variances(x, mu0, mu1):
        dev = x - jnp.where(first, mu0, mu1)
        sq = dev * dev
        var0 = jnp.sum(jnp.where(first, sq, 0.0), axis=1, keepdims=True) * (1.0 / ML_DH)
        var1 = jnp.sum(jnp.where(first, 0.0, sq), axis=1, keepdims=True) * (1.0 / ML_DH)
        return dev, var0, var1

    def finish(c, dev, var0, var1, invs, zs):
        f0 = invs[0] * lax.rsqrt(invs[0] * invs[0] * var0 + LN_EPS)
        f1 = invs[1] * lax.rsqrt(invs[1] * invs[1] * var1 + LN_EPS)
        y_ref[0, chunk_rows(c), :] = (dev * jnp.where(first, f0, f1) * zs).astype(y_ref.dtype)

    def group(gi, carry):
        cs = [gi * ML_GROUP + i for i in range(ML_GROUP)]
        n = range(ML_GROUP)
        raws = [raw_scores(c) for c in cs]
        gts = [gates(c) for c in cs]
        wts = [weights(c, raws[i]) for i, c in enumerate(cs)]
        nums = [numerators(c, wts[i]) for i, c in enumerate(cs)]
        mus = [means(nums[i][0], gts[i][0]) for i in n]
        vrs = [variances(*mus[i]) for i in n]
        for i, c in enumerate(cs):
            finish(c, *vrs[i], nums[i][1], gts[i][1])
        return carry

    lax.fori_loop(0, nc // ML_GROUP, group, 0)


def _mlstm_branch(p, gates, gbias, cw, cb, wq, wkt, wv, ng, l):
    b, t, _ = p.shape
    nc = t // ML_CHUNK
    ub, ob, zb = C_ML_U // ML_PW, C_ML_O // ML_PW, C_ML_Z // ML_PW
    seq = lambda base: pl.BlockSpec((1, t, ML_PW), lambda pr, bi: (bi, 0, base + pr))
    vec = pl.BlockSpec((cb.shape[0], ML_PW), lambda pr, bi: (0, pr))
    taps = pl.BlockSpec((None, ML_QKV_BLOCK, ML_PW), lambda pr, bi: (l, 0, pr))
    gate_rows = lambda: pltpu.VMEM((2, nc, ML_CHUNK), F32)
    pair_states = lambda: pltpu.VMEM((2, 2, ML_DH, ML_DHP), F32)
    return pl.pallas_call(
        functools.partial(_mlstm_kernel, layer=l),
        grid=(ML_PAIRS, b),
        in_specs=[
            seq(ub), seq(ob), seq(zb),
            pl.BlockSpec((2, 1, nc, ML_CHUNK), lambda pr, bi: (pr, bi, 0, 0)),
            pl.BlockSpec((2, 1, nc, ML_CHUNK), lambda pr, bi: (ML_PAIRS + pr, bi, 0, 0)),
            pl.BlockSpec((None, 2, 1, ML_CHUNK), lambda pr, bi: (l, pr, 0, 0)),
            pl.BlockSpec((None, 2, 1, ML_CHUNK), lambda pr, bi: (l, ML_PAIRS + pr, 0, 0)),
            pl.BlockSpec((None, cw.shape[1], ML_PW), lambda pr, bi: (l, 0, pr)),
            vec, taps, taps, taps, vec,
        ],
        out_specs=pl.BlockSpec((1, t, ML_PW), lambda pr, bi: (bi, 0, pr)),
        out_shape=jax.ShapeDtypeStruct((b, t, ML_W), BF16),
        scratch_shapes=[
            pltpu.VMEM((2, ML_PW, ML_DHP), BF16),
            pltpu.VMEM((2, ML_DHP, ML_PW), BF16),
            pltpu.VMEM((2, ML_DHP, ML_DHP), BF16),
            pltpu.VMEM((2, t, ML_DHP), BF16),
            pltpu.VMEM((2, nc, ML_DHP, ML_CHUNK), BF16),
            pltpu.VMEM((2, t, ML_DHP), BF16),
            pltpu.VMEM((2, nc, ML_DHP, ML_DHP), BF16),
            pltpu.VMEM((2, ML_DH, ML_DHP), F32),
            gate_rows(), gate_rows(),
            pltpu.VMEM((2, nc, SUBLANE, ML_CHUNK), F32),
            gate_rows(), gate_rows(),
            pair_states(), pair_states(),
        ],
        compiler_params=pltpu.CompilerParams(
            dimension_semantics=("arbitrary", "arbitrary"), vmem_limit_bytes=_vmem_limit(56 << 20)),
        name="mlstm",
    )(p, p, p, gates, gates, gbias, gbias, cw, cb, wq, wkt, wv, ng)


MOBA_SEL_LANE = 6
MOBA_AHEAD = 3


def _moba_key_ext(nb):
    ext = np.zeros((nb, MOBA_BLOCK, MO_DH), np.float32)
    for n in range(nb):
        ext[n, :, 0:3] = n * MOBA_BLOCK
        ext[n, :, 3:6] = np.arange(MOBA_BLOCK, dtype=np.float32)[:, None]
        ext[n, :, MOBA_SEL_LANE + n] = 1.0
    return jnp.asarray(ext, dtype=BF16)


def _moba_slope_rows(slopes):
    rows = np.zeros((len(slopes), 1, MO_DH), np.float32)
    for h, s in enumerate(slopes):
        rest = np.float32(np.float32(s) * np.float32(LOG2E))
        for i in range(3):
            piece = np.float32(rest).astype(BF16).astype(np.float32)
            rows[h, 0, i] = rows[h, 0, 3 + i] = piece
            rest = np.float32(rest - piece)
    return jnp.asarray(rows)


def _moba_kernel(q_ref, k_ref, v_ref, z_ref, srow_ref, kext_ref, o_ref, l_s, qa_s):
    t = q_ref.shape[1]
    nb = t // MOBA_BLOCK
    bs = MOBA_BLOCK
    qscale = (MO_DH ** -0.5) * LOG2E

    krow = lax.broadcasted_iota(jnp.int32, (LANE, MO_DH), 0)
    kmean = jnp.zeros((LANE, MO_DH), F32)
    for n in range(nb):
        mean_n = jnp.sum(k_ref[0, n * bs:(n + 1) * bs, :].astype(F32), axis=0, keepdims=True) * (1.0 / bs)
        kmean = jnp.where(krow == MOBA_SEL_LANE + n, mean_n, kmean)
    kmean = kmean.astype(BF16)

    lane = lax.broadcasted_iota(jnp.int32, (bs, MO_DH), 1)
    causal = (lax.broadcasted_iota(jnp.int32, (bs, bs), 0) >= lax.broadcasted_iota(jnp.int32, (bs, bs), 1))
    ones_col = jnp.where(lane == 0, 1.0, 0.0).astype(BF16)
    slope_cols = jnp.where(lane < MOBA_SEL_LANE, srow_ref[0], 0.0)

    def prepare(qb):
        q_b = q_ref[0, qb * bs:(qb + 1) * bs, :]
        q_ext = slope_cols
        if qb > MOBA_TOPK:
            gate = lax.dot_general(q_b, kmean, NT_DIMS, preferred_element_type=F32)
            beaten = jnp.zeros((bs, MO_DH), F32)
            for m in range(qb):
                gm = gate[:, MOBA_SEL_LANE + m:MOBA_SEL_LANE + m + 1]
                wins = (gm > gate) | ((gm == gate) & (lane > MOBA_SEL_LANE + m))
                beaten = beaten + jnp.where(wins, 1.0, 0.0)
            past = (lane >= MOBA_SEL_LANE) & (lane < MOBA_SEL_LANE + qb)
            q_ext = jnp.where(past & (beaten >= float(MOBA_TOPK)), NEG_INF, slope_cols)
        qa_s[qb] = jnp.concatenate([(q_b.astype(F32) * qscale).astype(BF16), q_ext.astype(BF16)], axis=1)

    def scores(qb):
        q_aug = qa_s[qb]
        l_q = l_s.at[qb % l_s.shape[0]]
        mx = None
        for n in range(qb + 1):
            k_aug = jnp.concatenate([k_ref[0, n * bs:(n + 1) * bs, :], kext_ref[n]], axis=1)
            logit = lax.dot_general(q_aug, k_aug, NT_DIMS, preferred_element_type=F32)
            if n == qb:
                logit = jnp.where(causal, logit, NEG_INF)
            l_q[:, n * bs:(n + 1) * bs] = logit
            half = jnp.maximum(logit[:, :LANE], logit[:, LANE:])
            mx = half if mx is None else jnp.maximum(mx, half)
        return jnp.max(mx, axis=1, keepdims=True)

    def outputs(qb, m_row):
        l_q = l_s.at[qb % l_s.shape[0]]
        acc = jnp.zeros((bs, 2 * MO_DH), F32)
        for n in range(qb + 1):
            pr = jnp.exp2(l_q[:, n * bs:(n + 1) * bs] - m_row).astype(BF16)
            v_aug = jnp.concatenate([v_ref[0, n * bs:(n + 1) * bs, :], ones_col], axis=1)
            acc = acc + jnp.dot(pr, v_aug, preferred_element_type=F32)
        z = z_ref[0, qb * bs:(qb + 1) * bs, :].astype(F32)
        inv = 1.0 / acc[:, MO_DH:MO_DH + 1]
        o_ref[0, qb * bs:(qb + 1) * bs, :] = (acc[:, :MO_DH] * (inv * _silu(z))).astype(o_ref.dtype)

    ahead = l_s.shape[0] - 1
    for qb in range(min(ahead + 1, nb)):
        prepare(qb)
    m_rows = {qb: scores(qb) for qb in range(min(ahead, nb))}
    for qb in range(nb):
        if qb + ahead + 1 < nb:
            prepare(qb + ahead + 1)
        if qb + ahead < nb:
            m_rows[qb + ahead] = scores(qb + ahead)
        outputs(qb, m_rows.pop(qb))


def _moba_branch(p, srows, kext):
    b, t, _ = p.shape
    nb = t // MOBA_BLOCK
    qb, kb, vb, zb = (c // MO_DH for c in (C_MO_Q, C_MO_K, C_MO_V, C_MO_Z))
    seq = lambda base: pl.BlockSpec((1, t, MO_DH), lambda bi, h: (bi, 0, base + h))
    return pl.pallas_call(
        _moba_kernel,
        grid=(b, MO_HEADS),
        in_specs=[seq(qb), seq(kb), seq(vb), seq(zb),
                  pl.BlockSpec((1, 1, MO_DH), lambda bi, h: (h, 0, 0)),
                  pl.BlockSpec((nb, MOBA_BLOCK, MO_DH), lambda bi, h: (0, 0, 0))],
        out_specs=pl.BlockSpec((1, t, MO_DH), lambda bi, h: (bi, 0, h)),
        out_shape=jax.ShapeDtypeStruct((b, t, MO_W), BF16),
        scratch_shapes=[pltpu.VMEM((MOBA_AHEAD + 1, MOBA_BLOCK, t), F32),
                        pltpu.VMEM((nb, MOBA_BLOCK, 2 * MO_DH), BF16)],
        compiler_params=pltpu.CompilerParams(
            dimension_semantics=("arbitrary", "arbitrary"), vmem_limit_bytes=_vmem_limit(V7X_VMEM_BYTES)),
        name="moba",
    )(p, p, p, p, srows, kext)


def _stage_weights(w_hbm, layer, dst_ref, stage_ref, sem, overlap=None):
    slots, tile_rows = stage_ref.shape[0], stage_ref.shape[1]
    n_tiles = dst_ref.shape[0] // tile_rows

    def copy(k):
        return pltpu.make_async_copy(w_hbm.at[layer, pl.ds(k * tile_rows, tile_rows), :], stage_ref.at[k % slots],
                                     sem.at[k % slots])

    for k in range(min(slots - 1, n_tiles)):
        copy(k).start()
    if overlap is not None:
        overlap()
    for k in range(n_tiles):
        if k + slots - 1 < n_tiles:
            copy(k + slots - 1).start()
        copy(k).wait()
        dst_ref[k * tile_rows:(k + 1) * tile_rows, :] = stage_ref[k % slots].astype(BF16)


def _mem_kv_kernel(mem_ref, g_ref, w_hbm, o_ref, w_s, stage_s, sem, hm_s):
    layer = pl.program_id(0)
    g = g_ref[pl.ds(layer, 1), :]

    def normalise():
        for bi in range(mem_ref.shape[0]):
            hm_s[bi] = _rmsnorm_rows(mem_ref[bi], g).astype(BF16)

    _stage_weights(w_hbm, layer, w_s, stage_s, sem, overlap=normalise)
    for bi in range(mem_ref.shape[0]):
        o_ref[bi] = jnp.dot(hm_s[bi], w_s[...], preferred_element_type=F32).astype(o_ref.dtype)


def _mem_kv(mem, g, wkv):
    b, m, _ = mem.shape
    depth = wkv.shape[0]
    return pl.pallas_call(
        _mem_kv_kernel,
        grid=(depth,),
        in_specs=[pl.BlockSpec((b, m, D_MODEL), lambda l: (0, 0, 0)),
                  pl.BlockSpec(g.shape, lambda l: (0, 0)),
                  pl.BlockSpec(memory_space=pl.ANY)],
        out_specs=pl.BlockSpec((None, b, m, 2 * XA_W), lambda l: (l, 0, 0, 0)),
        out_shape=jax.ShapeDtypeStruct((depth, b, m, 2 * XA_W), BF16),
        scratch_shapes=[pltpu.VMEM((D_MODEL, 2 * XA_W), BF16),
                        pltpu.VMEM((MEM_KV_SLOTS, STAGE_ROWS, 2 * XA_W), F32),
                        pltpu.SemaphoreType.DMA((MEM_KV_SLOTS,)),
                        pltpu.VMEM((b, m, D_MODEL), BF16)],
        compiler_params=pltpu.CompilerParams(
            dimension_semantics=("arbitrary",), vmem_limit_bytes=_vmem_limit(V7X_VMEM_BYTES)),
        name="mem_kv",
    )(mem, g, wkv)


def _post_kernel(x_ref, yl_ref, ym_ref, yo_ref, wout_hbm, g_ref, wq_hbm, kv_ref, wo_hbm, fg_ref, o_ref,
                 x1_s, wout_ref, wq_ref, wo_ref, stage_s, stage_q_s, sem, sem_q, *, final_norm, layer):
    @pl.when((pl.program_id(0) == 0) & (pl.program_id(1) == 0))
    def _():
        _stage_weights(wout_hbm, layer, wout_ref, stage_s, sem)
        _stage_weights(wo_hbm, layer, wo_ref, stage_s, sem)
        _stage_weights(wq_hbm, layer, wq_ref, stage_q_s, sem_q)

    sub = x_ref.shape[1] // POST_SUB
    groups = [pl.ds(i * sub, sub) for i in range(POST_SUB)]

    for r in groups:
        acc = jnp.dot(yl_ref[0, r, :], wout_ref[0:LRU_W, :], preferred_element_type=F32)
        acc = acc + jnp.dot(ym_ref[0, r, :], wout_ref[LRU_W:LRU_W + ML_W, :], preferred_element_type=F32)
        acc = acc + jnp.dot(yo_ref[0, r, :], wout_ref[LRU_W + ML_W:, :], preferred_element_type=F32)
        x1_s[r, :] = x_ref[0, r, :] + acc
    qs = []
    for r in groups:
        hx = _rmsnorm_rows(x1_s[r, :], _layer_row(g_ref, layer)).astype(BF16)
        qs.append(jnp.dot(hx, wq_ref[...], preferred_element_type=F32).astype(BF16))
    scores = []
    for q in qs:
        scores.append([
            lax.dot_general(q[:, h * XA_DH:(h + 1) * XA_DH], kv_ref[0, :, h * XA_DH:(h + 1) * XA_DH], NT_DIMS,
                            preferred_element_type=F32) * (XA_DH ** -0.5)
            for h in range(XA_HEADS)])
    outs = []
    for sc in scores:
        heads = []
        for h, s in enumerate(sc):
            e = jnp.exp(s - jnp.max(s, axis=1, keepdims=True))
            pr = e / jnp.sum(e, axis=1, keepdims=True)
            v_h = kv_ref[0, :, XA_W + h * XA_DH:XA_W + (h + 1) * XA_DH]
            heads.append(jnp.dot(pr.astype(BF16), v_h, preferred_element_type=F32).astype(BF16))
        outs.append(jnp.concatenate(heads, axis=1))
    for r, o in zip(groups, outs):
        y = x1_s[r, :] + jnp.dot(o, wo_ref[...], preferred_element_type=F32)
        if final_norm:
            y = _rmsnorm_rows(y, fg_ref[...])
        o_ref[0, r, :] = y


def _post_mix(x, yl, ym, yo, wout, g, wq, kv, wo, fg, l, final_norm):
    b, t, _ = x.shape
    m = kv.shape[2]
    rows = lambda width: pl.BlockSpec((1, POST_TM, width), lambda bi, i: (bi, i, 0))
    hbm = pl.BlockSpec(memory_space=pl.ANY)
    weights = (D_MODEL * D_MODEL + D_MODEL * XA_W + XA_W * D_MODEL) * 2
    stages = POST_SLOTS * STAGE_ROWS * (D_MODEL + XA_W) * 4
    tiles = 2 * POST_TM * (2 * D_MODEL * 4 + D_MODEL * 2) + 2 * m * 2 * XA_W * 2
    vmem = weights + stages + tiles + 6 * POST_TM * D_MODEL * 4
    return pl.pallas_call(
        functools.partial(_post_kernel, final_norm=final_norm, layer=l),
        grid=(b, t // POST_TM),
        in_specs=[rows(D_MODEL), rows(LRU_W), rows(ML_W), rows(MO_W),
                  hbm,
                  pl.BlockSpec(g.shape, lambda bi, i: (0, 0)),
                  hbm,
                  pl.BlockSpec((None, 1, m, 2 * XA_W), lambda bi, i: (l, bi, 0, 0)),
                  hbm,
                  pl.BlockSpec((1, D_MODEL), lambda bi, i: (0, 0))],
        out_specs=rows(D_MODEL),
        out_shape=jax.ShapeDtypeStruct((b, t, D_MODEL), F32),
        scratch_shapes=[pltpu.VMEM((POST_TM, D_MODEL), F32),
                        pltpu.VMEM((D_MODEL, D_MODEL), BF16),
                        pltpu.VMEM((D_MODEL, XA_W), BF16),
                        pltpu.VMEM((XA_W, D_MODEL), BF16),
                        pltpu.VMEM((POST_SLOTS, STAGE_ROWS, D_MODEL), F32),
                        pltpu.VMEM((POST_SLOTS, STAGE_ROWS, XA_W), F32),
                        pltpu.SemaphoreType.DMA((POST_SLOTS,)),
                        pltpu.SemaphoreType.DMA((POST_SLOTS,))],
        compiler_params=pltpu.CompilerParams(
            dimension_semantics=("arbitrary", "arbitrary"), vmem_limit_bytes=_vmem_limit(vmem)),
        name="post_mix",
    )(x, yl, ym, yo, wout, g, wq, kv, wo, fg)


def _alibi_slopes(n):
    def pow2(m):
        start = 2.0 ** (-8.0 / m)
        return [start ** (i + 1) for i in range(m)]
    if math.log2(n).is_integer():
        s = pow2(n)
    else:
        c = 2 ** int(math.floor(math.log2(n)))
        s = pow2(c) + pow2(2 * c)[0::2][:n - c]
    return np.asarray(s, dtype=np.float32)


def _taps(w, perm):
    depth = w.shape[0]
    return jnp.transpose(w, perm).reshape(depth, ML_QKV_BLOCK, ML_W)


def kernel(x, mem, mix_norm_g, w_in, lru_conv_w, lru_conv_b, lru_wa, lru_ba, lru_wx, lru_bx, lru_lambda,
           ml_conv_w, ml_conv_b, ml_wq, ml_wk, ml_wv, ml_bi, ml_bf, ml_norm_g, w_out, xa_norm_g, mem_norm_g,
           xa_wq, xa_wkv, xa_wo, final_norm_g):
    b, t, d = x.shape
    depth = w_in.shape[0]
    nc = t // ML_CHUNK
    moba_srows = _moba_slope_rows(_alibi_slopes(MO_HEADS))
    moba_kext = _moba_key_ext(t // MOBA_BLOCK)
    w_in_t = jnp.swapaxes(w_in, 1, 2)
    lru_wax = jnp.concatenate([lru_wa, lru_wx], axis=-1).astype(BF16)
    ml_wq_t = _taps(ml_wq, (0, 2, 1, 3))
    ml_wkt_t = _taps(ml_wk, (0, 3, 1, 2))
    ml_wv_t = _taps(ml_wv, (0, 2, 1, 3))
    ml_gbias = jnp.broadcast_to(jnp.concatenate([ml_bi, ml_bf], axis=1)[:, :, None, None],
                                (depth, 2 * ML_HEADS, 1, ML_CHUNK))
    kv = _mem_kv(mem, mem_norm_g, xa_wkv)
    for l in range(depth):
        p, gates = _in_proj(x.reshape(b * t, d), mix_norm_g, w_in_t, l)
        p, gates = p.reshape(b, t, IN_COLS_P), gates.reshape(2 * ML_HEADS, b, nc, ML_CHUNK)
        y_lru = _lru_branch(p, lru_conv_w, lru_wax, lru_conv_b, lru_ba, lru_bx, lru_lambda, l)
        y_ml = _mlstm_branch(p, gates, ml_gbias, ml_conv_w, ml_conv_b, ml_wq_t, ml_wkt_t, ml_wv_t, ml_norm_g, l)
        y_mo = _moba_branch(p, moba_srows, moba_kext)
        x = _post_mix(x, y_lru, y_ml, y_mo, w_out, xa_norm_g, xa_wq, kv, xa_wo, final_norm_g[None, :], l,
                      final_norm=(l == depth - 1))
    return x
```
